```python
import math
import jax
import jax.numpy as jnp
from jax import lax
import numpy as np

D_MODEL = 2048
BATCH = 1
SEQ = 8192
DEPTH = 2
DEC_BATCH = 128
DEC_SEQ = 8
PAST_LEN = 2048
PAGE_SIZE = 128

N_EVEN = (DEPTH + 1) // 2
N_ODD = DEPTH // 2
N_MOD = 9
D_FF = 5504
EPS = 1e-6
A_HEADS = 8
A_DK = 128
A_DV = 128
A_QK = A_HEADS * A_DK
A_WIDTH = A_HEADS * A_DV
A_CHUNK = 64
B_HEADS = 16
B_KV_HEADS = 4
B_DH = 64
B_GROUP = B_HEADS // B_KV_HEADS
B_WIDTH = B_HEADS * B_DH
KV_WIDTH = B_KV_HEADS * B_DH
CMP_BLOCK = 32
CMP_STRIDE = 16
CMP_HIDDEN = 256
SEL_BLOCK = 64
N_SELECT = 16
WINDOW = 512
Q_BLOCK = 128
N_BUCKETS = 32
MAX_DISTANCE = 128
MIX_WIDTH = A_WIDTH + B_WIDTH
IN_DIM = 2 * A_QK + 2 * A_WIDTH + B_WIDTH + 6 * KV_WIDTH + 3 * B_HEADS
CONV_WIDTH = 31
CONV_DIM = D_MODEL
NEG = -1e30
FORCE = 1e9
POOL_NUM = 5
POOL_DEN = 4

kernel_name = 'hybrid_hgrn2_nsa_conformer_step'


def rms_norm(x, gain):
    xf = x.astype(jnp.float32)
    y = xf * lax.rsqrt(jnp.mean(xf * xf, axis=-1, keepdims=True) + EPS)
    return (y * gain.astype(jnp.float32)).astype(x.dtype)


def layer_norm(x, gain, bias):
    xf = x.astype(jnp.float32)
    xc = xf - jnp.mean(xf, axis=-1, keepdims=True)
    y = xc * lax.rsqrt(jnp.mean(xc * xc, axis=-1, keepdims=True) + EPS)
    return (y * gain.astype(jnp.float32) + bias.astype(jnp.float32)).astype(x.dtype)


def modulate(h, shift, scale):
    return h * (1 + scale) + shift


def swiglu(h, w_gate, w_up, w_down):
    return (jax.nn.silu(h @ w_gate) * (h @ w_up)) @ w_down


def rel_bucket(dist):
    n = jnp.maximum(dist, 0)
    exact = N_BUCKETS // 2
    nf = jnp.maximum(n, 1).astype(jnp.float32)
    large = exact + (jnp.log(nf / exact) / math.log(MAX_DISTANCE / exact) * (N_BUCKETS - exact)).astype(jnp.int32)
    return jnp.where(n < exact, n, jnp.minimum(large, N_BUCKETS - 1))


def hgrn2_recurrence(q, logf, k, v, s0):
    b, t, h, dk = q.shape
    c = A_CHUNK if t % A_CHUNK == 0 else t
    n = t // c

    def chunks(a):
        return jnp.moveaxis(a.reshape(b, n, c, *a.shape[2:]), 1, 0)

    causal = jnp.tril(jnp.ones((c, c), bool))

    def step(S, inp):
        qc, lfc, kc, vc = inp
        cum = jnp.cumsum(lfc, axis=1)
        diff = cum[:, :, None] - cum[:, None, :]
        decay = jnp.exp(jnp.where(causal[None, :, :, None, None], diff, -jnp.inf))
        scores = jnp.einsum('bthk,btshk,bshk->bhts', qc, decay, kc)
        o = jnp.einsum('bhts,bshv->bthv', scores, vc) + jnp.einsum('bthk,bhkv->bthv', qc * jnp.exp(cum), S)
        last = cum[:, -1]
        S = S * jnp.exp(last)[..., None] + jnp.einsum('bshk,bshv->bhkv', kc * jnp.exp(last[:, None] - cum), vc)
        return S, o

    S, o = lax.scan(step, s0, (chunks(q), chunks(logf), chunks(k), chunks(v)))
    return jnp.moveaxis(o, 0, 1).reshape(b, t, h, -1), S


def compress(rows, pe, w1, b1, w2, b2):
    b, L, g, d = rows.shape
    nblk = L // CMP_STRIDE
    n_sub = CMP_BLOCK // CMP_STRIDE
    n_tok = nblk - n_sub + 1
    xb = rows[:, :nblk * CMP_STRIDE].reshape(b, nblk, CMP_STRIDE, g, d).astype(jnp.float32)
    h = b1.astype(jnp.float32)
    for m in range(n_sub):
        sl = slice(m * CMP_STRIDE, (m + 1) * CMP_STRIDE)
        part = jnp.einsum('bnpgd,pdh->bngh', xb + pe[sl][None, None, :, None, :], w1[sl])
        h = h + part[:, m:m + n_tok]
    return jax.nn.silu(h) @ w2 + b2


def to_sel_blocks(rows):
    b, L, g, d = rows.shape
    n_blk = -(-L // SEL_BLOCK)
    rows = jnp.pad(rows, ((0, 0), (0, n_blk * SEL_BLOCK - L), (0, 0), (0, 0)))
    return rows.reshape(b, n_blk, SEL_BLOCK, g, d).transpose(0, 3, 1, 2, 4).reshape(b, g, n_blk, SEL_BLOCK * d)


def nsa_prepare(kv_cmp, kv_sel, pe, w1, b1, w2, b2):
    cmp_k = compress(kv_cmp[:, :, 0], pe[0], w1[0], b1[0], w2[0], b2[0])
    cmp_v = compress(kv_cmp[:, :, 1], pe[1], w1[1], b1[1], w2[1], b2[1])
    cmp_end = jnp.arange(cmp_k.shape[1]) * CMP_STRIDE + (CMP_BLOCK - 1)
    return cmp_k, cmp_v, cmp_end, to_sel_blocks(kv_sel[:, :, 0]), to_sel_blocks(kv_sel[:, :, 1])


def nsa_core(q, qpos, gates, cmp_k, cmp_v, cmp_end, sel_kb, sel_vb, win_k, win_v, win_pos, rel_bias):
    f32 = jnp.float32
    b, nq = q.shape[:2]
    G, R = B_KV_HEADS, B_GROUP
    qg = q.reshape(b, nq, G, R, B_DH).astype(f32) * (B_DH ** -0.5)
    table = rel_bias.astype(f32)
    n_cmp = cmp_k.shape[1]
    dist_c = qpos[:, None] - cmp_end[None, :]
    bias_c = table[rel_bucket(dist_c)].reshape(nq, n_cmp, G, R).transpose(2, 3, 0, 1)
    s_c = jnp.einsum('bqgrd,bngd->bgrqn', qg, cmp_k.astype(f32)) + bias_c
    mask_c = dist_c >= 0
    p_c = jnp.where(mask_c, jax.nn.softmax(jnp.where(mask_c, s_c, NEG), axis=-1), 0.0)
    o_c = jnp.einsum('bgrqn,bngd->bqgrd', p_c, cmp_v.astype(f32))
    n_blk = sel_kb.shape[2]
    blk_start = jnp.arange(n_blk) * SEL_BLOCK
    cmp_start = cmp_end - (CMP_BLOCK - 1)
    overlap = ((cmp_start[:, None] < blk_start[None, :] + SEL_BLOCK) & (cmp_end[:, None] >= blk_start[None, :])).astype(f32)
    imp = jnp.einsum('bgrqn,nj->bgqj', p_c, overlap)
    cur = qpos // SEL_BLOCK
    j = jnp.arange(n_blk)
    forced = (j[None, :] == 0) | (j[None, :] == cur[:, None]) | (j[None, :] == cur[:, None] - 1)
    valid = blk_start[None, :] <= qpos[:, None]
    imp = jnp.where(forced, FORCE, jnp.where(valid, imp, -FORCE))
    n_sel = min(N_SELECT, n_blk)
    _, idx = lax.top_k(imp, n_sel)
    bi = jnp.arange(b)[:, None, None, None]
    gi = jnp.arange(G)[None, :, None, None]
    k_s = sel_kb[bi, gi, idx].reshape(b, G, nq, n_sel, SEL_BLOCK, B_DH).astype(f32)
    v_s = sel_vb[bi, gi, idx].reshape(b, G, nq, n_sel, SEL_BLOCK, B_DH).astype(f32)
    kpos_s = idx[..., None] * SEL_BLOCK + jnp.arange(SEL_BLOCK)
    dist_s = qpos[None, None, :, None, None] - kpos_s
    table_g = table.reshape(N_BUCKETS, G, R).transpose(1, 0, 2)
    bias_s = jnp.moveaxis(table_g[gi[..., None], rel_bucket(dist_s)], -1, 2)
    s_s = jnp.einsum('bqgrd,bgqnpd->bgrqnp', qg, k_s) + bias_s
    s_s = jnp.where((dist_s >= 0)[:, :, None], s_s, NEG).reshape(b, G, R, nq, n_sel * SEL_BLOCK)
    p_s = jax.nn.softmax(s_s, axis=-1).reshape(b, G, R, nq, n_sel, SEL_BLOCK)
    o_s = jnp.einsum('bgrqnp,bgqnpd->bqgrd', p_s, v_s)
    n_w = win_k.shape[1]
    dist_w = qpos[:, None] - win_pos[None, :]
    mask_w = (dist_w >= 0) & (dist_w < WINDOW) & (win_pos[None, :] >= 0)
    bias_w = table[rel_bucket(dist_w)].reshape(nq, n_w, G, R).transpose(2, 3, 0, 1)
    s_w = jnp.einsum('bqgrd,bkgd->bgrqk', qg, win_k.astype(f32)) + bias_w
    p_w = jax.nn.softmax(jnp.where(mask_w, s_w, NEG), axis=-1)
    o_w = jnp.einsum('bgrqk,bkgd->bqgrd', p_w, win_v.astype(f32))
    g = gates.astype(f32).reshape(b, nq, 3, G, R, 1)
    o = g[:, :, 0] * o_c + g[:, :, 1] * o_s + g[:, :, 2] * o_w
    return o.reshape(b, nq, B_WIDTH).astype(q.dtype)


def nsa_prompt(q, gates, prep, kv_win, rel_bias):
    b, t = q.shape[:2]
    cmp_k, cmp_v, cmp_end, sel_kb, sel_vb = prep
    win_pad = jnp.pad(kv_win, ((0, 0), (WINDOW, 0), (0, 0), (0, 0), (0, 0)))

    def body(start):
        qpos = start + jnp.arange(Q_BLOCK)
        qb = lax.dynamic_slice_in_dim(q, start, Q_BLOCK, axis=1)
        gb = lax.dynamic_slice_in_dim(gates, start, Q_BLOCK, axis=1)
        wb = lax.dynamic_slice_in_dim(win_pad, start, Q_BLOCK + WINDOW, axis=1)
        wpos = start - WINDOW + jnp.arange(Q_BLOCK + WINDOW)
        return nsa_core(qb, qpos, gb, cmp_k, cmp_v, cmp_end, sel_kb, sel_vb, wb[:, :, 0], wb[:, :, 1], wpos, rel_bias)

    out = lax.map(body, jnp.arange(t // Q_BLOCK) * Q_BLOCK)
    return jnp.moveaxis(out, 0, 1).reshape(b, t, B_WIDTH)


def nsa_sample(q, gates, prep, win_rows, win_pos, pos0, rel_bias):
    b, t = q.shape[:2]
    cmp_k, cmp_v, cmp_end, sel_kb, sel_vb = prep

    def body(i):
        qb = lax.dynamic_slice_in_dim(q, i, 1, axis=1)
        gb = lax.dynamic_slice_in_dim(gates, i, 1, axis=1)
        qpos = (pos0 + i)[None]
        return nsa_core(qb, qpos, gb, cmp_k, cmp_v, cmp_end, sel_kb, sel_vb, win_rows[:, :, 0], win_rows[:, :, 1], win_pos, rel_bias)

    out = lax.map(body, jnp.arange(t))
    return jnp.moveaxis(out, 0, 1).reshape(b, t, B_WIDTH)


def even_mixer(h, l, e, P, s0, past_cmp, past_sel, win_buf):
    f32 = jnp.float32
    b, t, _ = h.shape
    u = h @ P['w_in_even'][e]
    c1 = A_QK
    c2 = c1 + A_QK
    c3 = c2 + A_WIDTH
    c4 = c3 + A_WIDTH
    c5 = c4 + B_WIDTH
    c6 = c5 + 6 * KV_WIDTH
    a_q, a_f, a_i, a_g, b_q, b_kv, b_g = jnp.split(u, [c1, c2, c3, c4, c5, c6], axis=-1)
    lb = jnp.cumsum(jax.nn.softmax(P['hgrn_lower_bound'].astype(f32), axis=0), axis=0)[l].reshape(A_HEADS, A_DK)
    f = lb + (1.0 - lb) * jax.nn.sigmoid(a_f.reshape(b, t, A_HEADS, A_DK).astype(f32))
    if s0 is None:
        s0 = jnp.zeros((b, A_HEADS, A_DK, A_DV), f32)
    o_a, s_new = hgrn2_recurrence(a_q.reshape(b, t, A_HEADS, A_DK).astype(f32), jnp.log(f), 1.0 - f,
                                  a_i.reshape(b, t, A_HEADS, A_DV).astype(f32), s0.astype(f32))
    o_a = rms_norm(o_a, P['hgrn_norm_g'][e].reshape(A_HEADS, A_DV)) * jax.nn.silu(a_g.reshape(b, t, A_HEADS, A_DV).astype(f32))
    o_a = o_a.reshape(b, t, A_WIDTH).astype(h.dtype)
    kv = b_kv.reshape(b, t, 3, 2, B_KV_HEADS, B_DH)
    kv_cmp, kv_sel, kv_win = kv[:, :, 0], kv[:, :, 1], kv[:, :, 2]
    q = b_q.reshape(b, t, B_HEADS, B_DH)
    gates = jax.nn.sigmoid(b_g.astype(f32)).reshape(b, t, 3, B_HEADS)
    cmp_w = (P['cmp_pe'][e], P['cmp_w1'][e], P['cmp_b1'][e], P['cmp_w2'][e], P['cmp_b2'][e])
    if past_cmp is None:
        prep = nsa_prepare(kv_cmp, kv_sel, *cmp_w)
        o_b = nsa_prompt(q, gates, prep, kv_win, P['rel_bias'])
        win_new = kv_win[:, t - min(WINDOW, t):]
    else:
        past = past_cmp.shape[1]
        full_cmp = jnp.concatenate([past_cmp.astype(kv_cmp.dtype), kv_cmp], axis=1)
        full_sel = jnp.concatenate([past_sel.astype(kv_sel.dtype), kv_sel], axis=1)
        prep = nsa_prepare(full_cmp, full_sel, *cmp_w)
        n_buf = win_buf.shape[1]
        win_rows = jnp.concatenate([win_buf.astype(kv_win.dtype), kv_win], axis=1)
        win_pos = past - n_buf + jnp.arange(n_buf + t)
        o_b = nsa_sample(q, gates, prep, win_rows, win_pos, past, P['rel_bias'])
        win_new = win_rows[:, t:]
    y = jnp.concatenate([o_a, o_b], axis=-1) @ P['w_out_even'][e]
    return y, (kv_cmp, kv_sel, win_new, s_new)


def conv_module(h, buf, w_pw1, b_pw1, w_dw, b_dw, ln_g, ln_b, w_pw2, b_pw2):
    u = h @ w_pw1 + b_pw1
    a, gt = jnp.split(u, 2, axis=-1)
    u = a * jax.nn.sigmoid(gt)
    ext = jnp.concatenate([buf.astype(u.dtype), u], axis=1)
    y = lax.conv_general_dilated(ext, w_dw[:, None, :].astype(u.dtype), window_strides=(1,), padding='VALID',
                                 dimension_numbers=('NWC', 'WIO', 'NWC'), feature_group_count=CONV_DIM) + b_dw
    y = jax.nn.silu(layer_norm(y, ln_g, ln_b))
    return y @ w_pw2 + b_pw2, ext[:, ext.shape[1] - (CONV_WIDTH - 1):]


def trunk(x, c, P, caches):
    b = x.shape[0]
    new_cmp, new_sel, new_win, new_hgrn, new_conv = [], [], [], [], []
    for l in range(DEPTH):
        i = l // 2
        mod = (jax.nn.silu(c) @ P['ada_w'][l] + P['ada_b'][l]).reshape(b, N_MOD, 1, D_MODEL)
        h = modulate(rms_norm(x, P['norm_g'][l, 0]), mod[:, 0], mod[:, 1])
        x = x + 0.5 * mod[:, 2] * swiglu(h, P['ffn_w_gate'][l, 0], P['ffn_w_up'][l, 0], P['ffn_w_down'][l, 0])
        h = modulate(rms_norm(x, P['norm_g'][l, 1]), mod[:, 3], mod[:, 4])
        if l % 2 == 0:
            if caches is None:
                y, (kc, ksel, wn, sn) = even_mixer(h, l, i, P, None, None, None, None)
            else:
                y, (kc, ksel, wn, sn) = even_mixer(h, l, i, P, caches['hgrn'][i], caches['cmp'][i], caches['sel'][i], caches['win'][i])
            new_cmp.append(kc)
            new_sel.append(ksel)
            new_win.append(wn)
            new_hgrn.append(sn)
        else:
            buf = jnp.zeros((b, CONV_WIDTH - 1, CONV_DIM), x.dtype) if caches is None else caches['conv'][i]
            y, cb = conv_module(h, buf, P['conv_w_pw1'][i], P['conv_b_pw1'][i], P['conv_w_dw'][i], P['conv_b_dw'][i],
                                P['conv_ln_g'][i], P['conv_ln_b'][i], P['conv_w_pw2'][i], P['conv_b_pw2'][i])
            new_conv.append(cb)
        x = x + mod[:, 5] * y
        h = modulate(rms_norm(x, P['norm_g'][l, 2]), mod[:, 6], mod[:, 7])
        x = x + 0.5 * mod[:, 8] * swiglu(h, P['ffn_w_gate'][l, 1], P['ffn_w_up'][l, 1], P['ffn_w_down'][l, 1])
    y = rms_norm(x, P['final_norm_g'])
    return y, (jnp.stack(new_cmp), jnp.stack(new_sel), jnp.stack(new_win), jnp.stack(new_hgrn), jnp.stack(new_conv))


def setup_inputs(seed: int = 0) -> dict:
    key = jax.random.key(seed)
    ks = iter(jax.random.split(key, 40))

    def nrm(shape, s):
        return jax.random.normal(next(ks), shape, jnp.float32) * s

    n_pages = PAST_LEN // PAGE_SIZE
    n_pool = (DEC_BATCH * n_pages * POOL_NUM) // POOL_DEN
    w_buf = min(WINDOW, PAST_LEN)
    x_prompt = nrm((BATCH, SEQ, D_MODEL), 1.0)
    x_sample = nrm((DEC_BATCH, DEC_SEQ, D_MODEL), 1.0)
    cache_cmp = nrm((N_EVEN, n_pool, PAGE_SIZE, 2, B_KV_HEADS, B_DH), 1.0)
    cache_sel = nrm((N_EVEN, n_pool, PAGE_SIZE, 2, B_KV_HEADS, B_DH), 1.0)
    state_win = nrm((N_EVEN, DEC_BATCH, w_buf, 2, B_KV_HEADS, B_DH), 1.0)
    state_hgrn = nrm((N_EVEN, DEC_BATCH, A_HEADS, A_DK, A_DV), 0.5)
    state_conv = nrm((N_ODD, DEC_BATCH, CONV_WIDTH - 1, CONV_DIM), 0.5)
    perm = jax.random.permutation(next(ks), n_pool)
    page_table = perm[:DEC_BATCH * n_pages].reshape(DEC_BATCH, n_pages).astype(jnp.int32)
    return {
        'x_prompt': x_prompt,
        'x_sample': x_sample,
        'cache_cmp': cache_cmp,
        'cache_sel': cache_sel,
        'state_win': state_win,
        'state_hgrn': state_hgrn,
        'state_conv': state_conv,
        'page_table': page_table,
        'c_prompt': nrm((BATCH, D_MODEL), 1.0),
        'c_sample': nrm((DEC_BATCH, D_MODEL), 1.0),
        'norm_g': 1.0 + nrm((DEPTH, 3, D_MODEL), 0.02),
        'ada_w': nrm((DEPTH, D_MODEL, N_MOD * D_MODEL), 0.5 * D_MODEL ** -0.5),
        'ada_b': nrm((DEPTH, N_MOD * D_MODEL), 0.02),
        'ffn_w_gate': nrm((DEPTH, 2, D_MODEL, D_FF), D_MODEL ** -0.5),
        'ffn_w_up': nrm((DEPTH, 2, D_MODEL, D_FF), D_MODEL ** -0.5),
        'ffn_w_down': nrm((DEPTH, 2, D_FF, D_MODEL), D_FF ** -0.5),
        'w_in_even': nrm((N_EVEN, D_MODEL, IN_DIM), D_MODEL ** -0.5),
        'hgrn_lower_bound': nrm((DEPTH + 1, A_QK), 0.5),
        'hgrn_norm_g': 1.0 + nrm((N_EVEN, A_WIDTH), 0.02),
        'cmp_pe': nrm((N_EVEN, 2, CMP_BLOCK, B_DH), 0.1),
        'cmp_w1': nrm((N_EVEN, 2, CMP_BLOCK, B_DH, CMP_HIDDEN), (CMP_BLOCK * B_DH) ** -0.5),
        'cmp_b1': nrm((N_EVEN, 2, CMP_HIDDEN), 0.02),
        'cmp_w2': nrm((N_EVEN, 2, CMP_HIDDEN, B_DH), CMP_HIDDEN ** -0.5),
        'cmp_b2': nrm((N_EVEN, 2, B_DH), 0.02),
        'rel_bias': nrm((N_BUCKETS, B_HEADS), 0.5),
        'w_out_even': nrm((N_EVEN, MIX_WIDTH, D_MODEL), MIX_WIDTH ** -0.5),
        'conv_w_pw1': nrm((N_ODD, D_MODEL, 2 * CONV_DIM), D_MODEL ** -0.5),
        'conv_b_pw1': nrm((N_ODD, 2 * CONV_DIM), 0.02),
        'conv_w_dw': nrm((N_ODD, CONV_WIDTH, CONV_DIM), CONV_WIDTH ** -0.5),
        'conv_b_dw': nrm((N_ODD, CONV_DIM), 0.02),
        'conv_ln_g': 1.0 + nrm((N_ODD, CONV_DIM), 0.02),
        'conv_ln_b': nrm((N_ODD, CONV_DIM), 0.02),
        'conv_w_pw2': nrm((N_ODD, CONV_DIM, D_MODEL), CONV_DIM ** -0.5),
        'conv_b_pw2': nrm((N_ODD, D_MODEL), 0.02),
        'final_norm_g': 1.0 + nrm((D_MODEL,), 0.02),
    }


def reference(x_prompt, x_sample, cache_cmp, cache_sel, state_win, state_hgrn, state_conv, page_table,
              c_prompt, c_sample, norm_g, ada_w, ada_b, ffn_w_gate, ffn_w_up, ffn_w_down, w_in_even,
              hgrn_lower_bound, hgrn_norm_g, cmp_pe, cmp_w1, cmp_b1, cmp_w2, cmp_b2, rel_bias, w_out_even,
              conv_w_pw1, conv_b_pw1, conv_w_dw, conv_b_dw, conv_ln_g, conv_ln_b, conv_w_pw2, conv_b_pw2,
              final_norm_g):
    P = {
        'norm_g': norm_g, 'ada_w': ada_w, 'ada_b': ada_b,
        'ffn_w_gate': ffn_w_gate, 'ffn_w_up': ffn_w_up, 'ffn_w_down': ffn_w_down,
        'w_in_even': w_in_even, 'hgrn_lower_bound': hgrn_lower_bound, 'hgrn_norm_g': hgrn_norm_g,
        'cmp_pe': cmp_pe, 'cmp_w1': cmp_w1, 'cmp_b1': cmp_b1, 'cmp_w2': cmp_w2, 'cmp_b2': cmp_b2,
        'rel_bias': rel_bias, 'w_out_even': w_out_even,
        'conv_w_pw1': conv_w_pw1, 'conv_b_pw1': conv_b_pw1, 'conv_w_dw': conv_w_dw, 'conv_b_dw': conv_b_dw,
        'conv_ln_g': conv_ln_g, 'conv_ln_b': conv_ln_b, 'conv_w_pw2': conv_w_pw2, 'conv_b_pw2': conv_b_pw2,
        'final_norm_g': final_norm_g,
    }
    y_prompt, (cmp_p, sel_p, win_p, hgrn_p, conv_p) = trunk(x_prompt, c_prompt, P, None)
    n_seq, n_pages = page_table.shape
    page = cache_cmp.shape[2]
    past_cmp = [cache_cmp[e][page_table].reshape(n_seq, n_pages * page, 2, B_KV_HEADS, B_DH) for e in range(N_EVEN)]
    past_sel = [cache_sel[e][page_table].reshape(n_seq, n_pages * page, 2, B_KV_HEADS, B_DH) for e in range(N_EVEN)]
    caches = {'hgrn': state_hgrn, 'cmp': past_cmp, 'sel': past_sel, 'win': state_win, 'conv': state_conv}
    y_sample, (cmp_s, sel_s, win_s, hgrn_s, conv_s) = trunk(x_sample, c_sample, P, caches)
    return (y_prompt, y_sample, cmp_p, cmp_s, sel_p, sel_s, win_p, win_s, hgrn_p, hgrn_s, conv_p, conv_s)
```

```python
import functools
import math

import numpy as np
import jax
import jax.numpy as jnp
from jax import lax
from jax.experimental import pallas as pl
from jax.experimental.pallas import tpu as pltpu

F32 = jnp.float32
BF16 = jnp.bfloat16

D_MODEL = 2048
SEQ = 8192
DEPTH = 2
DEC_BATCH = 128
DEC_SEQ = 8
PAST_LEN = 2048
PAGE_SIZE = 128
N_MOD = 9
D_FF = 5504
EPS = 1e-6
A_HEADS = 8
A_DK = 128
A_DV = 128
A_QK = A_HEADS * A_DK
A_WIDTH = A_HEADS * A_DV
B_HEADS = 16
B_KV_HEADS = 4
B_DH = 64
B_GROUP = B_HEADS // B_KV_HEADS
B_WIDTH = B_HEADS * B_DH
KV_WIDTH = B_KV_HEADS * B_DH
CMP_BLOCK = 32
CMP_STRIDE = 16
CMP_HIDDEN = 256
SEL_BLOCK = 64
N_SELECT = 16
WINDOW = 512
N_BUCKETS = 32
MAX_DISTANCE = 128
MIX_WIDTH = A_WIDTH + B_WIDTH
IN_DIM = 2 * A_QK + 2 * A_WIDTH + B_WIDTH + 6 * KV_WIDTH + 3 * B_HEADS
CONV_WIDTH = 31
CONV_DIM = D_MODEL
NEG = -1e30
FORCE = 1e9

V7X_VMEM_LIMIT_BYTES = 58 * 1024 * 1024
SUBLANES = 8
LANES = 128

N_SEQ_ROWS = DEC_BATCH + SUBLANES
PROMPT_ROW_BLOCK = DEC_BATCH // SUBLANES


def _params(*sem):
    return pltpu.CompilerParams(dimension_semantics=sem, vmem_limit_bytes=V7X_VMEM_LIMIT_BYTES)


def _silu(x):
    return x * jax.nn.sigmoid(x)


def _bdot(a, b):
    return jnp.dot(a.astype(BF16), b.astype(BF16), preferred_element_type=F32)


def _ada_kernel(c_ref, w_ref, b_ref, o_ref):
    o_ref[...] = _bdot(_silu(c_ref[...]), w_ref[...]) + b_ref[...]


def _ada_mod(c_all, ada_w, ada_b):
    n = c_all.shape[0]
    return pl.pallas_call(
        _ada_kernel,
        grid=(DEPTH, N_MOD),
        in_specs=[
            pl.BlockSpec((n, D_MODEL), lambda l, k: (0, 0)),
            pl.BlockSpec((None, D_MODEL, D_MODEL), lambda l, k: (l, 0, k)),
            pl.BlockSpec((None, None, 1, D_MODEL), lambda l, k: (l, k, 0, 0)),
        ],
        out_specs=pl.BlockSpec((None, None, n, D_MODEL), lambda l, k: (l, k, 0, 0)),
        out_shape=jax.ShapeDtypeStruct((DEPTH, N_MOD, n, D_MODEL), F32),
        compiler_params=_params("arbitrary", "arbitrary"),
        name="ada_mod",
    )(c_all, ada_w, ada_b.reshape(DEPTH, N_MOD, 1, D_MODEL))


def _norm_mod(x, g, shift, scale):
    ms = jnp.mean(x * x, axis=-1, keepdims=True)
    y = x * lax.rsqrt(ms + EPS) * g
    h = y * (1.0 + scale) + shift
    return h.reshape(x.shape[0] * x.shape[1], x.shape[2]).astype(BF16)


class _Rows:
    def __init__(self, m, bs, nt, seq_block):
        assert m % (bs * nt) == 0
        self.m, self.bs, self.nt, self.seq_block = m, bs, nt, seq_block
        self.tm = bs * nt
        self.n_tiles = m // self.tm

    def view(self, x2d):
        return x2d.reshape(self.m // self.bs, self.bs, x2d.shape[-1])

    def x_spec(self, width, col=lambda j: 0):
        return pl.BlockSpec((self.nt, self.bs, width), lambda i, j: (i, 0, col(j)))

    def mod_spec(self, layer, k, width=D_MODEL, col=lambda j: 0):
        sb = self.seq_block
        return pl.BlockSpec((None, None, self.bs, width), lambda i, j: (layer, k, sb, col(j)))


def _prompt_rows(tm):
    return _Rows(SEQ, SUBLANES, tm // SUBLANES, PROMPT_ROW_BLOCK)


def _sample_rows():
    return _Rows(DEC_BATCH * DEC_SEQ, DEC_BATCH, DEC_SEQ, 0)


FFN_TF = 256


def _ffn_kernel(x_ref, sh_ref, sc_ref, gt_ref, g_ref, wg_ref, wu_ref, wd_ref, fg_ref, o_ref, h_ref, *, final_norm):
    j = pl.program_id(1)
    nj = pl.num_programs(1)
    tf = wg_ref.shape[1]

    @pl.when(j == 0)
    def _():
        h_ref[...] = _norm_mod(x_ref[...], g_ref[...], sh_ref[...], sc_ref[...])
        o_ref[...] = jnp.zeros_like(o_ref)

    valid = D_FF - j * tf
    h = h_ref[...]
    a = _silu(_bdot(h, wg_ref[...])) * _bdot(h, wu_ref[...])
    col = lax.broadcasted_iota(jnp.int32, a.shape, 1)
    a = jnp.where(col < valid, a, 0.0)
    wd = wd_ref[...]
    row = lax.broadcasted_iota(jnp.int32, wd.shape, 0)
    wd = jnp.where(row < valid, wd, 0.0)
    o_ref[...] += _bdot(a, wd).reshape(o_ref.shape)

    @pl.when(j == nj - 1)
    def _():
        y = x_ref[...] + (0.5 * gt_ref[...]) * o_ref[...]
        if final_norm:
            ms = jnp.mean(y * y, axis=-1, keepdims=True)
            y = y * lax.rsqrt(ms + EPS) * fg_ref[...]
        o_ref[...] = y


def _ffn(x2d, rows, mod, layer, sub, P, final_norm=False):
    nf = pl.cdiv(D_FF, FFN_TF)
    half = sub // 2
    g_norm = P['norm_g'].reshape(DEPTH, 3, 1, D_MODEL)
    fg = P['final_norm_g'].reshape(1, D_MODEL)
    out = pl.pallas_call(
        functools.partial(_ffn_kernel, final_norm=final_norm),
        grid=(rows.n_tiles, nf),
        in_specs=[
            rows.x_spec(D_MODEL),
            rows.mod_spec(layer, 3 * sub), rows.mod_spec(layer, 3 * sub + 1), rows.mod_spec(layer, 3 * sub + 2),
            pl.BlockSpec((None, None, 1, D_MODEL), lambda i, j: (layer, sub, 0, 0)),
            pl.BlockSpec((None, None, D_MODEL, FFN_TF), lambda i, j: (layer, half, 0, j)),
            pl.BlockSpec((None, None, D_MODEL, FFN_TF), lambda i, j: (layer, half, 0, j)),
            pl.BlockSpec((None, None, FFN_TF, D_MODEL), lambda i, j: (layer, half, j, 0)),
            pl.BlockSpec((1, D_MODEL), lambda i, j: (0, 0)),
        ],
        out_specs=rows.x_spec(D_MODEL),
        out_shape=jax.ShapeDtypeStruct((rows.m // rows.bs, rows.bs, D_MODEL), F32),
        scratch_shapes=[pltpu.VMEM((rows.tm, D_MODEL), BF16)],
        compiler_params=_params("arbitrary", "arbitrary"),
        name="ffn_half_step",
    )(rows.view(x2d), mod, mod, mod, g_norm, P['ffn_w_gate'], P['ffn_w_up'], P['ffn_w_down'], fg)
    return out.reshape(rows.m, D_MODEL)


def _proj_kernel(x_ref, sh_ref, sc_ref, g_ref, w_ref, o_ref, h_ref):
    @pl.when(pl.program_id(1) == 0)
    def _():
        h_ref[...] = _norm_mod(x_ref[...], g_ref[...], sh_ref[...], sc_ref[...])

    o_ref[...] = _bdot(h_ref[...], w_ref[...])


def _proj(x2d, rows, mod, layer, g_norm4, w3, n_out, tn):
    return pl.pallas_call(
        _proj_kernel,
        grid=(rows.n_tiles, pl.cdiv(n_out, tn)),
        in_specs=[
            rows.x_spec(D_MODEL),
            rows.mod_spec(layer, 3), rows.mod_spec(layer, 4),
            pl.BlockSpec((None, None, 1, D_MODEL), lambda i, j: (layer, 1, 0, 0)),
            pl.BlockSpec((None, D_MODEL, tn), lambda i, j: (0, 0, j)),
        ],
        out_specs=pl.BlockSpec((rows.tm, tn), lambda i, j: (i, j)),
        out_shape=jax.ShapeDtypeStruct((rows.m, n_out), F32),
        scratch_shapes=[pltpu.VMEM((rows.tm, D_MODEL), BF16)],
        compiler_params=_params("arbitrary", "arbitrary"),
        name="prenorm_proj",
    )(rows.view(x2d), mod, mod, g_norm4, w3)


def _glu_proj_kernel(x_ref, sh_ref, sc_ref, g_ref, wa_ref, wg_ref, ba_ref, bg_ref, o_ref, h_ref):
    @pl.when(pl.program_id(1) == 0)
    def _():
        h_ref[...] = _norm_mod(x_ref[...], g_ref[...], sh_ref[...], sc_ref[...])

    h = h_ref[...]
    a = _bdot(h, wa_ref[...]) + ba_ref[...]
    gt = _bdot(h, wg_ref[...]) + bg_ref[...]
    o_ref[...] = a * jax.nn.sigmoid(gt)


def _glu_proj(x2d, rows, mod, layer, g_norm4, w3, b3, tn):
    nb = CONV_DIM // tn
    return pl.pallas_call(
        _glu_proj_kernel,
        grid=(rows.n_tiles, nb),
        in_specs=[
            rows.x_spec(D_MODEL),
            rows.mod_spec(layer, 3), rows.mod_spec(layer, 4),
            pl.BlockSpec((None, None, 1, D_MODEL), lambda i, j: (layer, 1, 0, 0)),
            pl.BlockSpec((None, D_MODEL, tn), lambda i, j: (0, 0, j)),
            pl.BlockSpec((None, D_MODEL, tn), lambda i, j: (0, 0, j + nb)),
            pl.BlockSpec((None, 1, tn), lambda i, j: (0, 0, j)),
            pl.BlockSpec((None, 1, tn), lambda i, j: (0, 0, j + nb)),
        ],
        out_specs=pl.BlockSpec((rows.tm, tn), lambda i, j: (i, j)),
        out_shape=jax.ShapeDtypeStruct((rows.m, CONV_DIM), F32),
        scratch_shapes=[pltpu.VMEM((rows.tm, D_MODEL), BF16)],
        compiler_params=_params("arbitrary", "arbitrary"),
        name="prenorm_glu_proj",
    )(rows.view(x2d), mod, mod, g_norm4, w3, w3, b3, b3)


def _out_kernel(*refs, n_in, has_bias):
    a_refs = refs[:n_in]
    w_refs = refs[n_in:2 * n_in]
    pos = 2 * n_in
    b_ref = refs[pos] if has_bias else None
    pos += int(has_bias)
    x_ref, gt_ref, o_ref = refs[pos:pos + 3]
    y = _bdot(a_refs[0][...], w_refs[0][...])
    for a_ref, w_ref in zip(a_refs[1:], w_refs[1:]):
        y += _bdot(a_ref[...], w_ref[...])
    if has_bias:
        y += b_ref[...]
    o_ref[...] = x_ref[...] + gt_ref[...] * y.reshape(o_ref.shape)


def _out_proj(acts, w3, k_offsets, bias3, x2d, rows, mod, layer, tn):
    n_in = len(acts)
    in_specs = [pl.BlockSpec((rows.tm, a.shape[1]), lambda i, j: (i, 0)) for a in acts]
    for a, off in zip(acts, k_offsets):
        kb = off // a.shape[1]
        in_specs.append(pl.BlockSpec((None, a.shape[1], tn), lambda i, j, kb=kb: (0, kb, j)))
    args = list(acts) + [w3] * n_in
    if bias3 is not None:
        in_specs.append(pl.BlockSpec((None, 1, tn), lambda i, j: (0, 0, j)))
        args.append(bias3)
    in_specs += [rows.x_spec(tn, col=lambda j: j), rows.mod_spec(layer, 5, width=tn, col=lambda j: j)]
    args += [rows.view(x2d), mod]
    out = pl.pallas_call(
        functools.partial(_out_kernel, n_in=n_in, has_bias=bias3 is not None),
        grid=(rows.n_tiles, D_MODEL // tn),
        in_specs=in_specs,
        out_specs=rows.x_spec(tn, col=lambda j: j),
        out_shape=jax.ShapeDtypeStruct((rows.m // rows.bs, rows.bs, D_MODEL), F32),
        compiler_params=_params("arbitrary", "arbitrary"),
        name="out_proj_residual",
    )(*args)
    return out.reshape(rows.m, D_MODEL)


HGRN_ROWS = 512


def _hgrn_prepare(f_ref, lb_ref, cum_s, kk_s, *, layer, c):
    p = lb_ref[...]
    e = jnp.exp(p - jnp.max(p, axis=0, keepdims=True))
    sm = e / jnp.sum(e, axis=0, keepdims=True)
    lb = jnp.sum(sm[:layer + 1], axis=0, keepdims=True)
    f = lb + (1.0 - lb) * jax.nn.sigmoid(f_ref[...])
    n = f.shape[0]
    r = lax.broadcasted_iota(jnp.int32, (n, n), 0)
    s = lax.broadcasted_iota(jnp.int32, (n, n), 1)
    tri = jnp.where((s <= r) & (s // c == r // c), 1.0, 0.0).astype(F32)
    cum_s[...] = jnp.dot(tri, jnp.log(f), preferred_element_type=F32, precision=lax.Precision.HIGHEST)
    kk_s[...] = 1.0 - f


def _hgrn_subchunk(r0, c, st, q_ref, v_ref, g_ref, gn_ref, cum_s, kk_s):
    rows = pl.ds(r0, c)
    cum = cum_s[rows, :]
    q = q_ref[rows, :]
    kk = kk_s[rows, :]
    vv = v_ref[rows, :]
    last = cum[c - 1:c, :]
    o = lax.dot_general((q * jnp.exp(cum)).astype(BF16), st.astype(BF16), (((1,), (1,)), ((), ())),
                        preferred_element_type=F32)
    srow = lax.broadcasted_iota(jnp.int32, (c, A_DK), 0)
    xs = []
    for t in range(c):
        d = jnp.where(srow <= t, cum[t:t + 1, :] - cum, NEG)
        xs.append(jnp.exp(d) * (q[t:t + 1, :] * kk))
    x = jnp.concatenate(xs, axis=0).astype(BF16)
    w = jnp.dot(x, jnp.ones((A_DK, A_DV), BF16), preferred_element_type=F32)
    o = o + jnp.sum(w.reshape(c, c, A_DV) * vv[None], axis=1)
    ke = kk * jnp.exp(last - cum)
    st_new = st * jnp.exp(last) + lax.dot_general(vv.astype(BF16), ke.astype(BF16), (((0,), (0,)), ((), ())),
                                                  preferred_element_type=F32)
    ms = jnp.mean(o * o, axis=-1, keepdims=True)
    y = o * lax.rsqrt(ms + EPS) * gn_ref[...] * _silu(g_ref[rows, :])
    return y, st_new


def _hgrn_prompt_kernel(q_ref, f_ref, v_ref, g_ref, lb_ref, gn_ref, o_ref, s_ref, st_ref, cum_s, kk_s, *, layer, c):
    i = pl.program_id(1)

    @pl.when(i == 0)
    def _():
        st_ref[...] = jnp.zeros_like(st_ref)

    _hgrn_prepare(f_ref, lb_ref, cum_s, kk_s, layer=layer, c=c)

    def body(n, carry):
        r0 = pl.multiple_of(n * c, c)
        y, st_new = _hgrn_subchunk(r0, c, st_ref[...], q_ref, v_ref, g_ref, gn_ref, cum_s, kk_s)
        st_ref[...] = st_new
        o_ref[pl.ds(r0, c), :] = y
        return carry

    lax.fori_loop(0, q_ref.shape[0] // c, body, 0)

    @pl.when(i == pl.num_programs(1) - 1)
    def _():
        s_ref[...] = st_ref[...].T


def _hgrn_sample_kernel(q_ref, f_ref, v_ref, g_ref, lb_ref, gn_ref, s0_ref, o_ref, s_ref, cum_s, kk_s, *, layer, c):
    _hgrn_prepare(f_ref, lb_ref, cum_s, kk_s, layer=layer, c=c)

    def body(n, carry):
        r0 = pl.multiple_of(n * c, c)
        y, st_new = _hgrn_subchunk(r0, c, s0_ref[n].T, q_ref, v_ref, g_ref, gn_ref, cum_s, kk_s)
        s_ref[n] = st_new.T
        o_ref[pl.ds(r0, c), :] = y
        return carry

    lax.fori_loop(0, q_ref.shape[0] // c, body, 0)


def _hgrn(u, lower_bound, norm_g2, layer, s0=None):
    m = u.shape[0]
    tc = HGRN_ROWS
    nb = A_QK // A_DK

    def col(seg):
        return pl.BlockSpec((tc, A_DK), lambda h, i, seg=seg: (i, seg * nb + h))

    in_specs = [col(0), col(1), col(2), col(3),
                pl.BlockSpec((DEPTH + 1, A_DK), lambda h, i: (0, h)),
                pl.BlockSpec((1, A_DV), lambda h, i: (0, h))]
    o_spec = pl.BlockSpec((tc, A_DV), lambda h, i: (i, h))
    scratch = [pltpu.VMEM((tc, A_DK), F32), pltpu.VMEM((tc, A_DK), F32)]
    if s0 is None:
        return pl.pallas_call(
            functools.partial(_hgrn_prompt_kernel, layer=layer, c=16),
            grid=(A_HEADS, m // tc),
            in_specs=in_specs,
            out_specs=[o_spec, pl.BlockSpec((None, A_DK, A_DV), lambda h, i: (h, 0, 0))],
            out_shape=[jax.ShapeDtypeStruct((m, A_WIDTH), F32), jax.ShapeDtypeStruct((A_HEADS, A_DK, A_DV), F32)],
            scratch_shapes=[pltpu.VMEM((A_DV, A_DK), F32)] + scratch,
            compiler_params=_params("arbitrary", "arbitrary"),
            name="hgrn2_prompt",
        )(u, u, u, u, lower_bound, norm_g2)
    c = DEC_SEQ
    ns = tc // c
    s_spec = pl.BlockSpec((ns, None, A_DK, A_DV), lambda h, i: (i, h, 0, 0))
    return pl.pallas_call(
        functools.partial(_hgrn_sample_kernel, layer=layer, c=c),
        grid=(A_HEADS, m // tc),
        in_specs=in_specs + [s_spec],
        out_specs=[o_spec, s_spec],
        out_shape=[jax.ShapeDtypeStruct((m, A_WIDTH), F32), jax.ShapeDtypeStruct(s0.shape, F32)],
        scratch_shapes=scratch,
        compiler_params=_params("arbitrary", "arbitrary"),
        name="hgrn2_sample",
    )(u, u, u, u, lower_bound, norm_g2, s0)


CONV_HALO = 32
CONV_ROWS = 256


def _ln_silu(y, g, b):
    mu = jnp.mean(y, axis=-1, keepdims=True)
    yc = y - mu
    var = jnp.mean(yc * yc, axis=-1, keepdims=True)
    return _silu(yc * lax.rsqrt(var + EPS) * g + b)


def _conv_prompt_kernel(cur_ref, halo_ref, w_ref, b_ref, g_ref, lb_ref, o_ref, ext_ref):
    i = pl.program_id(0)
    tt = cur_ref.shape[0]
    ext_ref[0:CONV_HALO, :] = jnp.where(i > 0, halo_ref[...], 0.0)
    ext_ref[CONV_HALO:, :] = cur_ref[...]
    off = CONV_HALO - (CONV_WIDTH - 1)
    y = jnp.zeros((tt, CONV_DIM), F32) + b_ref[...]
    for w in range(CONV_WIDTH):
        y = y + ext_ref[off + w:off + w + tt, :] * w_ref[w:w + 1, :]
    o_ref[...] = _ln_silu(y, g_ref[...], lb_ref[...]).astype(o_ref.dtype)


def _conv_prompt(glu, w_dw, b_dw, ln_g, ln_b):
    t = glu.shape[0]
    tt = CONV_ROWS
    r = tt // CONV_HALO
    vec = pl.BlockSpec((1, CONV_DIM), lambda i: (0, 0))
    return pl.pallas_call(
        _conv_prompt_kernel,
        grid=(t // tt,),
        in_specs=[
            pl.BlockSpec((tt, CONV_DIM), lambda i: (i, 0)),
            pl.BlockSpec((CONV_HALO, CONV_DIM), lambda i: (jnp.maximum(i * r - 1, 0), 0)),
            pl.BlockSpec((CONV_WIDTH, CONV_DIM), lambda i: (0, 0)),
            vec, vec, vec,
        ],
        out_specs=pl.BlockSpec((tt, CONV_DIM), lambda i: (i, 0)),
        out_shape=jax.ShapeDtypeStruct((t, CONV_DIM), BF16),
        scratch_shapes=[pltpu.VMEM((CONV_HALO + tt, CONV_DIM), F32)],
        compiler_params=_params("arbitrary"),
        name="conv_prompt",
    )(glu, glu, w_dw, b_dw, ln_g, ln_b)


def _conv_sample_kernel(u_ref, buf_ref, w_ref, b_ref, g_ref, lb_ref, o_ref, nb_ref):
    hist = CONV_WIDTH - 1
    nt = u_ref.shape[0]

    def ext(j):
        return buf_ref[j] if j < hist else u_ref[j - hist]

    for t in range(nt):
        y = ext(t) * w_ref[0:1, :] + b_ref[...]
        for w in range(1, CONV_WIDTH):
            y = y + ext(t + w) * w_ref[w:w + 1, :]
        o_ref[t] = _ln_silu(y, g_ref[...], lb_ref[...]).astype(o_ref.dtype)
    for j in range(hist):
        nb_ref[j] = ext(j + nt)


def _conv_sample(glu_t, buf_t, w_dw, b_dw, ln_g, ln_b):
    nt, ns, _ = glu_t.shape
    hist = CONV_WIDTH - 1
    bs = 32
    vec = pl.BlockSpec((1, CONV_DIM), lambda i: (0, 0))
    return pl.pallas_call(
        _conv_sample_kernel,
        grid=(ns // bs,),
        in_specs=[
            pl.BlockSpec((nt, bs, CONV_DIM), lambda i: (0, i, 0)),
            pl.BlockSpec((hist, bs, CONV_DIM), lambda i: (0, i, 0)),
            pl.BlockSpec((CONV_WIDTH, CONV_DIM), lambda i: (0, 0)),
            vec, vec, vec,
        ],
        out_specs=[pl.BlockSpec((nt, bs, CONV_DIM), lambda i: (0, i, 0)),
                   pl.BlockSpec((hist, bs, CONV_DIM), lambda i: (0, i, 0))],
        out_shape=[jax.ShapeDtypeStruct((nt, ns, CONV_DIM), BF16), jax.ShapeDtypeStruct((hist, ns, CONV_DIM), F32)],
        compiler_params=_params("arbitrary"),
        name="conv_sample",
    )(glu_t, buf_t, w_dw, b_dw, ln_g, ln_b)


PAIR = 2 * B_DH
N_PAIRS = KV_WIDTH // PAIR
BLOCKS_PER_TOKEN = SEL_BLOCK // CMP_STRIDE
N_SUB = CMP_BLOCK // CMP_STRIDE
CMP_OFF = 120
CMP_ROWS = 640
KV_PAD = WINDOW
FAR_TILE = 512
LOWEST = -3.0e38


def _half(shape, half):
    lane = lax.broadcasted_iota(jnp.int32, shape, len(shape) - 1)
    return (lane % PAIR) // B_DH == half


def _group_q(q_ref, g):
    parts = []
    for r in range(B_GROUP):
        h = g * B_GROUP + r
        x = q_ref[:, (h // 2) * PAIR:(h // 2 + 1) * PAIR]
        x = jnp.where(_half(x.shape, h % 2), x, 0.0) * (B_DH ** -0.5)
        if h % 2 != g % 2:
            x = pltpu.roll(x, B_DH, axis=1)
        parts.append(x)
    return jnp.concatenate(parts, axis=0).astype(BF16)


def _gate_rows(gate_ref, branch, g):
    sig = jax.nn.sigmoid(gate_ref[...])
    c0 = branch * B_HEADS + g * B_GROUP
    return jnp.concatenate([sig[:, c0 + r:c0 + r + 1] for r in range(B_GROUP)], axis=0)


def _scatter_heads(o, g):
    nq = o.shape[0] // B_GROUP
    outs = []
    for jp in range(2):
        acc = None
        for e in range(2):
            r = 2 * jp + e
            x = jnp.where(_half((nq, PAIR), g % 2), o[r * nq:(r + 1) * nq], 0.0)
            if e != g % 2:
                x = pltpu.roll(x, B_DH, axis=1)
            acc = x if acc is None else acc + x
        outs.append(acc)
    return jnp.concatenate(outs, axis=1)


def _nt_dot(a, b):
    return lax.dot_general(a, b, (((1,), (1,)), ((), ())), preferred_element_type=F32)


def _piece(qg, k, v, mask, bias=None):
    s = _nt_dot(qg, k)
    if bias is not None:
        s = s + bias
    s = jnp.where(mask, s, NEG)
    m = jnp.max(s, axis=-1, keepdims=True)
    p = jnp.where(mask, jnp.exp(s - m), 0.0)
    return m, jnp.sum(p, axis=-1, keepdims=True), jnp.dot(p.astype(BF16), v, preferred_element_type=F32), p


def _merge(a, b):
    m = jnp.maximum(a[0], b[0])
    ea = jnp.exp(a[0] - m)
    eb = jnp.exp(b[0] - m)
    return m, a[1] * ea + b[1] * eb, a[2] * ea + b[2] * eb


def _inv_or_zero(l):
    return jnp.where(l > 0.0, 1.0 / l, 0.0)


def _dot01(p, o01):
    hi = p.astype(BF16)
    r1 = p - hi.astype(F32)
    mid = r1.astype(BF16)
    lo = (r1 - mid.astype(F32)).astype(BF16)
    d = lambda a: jnp.dot(a, o01, preferred_element_type=F32)
    return d(hi) + d(mid) + d(lo)


def _overlap01(tok, blk):
    return jnp.where((tok >= BLOCKS_PER_TOKEN * blk - (N_SUB - 1)) & (tok <= BLOCKS_PER_TOKEN * blk + BLOCKS_PER_TOKEN - 1),
                     1.0, 0.0).astype(BF16)


def _select_blocks(imp, qpos):
    j = lax.broadcasted_iota(jnp.int32, imp.shape, 1)
    cur = qpos // SEL_BLOCK
    forced = (j == 0) | (j == cur) | (j == cur - 1)
    valid = j * SEL_BLOCK <= qpos
    imp = jnp.where(forced, FORCE, jnp.where(valid, imp, -FORCE))
    sel = jnp.zeros(imp.shape, F32)
    jf = j.astype(F32)
    for _ in range(N_SELECT):
        m = jnp.max(imp, axis=-1, keepdims=True)
        first = jnp.min(jnp.where(imp == m, jf, float(imp.shape[1])), axis=-1, keepdims=True)
        pick = jf == first
        sel = jnp.where(pick, 1.0, sel)
        imp = jnp.where(pick, LOWEST, imp)
    return sel


def _expand_blocks(sel_rows, first_blk, n_keys):
    jj = lax.broadcasted_iota(jnp.int32, (sel_rows.shape[1], n_keys), 0)
    kk = lax.broadcasted_iota(jnp.int32, (sel_rows.shape[1], n_keys), 1)
    e = jnp.where(jj == first_blk + kk // SEL_BLOCK, 1.0, 0.0).astype(BF16)
    return jnp.dot(sel_rows, e, preferred_element_type=F32)


def _bias_tables(rel_bias):
    d = np.arange(MAX_DISTANCE + 1)
    exact = N_BUCKETS // 2
    large = exact + (np.log(np.maximum(d, 1).astype(np.float32) / exact) / math.log(MAX_DISTANCE / exact)
                     * (N_BUCKETS - exact)).astype(np.int32)
    bucket = np.where(d < exact, d, np.minimum(large, N_BUCKETS - 1))
    by_dist = rel_bias.astype(F32)[bucket]

    def tile(dist, nq):
        t = by_dist[np.clip(dist, 0, MAX_DISTANCE)]
        t = t.reshape(nq, dist.shape[1], B_KV_HEADS, B_GROUP).transpose(2, 3, 0, 1)
        return t.reshape(B_KV_HEADS, B_GROUP * nq, dist.shape[1])

    nq = PAIR
    i = np.arange(nq)[:, None]
    far = by_dist[MAX_DISTANCE]
    shift = jnp.repeat(far.reshape(B_KV_HEADS, B_GROUP), nq, axis=1)[:, :, None]
    prompt_cmp = tile(i - CMP_STRIDE * (np.arange(PAIR)[None, :] - CMP_OFF) - (CMP_BLOCK - 1), nq) - shift
    prompt_kv = tile(i + nq - np.arange(2 * nq)[None, :], nq) - shift
    t = np.arange(DEC_SEQ)[:, None]
    qpos = PAST_LEN + t
    s_cmp = tile(qpos - (CMP_STRIDE * np.arange(PAIR)[None, :] + CMP_BLOCK - 1), DEC_SEQ)
    s_sel = tile(qpos - np.arange(PAST_LEN + PAIR)[None, :], DEC_SEQ)
    s_win = tile(qpos - (PAST_LEN - WINDOW + np.arange(WINDOW + PAIR)[None, :]), DEC_SEQ)
    flat = lambda a: a.reshape(B_KV_HEADS * B_GROUP * DEC_SEQ, a.shape[-1])
    return prompt_cmp, prompt_kv, flat(s_cmp), flat(s_sel), flat(s_win)


def _cmp_weights(pe, w1, b1, w2, b2):
    eye = jnp.eye(2, dtype=F32)
    w1r = w1.reshape(2, N_SUB, CMP_STRIDE, B_DH, CMP_HIDDEN)
    w1p = jnp.einsum('kmpdh,ef->kmpedfh', w1r, eye).reshape(2, N_SUB, CMP_STRIDE * PAIR, 2 * CMP_HIDDEN).astype(BF16)
    pep = jnp.tile(pe.reshape(2, N_SUB, CMP_STRIDE, 1, B_DH), (1, 1, 1, 2, 1)).reshape(2, N_SUB, 1, CMP_STRIDE * PAIR)
    b1p = jnp.tile(b1, (1, 2)).reshape(2, 1, 2 * CMP_HIDDEN)
    w2p = jnp.einsum('khd,ef->kehfd', w2, eye).reshape(2, 2 * CMP_HIDDEN, PAIR).astype(BF16)
    b2p = jnp.tile(b2, (1, 2)).reshape(2, 1, PAIR)
    return pep, w1p, b1p, w2p, b2p


def _compress_pair(src, pe_ref, w1_ref, b1_ref, w2_ref, b2_ref):
    x = jnp.concatenate([src(p) for p in range(CMP_STRIDE)], axis=1)
    m_rows = x.shape[0]
    h = b1_ref[...]
    for m in range(N_SUB):
        part = jnp.dot((x + pe_ref[m]).astype(BF16), w1_ref[m], preferred_element_type=F32)
        h = h + (part if m == 0 else pltpu.roll(part, m_rows - m, axis=0))
    return jnp.dot(_silu(h).astype(BF16), w2_ref[...], preferred_element_type=F32) + b2_ref[...]


def _cmp_weight_specs(kv_of):
    return [
        pl.BlockSpec((None, N_SUB, 1, CMP_STRIDE * PAIR), lambda *a: (kv_of(*a), 0, 0, 0)),
        pl.BlockSpec((None, N_SUB, CMP_STRIDE * PAIR, 2 * CMP_HIDDEN), lambda *a: (kv_of(*a), 0, 0, 0)),
        pl.BlockSpec((None, 1, 2 * CMP_HIDDEN), lambda *a: (kv_of(*a), 0, 0)),
        pl.BlockSpec((None, 2 * CMP_HIDDEN, PAIR), lambda *a: (kv_of(*a), 0, 0)),
        pl.BlockSpec((None, 1, PAIR), lambda *a: (kv_of(*a), 0, 0)),
    ]


def _compress_prompt_kernel(rows_ref, pe_ref, w1_ref, b1_ref, w2_ref, b2_ref, o_ref):
    n_blk = rows_ref.shape[0] // CMP_STRIDE
    tok = _compress_pair(lambda p: rows_ref[pl.ds(p, n_blk, stride=CMP_STRIDE), :], pe_ref, w1_ref, b1_ref, w2_ref, b2_ref)
    o_ref[...] = jnp.zeros_like(o_ref)
    o_ref[CMP_OFF:CMP_OFF + n_blk, :] = tok


def _compress_prompt(u, weights):
    t = u.shape[0]
    col0 = (2 * A_QK + 2 * A_WIDTH + B_WIDTH) // PAIR
    return pl.pallas_call(
        _compress_prompt_kernel,
        grid=(2 * N_PAIRS,),
        in_specs=[pl.BlockSpec((t, PAIR), lambda c: (0, col0 + c))] + _cmp_weight_specs(lambda c: c // N_PAIRS),
        out_specs=pl.BlockSpec((None, CMP_ROWS, PAIR), lambda c: (c, 0, 0)),
        out_shape=jax.ShapeDtypeStruct((2 * N_PAIRS, CMP_ROWS, PAIR), F32),
        compiler_params=_params("arbitrary"),
        name="nsa_compress_prompt",
    )(u, *weights)


CMP_SEQ_BLOCK = 4
N_PAGES = PAST_LEN // PAGE_SIZE


def _compress_sample_kernel(pt_ref, *refs):
    n_src = CMP_SEQ_BLOCK * N_PAGES
    pages = refs[:n_src]
    pe_ref, w1_ref, b1_ref, w2_ref, b2_ref, o_ref = refs[n_src:]
    n_blk = PAGE_SIZE // CMP_STRIDE
    tok = _compress_pair(
        lambda p: jnp.concatenate([pg[pl.ds(p, n_blk, stride=CMP_STRIDE), :] for pg in pages], axis=0),
        pe_ref, w1_ref, b1_ref, w2_ref, b2_ref)
    o_ref[...] = tok.reshape(o_ref.shape)


def _compress_sample(cache, page_table, weights):
    n_seq = page_table.shape[0]
    sb = CMP_SEQ_BLOCK
    n_tok = PAST_LEN // CMP_STRIDE
    page_specs = [pl.BlockSpec((None, PAGE_SIZE, PAIR), lambda c, i, pt, s=s, j=j: (pt[i * sb + s, j], 0, c))
                  for s in range(sb) for j in range(N_PAGES)]
    return pl.pallas_call(
        _compress_sample_kernel,
        grid_spec=pltpu.PrefetchScalarGridSpec(
            num_scalar_prefetch=1,
            grid=(2 * N_PAIRS, n_seq // sb),
            in_specs=page_specs + _cmp_weight_specs(lambda c, i, pt: c // N_PAIRS),
            out_specs=pl.BlockSpec((sb, None, n_tok, PAIR), lambda c, i, pt: (i, c, 0, 0)),
        ),
        out_shape=jax.ShapeDtypeStruct((n_seq, 2 * N_PAIRS, n_tok, PAIR), F32),
        compiler_params=_params("arbitrary", "arbitrary"),
        name="nsa_compress_sample",
    )(page_table, *([cache] * (sb * N_PAGES)), *weights)


Q_ROWS = 128


def _nsa_cmp_prompt_kernel(q_ref, gate_ref, cmp_ref, bias_ref, oc_ref, sel_ref):
    qb = pl.program_id(0)
    nq = q_ref.shape[0]
    rows = B_GROUP * nq
    n_tok = CMP_ROWS - PAIR
    near0 = pl.multiple_of(qb * (nq // CMP_STRIDE), SUBLANES)
    tok0 = near0 - CMP_OFF
    mask_far = lax.broadcasted_iota(jnp.int32, (rows, n_tok), 1) < tok0
    i = lax.broadcasted_iota(jnp.int32, (rows, PAIR), 0) % nq
    mn = lax.broadcasted_iota(jnp.int32, (rows, PAIR), 1)
    dist = i - CMP_STRIDE * (mn - CMP_OFF) - (CMP_BLOCK - 1)
    mask_near = (dist >= 0) & (tok0 + mn >= 0)
    o_far = _overlap01(lax.broadcasted_iota(jnp.int32, (n_tok, PAIR), 0), lax.broadcasted_iota(jnp.int32, (n_tok, PAIR), 1))
    o_near = _overlap01(tok0 + lax.broadcasted_iota(jnp.int32, (PAIR, PAIR), 0), lax.broadcasted_iota(jnp.int32, (PAIR, PAIR), 1))
    qpos = qb * nq + lax.broadcasted_iota(jnp.int32, (nq, PAIR), 0)
    for g in range(B_KV_HEADS):
        qg = _group_q(q_ref, g)
        kp, vp = g // 2, N_PAIRS + g // 2
        far = _piece(qg, cmp_ref[kp, CMP_OFF:CMP_OFF + n_tok, :].astype(BF16),
                     cmp_ref[vp, CMP_OFF:CMP_OFF + n_tok, :].astype(BF16), mask_far)
        near = _piece(qg, cmp_ref[kp, pl.ds(near0, PAIR), :].astype(BF16),
                      cmp_ref[vp, pl.ds(near0, PAIR), :].astype(BF16), mask_near, bias_ref[g])
        m, l, acc = _merge(far[:3], near[:3])
        linv = _inv_or_zero(l)
        oc_ref[:, g * 2 * PAIR:(g + 1) * 2 * PAIR] = _scatter_heads(acc * linv * _gate_rows(gate_ref, 0, g), g)
        pf = jnp.sum((far[3] * (jnp.exp(far[0] - m) * linv)).reshape(B_GROUP, nq, n_tok), axis=0)
        pn = jnp.sum((near[3] * (jnp.exp(near[0] - m) * linv)).reshape(B_GROUP, nq, PAIR), axis=0)
        imp = _dot01(pf, o_far) + _dot01(pn, o_near)
        sel_ref[:, g * PAIR:(g + 1) * PAIR] = _select_blocks(imp, qpos)


def _nsa_selwin_prompt_kernel(q_ref, gate_ref, sel_ref, oc_ref, kv_ref, bias_ref, o_ref):
    qb = pl.program_id(0)
    nq = q_ref.shape[0]
    rows = B_GROUP * nq
    qs = qb * nq
    i = lax.broadcasted_iota(jnp.int32, (rows, 2 * nq), 0) % nq
    kn = lax.broadcasted_iota(jnp.int32, (rows, 2 * nq), 1)
    near_ok = (i + nq - kn >= 0) & (qs - nq + kn >= 0)
    near_rows = pl.ds(pl.multiple_of(qs + KV_PAD - nq, nq), 2 * nq)
    n_wf = WINDOW - nq
    iw = lax.broadcasted_iota(jnp.int32, (rows, n_wf), 0) % nq
    kw = lax.broadcasted_iota(jnp.int32, (rows, n_wf), 1)
    winfar_ok = (kw > iw) & (qs - WINDOW + kw >= 0)
    winfar_rows = pl.ds(pl.multiple_of(qs, nq), n_wf)
    kf = lax.broadcasted_iota(jnp.int32, (rows, FAR_TILE), 1)
    for g in range(B_KV_HEADS):
        qg = _group_q(q_ref, g)
        cols = lambda base: slice(base + (g // 2) * PAIR, base + (g // 2 + 1) * PAIR)
        selm = sel_ref[:, g * PAIR:(g + 1) * PAIR].astype(BF16)
        selm = jnp.concatenate([selm] * B_GROUP, axis=0)

        def far_body(t, st):
            r = pl.ds(pl.multiple_of(KV_PAD + t * FAR_TILE, FAR_TILE), FAR_TILE)
            mask = (_expand_blocks(selm, t * (FAR_TILE // SEL_BLOCK), FAR_TILE) > 0.5) & (t * FAR_TILE + kf < qs - nq)
            return _merge(st, _piece(qg, kv_ref[r, cols(0)], kv_ref[r, cols(KV_WIDTH)], mask)[:3])

        init = (jnp.full((rows, 1), NEG, F32), jnp.zeros((rows, 1), F32), jnp.zeros((rows, PAIR), F32))
        st = lax.fori_loop(0, (qb + 2) // (FAR_TILE // nq), far_body, init)
        near_sel = (_expand_blocks(selm, 2 * qb - 2, 2 * nq) > 0.5) & near_ok
        st = _merge(st, _piece(qg, kv_ref[near_rows, cols(0)], kv_ref[near_rows, cols(KV_WIDTH)], near_sel, bias_ref[g])[:3])
        o_s = st[2] * _inv_or_zero(st[1])
        wf = _piece(qg, kv_ref[winfar_rows, cols(2 * KV_WIDTH)], kv_ref[winfar_rows, cols(3 * KV_WIDTH)], winfar_ok)
        wn = _piece(qg, kv_ref[near_rows, cols(2 * KV_WIDTH)], kv_ref[near_rows, cols(3 * KV_WIDTH)], near_ok, bias_ref[g])
        sw = _merge(wf[:3], wn[:3])
        o_w = sw[2] * _inv_or_zero(sw[1])
        o = o_s * _gate_rows(gate_ref, 1, g) + o_w * _gate_rows(gate_ref, 2, g)
        blk = slice(g * 2 * PAIR, (g + 1) * 2 * PAIR)
        o_ref[:, blk] = _scatter_heads(o, g) + oc_ref[:, blk]


def _nsa_prompt(u, kv_pad, cmp_tok, bias_cmp, bias_kv):
    t = u.shape[0]
    nq = Q_ROWS
    q_col = (2 * A_QK + 2 * A_WIDTH) // B_WIDTH
    g_col = (IN_DIM - 3 * B_HEADS) // PAIR
    q_spec = pl.BlockSpec((nq, B_WIDTH), lambda i: (i, q_col))
    gate_spec = pl.BlockSpec((nq, PAIR), lambda i: (i, g_col))
    oc, sel = pl.pallas_call(
        _nsa_cmp_prompt_kernel,
        grid=(t // nq,),
        in_specs=[q_spec, gate_spec,
                  pl.BlockSpec(cmp_tok.shape, lambda i: (0, 0, 0)),
                  pl.BlockSpec(bias_cmp.shape, lambda i: (0, 0, 0))],
        out_specs=[pl.BlockSpec((nq, B_WIDTH), lambda i: (i, 0)), pl.BlockSpec((nq, B_KV_HEADS * PAIR), lambda i: (i, 0))],
        out_shape=[jax.ShapeDtypeStruct((t, B_WIDTH), F32), jax.ShapeDtypeStruct((t, B_KV_HEADS * PAIR), F32)],
        compiler_params=_params("arbitrary"),
        name="nsa_cmp_select_prompt",
    )(u, u, cmp_tok, bias_cmp)
    return pl.pallas_call(
        _nsa_selwin_prompt_kernel,
        grid=(t // nq,),
        in_specs=[q_spec, gate_spec,
                  pl.BlockSpec((nq, B_KV_HEADS * PAIR), lambda i: (i, 0)),
                  pl.BlockSpec((nq, B_WIDTH), lambda i: (i, 0)),
                  pl.BlockSpec(kv_pad.shape, lambda i: (0, 0)),
                  pl.BlockSpec(bias_kv.shape, lambda i: (0, 0, 0))],
        out_specs=pl.BlockSpec((nq, B_WIDTH), lambda i: (i, 0)),
        out_shape=jax.ShapeDtypeStruct((t, B_WIDTH), F32),
        compiler_params=_params("arbitrary"),
        name="nsa_select_window_prompt",
    )(u, u, sel, oc, kv_pad, bias_kv)


SEL_KEYS = PAST_LEN + PAIR
WIN_KEYS = WINDOW + PAIR


def _nsa_sample_kernel(pt_ref, *refs):
    pages = refs[:N_PAGES]
    (q_ref, gate_ref, cmp_ref, selnew_ref, winnew_ref, win_ref, bc_ref, bs_ref, bw_ref,
     o_ref, winout_ref, kc_ref, wc_ref) = refs[N_PAGES:]
    nt = q_ref.shape[0]
    grp = B_GROUP * nt
    rows = B_KV_HEADS * grp
    pad = jnp.zeros((PAIR - nt, 4 * PAIR), F32)
    for j, pg in enumerate(pages):
        kc_ref[j * PAGE_SIZE:(j + 1) * PAGE_SIZE, :] = pg[...].astype(BF16)
    kc_ref[PAST_LEN:, :] = jnp.concatenate([selnew_ref[...], pad], axis=0).astype(BF16)
    wc_ref[:WINDOW, :] = win_ref[...].astype(BF16)
    wc_ref[WINDOW:, :] = jnp.concatenate([winnew_ref[...], pad], axis=0).astype(BF16)
    winout_ref[:WINDOW - nt, :] = win_ref[nt:, :]
    winout_ref[WINDOW - nt:, :] = winnew_ref[...]

    zero = jnp.zeros((grp, PAIR), BF16)
    qq = jnp.concatenate(
        [jnp.concatenate([_group_q(q_ref, g), zero] if g // 2 == 0 else [zero, _group_q(q_ref, g)], axis=1)
         for g in range(B_KV_HEADS)], axis=0)
    take = lambda acc, g: acc[g * grp:(g + 1) * grp, (g // 2) * PAIR:(g // 2 + 1) * PAIR]
    qpos = PAST_LEN + lax.broadcasted_iota(jnp.int32, (rows, 1), 0) % nt

    n_tok = cmp_ref.shape[1]
    ck = jnp.concatenate([cmp_ref[0], cmp_ref[1]], axis=1).astype(BF16)
    cv = jnp.concatenate([cmp_ref[2], cmp_ref[3]], axis=1).astype(BF16)
    mask_c = lax.broadcasted_iota(jnp.int32, (rows, n_tok), 1) < n_tok - (N_SUB - 1)
    mc, lc, acc_c, pc = _piece(qq, ck, cv, mask_c, bc_ref[...])
    pc = pc * _inv_or_zero(lc)
    p_all = jnp.concatenate([jnp.sum(pc[g * grp:(g + 1) * grp].reshape(B_GROUP, nt, n_tok), axis=0)
                             for g in range(B_KV_HEADS)], axis=0)
    o01 = _overlap01(lax.broadcasted_iota(jnp.int32, (n_tok, PAIR), 0), lax.broadcasted_iota(jnp.int32, (n_tok, PAIR), 1))
    qpos_gt = PAST_LEN + lax.broadcasted_iota(jnp.int32, (B_KV_HEADS * nt, PAIR), 0) % nt
    sel = _select_blocks(_dot01(p_all, o01), qpos_gt).astype(BF16)
    sel_rows = jnp.concatenate([sel[g * nt:(g + 1) * nt] for g in range(B_KV_HEADS) for _ in range(B_GROUP)], axis=0)

    ks = lax.broadcasted_iota(jnp.int32, (rows, SEL_KEYS), 1)
    mask_s = (_expand_blocks(sel_rows, 0, SEL_KEYS) > 0.5) & (ks <= qpos)
    ms, ls, acc_s, _ = _piece(qq, kc_ref[:, :2 * PAIR], kc_ref[:, 2 * PAIR:], mask_s, bs_ref[...])
    kw = PAST_LEN - WINDOW + lax.broadcasted_iota(jnp.int32, (rows, WIN_KEYS), 1)
    mask_w = (kw <= qpos) & (qpos - kw < WINDOW)
    mw, lw, acc_w, _ = _piece(qq, wc_ref[:, :2 * PAIR], wc_ref[:, 2 * PAIR:], mask_w, bw_ref[...])
    acc_c, acc_s, acc_w = acc_c * _inv_or_zero(lc), acc_s * _inv_or_zero(ls), acc_w * _inv_or_zero(lw)
    for g in range(B_KV_HEADS):
        o = (take(acc_c, g) * _gate_rows(gate_ref, 0, g) + take(acc_s, g) * _gate_rows(gate_ref, 1, g)
             + take(acc_w, g) * _gate_rows(gate_ref, 2, g))
        o_ref[:, g * 2 * PAIR:(g + 1) * 2 * PAIR] = _scatter_heads(o, g)


def _nsa_sample(u_s, cmp_tok, cache_sel, win_buf, page_table, bias_c, bias_s, bias_w):
    n_seq, nt, _ = u_s.shape
    q_col = (2 * A_QK + 2 * A_WIDTH) // B_WIDTH
    kv_col = (2 * A_QK + 2 * A_WIDTH + B_WIDTH) // (4 * PAIR)
    g_col = (IN_DIM - 3 * B_HEADS) // PAIR
    const = lambda a: pl.BlockSpec(a.shape, lambda b, pt: (0, 0))
    in_specs = [pl.BlockSpec((None, PAGE_SIZE, 4 * PAIR), lambda b, pt, j=j: (pt[b, j], 0, 0)) for j in range(N_PAGES)]
    in_specs += [
        pl.BlockSpec((None, nt, B_WIDTH), lambda b, pt: (b, 0, q_col)),
        pl.BlockSpec((None, nt, PAIR), lambda b, pt: (b, 0, g_col)),
        pl.BlockSpec((None,) + cmp_tok.shape[1:], lambda b, pt: (b, 0, 0, 0)),
        pl.BlockSpec((None, nt, 4 * PAIR), lambda b, pt: (b, 0, kv_col + 1)),
        pl.BlockSpec((None, nt, 4 * PAIR), lambda b, pt: (b, 0, kv_col + 2)),
        pl.BlockSpec((None, WINDOW, 4 * PAIR), lambda b, pt: (b, 0, 0)),
        const(bias_c), const(bias_s), const(bias_w),
    ]
    return pl.pallas_call(
        _nsa_sample_kernel,
        grid_spec=pltpu.PrefetchScalarGridSpec(
            num_scalar_prefetch=1,
            grid=(n_seq,),
            in_specs=in_specs,
            out_specs=[pl.BlockSpec((None, nt, B_WIDTH), lambda b, pt: (b, 0, 0)),
                       pl.BlockSpec((None, WINDOW, 4 * PAIR), lambda b, pt: (b, 0, 0))],
            scratch_shapes=[pltpu.VMEM((SEL_KEYS, 4 * PAIR), BF16), pltpu.VMEM((WIN_KEYS, 4 * PAIR), BF16)],
        ),
        out_shape=[jax.ShapeDtypeStruct((n_seq, nt, B_WIDTH), F32), jax.ShapeDtypeStruct(win_buf.shape, F32)],
        compiler_params=_params("arbitrary"),
        name="nsa_sample",
    )(page_table, *([cache_sel] * N_PAGES), u_s, u_s, cmp_tok, u_s, u_s, win_buf, bias_c, bias_s, bias_w)


PROMPT_TM = 1024
PROJ_TN = 512
COL_CMP = 2 * A_QK + 2 * A_WIDTH + B_WIDTH
COL_SEL = COL_CMP + 2 * KV_WIDTH
COL_WIN = COL_SEL + 2 * KV_WIDTH
COL_GATE = COL_WIN + 2 * KV_WIDTH


def _time_major(a):
    return a.transpose(1, 0, 2).reshape(a.shape[0] * a.shape[1], a.shape[2])


def _seq_major(a2d, n_seq):
    return a2d.reshape(a2d.shape[0] // n_seq, n_seq, a2d.shape[1]).transpose(1, 0, 2)


def _kv_rows(u3, col):
    return u3[..., col:col + 2 * KV_WIDTH].reshape(u3.shape[:-1] + (2, B_KV_HEADS, B_DH))


def kernel(x_prompt, x_sample, cache_cmp, cache_sel, state_win, state_hgrn, state_conv, page_table, c_prompt, c_sample, norm_g, ada_w, ada_b, ffn_w_gate, ffn_w_up, ffn_w_down, w_in_even, hgrn_lower_bound, hgrn_norm_g, cmp_pe, cmp_w1, cmp_b1, cmp_w2, cmp_b2, rel_bias, w_out_even, conv_w_pw1, conv_b_pw1, conv_w_dw, conv_b_dw, conv_ln_g, conv_ln_b, conv_w_pw2, conv_b_pw2, final_norm_g):
    P = {'norm_g': norm_g, 'ffn_w_gate': ffn_w_gate, 'ffn_w_up': ffn_w_up, 'ffn_w_down': ffn_w_down,
         'final_norm_g': final_norm_g}
    n_seq = x_sample.shape[0]
    n_pool = cache_cmp.shape[1]
    rows_p, rows_s = _prompt_rows(PROMPT_TM), _sample_rows()
    g4 = norm_g.reshape(DEPTH, 3, 1, D_MODEL)
    c_all = jnp.concatenate([c_sample, jnp.tile(c_prompt, (SUBLANES, 1))], axis=0)
    mod = _ada_mod(c_all, ada_w, ada_b)
    bias_pc, bias_pkv, bias_sc, bias_ss, bias_sw = _bias_tables(rel_bias)

    xp = x_prompt.reshape(SEQ, D_MODEL)
    xs = _time_major(x_sample)
    cmp_p, cmp_s, sel_p, sel_s, win_p, win_s, hgrn_p, hgrn_s, conv_p, conv_s = ([] for _ in range(10))
    for l in range(DEPTH):
        i = l // 2
        last = l == DEPTH - 1
        xp = _ffn(xp, rows_p, mod, l, 0, P)
        xs = _ffn(xs, rows_s, mod, l, 0, P)
        if l % 2 == 0:
            up = _proj(xp, rows_p, mod, l, g4, w_in_even[i:i + 1], IN_DIM, PROJ_TN)
            us = _seq_major(_proj(xs, rows_s, mod, l, g4, w_in_even[i:i + 1], IN_DIM, PROJ_TN), n_seq)
            gn = hgrn_norm_g[i:i + 1]
            oa_p, hp = _hgrn(up, hgrn_lower_bound, gn, l)
            oa_s, hs = _hgrn(us.reshape(n_seq * DEC_SEQ, IN_DIM), hgrn_lower_bound, gn, l, state_hgrn[i])
            weights = _cmp_weights(cmp_pe[i], cmp_w1[i], cmp_b1[i], cmp_w2[i], cmp_b2[i])
            kv_pad = jnp.pad(up[:, COL_SEL:COL_GATE].astype(BF16), ((KV_PAD, 0), (0, 0)))
            ob_p = _nsa_prompt(up, kv_pad, _compress_prompt(up, weights), bias_pc, bias_pkv)
            cmp_tok_s = _compress_sample(cache_cmp[i].reshape(n_pool, PAGE_SIZE, 4 * PAIR), page_table, weights)
            ob_s, wn = _nsa_sample(us, cmp_tok_s, cache_sel[i].reshape(n_pool, PAGE_SIZE, 4 * PAIR),
                                   state_win[i].reshape(n_seq, WINDOW, 4 * PAIR), page_table, bias_sc, bias_ss, bias_sw)
            xp = _out_proj([oa_p, ob_p], w_out_even[i:i + 1], [0, A_WIDTH], None, xp, rows_p, mod, l, PROJ_TN)
            xs = _out_proj([_time_major(oa_s.reshape(n_seq, DEC_SEQ, A_WIDTH)), _time_major(ob_s)],
                           w_out_even[i:i + 1], [0, A_WIDTH], None, xs, rows_s, mod, l, PROJ_TN)
            up3 = up[None]
            cmp_p.append(_kv_rows(up3, COL_CMP))
            sel_p.append(_kv_rows(up3, COL_SEL))
            win_p.append(_kv_rows(up3[:, SEQ - min(WINDOW, SEQ):], COL_WIN))
            cmp_s.append(_kv_rows(us, COL_CMP))
            sel_s.append(_kv_rows(us, COL_SEL))
            win_s.append(wn.reshape(n_seq, WINDOW, 2, B_KV_HEADS, B_DH))
            hgrn_p.append(hp[None])
            hgrn_s.append(hs)
        else:
            b_pw1 = conv_b_pw1[i].reshape(1, 1, 2 * CONV_DIM)
            b_pw2 = conv_b_pw2[i].reshape(1, 1, D_MODEL)
            vec = lambda a: a[i].reshape(1, CONV_DIM)
            glu_p = _glu_proj(xp, rows_p, mod, l, g4, conv_w_pw1[i:i + 1], b_pw1, PROJ_TN)
            act_p = _conv_prompt(glu_p, conv_w_dw[i], vec(conv_b_dw), vec(conv_ln_g), vec(conv_ln_b))
            xp = _out_proj([act_p], conv_w_pw2[i:i + 1], [0], b_pw2, xp, rows_p, mod, l, PROJ_TN)
            glu_s = _glu_proj(xs, rows_s, mod, l, g4, conv_w_pw1[i:i + 1], b_pw1, PROJ_TN)
            act_s, nb = _conv_sample(glu_s.reshape(DEC_SEQ, n_seq, CONV_DIM), state_conv[i].transpose(1, 0, 2),
                                     conv_w_dw[i], vec(conv_b_dw), vec(conv_ln_g), vec(conv_ln_b))
            xs = _out_proj([act_s.reshape(DEC_SEQ * n_seq, CONV_DIM)], conv_w_pw2[i:i + 1], [0], b_pw2, xs, rows_s, mod, l, PROJ_TN)
            conv_p.append(glu_p[None, SEQ - (CONV_WIDTH - 1):])
            conv_s.append(nb.transpose(1, 0, 2))
        xp = _ffn(xp, rows_p, mod, l, 2, P, final_norm=last)
        xs = _ffn(xs, rows_s, mod, l, 2, P, final_norm=last)
    y_prompt = xp.reshape(1, SEQ, D_MODEL)
    y_sample = _seq_major(xs, n_seq)
    st = jnp.stack
    return (y_prompt, y_sample, st(cmp_p), st(cmp_s), st(sel_p), st(sel_s), st(win_p), st(win_s),
            st(hgrn_p), st(hgrn_s), st(conv_p), st(conv_s))
```

```python
import functools
import math

import numpy as np
import jax
import jax.numpy as jnp
from jax import lax
from jax.experimental import pallas as pl
from jax.experimental.pallas import tpu as pltpu

F32 = jnp.float32
BF16 = jnp.bfloat16

D_MODEL = 2048
SEQ = 8192
DEPTH = 2
DEC_BATCH = 128
DEC_SEQ = 8
PAST_LEN = 2048
PAGE_SIZE = 128
N_MOD = 9
D_FF = 5504
EPS = 1e-6
A_HEADS = 8
A_DK = 128
A_DV = 128
A_QK = A_HEADS * A_DK
A_WIDTH = A_HEADS * A_DV
B_HEADS = 16
B_KV_HEADS = 4
B_DH = 64
B_GROUP = B_HEADS // B_KV_HEADS
B_WIDTH = B_HEADS * B_DH
KV_WIDTH = B_KV_HEADS * B_DH
CMP_BLOCK = 32
CMP_STRIDE = 16
CMP_HIDDEN = 256
SEL_BLOCK = 64
N_SELECT = 16
WINDOW = 512
N_BUCKETS = 32
MAX_DISTANCE = 128
MIX_WIDTH = A_WIDTH + B_WIDTH
IN_DIM = 2 * A_QK + 2 * A_WIDTH + B_WIDTH + 6 * KV_WIDTH + 3 * B_HEADS
CONV_WIDTH = 31
CONV_DIM = D_MODEL
NEG = -1e30
FORCE = 1e9

V7X_VMEM_LIMIT_BYTES = 60 * 1024 * 1024
SUBLANES = 8
LANES = 128

N_SEQ_ROWS = DEC_BATCH + SUBLANES
PROMPT_ROW_BLOCK = DEC_BATCH // SUBLANES


def _params(*sem):
    return pltpu.CompilerParams(dimension_semantics=sem, vmem_limit_bytes=V7X_VMEM_LIMIT_BYTES)


def _silu(x):
    return x * jax.nn.sigmoid(x)


def _bdot(a, b):
    return jnp.dot(a.astype(BF16), b.astype(BF16), preferred_element_type=F32)


def _ada_kernel(c_ref, w_ref, b_ref, o_ref):
    o_ref[...] = _bdot(_silu(c_ref[...]), w_ref[...]) + b_ref[...]


def _ada_mod(c_all, ada_w, ada_b):
    n = c_all.shape[0]
    return pl.pallas_call(
        _ada_kernel,
        grid=(DEPTH, N_MOD),
        in_specs=[
            pl.BlockSpec((n, D_MODEL), lambda l, k: (0, 0)),
            pl.BlockSpec((None, D_MODEL, D_MODEL), lambda l, k: (l, 0, k)),
            pl.BlockSpec((None, None, 1, D_MODEL), lambda l, k: (l, k, 0, 0)),
        ],
        out_specs=pl.BlockSpec((None, None, n, D_MODEL), lambda l, k: (l, k, 0, 0)),
        out_shape=jax.ShapeDtypeStruct((DEPTH, N_MOD, n, D_MODEL), F32),
        compiler_params=_params("arbitrary", "arbitrary"),
        name="ada_mod",
    )(c_all, ada_w, ada_b.reshape(DEPTH, N_MOD, 1, D_MODEL))


def _norm_mod(x, g, shift, scale):
    ms = jnp.mean(x * x, axis=-1, keepdims=True)
    y = x * lax.rsqrt(ms + EPS) * g
    h = y * (1.0 + scale) + shift
    return h.reshape(x.shape[0] * x.shape[1], x.shape[2]).astype(BF16)


class _Rows:
    def __init__(self, m, bs, nt, seq_block):
        assert m % (bs * nt) == 0
        self.m, self.bs, self.nt, self.seq_block = m, bs, nt, seq_block
        self.tm = bs * nt
        self.n_tiles = m // self.tm

    def view(self, x2d):
        return x2d.reshape(self.m // self.bs, self.bs, x2d.shape[-1])

    def x_spec(self, width, col=lambda j: 0, single_buffer=False):
        mode = dict(pipeline_mode=pl.Buffered(1)) if single_buffer else {}
        return pl.BlockSpec((self.nt, self.bs, width), lambda i, j: (i, 0, col(j)), **mode)

    def mod_spec(self, layer, k, width=D_MODEL, col=lambda j: 0):
        sb = self.seq_block
        return pl.BlockSpec((None, None, self.bs, width), lambda i, j: (layer, k, sb, col(j)))


def _prompt_rows(tm):
    return _Rows(SEQ, SUBLANES, tm // SUBLANES, PROMPT_ROW_BLOCK)


def _sample_rows():
    return _Rows(DEC_BATCH * DEC_SEQ, DEC_BATCH, DEC_SEQ, 0)


FFN_TF_F32 = 256
FFN_TF_BF16 = 512
FFN_ACC_CHUNKS = 4


def _ffn_kernel(x_ref, sh_ref, sc_ref, gt_ref, g_ref, wg_ref, wu_ref, wd_ref, fg_ref, o_ref, *rest, final_norm, emit):
    if emit:
        wg_o, wu_o, wd_o, h_ref = rest
    else:
        (h_ref,) = rest
    j = pl.program_id(1)
    nj = pl.num_programs(1)
    tf = wg_ref.shape[1]

    nt, bs = o_ref.shape[0], o_ref.shape[1]
    step = max(nt // FFN_ACC_CHUNKS, 1)

    @pl.when(j == 0)
    def _():
        for r in range(0, nt, step):
            h_ref[r * bs:(r + step) * bs, :] = _norm_mod(x_ref[r:r + step], g_ref[...], sh_ref[...], sc_ref[...])
        o_ref[...] = jnp.zeros_like(o_ref)

    wg, wu, wd = wg_ref[...].astype(BF16), wu_ref[...].astype(BF16), wd_ref[...].astype(BF16)
    if emit:
        wg_o[...], wu_o[...], wd_o[...] = wg, wu, wd
    valid = D_FF - j * tf
    h = h_ref[...]
    a = _silu(_bdot(h, wg)) * _bdot(h, wu)
    col = lax.broadcasted_iota(jnp.int32, a.shape, 1)
    a = jnp.where(col < valid, a, 0.0)
    row = lax.broadcasted_iota(jnp.int32, wd.shape, 0)
    wd = jnp.where(row < valid, wd, jnp.zeros_like(wd))
    a = a.astype(BF16)
    for r in range(0, nt, step):
        o_ref[r:r + step] += _bdot(a[r * bs:(r + step) * bs], wd).reshape(step, bs, o_ref.shape[2])

    @pl.when(j == nj - 1)
    def _():
        for r in range(0, nt, step):
            y = x_ref[r:r + step] + (0.5 * gt_ref[...]) * o_ref[r:r + step]
            if final_norm:
                ms = jnp.mean(y * y, axis=-1, keepdims=True)
                y = y * lax.rsqrt(ms + EPS) * fg_ref[...]
            o_ref[r:r + step] = y


def _ffn(x2d, rows, mod, layer, sub, P, w_bf16=None, final_norm=False):
    half = sub // 2
    emit = w_bf16 is None
    tf = FFN_TF_F32 if emit else FFN_TF_BF16
    g_norm = P['norm_g'].reshape(DEPTH, 3, 1, D_MODEL)
    fg = P['final_norm_g'].reshape(1, D_MODEL)
    if emit:
        weights = (P['ffn_w_gate'], P['ffn_w_up'], P['ffn_w_down'])
        w_specs = [pl.BlockSpec((None, None, D_MODEL, tf), lambda i, j: (layer, half, 0, j)),
                   pl.BlockSpec((None, None, D_MODEL, tf), lambda i, j: (layer, half, 0, j)),
                   pl.BlockSpec((None, None, tf, D_MODEL), lambda i, j: (layer, half, j, 0))]
    else:
        weights = w_bf16
        w_specs = [pl.BlockSpec((D_MODEL, tf), lambda i, j: (0, j)),
                   pl.BlockSpec((D_MODEL, tf), lambda i, j: (0, j)),
                   pl.BlockSpec((tf, D_MODEL), lambda i, j: (j, 0))]
    out_specs = [rows.x_spec(D_MODEL, single_buffer=emit)]
    out_shape = [jax.ShapeDtypeStruct((rows.m // rows.bs, rows.bs, D_MODEL), F32)]
    if emit:
        out_specs += [pl.BlockSpec((D_MODEL, tf), lambda i, j: (0, j)),
                      pl.BlockSpec((D_MODEL, tf), lambda i, j: (0, j)),
                      pl.BlockSpec((tf, D_MODEL), lambda i, j: (j, 0))]
        out_shape += [jax.ShapeDtypeStruct((D_MODEL, D_FF), BF16), jax.ShapeDtypeStruct((D_MODEL, D_FF), BF16),
                      jax.ShapeDtypeStruct((D_FF, D_MODEL), BF16)]
        assert rows.n_tiles == 1
    outs = pl.pallas_call(
        functools.partial(_ffn_kernel, final_norm=final_norm, emit=emit),
        grid=(rows.n_tiles, pl.cdiv(D_FF, tf)),
        in_specs=[
            rows.x_spec(D_MODEL, single_buffer=True),
            rows.mod_spec(layer, 3 * sub), rows.mod_spec(layer, 3 * sub + 1), rows.mod_spec(layer, 3 * sub + 2),
            pl.BlockSpec((None, None, 1, D_MODEL), lambda i, j: (layer, sub, 0, 0)),
            *w_specs,
            pl.BlockSpec((1, D_MODEL), lambda i, j: (0, 0)),
        ],
        out_specs=out_specs,
        out_shape=out_shape,
        scratch_shapes=[pltpu.VMEM((rows.tm, D_MODEL), BF16)],
        compiler_params=_params("arbitrary", "arbitrary"),
        name="ffn_half_step",
    )(rows.view(x2d), mod, mod, mod, g_norm, *weights, fg)
    return outs[0].reshape(rows.m, D_MODEL), tuple(outs[1:])


def _proj_kernel(x_ref, sh_ref, sc_ref, g_ref, w_ref, o_ref, h_ref):
    @pl.when(pl.program_id(1) == 0)
    def _():
        h_ref[...] = _norm_mod(x_ref[...], g_ref[...], sh_ref[...], sc_ref[...])

    o_ref[...] = _bdot(h_ref[...], w_ref[...])


def _proj(x2d, rows, mod, layer, g_norm4, w3, n_out, tn):
    return pl.pallas_call(
        _proj_kernel,
        grid=(rows.n_tiles, pl.cdiv(n_out, tn)),
        in_specs=[
            rows.x_spec(D_MODEL),
            rows.mod_spec(layer, 3), rows.mod_spec(layer, 4),
            pl.BlockSpec((None, None, 1, D_MODEL), lambda i, j: (layer, 1, 0, 0)),
            pl.BlockSpec((None, D_MODEL, tn), lambda i, j: (0, 0, j)),
        ],
        out_specs=pl.BlockSpec((rows.tm, tn), lambda i, j: (i, j)),
        out_shape=jax.ShapeDtypeStruct((rows.m, n_out), F32),
        scratch_shapes=[pltpu.VMEM((rows.tm, D_MODEL), BF16)],
        compiler_params=_params("arbitrary", "arbitrary"),
        name="prenorm_proj",
    )(rows.view(x2d), mod, mod, g_norm4, w3)


def _glu_proj_kernel(x_ref, sh_ref, sc_ref, g_ref, wa_ref, wg_ref, ba_ref, bg_ref, o_ref, h_ref):
    @pl.when(pl.program_id(1) == 0)
    def _():
        h_ref[...] = _norm_mod(x_ref[...], g_ref[...], sh_ref[...], sc_ref[...])

    h = h_ref[...]
    a = _bdot(h, wa_ref[...]) + ba_ref[...]
    gt = _bdot(h, wg_ref[...]) + bg_ref[...]
    o_ref[...] = a * jax.nn.sigmoid(gt)


def _glu_proj(x2d, rows, mod, layer, g_norm4, w3, b3, tn):
    nb = CONV_DIM // tn
    return pl.pallas_call(
        _glu_proj_kernel,
        grid=(rows.n_tiles, nb),
        in_specs=[
            rows.x_spec(D_MODEL),
            rows.mod_spec(layer, 3), rows.mod_spec(layer, 4),
            pl.BlockSpec((None, None, 1, D_MODEL), lambda i, j: (layer, 1, 0, 0)),
            pl.BlockSpec((None, D_MODEL, tn), lambda i, j: (0, 0, j)),
            pl.BlockSpec((None, D_MODEL, tn), lambda i, j: (0, 0, j + nb)),
            pl.BlockSpec((None, 1, tn), lambda i, j: (0, 0, j)),
            pl.BlockSpec((None, 1, tn), lambda i, j: (0, 0, j + nb)),
        ],
        out_specs=pl.BlockSpec((rows.tm, tn), lambda i, j: (i, j)),
        out_shape=jax.ShapeDtypeStruct((rows.m, CONV_DIM), F32),
        scratch_shapes=[pltpu.VMEM((rows.tm, D_MODEL), BF16)],
        compiler_params=_params("arbitrary", "arbitrary"),
        name="prenorm_glu_proj",
    )(rows.view(x2d), mod, mod, g_norm4, w3, w3, b3, b3)


def _out_kernel(*refs, n_in, has_bias):
    a_refs = refs[:n_in]
    w_refs = refs[n_in:2 * n_in]
    pos = 2 * n_in
    b_ref = refs[pos] if has_bias else None
    pos += int(has_bias)
    x_ref, gt_ref, o_ref = refs[pos:pos + 3]
    y = _bdot(a_refs[0][...], w_refs[0][...])
    for a_ref, w_ref in zip(a_refs[1:], w_refs[1:]):
        y += _bdot(a_ref[...], w_ref[...])
    if has_bias:
        y += b_ref[...]
    o_ref[...] = x_ref[...] + gt_ref[...] * y.reshape(o_ref.shape)


def _out_proj(acts, w3, k_offsets, bias3, x2d, rows, mod, layer, tn):
    n_in = len(acts)
    in_specs = [pl.BlockSpec((rows.tm, a.shape[1]), lambda i, j: (i, 0)) for a in acts]
    for a, off in zip(acts, k_offsets):
        kb = off // a.shape[1]
        in_specs.append(pl.BlockSpec((None, a.shape[1], tn), lambda i, j, kb=kb: (0, kb, j)))
    args = list(acts) + [w3] * n_in
    if bias3 is not None:
        in_specs.append(pl.BlockSpec((None, 1, tn), lambda i, j: (0, 0, j)))
        args.append(bias3)
    in_specs += [rows.x_spec(tn, col=lambda j: j), rows.mod_spec(layer, 5, width=tn, col=lambda j: j)]
    args += [rows.view(x2d), mod]
    out = pl.pallas_call(
        functools.partial(_out_kernel, n_in=n_in, has_bias=bias3 is not None),
        grid=(rows.n_tiles, D_MODEL // tn),
        in_specs=in_specs,
        out_specs=rows.x_spec(tn, col=lambda j: j),
        out_shape=jax.ShapeDtypeStruct((rows.m // rows.bs, rows.bs, D_MODEL), F32),
        compiler_params=_params("arbitrary", "arbitrary"),
        name="out_proj_residual",
    )(*args)
    return out.reshape(rows.m, D_MODEL)


HGRN_ROWS = 512
HGRN_HEADS = 4
HGRN_TRI = 128


def _hgrn_prepare(f_ref, lb_ref, cum_s, kk_s, *, layer, c):
    p = lb_ref[...]
    e = jnp.exp(p - jnp.max(p, axis=0, keepdims=True))
    sm = e / jnp.sum(e, axis=0, keepdims=True)
    lb = jnp.sum(sm[:layer + 1], axis=0, keepdims=True)
    f = lb + (1.0 - lb) * jax.nn.sigmoid(f_ref[...])
    lf = jnp.log(f)
    n = HGRN_TRI
    r = lax.broadcasted_iota(jnp.int32, (n, n), 0)
    s = lax.broadcasted_iota(jnp.int32, (n, n), 1)
    tri = jnp.where((s <= r) & (s // c == r // c), 1.0, 0.0).astype(F32)
    for b in range(f.shape[0] // n):
        cum_s[b * n:(b + 1) * n, :] = jnp.dot(tri, lf[b * n:(b + 1) * n], preferred_element_type=F32,
                                              precision=lax.Precision.HIGHEST)
    kk_s[...] = 1.0 - f


def _hgrn_subchunk(r0, c, st, hh, q_ref, v_ref, g_ref, gn_ref, cum_s, kk_s):
    rows = pl.ds(r0, c)
    cols = slice(hh * A_DK, (hh + 1) * A_DK)
    cum = cum_s[rows, cols]
    q = q_ref[rows, cols]
    kk = kk_s[rows, cols]
    vv = v_ref[rows, cols]
    last = cum[c - 1:c, :]
    o = lax.dot_general((q * jnp.exp(cum)).astype(BF16), st.astype(BF16), (((1,), (1,)), ((), ())),
                        preferred_element_type=F32)
    srow = lax.broadcasted_iota(jnp.int32, (c, A_DK), 0)
    xs = []
    for t in range(c):
        d = jnp.where(srow <= t, cum[t:t + 1, :] - cum, NEG)
        xs.append(jnp.exp(d) * (q[t:t + 1, :] * kk))
    x = jnp.concatenate(xs, axis=0).astype(BF16)
    w = jnp.dot(x, jnp.ones((A_DK, A_DV), BF16), preferred_element_type=F32)
    o = o + jnp.sum(w.reshape(c, c, A_DV) * vv[None], axis=1)
    ke = kk * jnp.exp(last - cum)
    st_new = st * jnp.exp(last) + lax.dot_general(vv.astype(BF16), ke.astype(BF16), (((0,), (0,)), ((), ())),
                                                  preferred_element_type=F32)
    ms = jnp.mean(o * o, axis=-1, keepdims=True)
    y = o * lax.rsqrt(ms + EPS) * gn_ref[:, cols] * _silu(g_ref[rows, cols])
    return y, st_new


def _hgrn_prompt_kernel(q_ref, f_ref, v_ref, g_ref, lb_ref, gn_ref, o_ref, s_ref, st_ref, cum_s, kk_s, *, layer, c):
    i = pl.program_id(1)

    @pl.when(i == 0)
    def _():
        st_ref[...] = jnp.zeros_like(st_ref)

    _hgrn_prepare(f_ref, lb_ref, cum_s, kk_s, layer=layer, c=c)

    def body(n, carry):
        r0 = pl.multiple_of(n * c, c)
        for hh in range(HGRN_HEADS):
            y, st_new = _hgrn_subchunk(r0, c, st_ref[hh], hh, q_ref, v_ref, g_ref, gn_ref, cum_s, kk_s)
            st_ref[hh] = st_new
            o_ref[pl.ds(r0, c), hh * A_DV:(hh + 1) * A_DV] = y
        return carry

    lax.fori_loop(0, q_ref.shape[0] // c, body, 0)

    @pl.when(i == pl.num_programs(1) - 1)
    def _():
        for hh in range(HGRN_HEADS):
            s_ref[hh] = st_ref[hh].T


def _hgrn_sample_kernel(q_ref, f_ref, v_ref, g_ref, lb_ref, gn_ref, s0_ref, o_ref, s_ref, cum_s, kk_s, *, layer, c):
    _hgrn_prepare(f_ref, lb_ref, cum_s, kk_s, layer=layer, c=c)

    def body(n, carry):
        r0 = pl.multiple_of(n * c, c)
        for hh in range(HGRN_HEADS):
            y, st_new = _hgrn_subchunk(r0, c, s0_ref[n, hh].T, hh, q_ref, v_ref, g_ref, gn_ref, cum_s, kk_s)
            s_ref[n, hh] = st_new.T
            o_ref[pl.ds(r0, c), hh * A_DV:(hh + 1) * A_DV] = y
        return carry

    lax.fori_loop(0, q_ref.shape[0] // c, body, 0)


def _hgrn(u, lower_bound, norm_g2, layer, s0=None):
    m = u.shape[0]
    hb = HGRN_HEADS
    w = hb * A_DK
    nb = A_QK // w
    tc = HGRN_ROWS if s0 is None else HGRN_ROWS // 2

    def col(seg):
        return pl.BlockSpec((tc, w), lambda h, i, seg=seg: (i, seg * nb + h))

    in_specs = [col(0), col(1), col(2), col(3),
                pl.BlockSpec((DEPTH + 1, w), lambda h, i: (0, h)),
                pl.BlockSpec((1, w), lambda h, i: (0, h))]
    o_spec = pl.BlockSpec((tc, w), lambda h, i: (i, h))
    scratch = [pltpu.VMEM((tc, w), F32), pltpu.VMEM((tc, w), F32)]
    if s0 is None:
        return pl.pallas_call(
            functools.partial(_hgrn_prompt_kernel, layer=layer, c=16),
            grid=(nb, m // tc),
            in_specs=in_specs,
            out_specs=[o_spec, pl.BlockSpec((hb, A_DK, A_DV), lambda h, i: (h, 0, 0))],
            out_shape=[jax.ShapeDtypeStruct((m, A_WIDTH), F32), jax.ShapeDtypeStruct((A_HEADS, A_DK, A_DV), F32)],
            scratch_shapes=[pltpu.VMEM((hb, A_DV, A_DK), F32)] + scratch,
            compiler_params=_params("arbitrary", "arbitrary"),
            name="hgrn2_prompt",
        )(u, u, u, u, lower_bound, norm_g2)
    c = DEC_SEQ
    ns = tc // c
    s_spec = pl.BlockSpec((ns, hb, A_DK, A_DV), lambda h, i: (i, h, 0, 0))
    return pl.pallas_call(
        functools.partial(_hgrn_sample_kernel, layer=layer, c=c),
        grid=(nb, m // tc),
        in_specs=in_specs + [s_spec],
        out_specs=[o_spec, s_spec],
        out_shape=[jax.ShapeDtypeStruct((m, A_WIDTH), F32), jax.ShapeDtypeStruct(s0.shape, F32)],
        scratch_shapes=scratch,
        compiler_params=_params("arbitrary", "arbitrary"),
        name="hgrn2_sample",
    )(u, u, u, u, lower_bound, norm_g2, s0)


CONV_HALO = 32
CONV_ROWS = 256


def _ln_silu(y, g, b):
    mu = jnp.mean(y, axis=-1, keepdims=True)
    yc = y - mu
    var = jnp.mean(yc * yc, axis=-1, keepdims=True)
    return _silu(yc * lax.rsqrt(var + EPS) * g + b)


def _conv_prompt_kernel(cur_ref, halo_ref, w_ref, b_ref, g_ref, lb_ref, o_ref, ext_ref):
    i = pl.program_id(0)
    tt = cur_ref.shape[0]
    ext_ref[0:CONV_HALO, :] = jnp.where(i > 0, halo_ref[...], 0.0)
    ext_ref[CONV_HALO:, :] = cur_ref[...]
    off = CONV_HALO - (CONV_WIDTH - 1)
    y = jnp.zeros((tt, CONV_DIM), F32) + b_ref[...]
    for w in range(CONV_WIDTH):
        y = y + ext_ref[off + w:off + w + tt, :] * w_ref[w:w + 1, :]
    o_ref[...] = _ln_silu(y, g_ref[...], lb_ref[...]).astype(o_ref.dtype)


def _conv_prompt(glu, w_dw, b_dw, ln_g, ln_b):
    t = glu.shape[0]
    tt = CONV_ROWS
    r = tt // CONV_HALO
    vec = pl.BlockSpec((1, CONV_DIM), lambda i: (0, 0))
    return pl.pallas_call(
        _conv_prompt_kernel,
        grid=(t // tt,),
        in_specs=[
            pl.BlockSpec((tt, CONV_DIM), lambda i: (i, 0)),
            pl.BlockSpec((CONV_HALO, CONV_DIM), lambda i: (jnp.maximum(i * r - 1, 0), 0)),
            pl.BlockSpec((CONV_WIDTH, CONV_DIM), lambda i: (0, 0)),
            vec, vec, vec,
        ],
        out_specs=pl.BlockSpec((tt, CONV_DIM), lambda i: (i, 0)),
        out_shape=jax.ShapeDtypeStruct((t, CONV_DIM), BF16),
        scratch_shapes=[pltpu.VMEM((CONV_HALO + tt, CONV_DIM), F32)],
        compiler_params=_params("arbitrary"),
        name="conv_prompt",
    )(glu, glu, w_dw, b_dw, ln_g, ln_b)


def _conv_sample_kernel(u_ref, buf_ref, w_ref, b_ref, g_ref, lb_ref, o_ref, nb_ref):
    hist = CONV_WIDTH - 1
    nt = u_ref.shape[0]

    def ext(j):
        return buf_ref[j] if j < hist else u_ref[j - hist]

    for t in range(nt):
        y = ext(t) * w_ref[0:1, :] + b_ref[...]
        for w in range(1, CONV_WIDTH):
            y = y + ext(t + w) * w_ref[w:w + 1, :]
        o_ref[t] = _ln_silu(y, g_ref[...], lb_ref[...]).astype(o_ref.dtype)
    for j in range(hist):
        nb_ref[j] = ext(j + nt)


def _conv_sample(glu_t, buf_t, w_dw, b_dw, ln_g, ln_b):
    nt, ns, _ = glu_t.shape
    hist = CONV_WIDTH - 1
    bs = 32
    vec = pl.BlockSpec((1, CONV_DIM), lambda i: (0, 0))
    return pl.pallas_call(
        _conv_sample_kernel,
        grid=(ns // bs,),
        in_specs=[
            pl.BlockSpec((nt, bs, CONV_DIM), lambda i: (0, i, 0)),
            pl.BlockSpec((hist, bs, CONV_DIM), lambda i: (0, i, 0)),
            pl.BlockSpec((CONV_WIDTH, CONV_DIM), lambda i: (0, 0)),
            vec, vec, vec,
        ],
        out_specs=[pl.BlockSpec((nt, bs, CONV_DIM), lambda i: (0, i, 0)),
                   pl.BlockSpec((hist, bs, CONV_DIM), lambda i: (0, i, 0))],
        out_shape=[jax.ShapeDtypeStruct((nt, ns, CONV_DIM), BF16), jax.ShapeDtypeStruct((hist, ns, CONV_DIM), F32)],
        compiler_params=_params("arbitrary"),
        name="conv_sample",
    )(glu_t, buf_t, w_dw, b_dw, ln_g, ln_b)


PAIR = 2 * B_DH
N_PAIRS = KV_WIDTH // PAIR
BLOCKS_PER_TOKEN = SEL_BLOCK // CMP_STRIDE
N_SUB = CMP_BLOCK // CMP_STRIDE
CMP_OFF = 120
CMP_ROWS = 640
KV_PAD = WINDOW
FAR_TILE = 512
LOWEST = -3.0e38
MASK_C = 2.0 ** 100


def _half(shape, half):
    lane = lax.broadcasted_iota(jnp.int32, shape, len(shape) - 1)
    return (lane % PAIR) // B_DH == half


def _group_q(q_ref, g):
    parts = []
    for r in range(B_GROUP):
        h = g * B_GROUP + r
        x = q_ref[:, (h // 2) * PAIR:(h // 2 + 1) * PAIR]
        x = jnp.where(_half(x.shape, h % 2), x, 0.0) * (B_DH ** -0.5)
        if h % 2 != g % 2:
            x = pltpu.roll(x, B_DH, axis=1)
        parts.append(x)
    return jnp.concatenate(parts, axis=0).astype(BF16)


def _gate_rows(gate_ref, branch, g):
    sig = jax.nn.sigmoid(gate_ref[...])
    c0 = branch * B_HEADS + g * B_GROUP
    return jnp.concatenate([sig[:, c0 + r:c0 + r + 1] for r in range(B_GROUP)], axis=0)


def _scatter_heads(o, g):
    nq = o.shape[0] // B_GROUP
    outs = []
    for jp in range(2):
        acc = None
        for e in range(2):
            r = 2 * jp + e
            x = jnp.where(_half((nq, PAIR), g % 2), o[r * nq:(r + 1) * nq], 0.0)
            if e != g % 2:
                x = pltpu.roll(x, B_DH, axis=1)
            acc = x if acc is None else acc + x
        outs.append(acc)
    return jnp.concatenate(outs, axis=1)


def _nt_dot(a, b):
    return lax.dot_general(a, b, (((1,), (1,)), ((), ())), preferred_element_type=F32)


def _piece(qg, k, v, mask, bias=None):
    s = _nt_dot(qg, k)
    if bias is not None:
        s = s + bias
    s = jnp.where(mask, s, NEG)
    m = jnp.max(s, axis=-1, keepdims=True)
    p = jnp.where(mask, jnp.exp(s - m), 0.0)
    return m, jnp.sum(p, axis=-1, keepdims=True), jnp.dot(p.astype(BF16), v, preferred_element_type=F32), p


def _merge(a, b):
    m = jnp.maximum(a[0], b[0])
    ea = jnp.exp(a[0] - m)
    eb = jnp.exp(b[0] - m)
    return m, a[1] * ea + b[1] * eb, a[2] * ea + b[2] * eb


def _inv_or_zero(l):
    return jnp.where(l > 0.0, 1.0 / l, 0.0)


def _dot01(p, o01):
    hi = p.astype(BF16)
    r1 = p - hi.astype(F32)
    mid = r1.astype(BF16)
    lo = (r1 - mid.astype(F32)).astype(BF16)
    d = lambda a: jnp.dot(a, o01, preferred_element_type=F32)
    return d(hi) + d(mid) + d(lo)


def _overlap01(tok, blk):
    return jnp.where((tok >= BLOCKS_PER_TOKEN * blk - (N_SUB - 1)) & (tok <= BLOCKS_PER_TOKEN * blk + BLOCKS_PER_TOKEN - 1),
                     1.0, 0.0).astype(BF16)


def _select_blocks(imp, qpos):
    j = lax.broadcasted_iota(jnp.int32, imp.shape, 1)
    cur = qpos // SEL_BLOCK
    forced = (j == 0) | (j == cur) | (j == cur - 1)
    valid = j * SEL_BLOCK <= qpos
    imp = jnp.where(forced, FORCE, jnp.where(valid, imp, -FORCE))
    sel = jnp.zeros(imp.shape, F32)
    jf = j.astype(F32)
    for _ in range(N_SELECT):
        m = jnp.max(imp, axis=-1, keepdims=True)
        first = jnp.min(jnp.where(imp == m, jf, float(imp.shape[1])), axis=-1, keepdims=True)
        pick = jf == first
        sel = jnp.where(pick, 1.0, sel)
        imp = jnp.where(pick, LOWEST, imp)
    return sel


def _expand_blocks(sel_rows, first_blk, n_keys):
    jj = lax.broadcasted_iota(jnp.int32, (sel_rows.shape[1], n_keys), 0)
    kk = lax.broadcasted_iota(jnp.int32, (sel_rows.shape[1], n_keys), 1)
    e = jnp.where(jj == first_blk + kk // SEL_BLOCK, 1.0, 0.0).astype(BF16)
    return jnp.dot(sel_rows, e, preferred_element_type=F32)


def _bias_tables(rel_bias):
    d = np.arange(MAX_DISTANCE + 1)
    exact = N_BUCKETS // 2
    large = exact + (np.log(np.maximum(d, 1).astype(np.float32) / exact) / math.log(MAX_DISTANCE / exact)
                     * (N_BUCKETS - exact)).astype(np.int32)
    bucket = np.where(d < exact, d, np.minimum(large, N_BUCKETS - 1))
    by_dist = rel_bias.astype(F32)[bucket]

    def tile(dist, nq):
        t = by_dist[np.clip(dist, 0, MAX_DISTANCE)]
        t = t.reshape(nq, dist.shape[1], B_KV_HEADS, B_GROUP).transpose(2, 3, 0, 1)
        return t.reshape(B_KV_HEADS, B_GROUP * nq, dist.shape[1])

    nq = PAIR
    i = np.arange(nq)[:, None]
    far = by_dist[MAX_DISTANCE]
    shift = jnp.repeat(far.reshape(B_KV_HEADS, B_GROUP), nq, axis=1)[:, :, None]
    prompt_cmp = tile(i - CMP_STRIDE * (np.arange(PAIR)[None, :] - CMP_OFF) - (CMP_BLOCK - 1), nq) - shift
    dist_kv = i + nq - np.arange(2 * nq)[None, :]
    causal = np.tile(dist_kv >= 0, (B_GROUP, 1))[None]
    prompt_kv = jnp.where(causal, tile(dist_kv, nq) - shift, -MASK_C)
    t = np.arange(DEC_SEQ)[:, None]
    qpos = PAST_LEN + t
    s_cmp = tile(qpos - (CMP_STRIDE * np.arange(PAIR)[None, :] + CMP_BLOCK - 1), DEC_SEQ)
    s_sel = tile(qpos - np.arange(PAST_LEN + PAIR)[None, :], DEC_SEQ)
    s_win = tile(qpos - (PAST_LEN - WINDOW + np.arange(WINDOW + PAIR)[None, :]), DEC_SEQ)
    flat = lambda a: a.reshape(B_KV_HEADS * B_GROUP * DEC_SEQ, a.shape[-1])
    return prompt_cmp, prompt_kv, flat(s_cmp), flat(s_sel), flat(s_win)


def _cmp_weights(pe, w1, b1, w2, b2):
    eye = jnp.eye(2, dtype=F32)
    w1r = w1.reshape(2, N_SUB, CMP_STRIDE, B_DH, CMP_HIDDEN)
    w1p = jnp.einsum('kmpdh,ef->kmpedfh', w1r, eye).reshape(2, N_SUB, CMP_STRIDE * PAIR, 2 * CMP_HIDDEN).astype(BF16)
    pep = jnp.tile(pe.reshape(2, N_SUB, CMP_STRIDE, 1, B_DH), (1, 1, 1, 2, 1)).reshape(2, N_SUB, 1, CMP_STRIDE * PAIR)
    b1p = jnp.tile(b1, (1, 2)).reshape(2, 1, 2 * CMP_HIDDEN)
    w2p = jnp.einsum('khd,ef->kehfd', w2, eye).reshape(2, 2 * CMP_HIDDEN, PAIR).astype(BF16)
    b2p = jnp.tile(b2, (1, 2)).reshape(2, 1, PAIR)
    return pep, w1p, b1p, w2p, b2p


def _compress_pair(src, pe_ref, w1_ref, b1_ref, w2_ref, b2_ref):
    x = jnp.concatenate([src(p) for p in range(CMP_STRIDE)], axis=1)
    m_rows = x.shape[0]
    h = b1_ref[...]
    for m in range(N_SUB):
        part = jnp.dot((x + pe_ref[m]).astype(BF16), w1_ref[m], preferred_element_type=F32)
        h = h + (part if m == 0 else pltpu.roll(part, m_rows - m, axis=0))
    return jnp.dot(_silu(h).astype(BF16), w2_ref[...], preferred_element_type=F32) + b2_ref[...]


def _cmp_weight_specs(kv_of):
    return [
        pl.BlockSpec((None, N_SUB, 1, CMP_STRIDE * PAIR), lambda *a: (kv_of(*a), 0, 0, 0)),
        pl.BlockSpec((None, N_SUB, CMP_STRIDE * PAIR, 2 * CMP_HIDDEN), lambda *a: (kv_of(*a), 0, 0, 0)),
        pl.BlockSpec((None, 1, 2 * CMP_HIDDEN), lambda *a: (kv_of(*a), 0, 0)),
        pl.BlockSpec((None, 2 * CMP_HIDDEN, PAIR), lambda *a: (kv_of(*a), 0, 0)),
        pl.BlockSpec((None, 1, PAIR), lambda *a: (kv_of(*a), 0, 0)),
    ]


def _compress_prompt_kernel(rows_ref, pe_ref, w1_ref, b1_ref, w2_ref, b2_ref, o_ref):
    n_blk = rows_ref.shape[0] // CMP_STRIDE
    tok = _compress_pair(lambda p: rows_ref[pl.ds(p, n_blk, stride=CMP_STRIDE), :], pe_ref, w1_ref, b1_ref, w2_ref, b2_ref)
    o_ref[...] = jnp.zeros_like(o_ref)
    o_ref[CMP_OFF:CMP_OFF + n_blk, :] = tok


def _compress_prompt(u, weights):
    t = u.shape[0]
    col0 = (2 * A_QK + 2 * A_WIDTH + B_WIDTH) // PAIR
    return pl.pallas_call(
        _compress_prompt_kernel,
        grid=(2 * N_PAIRS,),
        in_specs=[pl.BlockSpec((t, PAIR), lambda c: (0, col0 + c))] + _cmp_weight_specs(lambda c: c // N_PAIRS),
        out_specs=pl.BlockSpec((None, CMP_ROWS, PAIR), lambda c: (c, 0, 0)),
        out_shape=jax.ShapeDtypeStruct((2 * N_PAIRS, CMP_ROWS, PAIR), F32),
        compiler_params=_params("arbitrary"),
        name="nsa_compress_prompt",
    )(u, *weights)


CMP_SEQ_BLOCK = 4
N_PAGES = PAST_LEN // PAGE_SIZE


def _compress_sample_kernel(pt_ref, *refs):
    n_src = CMP_SEQ_BLOCK * N_PAGES
    pages = refs[:n_src]
    pe_ref, w1_ref, b1_ref, w2_ref, b2_ref, o_ref = refs[n_src:]
    n_blk = PAGE_SIZE // CMP_STRIDE
    tok = _compress_pair(
        lambda p: jnp.concatenate([pg[pl.ds(p, n_blk, stride=CMP_STRIDE), :] for pg in pages], axis=0),
        pe_ref, w1_ref, b1_ref, w2_ref, b2_ref)
    o_ref[...] = tok.reshape(o_ref.shape)


def _compress_sample(cache, page_table, weights):
    n_seq = page_table.shape[0]
    sb = CMP_SEQ_BLOCK
    n_tok = PAST_LEN // CMP_STRIDE
    page_specs = [pl.BlockSpec((None, PAGE_SIZE, PAIR), lambda c, i, pt, s=s, j=j: (pt[i * sb + s, j], 0, c))
                  for s in range(sb) for j in range(N_PAGES)]
    return pl.pallas_call(
        _compress_sample_kernel,
        grid_spec=pltpu.PrefetchScalarGridSpec(
            num_scalar_prefetch=1,
            grid=(2 * N_PAIRS, n_seq // sb),
            in_specs=page_specs + _cmp_weight_specs(lambda c, i, pt: c // N_PAIRS),
            out_specs=pl.BlockSpec((sb, None, n_tok, PAIR), lambda c, i, pt: (i, c, 0, 0)),
        ),
        out_shape=jax.ShapeDtypeStruct((n_seq, 2 * N_PAIRS, n_tok, PAIR), F32),
        compiler_params=_params("arbitrary", "arbitrary"),
        name="nsa_compress_sample",
    )(page_table, *([cache] * (sb * N_PAGES)), *weights)


Q_ROWS = 128


def _nsa_cmp_prompt_kernel(q_ref, gate_ref, cmp_ref, bias_ref, oc_ref, sel_ref):
    qb = pl.program_id(0)
    nq = q_ref.shape[0]
    rows = B_GROUP * nq
    n_tok = CMP_ROWS - PAIR
    near0 = pl.multiple_of(qb * (nq // CMP_STRIDE), SUBLANES)
    tok0 = near0 - CMP_OFF
    mask_far = lax.broadcasted_iota(jnp.int32, (rows, n_tok), 1) < tok0
    i = lax.broadcasted_iota(jnp.int32, (rows, PAIR), 0) % nq
    mn = lax.broadcasted_iota(jnp.int32, (rows, PAIR), 1)
    dist = i - CMP_STRIDE * (mn - CMP_OFF) - (CMP_BLOCK - 1)
    mask_near = (dist >= 0) & (tok0 + mn >= 0)
    o_far = _overlap01(lax.broadcasted_iota(jnp.int32, (n_tok, PAIR), 0), lax.broadcasted_iota(jnp.int32, (n_tok, PAIR), 1))
    o_near = _overlap01(tok0 + lax.broadcasted_iota(jnp.int32, (PAIR, PAIR), 0), lax.broadcasted_iota(jnp.int32, (PAIR, PAIR), 1))
    qpos = qb * nq + lax.broadcasted_iota(jnp.int32, (nq, PAIR), 0)
    for g in range(B_KV_HEADS):
        qg = _group_q(q_ref, g)
        kp, vp = g // 2, N_PAIRS + g // 2
        far = _piece(qg, cmp_ref[kp, CMP_OFF:CMP_OFF + n_tok, :].astype(BF16),
                     cmp_ref[vp, CMP_OFF:CMP_OFF + n_tok, :].astype(BF16), mask_far)
        near = _piece(qg, cmp_ref[kp, pl.ds(near0, PAIR), :].astype(BF16),
                      cmp_ref[vp, pl.ds(near0, PAIR), :].astype(BF16), mask_near, bias_ref[g])
        m, l, acc = _merge(far[:3], near[:3])
        linv = _inv_or_zero(l)
        oc_ref[:, g * 2 * PAIR:(g + 1) * 2 * PAIR] = _scatter_heads(acc * linv * _gate_rows(gate_ref, 0, g), g)
        pf = jnp.sum((far[3] * (jnp.exp(far[0] - m) * linv)).reshape(B_GROUP, nq, n_tok), axis=0)
        pn = jnp.sum((near[3] * (jnp.exp(near[0] - m) * linv)).reshape(B_GROUP, nq, PAIR), axis=0)
        imp = _dot01(pf, o_far) + _dot01(pn, o_near)
        sel_ref[:, g * PAIR:(g + 1) * PAIR] = _select_blocks(imp, qpos)


def _flash_step(st, qx, kx, v, bias=None):
    m_old, l, acc = st
    s = _nt_dot(qx, kx)
    if bias is not None:
        s = s + bias
    m = jnp.maximum(m_old, jnp.max(s, axis=-1, keepdims=True))
    p = jnp.exp(s - m)
    alpha = jnp.exp(m_old - m)
    return m, alpha * l + jnp.sum(p, axis=-1, keepdims=True), alpha * acc + jnp.dot(p.astype(BF16), v, preferred_element_type=F32)


def _nsa_selwin_prompt_kernel(q_ref, gate_ref, sel_ref, oc_ref, kv_ref, oh_ref, ohr_ref, bias_ref, wmask_ref, o_ref):
    qb = pl.program_id(0)
    nq = q_ref.shape[0]
    rows = B_GROUP * nq
    qs = qb * nq
    per_q = nq // SEL_BLOCK
    n_wf = WINDOW - nq
    near_rows = pl.ds(pl.multiple_of(qs + KV_PAD - nq, nq), 2 * nq)
    winfar_rows = pl.ds(pl.multiple_of(qs, nq), n_wf)
    lane = lax.broadcasted_iota(jnp.int32, (rows, PAIR), 1)
    first_near = per_q * (qb - 1)
    first_win = per_q * qb - WINDOW // SEL_BLOCK
    jj = lax.broadcasted_iota(jnp.int32, (PAIR, PAIR), 0)
    bb = lax.broadcasted_iota(jnp.int32, (PAIR, PAIR), 1)
    to_near = jnp.where(jj == first_near + bb, 1.0, 0.0).astype(BF16)
    exists_near = jnp.where(first_near + lane >= 0, 0.0, -MASK_C).astype(BF16)
    exists_win = jnp.where(first_win + lane >= 0, 0.0, -MASK_C).astype(BF16)
    init = (jnp.full((rows, 1), NEG, F32), jnp.zeros((rows, 1), F32), jnp.zeros((rows, PAIR), F32))
    ohr_near = ohr_ref[:2 * nq, :]
    ohr_win = ohr_ref[:n_wf, :]
    for g in range(B_KV_HEADS):
        qg = _group_q(q_ref, g)
        cols = lambda base: slice(base + (g // 2) * PAIR, base + (g // 2 + 1) * PAIR)
        selm = jnp.concatenate([sel_ref[:, g * PAIR:(g + 1) * PAIR]] * B_GROUP, axis=0)
        far_vec = jnp.where((selm > 0.5) & (lane < first_near), 0.0, -MASK_C).astype(BF16)
        near_sel = jnp.dot(selm.astype(BF16), to_near, preferred_element_type=F32)
        near_vec = jnp.where(near_sel > 0.5, 0.0, -MASK_C).astype(BF16) + exists_near
        qx_far = jnp.concatenate([qg, far_vec], axis=1)

        def far_body(t, st):
            r = pl.ds(pl.multiple_of(KV_PAD + t * FAR_TILE, FAR_TILE), FAR_TILE)
            kx = jnp.concatenate([kv_ref[r, cols(0)], oh_ref[r, :]], axis=1)
            return _flash_step(st, qx_far, kx, kv_ref[r, cols(KV_WIDTH)])

        st = lax.fori_loop(0, (qb + 2) // (FAR_TILE // nq), far_body, init)
        st = _flash_step(st, jnp.concatenate([qg, near_vec], axis=1),
                         jnp.concatenate([kv_ref[near_rows, cols(0)], ohr_near], axis=1),
                         kv_ref[near_rows, cols(KV_WIDTH)], bias_ref[g])
        o_s = st[2] * _inv_or_zero(st[1])
        sw = _flash_step(init, jnp.concatenate([qg, exists_win], axis=1),
                         jnp.concatenate([kv_ref[winfar_rows, cols(2 * KV_WIDTH)], ohr_win], axis=1),
                         kv_ref[winfar_rows, cols(3 * KV_WIDTH)], wmask_ref[...])
        sw = _flash_step(sw, jnp.concatenate([qg, exists_near], axis=1),
                         jnp.concatenate([kv_ref[near_rows, cols(2 * KV_WIDTH)], ohr_near], axis=1),
                         kv_ref[near_rows, cols(3 * KV_WIDTH)], bias_ref[g])
        o_w = sw[2] * _inv_or_zero(sw[1])
        o = o_s * _gate_rows(gate_ref, 1, g) + o_w * _gate_rows(gate_ref, 2, g)
        blk = slice(g * 2 * PAIR, (g + 1) * 2 * PAIR)
        o_ref[:, blk] = _scatter_heads(o, g) + oc_ref[:, blk]


def _block_onehots(t):
    pos = np.arange(KV_PAD + t) - KV_PAD
    absolute = (pos[:, None] // SEL_BLOCK == np.arange(PAIR)[None, :]) & (pos[:, None] >= 0)
    relative = np.arange(WINDOW)[:, None] // SEL_BLOCK == np.arange(PAIR)[None, :]
    i = np.arange(B_GROUP * Q_ROWS)[:, None] % Q_ROWS
    in_window = np.arange(WINDOW - Q_ROWS)[None, :] > i
    return (jnp.asarray(absolute, BF16), jnp.asarray(relative, BF16),
            jnp.asarray(np.where(in_window, 0.0, -MASK_C), F32))


def _nsa_prompt(u, kv_pad, cmp_tok, bias_cmp, bias_kv):
    t = u.shape[0]
    nq = Q_ROWS
    q_col = (2 * A_QK + 2 * A_WIDTH) // B_WIDTH
    g_col = (IN_DIM - 3 * B_HEADS) // PAIR
    q_spec = pl.BlockSpec((nq, B_WIDTH), lambda i: (i, q_col))
    gate_spec = pl.BlockSpec((nq, PAIR), lambda i: (i, g_col))
    oc, sel = pl.pallas_call(
        _nsa_cmp_prompt_kernel,
        grid=(t // nq,),
        in_specs=[q_spec, gate_spec,
                  pl.BlockSpec(cmp_tok.shape, lambda i: (0, 0, 0)),
                  pl.BlockSpec(bias_cmp.shape, lambda i: (0, 0, 0))],
        out_specs=[pl.BlockSpec((nq, B_WIDTH), lambda i: (i, 0)), pl.BlockSpec((nq, B_KV_HEADS * PAIR), lambda i: (i, 0))],
        out_shape=[jax.ShapeDtypeStruct((t, B_WIDTH), F32), jax.ShapeDtypeStruct((t, B_KV_HEADS * PAIR), F32)],
        compiler_params=_params("arbitrary"),
        name="nsa_cmp_select_prompt",
    )(u, u, cmp_tok, bias_cmp)
    oh_abs, oh_rel, win_mask = _block_onehots(t)
    whole = lambda a: pl.BlockSpec(a.shape, lambda i: (0,) * a.ndim)
    return pl.pallas_call(
        _nsa_selwin_prompt_kernel,
        grid=(t // nq,),
        in_specs=[q_spec, gate_spec,
                  pl.BlockSpec((nq, B_KV_HEADS * PAIR), lambda i: (i, 0)),
                  pl.BlockSpec((nq, B_WIDTH), lambda i: (i, 0)),
                  whole(kv_pad), whole(oh_abs), whole(oh_rel), whole(bias_kv), whole(win_mask)],
        out_specs=pl.BlockSpec((nq, B_WIDTH), lambda i: (i, 0)),
        out_shape=jax.ShapeDtypeStruct((t, B_WIDTH), F32),
        compiler_params=_params("arbitrary"),
        name="nsa_select_window_prompt",
    )(u, u, sel, oc, kv_pad, oh_abs, oh_rel, bias_kv, win_mask)


SEL_KEYS = PAST_LEN + PAIR
WIN_KEYS = WINDOW + PAIR


def _nsa_sample_kernel(pt_ref, *refs):
    pages = refs[:N_PAGES]
    (q_ref, gate_ref, cmp_ref, selnew_ref, winnew_ref, win_ref, bc_ref, bs_ref, bw_ref,
     o_ref, winout_ref, kc_ref, wc_ref) = refs[N_PAGES:]
    nt = q_ref.shape[0]
    grp = B_GROUP * nt
    rows = B_KV_HEADS * grp
    pad = jnp.zeros((PAIR - nt, 4 * PAIR), F32)
    for j, pg in enumerate(pages):
        kc_ref[j * PAGE_SIZE:(j + 1) * PAGE_SIZE, :] = pg[...].astype(BF16)
    kc_ref[PAST_LEN:, :] = jnp.concatenate([selnew_ref[...], pad], axis=0).astype(BF16)
    wc_ref[:WINDOW, :] = win_ref[...].astype(BF16)
    wc_ref[WINDOW:, :] = jnp.concatenate([winnew_ref[...], pad], axis=0).astype(BF16)
    winout_ref[:WINDOW - nt, :] = win_ref[nt:, :]
    winout_ref[WINDOW - nt:, :] = winnew_ref[...]

    zero = jnp.zeros((grp, PAIR), BF16)
    qq = jnp.concatenate(
        [jnp.concatenate([_group_q(q_ref, g), zero] if g // 2 == 0 else [zero, _group_q(q_ref, g)], axis=1)
         for g in range(B_KV_HEADS)], axis=0)
    take = lambda acc, g: acc[g * grp:(g + 1) * grp, (g // 2) * PAIR:(g // 2 + 1) * PAIR]
    qpos = PAST_LEN + lax.broadcasted_iota(jnp.int32, (rows, 1), 0) % nt

    n_tok = cmp_ref.shape[1]
    ck = jnp.concatenate([cmp_ref[0], cmp_ref[1]], axis=1).astype(BF16)
    cv = jnp.concatenate([cmp_ref[2], cmp_ref[3]], axis=1).astype(BF16)
    mask_c = lax.broadcasted_iota(jnp.int32, (rows, n_tok), 1) < n_tok - (N_SUB - 1)
    mc, lc, acc_c, pc = _piece(qq, ck, cv, mask_c, bc_ref[...])
    pc = pc * _inv_or_zero(lc)
    p_all = jnp.concatenate([jnp.sum(pc[g * grp:(g + 1) * grp].reshape(B_GROUP, nt, n_tok), axis=0)
                             for g in range(B_KV_HEADS)], axis=0)
    o01 = _overlap01(lax.broadcasted_iota(jnp.int32, (n_tok, PAIR), 0), lax.broadcasted_iota(jnp.int32, (n_tok, PAIR), 1))
    qpos_gt = PAST_LEN + lax.broadcasted_iota(jnp.int32, (B_KV_HEADS * nt, PAIR), 0) % nt
    sel = _select_blocks(_dot01(p_all, o01), qpos_gt).astype(BF16)
    sel_rows = jnp.concatenate([sel[g * nt:(g + 1) * nt] for g in range(B_KV_HEADS) for _ in range(B_GROUP)], axis=0)

    ks = lax.broadcasted_iota(jnp.int32, (rows, SEL_KEYS), 1)
    mask_s = (_expand_blocks(sel_rows, 0, SEL_KEYS) > 0.5) & (ks <= qpos)
    ms, ls, acc_s, _ = _piece(qq, kc_ref[:, :2 * PAIR], kc_ref[:, 2 * PAIR:], mask_s, bs_ref[...])
    kw = PAST_LEN - WINDOW + lax.broadcasted_iota(jnp.int32, (rows, WIN_KEYS), 1)
    mask_w = (kw <= qpos) & (qpos - kw < WINDOW)
    mw, lw, acc_w, _ = _piece(qq, wc_ref[:, :2 * PAIR], wc_ref[:, 2 * PAIR:], mask_w, bw_ref[...])
    acc_c, acc_s, acc_w = acc_c * _inv_or_zero(lc), acc_s * _inv_or_zero(ls), acc_w * _inv_or_zero(lw)
    for g in range(B_KV_HEADS):
        o = (take(acc_c, g) * _gate_rows(gate_ref, 0, g) + take(acc_s, g) * _gate_rows(gate_ref, 1, g)
             + take(acc_w, g) * _gate_rows(gate_ref, 2, g))
        o_ref[:, g * 2 * PAIR:(g + 1) * 2 * PAIR] = _scatter_heads(o, g)


def _nsa_sample(u_s, cmp_tok, cache_sel, win_buf, page_table, bias_c, bias_s, bias_w):
    n_seq, nt, _ = u_s.shape
    q_col = (2 * A_QK + 2 * A_WIDTH) // B_WIDTH
    kv_col = (2 * A_QK + 2 * A_WIDTH + B_WIDTH) // (4 * PAIR)
    g_col = (IN_DIM - 3 * B_HEADS) // PAIR
    const = lambda a: pl.BlockSpec(a.shape, lambda b, pt: (0, 0))
    in_specs = [pl.BlockSpec((None, PAGE_SIZE, 4 * PAIR), lambda b, pt, j=j: (pt[b, j], 0, 0)) for j in range(N_PAGES)]
    in_specs += [
        pl.BlockSpec((None, nt, B_WIDTH), lambda b, pt: (b, 0, q_col)),
        pl.BlockSpec((None, nt, PAIR), lambda b, pt: (b, 0, g_col)),
        pl.BlockSpec((None,) + cmp_tok.shape[1:], lambda b, pt: (b, 0, 0, 0)),
        pl.BlockSpec((None, nt, 4 * PAIR), lambda b, pt: (b, 0, kv_col + 1)),
        pl.BlockSpec((None, nt, 4 * PAIR), lambda b, pt: (b, 0, kv_col + 2)),
        pl.BlockSpec((None, WINDOW, 4 * PAIR), lambda b, pt: (b, 0, 0)),
        const(bias_c), const(bias_s), const(bias_w),
    ]
    return pl.pallas_call(
        _nsa_sample_kernel,
        grid_spec=pltpu.PrefetchScalarGridSpec(
            num_scalar_prefetch=1,
            grid=(n_seq,),
            in_specs=in_specs,
            out_specs=[pl.BlockSpec((None, nt, B_WIDTH), lambda b, pt: (b, 0, 0)),
                       pl.BlockSpec((None, WINDOW, 4 * PAIR), lambda b, pt: (b, 0, 0))],
            scratch_shapes=[pltpu.VMEM((SEL_KEYS, 4 * PAIR), BF16), pltpu.VMEM((WIN_KEYS, 4 * PAIR), BF16)],
        ),
        out_shape=[jax.ShapeDtypeStruct((n_seq, nt, B_WIDTH), F32), jax.ShapeDtypeStruct(win_buf.shape, F32)],
        compiler_params=_params("arbitrary"),
        name="nsa_sample",
    )(page_table, *([cache_sel] * N_PAGES), u_s, u_s, cmp_tok, u_s, u_s, win_buf, bias_c, bias_s, bias_w)


PROMPT_TM = 1024
PROJ_TN = 512
COL_CMP = 2 * A_QK + 2 * A_WIDTH + B_WIDTH
COL_SEL = COL_CMP + 2 * KV_WIDTH
COL_WIN = COL_SEL + 2 * KV_WIDTH
COL_GATE = COL_WIN + 2 * KV_WIDTH


def _time_major(a):
    return a.transpose(1, 0, 2).reshape(a.shape[0] * a.shape[1], a.shape[2])


def _seq_major(a2d, n_seq):
    return a2d.reshape(a2d.shape[0] // n_seq, n_seq, a2d.shape[1]).transpose(1, 0, 2)


def _kv_rows(u3, col):
    return u3[..., col:col + 2 * KV_WIDTH].reshape(u3.shape[:-1] + (2, B_KV_HEADS, B_DH))


def kernel(x_prompt, x_sample, cache_cmp, cache_sel, state_win, state_hgrn, state_conv, page_table, c_prompt, c_sample, norm_g, ada_w, ada_b, ffn_w_gate, ffn_w_up, ffn_w_down, w_in_even, hgrn_lower_bound, hgrn_norm_g, cmp_pe, cmp_w1, cmp_b1, cmp_w2, cmp_b2, rel_bias, w_out_even, conv_w_pw1, conv_b_pw1, conv_w_dw, conv_b_dw, conv_ln_g, conv_ln_b, conv_w_pw2, conv_b_pw2, final_norm_g):
    P = {'norm_g': norm_g, 'ffn_w_gate': ffn_w_gate, 'ffn_w_up': ffn_w_up, 'ffn_w_down': ffn_w_down,
         'final_norm_g': final_norm_g}
    n_seq = x_sample.shape[0]
    n_pool = cache_cmp.shape[1]
    rows_p, rows_s = _prompt_rows(PROMPT_TM), _sample_rows()
    g4 = norm_g.reshape(DEPTH, 3, 1, D_MODEL)
    c_all = jnp.concatenate([c_sample, jnp.tile(c_prompt, (SUBLANES, 1))], axis=0)
    mod = _ada_mod(c_all, ada_w, ada_b)
    bias_pc, bias_pkv, bias_sc, bias_ss, bias_sw = _bias_tables(rel_bias)

    xp = x_prompt.reshape(SEQ, D_MODEL)
    xs = _time_major(x_sample)
    cmp_p, cmp_s, sel_p, sel_s, win_p, win_s, hgrn_p, hgrn_s, conv_p, conv_s = ([] for _ in range(10))
    for l in range(DEPTH):
        i = l // 2
        last = l == DEPTH - 1
        xs, w_bf16 = _ffn(xs, rows_s, mod, l, 0, P)
        xp, _ = _ffn(xp, rows_p, mod, l, 0, P, w_bf16=w_bf16)
        if l % 2 == 0:
            up = _proj(xp, rows_p, mod, l, g4, w_in_even[i:i + 1], IN_DIM, PROJ_TN)
            us = _seq_major(_proj(xs, rows_s, mod, l, g4, w_in_even[i:i + 1], IN_DIM, PROJ_TN), n_seq)
            gn = hgrn_norm_g[i:i + 1]
            oa_p, hp = _hgrn(up, hgrn_lower_bound, gn, l)
            oa_s, hs = _hgrn(us.reshape(n_seq * DEC_SEQ, IN_DIM), hgrn_lower_bound, gn, l, state_hgrn[i])
            weights = _cmp_weights(cmp_pe[i], cmp_w1[i], cmp_b1[i], cmp_w2[i], cmp_b2[i])
            kv_pad = jnp.pad(up[:, COL_SEL:COL_GATE].astype(BF16), ((KV_PAD, 0), (0, 0)))
            ob_p = _nsa_prompt(up, kv_pad, _compress_prompt(up, weights), bias_pc, bias_pkv)
            cmp_tok_s = _compress_sample(cache_cmp[i].reshape(n_pool, PAGE_SIZE, 4 * PAIR), page_table, weights)
            ob_s, wn = _nsa_sample(us, cmp_tok_s, cache_sel[i].reshape(n_pool, PAGE_SIZE, 4 * PAIR),
                                   state_win[i].reshape(n_seq, WINDOW, 4 * PAIR), page_table, bias_sc, bias_ss, bias_sw)
            xp = _out_proj([oa_p, ob_p], w_out_even[i:i + 1], [0, A_WIDTH], None, xp, rows_p, mod, l, PROJ_TN)
            xs = _out_proj([_time_major(oa_s.reshape(n_seq, DEC_SEQ, A_WIDTH)), _time_major(ob_s)],
                           w_out_even[i:i + 1], [0, A_WIDTH], None, xs, rows_s, mod, l, PROJ_TN)
            up3 = up[None]
            cmp_p.append(_kv_rows(up3, COL_CMP))
            sel_p.append(_kv_rows(up3, COL_SEL))
            win_p.append(_kv_rows(up3[:, SEQ - min(WINDOW, SEQ):], COL_WIN))
            cmp_s.append(_kv_rows(us, COL_CMP))
            sel_s.append(_kv_rows(us, COL_SEL))
            win_s.append(wn.reshape(n_seq, WINDOW, 2, B_KV_HEADS, B_DH))
            hgrn_p.append(hp[None])
            hgrn_s.append(hs)
        else:
            b_pw1 = conv_b_pw1[i].reshape(1, 1, 2 * CONV_DIM)
            b_pw2 = conv_b_pw2[i].reshape(1, 1, D_MODEL)
            vec = lambda a: a[i].reshape(1, CONV_DIM)
            glu_p = _glu_proj(xp, rows_p, mod, l, g4, conv_w_pw1[i:i + 1], b_pw1, PROJ_TN)
            act_p = _conv_prompt(glu_p, conv_w_dw[i], vec(conv_b_dw), vec(conv_ln_g), vec(conv_ln_b))
            xp = _out_proj([act_p], conv_w_pw2[i:i + 1], [0], b_pw2, xp, rows_p, mod, l, PROJ_TN)
            glu_s = _glu_proj(xs, rows_s, mod, l, g4, conv_w_pw1[i:i + 1], b_pw1, PROJ_TN)
            act_s, nb = _conv_sample(glu_s.reshape(DEC_SEQ, n_seq, CONV_DIM), state_conv[i].transpose(1, 0, 2),
                                     conv_w_dw[i], vec(conv_b_dw), vec(conv_ln_g), vec(conv_ln_b))
            xs = _out_proj([act_s.reshape(DEC_SEQ * n_seq, CONV_DIM)], conv_w_pw2[i:i + 1], [0], b_pw2, xs, rows_s, mod, l, PROJ_TN)
            conv_p.append(glu_p[None, SEQ - (CONV_WIDTH - 1):])
            conv_s.append(nb.transpose(1, 0, 2))
        xs, w_bf16 = _ffn(xs, rows_s, mod, l, 2, P, final_norm=last)
        xp, _ = _ffn(xp, rows_p, mod, l, 2, P, w_bf16=w_bf16, final_norm=last)
    y_prompt = xp.reshape(1, SEQ, D_MODEL)
    y_sample = _seq_major(xs, n_seq)
    st = jnp.stack
    return (y_prompt, y_sample, st(cmp_p), st(cmp_s), st(sel_p), st(sel_s), st(win_p), st(win_s),
            st(hgrn_p), st(hgrn_s), st(conv_p), st(conv_s))
```

```python
import functools
import math

import numpy as np
import jax
import jax.numpy as jnp
from jax import lax
from jax.experimental import pallas as pl
from jax.experimental.pallas import tpu as pltpu

F32 = jnp.float32
BF16 = jnp.bfloat16

D_MODEL = 2048
SEQ = 8192
DEPTH = 2
DEC_BATCH = 128
DEC_SEQ = 8
PAST_LEN = 2048
PAGE_SIZE = 128
N_MOD = 9
D_FF = 5504
EPS = 1e-6
A_HEADS = 8
A_DK = 128
A_DV = 128
A_QK = A_HEADS * A_DK
A_WIDTH = A_HEADS * A_DV
B_HEADS = 16
B_KV_HEADS = 4
B_DH = 64
B_GROUP = B_HEADS // B_KV_HEADS
B_WIDTH = B_HEADS * B_DH
KV_WIDTH = B_KV_HEADS * B_DH
CMP_BLOCK = 32
CMP_STRIDE = 16
CMP_HIDDEN = 256
SEL_BLOCK = 64
N_SELECT = 16
WINDOW = 512
N_BUCKETS = 32
MAX_DISTANCE = 128
MIX_WIDTH = A_WIDTH + B_WIDTH
IN_DIM = 2 * A_QK + 2 * A_WIDTH + B_WIDTH + 6 * KV_WIDTH + 3 * B_HEADS
CONV_WIDTH = 31
CONV_DIM = D_MODEL
NEG = -1e30
FORCE = 1e9

V7X_VMEM_LIMIT_BYTES = 60 * 1024 * 1024
SUBLANES = 8
LANES = 128

N_SEQ_ROWS = DEC_BATCH + SUBLANES
PROMPT_ROW_BLOCK = DEC_BATCH // SUBLANES


def _params(*sem):
    return pltpu.CompilerParams(dimension_semantics=sem, vmem_limit_bytes=V7X_VMEM_LIMIT_BYTES)


def _silu(x):
    return x * jax.nn.sigmoid(x)


def _bdot(a, b):
    return jnp.dot(a.astype(BF16), b.astype(BF16), preferred_element_type=F32)


def _ada_kernel(c_ref, w_ref, b_ref, o_ref):
    o_ref[...] = _bdot(_silu(c_ref[...]), w_ref[...]) + b_ref[...]


def _ada_mod(c_all, ada_w, ada_b):
    n = c_all.shape[0]
    return pl.pallas_call(
        _ada_kernel,
        grid=(DEPTH, N_MOD),
        in_specs=[
            pl.BlockSpec((n, D_MODEL), lambda l, k: (0, 0)),
            pl.BlockSpec((None, D_MODEL, D_MODEL), lambda l, k: (l, 0, k)),
            pl.BlockSpec((None, None, 1, D_MODEL), lambda l, k: (l, k, 0, 0)),
        ],
        out_specs=pl.BlockSpec((None, None, n, D_MODEL), lambda l, k: (l, k, 0, 0)),
        out_shape=jax.ShapeDtypeStruct((DEPTH, N_MOD, n, D_MODEL), F32),
        compiler_params=_params("arbitrary", "arbitrary"),
        name="ada_mod",
    )(c_all, ada_w, ada_b.reshape(DEPTH, N_MOD, 1, D_MODEL))


def _norm_mod(x, g, shift, scale):
    ms = jnp.mean(x * x, axis=-1, keepdims=True)
    y = x * lax.rsqrt(ms + EPS) * g
    h = y * (1.0 + scale) + shift
    return h.reshape(x.shape[0] * x.shape[1], x.shape[2]).astype(BF16)


class _Rows:
    def __init__(self, m, bs, nt, seq_block):
        assert m % (bs * nt) == 0
        self.m, self.bs, self.nt, self.seq_block = m, bs, nt, seq_block
        self.tm = bs * nt
        self.n_tiles = m // self.tm

    def view(self, x2d):
        return x2d.reshape(self.m // self.bs, self.bs, x2d.shape[-1])

    def x_spec(self, width, col=lambda j: 0, single_buffer=False):
        mode = dict(pipeline_mode=pl.Buffered(1)) if single_buffer else {}
        return pl.BlockSpec((self.nt, self.bs, width), lambda i, j: (i, 0, col(j)), **mode)

    def mod_spec(self, layer, k, width=D_MODEL, col=lambda j: 0):
        sb = self.seq_block
        return pl.BlockSpec((None, None, self.bs, width), lambda i, j: (layer, k, sb, col(j)))


def _prompt_rows(tm):
    return _Rows(SEQ, SUBLANES, tm // SUBLANES, PROMPT_ROW_BLOCK)


def _sample_rows():
    return _Rows(DEC_BATCH * DEC_SEQ, DEC_BATCH, DEC_SEQ, 0)


FFN_TF_F32 = 256
FFN_TF_BF16 = 512
FFN_ACC_CHUNKS = 4


def _ffn_kernel(x_ref, sh_ref, sc_ref, gt_ref, g_ref, wg_ref, wu_ref, wd_ref, fg_ref, o_ref, *rest, final_norm, emit):
    if emit:
        wg_o, wu_o, wd_o, h_ref = rest
    else:
        (h_ref,) = rest
    j = pl.program_id(1)
    nj = pl.num_programs(1)
    tf = wg_ref.shape[1]

    nt, bs = o_ref.shape[0], o_ref.shape[1]
    step = max(nt // FFN_ACC_CHUNKS, 1)

    @pl.when(j == 0)
    def _():
        for r in range(0, nt, step):
            h_ref[r * bs:(r + step) * bs, :] = _norm_mod(x_ref[r:r + step], g_ref[...], sh_ref[...], sc_ref[...])
        o_ref[...] = jnp.zeros_like(o_ref)

    wg, wu, wd = wg_ref[...].astype(BF16), wu_ref[...].astype(BF16), wd_ref[...].astype(BF16)
    if emit:
        wg_o[...], wu_o[...], wd_o[...] = wg, wu, wd
    valid = D_FF - j * tf
    h = h_ref[...]
    a = _silu(_bdot(h, wg)) * _bdot(h, wu)
    col = lax.broadcasted_iota(jnp.int32, a.shape, 1)
    a = jnp.where(col < valid, a, 0.0)
    row = lax.broadcasted_iota(jnp.int32, wd.shape, 0)
    wd = jnp.where(row < valid, wd, jnp.zeros_like(wd))
    a = a.astype(BF16)
    for r in range(0, nt, step):
        o_ref[r:r + step] += _bdot(a[r * bs:(r + step) * bs], wd).reshape(step, bs, o_ref.shape[2])

    @pl.when(j == nj - 1)
    def _():
        for r in range(0, nt, step):
            y = x_ref[r:r + step] + (0.5 * gt_ref[...]) * o_ref[r:r + step]
            if final_norm:
                ms = jnp.mean(y * y, axis=-1, keepdims=True)
                y = y * lax.rsqrt(ms + EPS) * fg_ref[...]
            o_ref[r:r + step] = y


def _ffn(x2d, rows, mod, layer, sub, P, w_bf16=None, final_norm=False):
    half = sub // 2
    emit = w_bf16 is None
    tf = FFN_TF_F32 if emit else FFN_TF_BF16
    g_norm = P['norm_g'].reshape(DEPTH, 3, 1, D_MODEL)
    fg = P['final_norm_g'].reshape(1, D_MODEL)
    if emit:
        weights = (P['ffn_w_gate'], P['ffn_w_up'], P['ffn_w_down'])
        w_specs = [pl.BlockSpec((None, None, D_MODEL, tf), lambda i, j: (layer, half, 0, j)),
                   pl.BlockSpec((None, None, D_MODEL, tf), lambda i, j: (layer, half, 0, j)),
                   pl.BlockSpec((None, None, tf, D_MODEL), lambda i, j: (layer, half, j, 0))]
    else:
        weights = w_bf16
        w_specs = [pl.BlockSpec((D_MODEL, tf), lambda i, j: (0, j)),
                   pl.BlockSpec((D_MODEL, tf), lambda i, j: (0, j)),
                   pl.BlockSpec((tf, D_MODEL), lambda i, j: (j, 0))]
    out_specs = [rows.x_spec(D_MODEL, single_buffer=emit)]
    out_shape = [jax.ShapeDtypeStruct((rows.m // rows.bs, rows.bs, D_MODEL), F32)]
    if emit:
        out_specs += [pl.BlockSpec((D_MODEL, tf), lambda i, j: (0, j)),
                      pl.BlockSpec((D_MODEL, tf), lambda i, j: (0, j)),
                      pl.BlockSpec((tf, D_MODEL), lambda i, j: (j, 0))]
        out_shape += [jax.ShapeDtypeStruct((D_MODEL, D_FF), BF16), jax.ShapeDtypeStruct((D_MODEL, D_FF), BF16),
                      jax.ShapeDtypeStruct((D_FF, D_MODEL), BF16)]
        assert rows.n_tiles == 1
    outs = pl.pallas_call(
        functools.partial(_ffn_kernel, final_norm=final_norm, emit=emit),
        grid=(rows.n_tiles, pl.cdiv(D_FF, tf)),
        in_specs=[
            rows.x_spec(D_MODEL, single_buffer=True),
            rows.mod_spec(layer, 3 * sub), rows.mod_spec(layer, 3 * sub + 1), rows.mod_spec(layer, 3 * sub + 2),
            pl.BlockSpec((None, None, 1, D_MODEL), lambda i, j: (layer, sub, 0, 0)),
            *w_specs,
            pl.BlockSpec((1, D_MODEL), lambda i, j: (0, 0)),
        ],
        out_specs=out_specs,
        out_shape=out_shape,
        scratch_shapes=[pltpu.VMEM((rows.tm, D_MODEL), BF16)],
        compiler_params=_params("arbitrary", "arbitrary"),
        name="ffn_half_step",
    )(rows.view(x2d), mod, mod, mod, g_norm, *weights, fg)
    return outs[0].reshape(rows.m, D_MODEL), tuple(outs[1:])


def _proj_kernel(x_ref, sh_ref, sc_ref, g_ref, w_ref, o_ref, h_ref):
    @pl.when(pl.program_id(1) == 0)
    def _():
        h_ref[...] = _norm_mod(x_ref[...], g_ref[...], sh_ref[...], sc_ref[...])

    o_ref[...] = _bdot(h_ref[...], w_ref[...])


def _proj(x2d, rows, mod, layer, g_norm4, w3, n_out, tn):
    return pl.pallas_call(
        _proj_kernel,
        grid=(rows.n_tiles, pl.cdiv(n_out, tn)),
        in_specs=[
            rows.x_spec(D_MODEL),
            rows.mod_spec(layer, 3), rows.mod_spec(layer, 4),
            pl.BlockSpec((None, None, 1, D_MODEL), lambda i, j: (layer, 1, 0, 0)),
            pl.BlockSpec((None, D_MODEL, tn), lambda i, j: (0, 0, j)),
        ],
        out_specs=pl.BlockSpec((rows.tm, tn), lambda i, j: (i, j)),
        out_shape=jax.ShapeDtypeStruct((rows.m, n_out), F32),
        scratch_shapes=[pltpu.VMEM((rows.tm, D_MODEL), BF16)],
        compiler_params=_params("arbitrary", "arbitrary"),
        name="prenorm_proj",
    )(rows.view(x2d), mod, mod, g_norm4, w3)


def _glu_proj_kernel(x_ref, sh_ref, sc_ref, g_ref, wa_ref, wg_ref, ba_ref, bg_ref, o_ref, h_ref):
    @pl.when(pl.program_id(1) == 0)
    def _():
        h_ref[...] = _norm_mod(x_ref[...], g_ref[...], sh_ref[...], sc_ref[...])

    h = h_ref[...]
    a = _bdot(h, wa_ref[...]) + ba_ref[...]
    gt = _bdot(h, wg_ref[...]) + bg_ref[...]
    o_ref[...] = a * jax.nn.sigmoid(gt)


def _glu_proj(x2d, rows, mod, layer, g_norm4, w3, b3, tn):
    nb = CONV_DIM // tn
    return pl.pallas_call(
        _glu_proj_kernel,
        grid=(rows.n_tiles, nb),
        in_specs=[
            rows.x_spec(D_MODEL),
            rows.mod_spec(layer, 3), rows.mod_spec(layer, 4),
            pl.BlockSpec((None, None, 1, D_MODEL), lambda i, j: (layer, 1, 0, 0)),
            pl.BlockSpec((None, D_MODEL, tn), lambda i, j: (0, 0, j)),
            pl.BlockSpec((None, D_MODEL, tn), lambda i, j: (0, 0, j + nb)),
            pl.BlockSpec((None, 1, tn), lambda i, j: (0, 0, j)),
            pl.BlockSpec((None, 1, tn), lambda i, j: (0, 0, j + nb)),
        ],
        out_specs=pl.BlockSpec((rows.tm, tn), lambda i, j: (i, j)),
        out_shape=jax.ShapeDtypeStruct((rows.m, CONV_DIM), F32),
        scratch_shapes=[pltpu.VMEM((rows.tm, D_MODEL), BF16)],
        compiler_params=_params("arbitrary", "arbitrary"),
        name="prenorm_glu_proj",
    )(rows.view(x2d), mod, mod, g_norm4, w3, w3, b3, b3)


def _out_kernel(*refs, n_in, has_bias):
    a_refs = refs[:n_in]
    w_refs = refs[n_in:2 * n_in]
    pos = 2 * n_in
    b_ref = refs[pos] if has_bias else None
    pos += int(has_bias)
    x_ref, gt_ref, o_ref = refs[pos:pos + 3]
    y = _bdot(a_refs[0][...], w_refs[0][...])
    for a_ref, w_ref in zip(a_refs[1:], w_refs[1:]):
        y += _bdot(a_ref[...], w_ref[...])
    if has_bias:
        y += b_ref[...]
    o_ref[...] = x_ref[...] + gt_ref[...] * y.reshape(o_ref.shape)


def _out_proj(acts, w3, k_offsets, bias3, x2d, rows, mod, layer, tn):
    n_in = len(acts)
    in_specs = [pl.BlockSpec((rows.tm, a.shape[1]), lambda i, j: (i, 0)) for a in acts]
    for a, off in zip(acts, k_offsets):
        kb = off // a.shape[1]
        in_specs.append(pl.BlockSpec((None, a.shape[1], tn), lambda i, j, kb=kb: (0, kb, j)))
    args = list(acts) + [w3] * n_in
    if bias3 is not None:
        in_specs.append(pl.BlockSpec((None, 1, tn), lambda i, j: (0, 0, j)))
        args.append(bias3)
    in_specs += [rows.x_spec(tn, col=lambda j: j), rows.mod_spec(layer, 5, width=tn, col=lambda j: j)]
    args += [rows.view(x2d), mod]
    out = pl.pallas_call(
        functools.partial(_out_kernel, n_in=n_in, has_bias=bias3 is not None),
        grid=(rows.n_tiles, D_MODEL // tn),
        in_specs=in_specs,
        out_specs=rows.x_spec(tn, col=lambda j: j),
        out_shape=jax.ShapeDtypeStruct((rows.m // rows.bs, rows.bs, D_MODEL), F32),
        compiler_params=_params("arbitrary", "arbitrary"),
        name="out_proj_residual",
    )(*args)
    return out.reshape(rows.m, D_MODEL)


HGRN_ROWS = 512
HGRN_HEADS = 4
HGRN_TRI = 128


def _hgrn_prepare(f_ref, lb_ref, cum_s, kk_s, *, layer, c):
    p = lb_ref[...]
    e = jnp.exp(p - jnp.max(p, axis=0, keepdims=True))
    sm = e / jnp.sum(e, axis=0, keepdims=True)
    lb = jnp.sum(sm[:layer + 1], axis=0, keepdims=True)
    f = lb + (1.0 - lb) * jax.nn.sigmoid(f_ref[...])
    lf = jnp.log(f)
    n = HGRN_TRI
    r = lax.broadcasted_iota(jnp.int32, (n, n), 0)
    s = lax.broadcasted_iota(jnp.int32, (n, n), 1)
    tri = jnp.where((s <= r) & (s // c == r // c), 1.0, 0.0).astype(F32)
    for b in range(f.shape[0] // n):
        cum_s[b * n:(b + 1) * n, :] = jnp.dot(tri, lf[b * n:(b + 1) * n], preferred_element_type=F32,
                                              precision=lax.Precision.HIGHEST)
    kk_s[...] = 1.0 - f


def _hgrn_subchunk(r0, c, st, hh, q_ref, v_ref, g_ref, gn_ref, cum_s, kk_s):
    rows = pl.ds(r0, c)
    cols = slice(hh * A_DK, (hh + 1) * A_DK)
    cum = cum_s[rows, cols]
    q = q_ref[rows, cols]
    kk = kk_s[rows, cols]
    vv = v_ref[rows, cols]
    last = cum[c - 1:c, :]
    o = lax.dot_general((q * jnp.exp(cum)).astype(BF16), st.astype(BF16), (((1,), (1,)), ((), ())),
                        preferred_element_type=F32)
    srow = lax.broadcasted_iota(jnp.int32, (c, A_DK), 0)
    xs = []
    for t in range(c):
        d = jnp.where(srow <= t, cum[t:t + 1, :] - cum, NEG)
        xs.append(jnp.exp(d) * (q[t:t + 1, :] * kk))
    x = jnp.concatenate(xs, axis=0).astype(BF16)
    w = jnp.dot(x, jnp.ones((A_DK, A_DV), BF16), preferred_element_type=F32)
    o = o + jnp.sum(w.reshape(c, c, A_DV) * vv[None], axis=1)
    ke = kk * jnp.exp(last - cum)
    st_new = st * jnp.exp(last) + lax.dot_general(vv.astype(BF16), ke.astype(BF16), (((0,), (0,)), ((), ())),
                                                  preferred_element_type=F32)
    ms = jnp.mean(o * o, axis=-1, keepdims=True)
    y = o * lax.rsqrt(ms + EPS) * gn_ref[:, cols] * _silu(g_ref[rows, cols])
    return y, st_new


def _hgrn_prompt_kernel(q_ref, f_ref, v_ref, g_ref, lb_ref, gn_ref, o_ref, s_ref, st_ref, cum_s, kk_s, *, layer, c):
    i = pl.program_id(1)

    @pl.when(i == 0)
    def _():
        st_ref[...] = jnp.zeros_like(st_ref)

    _hgrn_prepare(f_ref, lb_ref, cum_s, kk_s, layer=layer, c=c)

    def body(n, carry):
        r0 = pl.multiple_of(n * c, c)
        for hh in range(HGRN_HEADS):
            y, st_new = _hgrn_subchunk(r0, c, st_ref[hh], hh, q_ref, v_ref, g_ref, gn_ref, cum_s, kk_s)
            st_ref[hh] = st_new
            o_ref[pl.ds(r0, c), hh * A_DV:(hh + 1) * A_DV] = y
        return carry

    lax.fori_loop(0, q_ref.shape[0] // c, body, 0)

    @pl.when(i == pl.num_programs(1) - 1)
    def _():
        for hh in range(HGRN_HEADS):
            s_ref[hh] = st_ref[hh].T


def _hgrn_sample_kernel(q_ref, f_ref, v_ref, g_ref, lb_ref, gn_ref, s0_ref, o_ref, s_ref, cum_s, kk_s, *, layer, c):
    _hgrn_prepare(f_ref, lb_ref, cum_s, kk_s, layer=layer, c=c)

    def body(n, carry):
        r0 = pl.multiple_of(n * c, c)
        for hh in range(HGRN_HEADS):
            y, st_new = _hgrn_subchunk(r0, c, s0_ref[n, hh].T, hh, q_ref, v_ref, g_ref, gn_ref, cum_s, kk_s)
            s_ref[n, hh] = st_new.T
            o_ref[pl.ds(r0, c), hh * A_DV:(hh + 1) * A_DV] = y
        return carry

    lax.fori_loop(0, q_ref.shape[0] // c, body, 0)


def _hgrn(u, lower_bound, norm_g2, layer, s0=None):
    m = u.shape[0]
    hb = HGRN_HEADS
    w = hb * A_DK
    nb = A_QK // w
    tc = HGRN_ROWS if s0 is None else HGRN_ROWS // 2

    def col(seg):
        return pl.BlockSpec((tc, w), lambda h, i, seg=seg: (i, seg * nb + h))

    in_specs = [col(0), col(1), col(2), col(3),
                pl.BlockSpec((DEPTH + 1, w), lambda h, i: (0, h)),
                pl.BlockSpec((1, w), lambda h, i: (0, h))]
    o_spec = pl.BlockSpec((tc, w), lambda h, i: (i, h))
    scratch = [pltpu.VMEM((tc, w), F32), pltpu.VMEM((tc, w), F32)]
    if s0 is None:
        return pl.pallas_call(
            functools.partial(_hgrn_prompt_kernel, layer=layer, c=16),
            grid=(nb, m // tc),
            in_specs=in_specs,
            out_specs=[o_spec, pl.BlockSpec((hb, A_DK, A_DV), lambda h, i: (h, 0, 0))],
            out_shape=[jax.ShapeDtypeStruct((m, A_WIDTH), F32), jax.ShapeDtypeStruct((A_HEADS, A_DK, A_DV), F32)],
            scratch_shapes=[pltpu.VMEM((hb, A_DV, A_DK), F32)] + scratch,
            compiler_params=_params("arbitrary", "arbitrary"),
            name="hgrn2_prompt",
        )(u, u, u, u, lower_bound, norm_g2)
    c = DEC_SEQ
    ns = tc // c
    s_spec = pl.BlockSpec((ns, hb, A_DK, A_DV), lambda h, i: (i, h, 0, 0))
    return pl.pallas_call(
        functools.partial(_hgrn_sample_kernel, layer=layer, c=c),
        grid=(nb, m // tc),
        in_specs=in_specs + [s_spec],
        out_specs=[o_spec, s_spec],
        out_shape=[jax.ShapeDtypeStruct((m, A_WIDTH), F32), jax.ShapeDtypeStruct(s0.shape, F32)],
        scratch_shapes=scratch,
        compiler_params=_params("arbitrary", "arbitrary"),
        name="hgrn2_sample",
    )(u, u, u, u, lower_bound, norm_g2, s0)


CONV_HALO = 32
CONV_ROWS = 256


def _ln_silu(y, g, b):
    mu = jnp.mean(y, axis=-1, keepdims=True)
    yc = y - mu
    var = jnp.mean(yc * yc, axis=-1, keepdims=True)
    return _silu(yc * lax.rsqrt(var + EPS) * g + b)


def _conv_prompt_kernel(cur_ref, halo_ref, w_ref, b_ref, g_ref, lb_ref, o_ref, ext_ref):
    i = pl.program_id(0)
    tt = cur_ref.shape[0]
    ext_ref[0:CONV_HALO, :] = jnp.where(i > 0, halo_ref[...], 0.0)
    ext_ref[CONV_HALO:, :] = cur_ref[...]
    off = CONV_HALO - (CONV_WIDTH - 1)
    y = jnp.zeros((tt, CONV_DIM), F32) + b_ref[...]
    for w in range(CONV_WIDTH):
        y = y + ext_ref[off + w:off + w + tt, :] * w_ref[w:w + 1, :]
    o_ref[...] = _ln_silu(y, g_ref[...], lb_ref[...]).astype(o_ref.dtype)


def _conv_prompt(glu, w_dw, b_dw, ln_g, ln_b):
    t = glu.shape[0]
    tt = CONV_ROWS
    r = tt // CONV_HALO
    vec = pl.BlockSpec((1, CONV_DIM), lambda i: (0, 0))
    return pl.pallas_call(
        _conv_prompt_kernel,
        grid=(t // tt,),
        in_specs=[
            pl.BlockSpec((tt, CONV_DIM), lambda i: (i, 0)),
            pl.BlockSpec((CONV_HALO, CONV_DIM), lambda i: (jnp.maximum(i * r - 1, 0), 0)),
            pl.BlockSpec((CONV_WIDTH, CONV_DIM), lambda i: (0, 0)),
            vec, vec, vec,
        ],
        out_specs=pl.BlockSpec((tt, CONV_DIM), lambda i: (i, 0)),
        out_shape=jax.ShapeDtypeStruct((t, CONV_DIM), BF16),
        scratch_shapes=[pltpu.VMEM((CONV_HALO + tt, CONV_DIM), F32)],
        compiler_params=_params("arbitrary"),
        name="conv_prompt",
    )(glu, glu, w_dw, b_dw, ln_g, ln_b)


def _conv_sample_kernel(u_ref, buf_ref, w_ref, b_ref, g_ref, lb_ref, o_ref, nb_ref):
    hist = CONV_WIDTH - 1
    nt = u_ref.shape[0]

    def ext(j):
        return buf_ref[j] if j < hist else u_ref[j - hist]

    for t in range(nt):
        y = ext(t) * w_ref[0:1, :] + b_ref[...]
        for w in range(1, CONV_WIDTH):
            y = y + ext(t + w) * w_ref[w:w + 1, :]
        o_ref[t] = _ln_silu(y, g_ref[...], lb_ref[...]).astype(o_ref.dtype)
    for j in range(hist):
        nb_ref[j] = ext(j + nt)


def _conv_sample(glu_t, buf_t, w_dw, b_dw, ln_g, ln_b):
    nt, ns, _ = glu_t.shape
    hist = CONV_WIDTH - 1
    bs = 32
    vec = pl.BlockSpec((1, CONV_DIM), lambda i: (0, 0))
    return pl.pallas_call(
        _conv_sample_kernel,
        grid=(ns // bs,),
        in_specs=[
            pl.BlockSpec((nt, bs, CONV_DIM), lambda i: (0, i, 0)),
            pl.BlockSpec((hist, bs, CONV_DIM), lambda i: (0, i, 0)),
            pl.BlockSpec((CONV_WIDTH, CONV_DIM), lambda i: (0, 0)),
            vec, vec, vec,
        ],
        out_specs=[pl.BlockSpec((nt, bs, CONV_DIM), lambda i: (0, i, 0)),
                   pl.BlockSpec((hist, bs, CONV_DIM), lambda i: (0, i, 0))],
        out_shape=[jax.ShapeDtypeStruct((nt, ns, CONV_DIM), BF16), jax.ShapeDtypeStruct((hist, ns, CONV_DIM), F32)],
        compiler_params=_params("arbitrary"),
        name="conv_sample",
    )(glu_t, buf_t, w_dw, b_dw, ln_g, ln_b)


PAIR = 2 * B_DH
N_PAIRS = KV_WIDTH // PAIR
BLOCKS_PER_TOKEN = SEL_BLOCK // CMP_STRIDE
N_SUB = CMP_BLOCK // CMP_STRIDE
CMP_OFF = 120
CMP_ROWS = 640
KV_PAD = WINDOW
FAR_TILE = 512
LOWEST = -3.0e38
MASK_C = 2.0 ** 100


def _half(shape, half):
    lane = lax.broadcasted_iota(jnp.int32, shape, len(shape) - 1)
    return (lane % PAIR) // B_DH == half


def _group_q(q_ref, g):
    parts = []
    for r in range(B_GROUP):
        h = g * B_GROUP + r
        x = q_ref[:, (h // 2) * PAIR:(h // 2 + 1) * PAIR]
        x = jnp.where(_half(x.shape, h % 2), x, 0.0) * (B_DH ** -0.5)
        if h % 2 != g % 2:
            x = pltpu.roll(x, B_DH, axis=1)
        parts.append(x)
    return jnp.concatenate(parts, axis=0).astype(BF16)


def _gate_rows(gate_ref, branch, g):
    sig = jax.nn.sigmoid(gate_ref[...])
    c0 = branch * B_HEADS + g * B_GROUP
    return jnp.concatenate([sig[:, c0 + r:c0 + r + 1] for r in range(B_GROUP)], axis=0)


def _scatter_heads(o, g):
    nq = o.shape[0] // B_GROUP
    outs = []
    for jp in range(2):
        acc = None
        for e in range(2):
            r = 2 * jp + e
            x = jnp.where(_half((nq, PAIR), g % 2), o[r * nq:(r + 1) * nq], 0.0)
            if e != g % 2:
                x = pltpu.roll(x, B_DH, axis=1)
            acc = x if acc is None else acc + x
        outs.append(acc)
    return jnp.concatenate(outs, axis=1)


def _nt_dot(a, b):
    return lax.dot_general(a, b, (((1,), (1,)), ((), ())), preferred_element_type=F32)


def _piece(qg, k, v, mask, bias=None):
    s = _nt_dot(qg, k)
    if bias is not None:
        s = s + bias
    s = jnp.where(mask, s, NEG)
    m = jnp.max(s, axis=-1, keepdims=True)
    p = jnp.where(mask, jnp.exp(s - m), 0.0)
    return m, jnp.sum(p, axis=-1, keepdims=True), jnp.dot(p.astype(BF16), v, preferred_element_type=F32), p


def _merge(a, b):
    m = jnp.maximum(a[0], b[0])
    ea = jnp.exp(a[0] - m)
    eb = jnp.exp(b[0] - m)
    return m, a[1] * ea + b[1] * eb, a[2] * ea + b[2] * eb


def _inv_or_zero(l):
    return jnp.where(l > 0.0, 1.0 / l, 0.0)


def _dot01(p, o01):
    hi = p.astype(BF16)
    r1 = p - hi.astype(F32)
    mid = r1.astype(BF16)
    lo = (r1 - mid.astype(F32)).astype(BF16)
    d = lambda a: jnp.dot(a, o01, preferred_element_type=F32)
    return d(hi) + d(mid) + d(lo)


def _overlap01(tok, blk):
    return jnp.where((tok >= BLOCKS_PER_TOKEN * blk - (N_SUB - 1)) & (tok <= BLOCKS_PER_TOKEN * blk + BLOCKS_PER_TOKEN - 1),
                     1.0, 0.0).astype(BF16)


def _select_blocks(imp, qpos):
    j = lax.broadcasted_iota(jnp.int32, imp.shape, 1)
    cur = qpos // SEL_BLOCK
    forced = (j == 0) | (j == cur) | (j == cur - 1)
    valid = j * SEL_BLOCK <= qpos
    imp = jnp.where(forced, FORCE, jnp.where(valid, imp, -FORCE))
    sel = jnp.zeros(imp.shape, F32)
    jf = j.astype(F32)
    for _ in range(N_SELECT):
        m = jnp.max(imp, axis=-1, keepdims=True)
        first = jnp.min(jnp.where(imp == m, jf, float(imp.shape[1])), axis=-1, keepdims=True)
        pick = jf == first
        sel = jnp.where(pick, 1.0, sel)
        imp = jnp.where(pick, LOWEST, imp)
    return sel


def _expand_blocks(sel_rows, first_blk, n_keys):
    jj = lax.broadcasted_iota(jnp.int32, (sel_rows.shape[1], n_keys), 0)
    kk = lax.broadcasted_iota(jnp.int32, (sel_rows.shape[1], n_keys), 1)
    e = jnp.where(jj == first_blk + kk // SEL_BLOCK, 1.0, 0.0).astype(BF16)
    return jnp.dot(sel_rows, e, preferred_element_type=F32)


def _bias_tables(rel_bias):
    d = np.arange(MAX_DISTANCE + 1)
    exact = N_BUCKETS // 2
    large = exact + (np.log(np.maximum(d, 1).astype(np.float32) / exact) / math.log(MAX_DISTANCE / exact)
                     * (N_BUCKETS - exact)).astype(np.int32)
    bucket = np.where(d < exact, d, np.minimum(large, N_BUCKETS - 1))
    by_dist = rel_bias.astype(F32)[bucket]

    def tile(c, nq, n, step=1):
        n1 = step * (n - 1) + 1
        dmin, dmax = c - (n1 - 1), c + nq - 1
        n_neg = max(0, min(0, dmax + 1) - dmin)
        lo, hi = max(dmin, 0), min(dmax, MAX_DISTANCE)
        n_far = max(0, dmax - max(dmin, MAX_DISTANCE + 1) + 1)
        v = jnp.concatenate([jnp.broadcast_to(by_dist[:1], (n_neg, B_HEADS)), by_dist[lo:hi + 1],
                             jnp.broadcast_to(by_dist[MAX_DISTANCE:], (n_far, B_HEADS))], axis=0)
        lv = nq + n1 - 1
        p = jnp.concatenate([v[::-1], jnp.zeros((1, B_HEADS), F32)], axis=0)
        rows = jnp.tile(p, (nq, 1))[:nq * lv].reshape(nq, lv, B_HEADS)[:, nq - 1:nq - 1 + n1:step]
        t = rows.reshape(nq, n, B_KV_HEADS, B_GROUP).transpose(2, 3, 0, 1)
        return t.reshape(B_KV_HEADS, B_GROUP * nq, n)

    nq = PAIR
    far = by_dist[MAX_DISTANCE]
    shift = jnp.repeat(far.reshape(B_KV_HEADS, B_GROUP), nq, axis=1)[:, :, None]
    n_var = 32
    m0 = PAIR - n_var
    c_cmp = -CMP_STRIDE * (m0 - CMP_OFF) - (CMP_BLOCK - 1)
    assert c_cmp + CMP_STRIDE >= MAX_DISTANCE
    prompt_cmp = jnp.concatenate([jnp.zeros((B_KV_HEADS, B_GROUP * nq, m0), F32),
                                  tile(c_cmp, nq, n_var, CMP_STRIDE) - shift], axis=2)
    causal = np.tile(np.arange(nq)[:, None] + nq - np.arange(2 * nq)[None, :] >= 0, (B_GROUP, 1))[None]
    prompt_kv = jnp.where(causal, tile(nq, nq, 2 * nq) - shift, -MASK_C)
    s_cmp = tile(PAST_LEN - (CMP_BLOCK - 1), DEC_SEQ, PAIR, CMP_STRIDE)
    s_sel = tile(PAST_LEN, DEC_SEQ, PAST_LEN + PAIR)
    s_win = tile(WINDOW, DEC_SEQ, WINDOW + PAIR)
    flat = lambda a: a.reshape(B_KV_HEADS * B_GROUP * DEC_SEQ, a.shape[-1])
    return prompt_cmp, prompt_kv, flat(s_cmp), flat(s_sel), flat(s_win)


def _cmp_weights(pe, w1, b1, w2, b2):
    eye = jnp.eye(2, dtype=F32)
    w1r = w1.reshape(2, N_SUB, CMP_STRIDE, B_DH, CMP_HIDDEN)
    w1p = jnp.einsum('kmpdh,ef->kmpedfh', w1r, eye).reshape(2, N_SUB, CMP_STRIDE * PAIR, 2 * CMP_HIDDEN).astype(BF16)
    pep = jnp.tile(pe.reshape(2, N_SUB, CMP_STRIDE, 1, B_DH), (1, 1, 1, 2, 1)).reshape(2, N_SUB, 1, CMP_STRIDE * PAIR)
    b1p = jnp.tile(b1, (1, 2)).reshape(2, 1, 2 * CMP_HIDDEN)
    w2p = jnp.einsum('khd,ef->kehfd', w2, eye).reshape(2, 2 * CMP_HIDDEN, PAIR).astype(BF16)
    b2p = jnp.tile(b2, (1, 2)).reshape(2, 1, PAIR)
    return pep, w1p, b1p, w2p, b2p


def _compress_pair(src, pe_ref, w1_ref, b1_ref, w2_ref, b2_ref):
    x = jnp.concatenate([src(p) for p in range(CMP_STRIDE)], axis=1)
    m_rows = x.shape[0]
    h = b1_ref[...]
    for m in range(N_SUB):
        part = jnp.dot((x + pe_ref[m]).astype(BF16), w1_ref[m], preferred_element_type=F32)
        h = h + (part if m == 0 else pltpu.roll(part, m_rows - m, axis=0))
    return jnp.dot(_silu(h).astype(BF16), w2_ref[...], preferred_element_type=F32) + b2_ref[...]


def _cmp_weight_specs(kv_of):
    return [
        pl.BlockSpec((None, N_SUB, 1, CMP_STRIDE * PAIR), lambda *a: (kv_of(*a), 0, 0, 0)),
        pl.BlockSpec((None, N_SUB, CMP_STRIDE * PAIR, 2 * CMP_HIDDEN), lambda *a: (kv_of(*a), 0, 0, 0)),
        pl.BlockSpec((None, 1, 2 * CMP_HIDDEN), lambda *a: (kv_of(*a), 0, 0)),
        pl.BlockSpec((None, 2 * CMP_HIDDEN, PAIR), lambda *a: (kv_of(*a), 0, 0)),
        pl.BlockSpec((None, 1, PAIR), lambda *a: (kv_of(*a), 0, 0)),
    ]


def _compress_prompt_kernel(rows_ref, pe_ref, w1_ref, b1_ref, w2_ref, b2_ref, o_ref):
    n_blk = rows_ref.shape[0] // CMP_STRIDE
    tok = _compress_pair(lambda p: rows_ref[pl.ds(p, n_blk, stride=CMP_STRIDE), :], pe_ref, w1_ref, b1_ref, w2_ref, b2_ref)
    o_ref[...] = jnp.zeros_like(o_ref)
    o_ref[CMP_OFF:CMP_OFF + n_blk, :] = tok


def _compress_prompt(u, weights):
    t = u.shape[0]
    col0 = (2 * A_QK + 2 * A_WIDTH + B_WIDTH) // PAIR
    return pl.pallas_call(
        _compress_prompt_kernel,
        grid=(2 * N_PAIRS,),
        in_specs=[pl.BlockSpec((t, PAIR), lambda c: (0, col0 + c))] + _cmp_weight_specs(lambda c: c // N_PAIRS),
        out_specs=pl.BlockSpec((None, CMP_ROWS, PAIR), lambda c: (c, 0, 0)),
        out_shape=jax.ShapeDtypeStruct((2 * N_PAIRS, CMP_ROWS, PAIR), F32),
        compiler_params=_params("arbitrary"),
        name="nsa_compress_prompt",
    )(u, *weights)


CMP_SEQ_BLOCK = 4
N_PAGES = PAST_LEN // PAGE_SIZE


def _compress_sample_kernel(pt_ref, *refs):
    n_src = CMP_SEQ_BLOCK * N_PAGES
    pages = refs[:n_src]
    pe_ref, w1_ref, b1_ref, w2_ref, b2_ref, o_ref, x_ref = refs[n_src:]
    n_blk = PAGE_SIZE // CMP_STRIDE
    for idx, pg in enumerate(pages):
        x_ref[idx] = pg[...].T
    tok = _compress_pair(
        lambda p: jnp.concatenate([x_ref[idx, pl.ds(p, n_blk, stride=CMP_STRIDE), :] for idx in range(n_src)], axis=0),
        pe_ref, w1_ref, b1_ref, w2_ref, b2_ref)
    o_ref[...] = tok.reshape(o_ref.shape)


def _compress_sample(cache, page_table, weights):
    n_seq = page_table.shape[0]
    sb = CMP_SEQ_BLOCK
    n_tok = PAST_LEN // CMP_STRIDE
    page_specs = [pl.BlockSpec((None, PAIR, PAGE_SIZE), lambda c, i, pt, s=s, j=j: (pt[i * sb + s, j], c, 0))
                  for s in range(sb) for j in range(N_PAGES)]
    return pl.pallas_call(
        _compress_sample_kernel,
        grid_spec=pltpu.PrefetchScalarGridSpec(
            num_scalar_prefetch=1,
            grid=(2 * N_PAIRS, n_seq // sb),
            in_specs=page_specs + _cmp_weight_specs(lambda c, i, pt: c // N_PAIRS),
            out_specs=pl.BlockSpec((sb, None, n_tok, PAIR), lambda c, i, pt: (i, c, 0, 0)),
            scratch_shapes=[pltpu.VMEM((sb * N_PAGES, PAGE_SIZE, PAIR), F32)],
        ),
        out_shape=jax.ShapeDtypeStruct((n_seq, 2 * N_PAIRS, n_tok, PAIR), F32),
        compiler_params=_params("arbitrary", "arbitrary"),
        name="nsa_compress_sample",
    )(page_table, *([cache] * (sb * N_PAGES)), *weights)


Q_ROWS = 128


def _nsa_cmp_prompt_kernel(q_ref, gate_ref, cmp_ref, bias_ref, oc_ref, sel_ref):
    qb = pl.program_id(0)
    nq = q_ref.shape[0]
    rows = B_GROUP * nq
    n_tok = CMP_ROWS - PAIR
    near0 = pl.multiple_of(qb * (nq // CMP_STRIDE), SUBLANES)
    tok0 = near0 - CMP_OFF
    mask_far = lax.broadcasted_iota(jnp.int32, (rows, n_tok), 1) < tok0
    i = lax.broadcasted_iota(jnp.int32, (rows, PAIR), 0) % nq
    mn = lax.broadcasted_iota(jnp.int32, (rows, PAIR), 1)
    dist = i - CMP_STRIDE * (mn - CMP_OFF) - (CMP_BLOCK - 1)
    mask_near = (dist >= 0) & (tok0 + mn >= 0)
    o_far = _overlap01(lax.broadcasted_iota(jnp.int32, (n_tok, PAIR), 0), lax.broadcasted_iota(jnp.int32, (n_tok, PAIR), 1))
    o_near = _overlap01(tok0 + lax.broadcasted_iota(jnp.int32, (PAIR, PAIR), 0), lax.broadcasted_iota(jnp.int32, (PAIR, PAIR), 1))
    qpos = qb * nq + lax.broadcasted_iota(jnp.int32, (nq, PAIR), 0)
    for g in range(B_KV_HEADS):
        qg = _group_q(q_ref, g)
        kp, vp = g // 2, N_PAIRS + g // 2
        far = _piece(qg, cmp_ref[kp, CMP_OFF:CMP_OFF + n_tok, :].astype(BF16),
                     cmp_ref[vp, CMP_OFF:CMP_OFF + n_tok, :].astype(BF16), mask_far)
        near = _piece(qg, cmp_ref[kp, pl.ds(near0, PAIR), :].astype(BF16),
                      cmp_ref[vp, pl.ds(near0, PAIR), :].astype(BF16), mask_near, bias_ref[g])
        m, l, acc = _merge(far[:3], near[:3])
        linv = _inv_or_zero(l)
        oc_ref[:, g * 2 * PAIR:(g + 1) * 2 * PAIR] = _scatter_heads(acc * linv * _gate_rows(gate_ref, 0, g), g)
        pf = jnp.sum((far[3] * (jnp.exp(far[0] - m) * linv)).reshape(B_GROUP, nq, n_tok), axis=0)
        pn = jnp.sum((near[3] * (jnp.exp(near[0] - m) * linv)).reshape(B_GROUP, nq, PAIR), axis=0)
        imp = _dot01(pf, o_far) + _dot01(pn, o_near)
        sel_ref[:, g * PAIR:(g + 1) * PAIR] = _select_blocks(imp, qpos)


def _flash_step(st, qx, kx, v, bias=None):
    m_old, l, acc = st
    s = _nt_dot(qx, kx)
    if bias is not None:
        s = s + bias
    m = jnp.maximum(m_old, jnp.max(s, axis=-1, keepdims=True))
    p = jnp.exp(s - m)
    alpha = jnp.exp(m_old - m)
    return m, alpha * l + jnp.sum(p, axis=-1, keepdims=True), alpha * acc + jnp.dot(p.astype(BF16), v, preferred_element_type=F32)


def _nsa_selwin_prompt_kernel(q_ref, gate_ref, sel_ref, oc_ref, kv_ref, oh_ref, ohr_ref, bias_ref, wmask_ref, o_ref):
    qb = pl.program_id(0)
    nq = q_ref.shape[0]
    rows = B_GROUP * nq
    qs = qb * nq
    per_q = nq // SEL_BLOCK
    n_wf = WINDOW - nq
    near_rows = pl.ds(pl.multiple_of(qs + KV_PAD - nq, nq), 2 * nq)
    winfar_rows = pl.ds(pl.multiple_of(qs, nq), n_wf)
    lane = lax.broadcasted_iota(jnp.int32, (rows, PAIR), 1)
    first_near = per_q * (qb - 1)
    first_win = per_q * qb - WINDOW // SEL_BLOCK
    jj = lax.broadcasted_iota(jnp.int32, (PAIR, PAIR), 0)
    bb = lax.broadcasted_iota(jnp.int32, (PAIR, PAIR), 1)
    to_near = jnp.where(jj == first_near + bb, 1.0, 0.0).astype(BF16)
    exists_near = jnp.where(first_near + lane >= 0, 0.0, -MASK_C).astype(BF16)
    exists_win = jnp.where(first_win + lane >= 0, 0.0, -MASK_C).astype(BF16)
    init = (jnp.full((rows, 1), NEG, F32), jnp.zeros((rows, 1), F32), jnp.zeros((rows, PAIR), F32))
    ohr_near = ohr_ref[:2 * nq, :]
    ohr_win = ohr_ref[:n_wf, :]
    for g in range(B_KV_HEADS):
        qg = _group_q(q_ref, g)
        cols = lambda base: slice(base + (g // 2) * PAIR, base + (g // 2 + 1) * PAIR)
        selm = jnp.concatenate([sel_ref[:, g * PAIR:(g + 1) * PAIR]] * B_GROUP, axis=0)
        far_vec = jnp.where((selm > 0.5) & (lane < first_near), 0.0, -MASK_C).astype(BF16)
        near_sel = jnp.dot(selm.astype(BF16), to_near, preferred_element_type=F32)
        near_vec = jnp.where(near_sel > 0.5, 0.0, -MASK_C).astype(BF16) + exists_near
        qx_far = jnp.concatenate([qg, far_vec], axis=1)

        def far_body(t, st):
            r = pl.ds(pl.multiple_of(KV_PAD + t * FAR_TILE, FAR_TILE), FAR_TILE)
            kx = jnp.concatenate([kv_ref[r, cols(0)], oh_ref[r, :]], axis=1)
            return _flash_step(st, qx_far, kx, kv_ref[r, cols(KV_WIDTH)])

        st = lax.fori_loop(0, (qb + 2) // (FAR_TILE // nq), far_body, init)
        st = _flash_step(st, jnp.concatenate([qg, near_vec], axis=1),
                         jnp.concatenate([kv_ref[near_rows, cols(0)], ohr_near], axis=1),
                         kv_ref[near_rows, cols(KV_WIDTH)], bias_ref[g])
        o_s = st[2] * _inv_or_zero(st[1])
        sw = _flash_step(init, jnp.concatenate([qg, exists_win], axis=1),
                         jnp.concatenate([kv_ref[winfar_rows, cols(2 * KV_WIDTH)], ohr_win], axis=1),
                         kv_ref[winfar_rows, cols(3 * KV_WIDTH)], wmask_ref[...])
        sw = _flash_step(sw, jnp.concatenate([qg, exists_near], axis=1),
                         jnp.concatenate([kv_ref[near_rows, cols(2 * KV_WIDTH)], ohr_near], axis=1),
                         kv_ref[near_rows, cols(3 * KV_WIDTH)], bias_ref[g])
        o_w = sw[2] * _inv_or_zero(sw[1])
        o = o_s * _gate_rows(gate_ref, 1, g) + o_w * _gate_rows(gate_ref, 2, g)
        blk = slice(g * 2 * PAIR, (g + 1) * 2 * PAIR)
        o_ref[:, blk] = _scatter_heads(o, g) + oc_ref[:, blk]


def _block_onehots(t):
    pos = np.arange(KV_PAD + t) - KV_PAD
    absolute = (pos[:, None] // SEL_BLOCK == np.arange(PAIR)[None, :]) & (pos[:, None] >= 0)
    relative = np.arange(WINDOW)[:, None] // SEL_BLOCK == np.arange(PAIR)[None, :]
    i = np.arange(B_GROUP * Q_ROWS)[:, None] % Q_ROWS
    in_window = np.arange(WINDOW - Q_ROWS)[None, :] > i
    return (jnp.asarray(absolute, BF16), jnp.asarray(relative, BF16),
            jnp.asarray(np.where(in_window, 0.0, -MASK_C), F32))


def _nsa_prompt(u, kv_pad, cmp_tok, bias_cmp, bias_kv):
    t = u.shape[0]
    nq = Q_ROWS
    q_col = (2 * A_QK + 2 * A_WIDTH) // B_WIDTH
    g_col = (IN_DIM - 3 * B_HEADS) // PAIR
    q_spec = pl.BlockSpec((nq, B_WIDTH), lambda i: (i, q_col))
    gate_spec = pl.BlockSpec((nq, PAIR), lambda i: (i, g_col))
    oc, sel = pl.pallas_call(
        _nsa_cmp_prompt_kernel,
        grid=(t // nq,),
        in_specs=[q_spec, gate_spec,
                  pl.BlockSpec(cmp_tok.shape, lambda i: (0, 0, 0)),
                  pl.BlockSpec(bias_cmp.shape, lambda i: (0, 0, 0))],
        out_specs=[pl.BlockSpec((nq, B_WIDTH), lambda i: (i, 0)), pl.BlockSpec((nq, B_KV_HEADS * PAIR), lambda i: (i, 0))],
        out_shape=[jax.ShapeDtypeStruct((t, B_WIDTH), F32), jax.ShapeDtypeStruct((t, B_KV_HEADS * PAIR), F32)],
        compiler_params=_params("arbitrary"),
        name="nsa_cmp_select_prompt",
    )(u, u, cmp_tok, bias_cmp)
    oh_abs, oh_rel, win_mask = _block_onehots(t)
    whole = lambda a: pl.BlockSpec(a.shape, lambda i: (0,) * a.ndim)
    return pl.pallas_call(
        _nsa_selwin_prompt_kernel,
        grid=(t // nq,),
        in_specs=[q_spec, gate_spec,
                  pl.BlockSpec((nq, B_KV_HEADS * PAIR), lambda i: (i, 0)),
                  pl.BlockSpec((nq, B_WIDTH), lambda i: (i, 0)),
                  whole(kv_pad), whole(oh_abs), whole(oh_rel), whole(bias_kv), whole(win_mask)],
        out_specs=pl.BlockSpec((nq, B_WIDTH), lambda i: (i, 0)),
        out_shape=jax.ShapeDtypeStruct((t, B_WIDTH), F32),
        compiler_params=_params("arbitrary"),
        name="nsa_select_window_prompt",
    )(u, u, sel, oc, kv_pad, oh_abs, oh_rel, bias_kv, win_mask)


SEL_KEYS = PAST_LEN + PAIR
WIN_KEYS = WINDOW + PAIR


NSA_SEQ_BLOCK = 2


def _piece_t(qg, kt, vt, mask, bias):
    s = jnp.dot(qg, kt, preferred_element_type=F32) + bias
    s = jnp.where(mask, s, NEG)
    m = jnp.max(s, axis=-1, keepdims=True)
    p = jnp.where(mask, jnp.exp(s - m), 0.0)
    return m, jnp.sum(p, axis=-1, keepdims=True), _nt_dot(p.astype(BF16), vt), p


def _nsa_sample_kernel(pt_ref, *refs):
    n_pg = NSA_SEQ_BLOCK * N_PAGES
    pages = refs[:n_pg]
    (q_ref, gate_ref, cmp_ref, selnew_ref, winnew_ref, win_ref, bc_ref, bs_ref, bw_ref,
     o_ref, winout_ref, kt_ref, vt_ref, wkt_ref, wvt_ref) = refs[n_pg:]
    for s in range(NSA_SEQ_BLOCK):
        _nsa_sample_one(pages[s * N_PAGES:(s + 1) * N_PAGES], q_ref.at[s], gate_ref.at[s], cmp_ref.at[s],
                        selnew_ref.at[s], winnew_ref.at[s], win_ref.at[s], bc_ref, bs_ref, bw_ref,
                        o_ref.at[s], winout_ref.at[s], kt_ref.at[s], vt_ref.at[s], wkt_ref.at[s], wvt_ref.at[s])


def _nsa_sample_one(pages, q_ref, gate_ref, cmp_ref, selnew_ref, winnew_ref, win_ref, bc_ref, bs_ref, bw_ref,
                    o_ref, winout_ref, kt_ref, vt_ref, wkt_ref, wvt_ref):
    nt = q_ref.shape[0]
    grp = B_GROUP * nt
    rows = B_KV_HEADS * grp
    half = 2 * PAIR
    pad = jnp.zeros((PAIR - nt, 2 * half), F32)
    sel_new_t = jnp.concatenate([selnew_ref[...], pad], axis=0).T
    win_new_t = jnp.concatenate([winnew_ref[...], pad], axis=0).T
    for j, pg in enumerate(pages):
        kt_ref[:, j * PAGE_SIZE:(j + 1) * PAGE_SIZE] = pg[:half, :].astype(BF16)
        vt_ref[:, j * PAGE_SIZE:(j + 1) * PAGE_SIZE] = pg[half:, :].astype(BF16)
    kt_ref[:, PAST_LEN:] = sel_new_t[:half].astype(BF16)
    vt_ref[:, PAST_LEN:] = sel_new_t[half:].astype(BF16)
    buf = win_ref[...]
    wkt_ref[:, :WINDOW] = buf[:half].astype(BF16)
    wvt_ref[:, :WINDOW] = buf[half:].astype(BF16)
    wkt_ref[:, WINDOW:] = win_new_t[:half].astype(BF16)
    wvt_ref[:, WINDOW:] = win_new_t[half:].astype(BF16)
    shifted = pltpu.roll(buf, WINDOW - nt, axis=1)
    tail = pltpu.roll(win_new_t, PAIR - nt, axis=1)
    lane = lax.broadcasted_iota(jnp.int32, tail.shape, 1)
    winout_ref[:, :WINDOW - PAIR] = shifted[:, :WINDOW - PAIR]
    winout_ref[:, WINDOW - PAIR:] = jnp.where(lane >= PAIR - nt, tail, shifted[:, WINDOW - PAIR:])

    zero = jnp.zeros((grp, PAIR), BF16)
    qq = jnp.concatenate(
        [jnp.concatenate([_group_q(q_ref, g), zero] if g // 2 == 0 else [zero, _group_q(q_ref, g)], axis=1)
         for g in range(B_KV_HEADS)], axis=0)
    take = lambda acc, g: acc[g * grp:(g + 1) * grp, (g // 2) * PAIR:(g // 2 + 1) * PAIR]
    qpos = PAST_LEN + lax.broadcasted_iota(jnp.int32, (rows, 1), 0) % nt

    n_tok = cmp_ref.shape[1]
    ck = jnp.concatenate([cmp_ref[0], cmp_ref[1]], axis=1).astype(BF16)
    cv = jnp.concatenate([cmp_ref[2], cmp_ref[3]], axis=1).astype(BF16)
    mask_c = lax.broadcasted_iota(jnp.int32, (rows, n_tok), 1) < n_tok - (N_SUB - 1)
    mc, lc, acc_c, pc = _piece(qq, ck, cv, mask_c, bc_ref[...])
    pc = pc * _inv_or_zero(lc)
    p_all = jnp.concatenate([jnp.sum(pc[g * grp:(g + 1) * grp].reshape(B_GROUP, nt, n_tok), axis=0)
                             for g in range(B_KV_HEADS)], axis=0)
    o01 = _overlap01(lax.broadcasted_iota(jnp.int32, (n_tok, PAIR), 0), lax.broadcasted_iota(jnp.int32, (n_tok, PAIR), 1))
    qpos_gt = PAST_LEN + lax.broadcasted_iota(jnp.int32, (B_KV_HEADS * nt, PAIR), 0) % nt
    sel = _select_blocks(_dot01(p_all, o01), qpos_gt).astype(BF16)
    sel_rows = jnp.concatenate([sel[g * nt:(g + 1) * nt] for g in range(B_KV_HEADS) for _ in range(B_GROUP)], axis=0)

    ks = lax.broadcasted_iota(jnp.int32, (rows, SEL_KEYS), 1)
    mask_s = (_expand_blocks(sel_rows, 0, SEL_KEYS) > 0.5) & (ks <= qpos)
    ms, ls, acc_s, _ = _piece_t(qq, kt_ref[...], vt_ref[...], mask_s, bs_ref[...])
    kw = PAST_LEN - WINDOW + lax.broadcasted_iota(jnp.int32, (rows, WIN_KEYS), 1)
    mask_w = (kw <= qpos) & (qpos - kw < WINDOW)
    mw, lw, acc_w, _ = _piece_t(qq, wkt_ref[...], wvt_ref[...], mask_w, bw_ref[...])
    acc_c, acc_s, acc_w = acc_c * _inv_or_zero(lc), acc_s * _inv_or_zero(ls), acc_w * _inv_or_zero(lw)
    for g in range(B_KV_HEADS):
        o = (take(acc_c, g) * _gate_rows(gate_ref, 0, g) + take(acc_s, g) * _gate_rows(gate_ref, 1, g)
             + take(acc_w, g) * _gate_rows(gate_ref, 2, g))
        o_ref[:, g * 2 * PAIR:(g + 1) * 2 * PAIR] = _scatter_heads(o, g)


def _nsa_sample(u_s, cmp_tok, cache_sel, win_buf, page_table, bias_c, bias_s, bias_w):
    n_seq, nt, _ = u_s.shape
    sb = NSA_SEQ_BLOCK
    q_col = (2 * A_QK + 2 * A_WIDTH) // B_WIDTH
    kv_col = (2 * A_QK + 2 * A_WIDTH + B_WIDTH) // (4 * PAIR)
    g_col = (IN_DIM - 3 * B_HEADS) // PAIR
    const = lambda a: pl.BlockSpec(a.shape, lambda b, pt: (0, 0))
    in_specs = [pl.BlockSpec((None, 4 * PAIR, PAGE_SIZE), lambda b, pt, s=s, j=j: (pt[b * sb + s, j], 0, 0))
                for s in range(sb) for j in range(N_PAGES)]
    in_specs += [
        pl.BlockSpec((sb, nt, B_WIDTH), lambda b, pt: (b, 0, q_col)),
        pl.BlockSpec((sb, nt, PAIR), lambda b, pt: (b, 0, g_col)),
        pl.BlockSpec((sb,) + cmp_tok.shape[1:], lambda b, pt: (b, 0, 0, 0)),
        pl.BlockSpec((sb, nt, 4 * PAIR), lambda b, pt: (b, 0, kv_col + 1)),
        pl.BlockSpec((sb, nt, 4 * PAIR), lambda b, pt: (b, 0, kv_col + 2)),
        pl.BlockSpec((sb, 4 * PAIR, WINDOW), lambda b, pt: (b, 0, 0)),
        const(bias_c), const(bias_s), const(bias_w),
    ]
    return pl.pallas_call(
        _nsa_sample_kernel,
        grid_spec=pltpu.PrefetchScalarGridSpec(
            num_scalar_prefetch=1,
            grid=(n_seq // sb,),
            in_specs=in_specs,
            out_specs=[pl.BlockSpec((sb, nt, B_WIDTH), lambda b, pt: (b, 0, 0)),
                       pl.BlockSpec((sb, 4 * PAIR, WINDOW), lambda b, pt: (b, 0, 0))],
            scratch_shapes=[pltpu.VMEM((sb, 2 * PAIR, SEL_KEYS), BF16), pltpu.VMEM((sb, 2 * PAIR, SEL_KEYS), BF16),
                            pltpu.VMEM((sb, 2 * PAIR, WIN_KEYS), BF16), pltpu.VMEM((sb, 2 * PAIR, WIN_KEYS), BF16)],
        ),
        out_shape=[jax.ShapeDtypeStruct((n_seq, nt, B_WIDTH), F32), jax.ShapeDtypeStruct(win_buf.shape, F32)],
        compiler_params=_params("arbitrary"),
        name="nsa_sample",
    )(page_table, *([cache_sel] * (sb * N_PAGES)), u_s, u_s, cmp_tok, u_s, u_s, win_buf, bias_c, bias_s, bias_w)


PROMPT_TM = 1024
PROJ_TN = 512
COL_CMP = 2 * A_QK + 2 * A_WIDTH + B_WIDTH
COL_SEL = COL_CMP + 2 * KV_WIDTH
COL_WIN = COL_SEL + 2 * KV_WIDTH
COL_GATE = COL_WIN + 2 * KV_WIDTH


def _time_major(a):
    return a.transpose(1, 0, 2).reshape(a.shape[0] * a.shape[1], a.shape[2])


def _seq_major(a2d, n_seq):
    return a2d.reshape(a2d.shape[0] // n_seq, n_seq, a2d.shape[1]).transpose(1, 0, 2)


def _kv_rows(u3, col):
    return u3[..., col:col + 2 * KV_WIDTH].reshape(u3.shape[:-1] + (2, B_KV_HEADS, B_DH))


def kernel(x_prompt, x_sample, cache_cmp, cache_sel, state_win, state_hgrn, state_conv, page_table, c_prompt, c_sample, norm_g, ada_w, ada_b, ffn_w_gate, ffn_w_up, ffn_w_down, w_in_even, hgrn_lower_bound, hgrn_norm_g, cmp_pe, cmp_w1, cmp_b1, cmp_w2, cmp_b2, rel_bias, w_out_even, conv_w_pw1, conv_b_pw1, conv_w_dw, conv_b_dw, conv_ln_g, conv_ln_b, conv_w_pw2, conv_b_pw2, final_norm_g):
    P = {'norm_g': norm_g, 'ffn_w_gate': ffn_w_gate, 'ffn_w_up': ffn_w_up, 'ffn_w_down': ffn_w_down,
         'final_norm_g': final_norm_g}
    n_seq = x_sample.shape[0]
    n_pool = cache_cmp.shape[1]
    rows_p, rows_s = _prompt_rows(PROMPT_TM), _sample_rows()
    g4 = norm_g.reshape(DEPTH, 3, 1, D_MODEL)
    c_all = jnp.concatenate([c_sample, jnp.tile(c_prompt, (SUBLANES, 1))], axis=0)
    mod = _ada_mod(c_all, ada_w, ada_b)
    bias_pc, bias_pkv, bias_sc, bias_ss, bias_sw = _bias_tables(rel_bias)

    xp = x_prompt.reshape(SEQ, D_MODEL)
    xs = _time_major(x_sample)
    cmp_p, cmp_s, sel_p, sel_s, win_p, win_s, hgrn_p, hgrn_s, conv_p, conv_s = ([] for _ in range(10))
    for l in range(DEPTH):
        i = l // 2
        last = l == DEPTH - 1
        xs, w_bf16 = _ffn(xs, rows_s, mod, l, 0, P)
        xp, _ = _ffn(xp, rows_p, mod, l, 0, P, w_bf16=w_bf16)
        if l % 2 == 0:
            up = _proj(xp, rows_p, mod, l, g4, w_in_even[i:i + 1], IN_DIM, PROJ_TN)
            us = _seq_major(_proj(xs, rows_s, mod, l, g4, w_in_even[i:i + 1], IN_DIM, PROJ_TN), n_seq)
            gn = hgrn_norm_g[i:i + 1]
            oa_p, hp = _hgrn(up, hgrn_lower_bound, gn, l)
            oa_s, hs = _hgrn(us.reshape(n_seq * DEC_SEQ, IN_DIM), hgrn_lower_bound, gn, l, state_hgrn[i])
            weights = _cmp_weights(cmp_pe[i], cmp_w1[i], cmp_b1[i], cmp_w2[i], cmp_b2[i])
            kv_pad = jnp.pad(up[:, COL_SEL:COL_GATE].astype(BF16), ((KV_PAD, 0), (0, 0)))
            ob_p = _nsa_prompt(up, kv_pad, _compress_prompt(up, weights), bias_pc, bias_pkv)
            cmp_t = cache_cmp[i].transpose(0, 2, 3, 4, 1).reshape(n_pool, 4 * PAIR, PAGE_SIZE)
            cmp_tok_s = _compress_sample(cmp_t, page_table, weights)
            sel_t = cache_sel[i].transpose(0, 2, 3, 4, 1).reshape(n_pool, 4 * PAIR, PAGE_SIZE)
            win_t = state_win[i].transpose(0, 2, 3, 4, 1).reshape(n_seq, 4 * PAIR, WINDOW)
            ob_s, wn = _nsa_sample(us, cmp_tok_s, sel_t, win_t, page_table, bias_sc, bias_ss, bias_sw)
            xp = _out_proj([oa_p, ob_p], w_out_even[i:i + 1], [0, A_WIDTH], None, xp, rows_p, mod, l, PROJ_TN)
            xs = _out_proj([_time_major(oa_s.reshape(n_seq, DEC_SEQ, A_WIDTH)), _time_major(ob_s)],
                           w_out_even[i:i + 1], [0, A_WIDTH], None, xs, rows_s, mod, l, PROJ_TN)
            up3 = up[None]
            cmp_p.append(_kv_rows(up3, COL_CMP))
            sel_p.append(_kv_rows(up3, COL_SEL))
            win_p.append(_kv_rows(up3[:, SEQ - min(WINDOW, SEQ):], COL_WIN))
            cmp_s.append(_kv_rows(us, COL_CMP))
            sel_s.append(_kv_rows(us, COL_SEL))
            win_s.append(wn.reshape(n_seq, 2, B_KV_HEADS, B_DH, WINDOW).transpose(0, 4, 1, 2, 3))
            hgrn_p.append(hp[None])
            hgrn_s.append(hs)
        else:
            b_pw1 = conv_b_pw1[i].reshape(1, 1, 2 * CONV_DIM)
            b_pw2 = conv_b_pw2[i].reshape(1, 1, D_MODEL)
            vec = lambda a: a[i].reshape(1, CONV_DIM)
            glu_p = _glu_proj(xp, rows_p, mod, l, g4, conv_w_pw1[i:i + 1], b_pw1, PROJ_TN)
            act_p = _conv_prompt(glu_p, conv_w_dw[i], vec(conv_b_dw), vec(conv_ln_g), vec(conv_ln_b))
            xp = _out_proj([act_p], conv_w_pw2[i:i + 1], [0], b_pw2, xp, rows_p, mod, l, PROJ_TN)
            glu_s = _glu_proj(xs, rows_s, mod, l, g4, conv_w_pw1[i:i + 1], b_pw1, PROJ_TN)
            act_s, nb = _conv_sample(glu_s.reshape(DEC_SEQ, n_seq, CONV_DIM), state_conv[i].transpose(1, 0, 2),
                                     conv_w_dw[i], vec(conv_b_dw), vec(conv_ln_g), vec(conv_ln_b))
            xs = _out_proj([act_s.reshape(DEC_SEQ * n_seq, CONV_DIM)], conv_w_pw2[i:i + 1], [0], b_pw2, xs, rows_s, mod, l, PROJ_TN)
            conv_p.append(glu_p[None, SEQ - (CONV_WIDTH - 1):])
            conv_s.append(nb.transpose(1, 0, 2))
        xs, w_bf16 = _ffn(xs, rows_s, mod, l, 2, P, final_norm=last)
        xp, _ = _ffn(xp, rows_p, mod, l, 2, P, w_bf16=w_bf16, final_norm=last)
    y_prompt = xp.reshape(1, SEQ, D_MODEL)
    y_sample = _seq_major(xs, n_seq)
    st = jnp.stack
    return (y_prompt, y_sample, st(cmp_p), st(cmp_s), st(sel_p), st(sel_s), st(win_p), st(win_s),
            st(hgrn_p), st(hgrn_s), st(conv_p), st(conv_s))
```

```python
import functools
import math

import numpy as np
import jax
import jax.numpy as jnp
from jax import lax
from jax.experimental import pallas as pl
from jax.experimental.pallas import tpu as pltpu

F32 = jnp.float32
BF16 = jnp.bfloat16

D_MODEL = 2048
SEQ = 8192
DEPTH = 2
DEC_BATCH = 128
DEC_SEQ = 8
PAST_LEN = 2048
PAGE_SIZE = 128
N_MOD = 9
D_FF = 5504
EPS = 1e-6
A_HEADS = 8
A_DK = 128
A_DV = 128
A_QK = A_HEADS * A_DK
A_WIDTH = A_HEADS * A_DV
B_HEADS = 16
B_KV_HEADS = 4
B_DH = 64
B_GROUP = B_HEADS // B_KV_HEADS
B_WIDTH = B_HEADS * B_DH
KV_WIDTH = B_KV_HEADS * B_DH
CMP_BLOCK = 32
CMP_STRIDE = 16
CMP_HIDDEN = 256
SEL_BLOCK = 64
N_SELECT = 16
WINDOW = 512
N_BUCKETS = 32
MAX_DISTANCE = 128
MIX_WIDTH = A_WIDTH + B_WIDTH
IN_DIM = 2 * A_QK + 2 * A_WIDTH + B_WIDTH + 6 * KV_WIDTH + 3 * B_HEADS
CONV_WIDTH = 31
CONV_DIM = D_MODEL
NEG = -1e30
FORCE = 1e9

V7X_VMEM_LIMIT_BYTES = 60 * 1024 * 1024
SUBLANES = 8
LANES = 128

N_SEQ_ROWS = DEC_BATCH + SUBLANES
PROMPT_ROW_BLOCK = DEC_BATCH // SUBLANES


def _params(*sem):
    return pltpu.CompilerParams(dimension_semantics=sem, vmem_limit_bytes=V7X_VMEM_LIMIT_BYTES)


def _silu(x):
    return x * jax.nn.sigmoid(x)


def _bdot(a, b):
    return jnp.dot(a.astype(BF16), b.astype(BF16), preferred_element_type=F32)


def _ada_kernel(c_ref, w_ref, b_ref, o_ref):
    o_ref[...] = _bdot(_silu(c_ref[...]), w_ref[...]) + b_ref[...]


def _ada_mod(c_all, ada_w, ada_b):
    n = c_all.shape[0]
    return pl.pallas_call(
        _ada_kernel,
        grid=(DEPTH, N_MOD),
        in_specs=[
            pl.BlockSpec((n, D_MODEL), lambda l, k: (0, 0)),
            pl.BlockSpec((None, D_MODEL, D_MODEL), lambda l, k: (l, 0, k)),
            pl.BlockSpec((None, None, 1, D_MODEL), lambda l, k: (l, k, 0, 0)),
        ],
        out_specs=pl.BlockSpec((None, None, n, D_MODEL), lambda l, k: (l, k, 0, 0)),
        out_shape=jax.ShapeDtypeStruct((DEPTH, N_MOD, n, D_MODEL), F32),
        compiler_params=_params("arbitrary", "arbitrary"),
        name="ada_mod",
    )(c_all, ada_w, ada_b.reshape(DEPTH, N_MOD, 1, D_MODEL))


def _norm_mod(x, g, shift, scale):
    ms = jnp.mean(x * x, axis=-1, keepdims=True)
    y = x * lax.rsqrt(ms + EPS) * g
    h = y * (1.0 + scale) + shift
    return h.reshape(x.shape[0] * x.shape[1], x.shape[2]).astype(BF16)


class _Rows:
    def __init__(self, m, bs, nt, seq_block):
        assert m % (bs * nt) == 0
        self.m, self.bs, self.nt, self.seq_block = m, bs, nt, seq_block
        self.tm = bs * nt
        self.n_tiles = m // self.tm

    def view(self, x2d):
        return x2d.reshape(self.m // self.bs, self.bs, x2d.shape[-1])

    def x_spec(self, width, col=lambda j: 0, single_buffer=False):
        mode = dict(pipeline_mode=pl.Buffered(1)) if single_buffer else {}
        return pl.BlockSpec((self.nt, self.bs, width), lambda i, j: (i, 0, col(j)), **mode)

    def mod_spec(self, layer, k, width=D_MODEL, col=lambda j: 0):
        sb = self.seq_block
        return pl.BlockSpec((None, None, self.bs, width), lambda i, j: (layer, k, sb, col(j)))


def _prompt_rows(tm):
    return _Rows(SEQ, SUBLANES, tm // SUBLANES, PROMPT_ROW_BLOCK)


def _sample_rows():
    return _Rows(DEC_BATCH * DEC_SEQ, DEC_BATCH, DEC_SEQ, 0)


FFN_TF_F32 = 256
FFN_TF_BF16 = 512
FFN_ACC_CHUNKS = 4


def _ffn_kernel(x_ref, sh_ref, sc_ref, gt_ref, g_ref, wg_ref, wu_ref, wd_ref, fg_ref, o_ref, *rest, final_norm, emit):
    if emit:
        wg_o, wu_o, wd_o, h_ref = rest
    else:
        (h_ref,) = rest
    j = pl.program_id(1)
    nj = pl.num_programs(1)
    tf = wg_ref.shape[1]

    nt, bs = o_ref.shape[0], o_ref.shape[1]
    step = max(nt // FFN_ACC_CHUNKS, 1)

    @pl.when(j == 0)
    def _():
        for r in range(0, nt, step):
            h_ref[r * bs:(r + step) * bs, :] = _norm_mod(x_ref[r:r + step], g_ref[...], sh_ref[...], sc_ref[...])
        o_ref[...] = jnp.zeros_like(o_ref)

    wg, wu, wd = wg_ref[...].astype(BF16), wu_ref[...].astype(BF16), wd_ref[...].astype(BF16)
    if emit:
        wg_o[...], wu_o[...], wd_o[...] = wg, wu, wd
    valid = D_FF - j * tf
    h = h_ref[...]
    a = _silu(_bdot(h, wg)) * _bdot(h, wu)
    col = lax.broadcasted_iota(jnp.int32, a.shape, 1)
    a = jnp.where(col < valid, a, 0.0)
    row = lax.broadcasted_iota(jnp.int32, wd.shape, 0)
    wd = jnp.where(row < valid, wd, jnp.zeros_like(wd))
    a = a.astype(BF16)
    for r in range(0, nt, step):
        o_ref[r:r + step] += _bdot(a[r * bs:(r + step) * bs], wd).reshape(step, bs, o_ref.shape[2])

    @pl.when(j == nj - 1)
    def _():
        for r in range(0, nt, step):
            y = x_ref[r:r + step] + (0.5 * gt_ref[...]) * o_ref[r:r + step]
            if final_norm:
                ms = jnp.mean(y * y, axis=-1, keepdims=True)
                y = y * lax.rsqrt(ms + EPS) * fg_ref[...]
            o_ref[r:r + step] = y


def _ffn(x2d, rows, mod, layer, sub, P, w_bf16=None, final_norm=False):
    half = sub // 2
    emit = w_bf16 is None
    tf = FFN_TF_F32 if emit else FFN_TF_BF16
    g_norm = P['norm_g'].reshape(DEPTH, 3, 1, D_MODEL)
    fg = P['final_norm_g'].reshape(1, D_MODEL)
    if emit:
        weights = (P['ffn_w_gate'], P['ffn_w_up'], P['ffn_w_down'])
        w_specs = [pl.BlockSpec((None, None, D_MODEL, tf), lambda i, j: (layer, half, 0, j)),
                   pl.BlockSpec((None, None, D_MODEL, tf), lambda i, j: (layer, half, 0, j)),
                   pl.BlockSpec((None, None, tf, D_MODEL), lambda i, j: (layer, half, j, 0))]
    else:
        weights = w_bf16
        w_specs = [pl.BlockSpec((D_MODEL, tf), lambda i, j: (0, j)),
                   pl.BlockSpec((D_MODEL, tf), lambda i, j: (0, j)),
                   pl.BlockSpec((tf, D_MODEL), lambda i, j: (j, 0))]
    out_specs = [rows.x_spec(D_MODEL, single_buffer=emit)]
    out_shape = [jax.ShapeDtypeStruct((rows.m // rows.bs, rows.bs, D_MODEL), F32)]
    if emit:
        out_specs += [pl.BlockSpec((D_MODEL, tf), lambda i, j: (0, j)),
                      pl.BlockSpec((D_MODEL, tf), lambda i, j: (0, j)),
                      pl.BlockSpec((tf, D_MODEL), lambda i, j: (j, 0))]
        out_shape += [jax.ShapeDtypeStruct((D_MODEL, D_FF), BF16), jax.ShapeDtypeStruct((D_MODEL, D_FF), BF16),
                      jax.ShapeDtypeStruct((D_FF, D_MODEL), BF16)]
        assert rows.n_tiles == 1
    outs = pl.pallas_call(
        functools.partial(_ffn_kernel, final_norm=final_norm, emit=emit),
        grid=(rows.n_tiles, pl.cdiv(D_FF, tf)),
        in_specs=[
            rows.x_spec(D_MODEL, single_buffer=True),
            rows.mod_spec(layer, 3 * sub), rows.mod_spec(layer, 3 * sub + 1), rows.mod_spec(layer, 3 * sub + 2),
            pl.BlockSpec((None, None, 1, D_MODEL), lambda i, j: (layer, sub, 0, 0)),
            *w_specs,
            pl.BlockSpec((1, D_MODEL), lambda i, j: (0, 0)),
        ],
        out_specs=out_specs,
        out_shape=out_shape,
        scratch_shapes=[pltpu.VMEM((rows.tm, D_MODEL), BF16)],
        compiler_params=_params("arbitrary", "arbitrary"),
        name="ffn_half_step",
    )(rows.view(x2d), mod, mod, mod, g_norm, *weights, fg)
    return outs[0].reshape(rows.m, D_MODEL), tuple(outs[1:])


def _proj_kernel(x_ref, sh_ref, sc_ref, g_ref, w_ref, o_ref, h_ref):
    @pl.when(pl.program_id(1) == 0)
    def _():
        h_ref[...] = _norm_mod(x_ref[...], g_ref[...], sh_ref[...], sc_ref[...])

    o_ref[...] = _bdot(h_ref[...], w_ref[...])


def _proj(x2d, rows, mod, layer, g_norm4, w3, n_out, tn):
    return pl.pallas_call(
        _proj_kernel,
        grid=(rows.n_tiles, pl.cdiv(n_out, tn)),
        in_specs=[
            rows.x_spec(D_MODEL),
            rows.mod_spec(layer, 3), rows.mod_spec(layer, 4),
            pl.BlockSpec((None, None, 1, D_MODEL), lambda i, j: (layer, 1, 0, 0)),
            pl.BlockSpec((None, D_MODEL, tn), lambda i, j: (0, 0, j)),
        ],
        out_specs=pl.BlockSpec((rows.tm, tn), lambda i, j: (i, j)),
        out_shape=jax.ShapeDtypeStruct((rows.m, n_out), F32),
        scratch_shapes=[pltpu.VMEM((rows.tm, D_MODEL), BF16)],
        compiler_params=_params("arbitrary", "arbitrary"),
        name="prenorm_proj",
    )(rows.view(x2d), mod, mod, g_norm4, w3)


def _glu_proj_kernel(x_ref, sh_ref, sc_ref, g_ref, wa_ref, wg_ref, ba_ref, bg_ref, o_ref, h_ref):
    @pl.when(pl.program_id(1) == 0)
    def _():
        h_ref[...] = _norm_mod(x_ref[...], g_ref[...], sh_ref[...], sc_ref[...])

    h = h_ref[...]
    a = _bdot(h, wa_ref[...]) + ba_ref[...]
    gt = _bdot(h, wg_ref[...]) + bg_ref[...]
    o_ref[...] = a * jax.nn.sigmoid(gt)


def _glu_proj(x2d, rows, mod, layer, g_norm4, w3, b3, tn):
    nb = CONV_DIM // tn
    return pl.pallas_call(
        _glu_proj_kernel,
        grid=(rows.n_tiles, nb),
        in_specs=[
            rows.x_spec(D_MODEL),
            rows.mod_spec(layer, 3), rows.mod_spec(layer, 4),
            pl.BlockSpec((None, None, 1, D_MODEL), lambda i, j: (layer, 1, 0, 0)),
            pl.BlockSpec((None, D_MODEL, tn), lambda i, j: (0, 0, j)),
            pl.BlockSpec((None, D_MODEL, tn), lambda i, j: (0, 0, j + nb)),
            pl.BlockSpec((None, 1, tn), lambda i, j: (0, 0, j)),
            pl.BlockSpec((None, 1, tn), lambda i, j: (0, 0, j + nb)),
        ],
        out_specs=pl.BlockSpec((rows.tm, tn), lambda i, j: (i, j)),
        out_shape=jax.ShapeDtypeStruct((rows.m, CONV_DIM), F32),
        scratch_shapes=[pltpu.VMEM((rows.tm, D_MODEL), BF16)],
        compiler_params=_params("arbitrary", "arbitrary"),
        name="prenorm_glu_proj",
    )(rows.view(x2d), mod, mod, g_norm4, w3, w3, b3, b3)


def _out_kernel(*refs, n_in, has_bias):
    a_refs = refs[:n_in]
    w_refs = refs[n_in:2 * n_in]
    pos = 2 * n_in
    b_ref = refs[pos] if has_bias else None
    pos += int(has_bias)
    x_ref, gt_ref, o_ref = refs[pos:pos + 3]
    y = _bdot(a_refs[0][...], w_refs[0][...])
    for a_ref, w_ref in zip(a_refs[1:], w_refs[1:]):
        y += _bdot(a_ref[...], w_ref[...])
    if has_bias:
        y += b_ref[...]
    o_ref[...] = x_ref[...] + gt_ref[...] * y.reshape(o_ref.shape)


def _out_proj(acts, w3, k_offsets, bias3, x2d, rows, mod, layer, tn):
    n_in = len(acts)
    in_specs = [pl.BlockSpec((rows.tm, a.shape[1]), lambda i, j: (i, 0)) for a in acts]
    for a, off in zip(acts, k_offsets):
        kb = off // a.shape[1]
        in_specs.append(pl.BlockSpec((None, a.shape[1], tn), lambda i, j, kb=kb: (0, kb, j)))
    args = list(acts) + [w3] * n_in
    if bias3 is not None:
        in_specs.append(pl.BlockSpec((None, 1, tn), lambda i, j: (0, 0, j)))
        args.append(bias3)
    in_specs += [rows.x_spec(tn, col=lambda j: j), rows.mod_spec(layer, 5, width=tn, col=lambda j: j)]
    args += [rows.view(x2d), mod]
    out = pl.pallas_call(
        functools.partial(_out_kernel, n_in=n_in, has_bias=bias3 is not None),
        grid=(rows.n_tiles, D_MODEL // tn),
        in_specs=in_specs,
        out_specs=rows.x_spec(tn, col=lambda j: j),
        out_shape=jax.ShapeDtypeStruct((rows.m // rows.bs, rows.bs, D_MODEL), F32),
        compiler_params=_params("arbitrary", "arbitrary"),
        name="out_proj_residual",
    )(*args)
    return out.reshape(rows.m, D_MODEL)


HGRN_ROWS = 512
HGRN_HEADS = 4
HGRN_TRI = 128


def _hgrn_prepare(f_ref, lb_ref, cum_s, kk_s, *, layer, c):
    p = lb_ref[...]
    e = jnp.exp(p - jnp.max(p, axis=0, keepdims=True))
    sm = e / jnp.sum(e, axis=0, keepdims=True)
    lb = jnp.sum(sm[:layer + 1], axis=0, keepdims=True)
    f = lb + (1.0 - lb) * jax.nn.sigmoid(f_ref[...])
    lf = jnp.log(f)
    n = HGRN_TRI
    r = lax.broadcasted_iota(jnp.int32, (n, n), 0)
    s = lax.broadcasted_iota(jnp.int32, (n, n), 1)
    tri = jnp.where((s <= r) & (s // c == r // c), 1.0, 0.0).astype(F32)
    for b in range(f.shape[0] // n):
        cum_s[b * n:(b + 1) * n, :] = jnp.dot(tri, lf[b * n:(b + 1) * n], preferred_element_type=F32,
                                              precision=lax.Precision.HIGHEST)
    kk_s[...] = 1.0 - f


def _hgrn_subchunk(r0, c, st, hh, q_ref, v_ref, g_ref, gn_ref, cum_s, kk_s):
    rows = pl.ds(r0, c)
    cols = slice(hh * A_DK, (hh + 1) * A_DK)
    cum = cum_s[rows, cols]
    q = q_ref[rows, cols]
    kk = kk_s[rows, cols]
    vv = v_ref[rows, cols]
    last = cum[c - 1:c, :]
    o = lax.dot_general((q * jnp.exp(cum)).astype(BF16), st.astype(BF16), (((1,), (1,)), ((), ())),
                        preferred_element_type=F32)
    srow = lax.broadcasted_iota(jnp.int32, (c, A_DK), 0)
    xs = []
    for t in range(c):
        d = jnp.where(srow <= t, cum[t:t + 1, :] - cum, NEG)
        xs.append(jnp.exp(d) * (q[t:t + 1, :] * kk))
    x = jnp.concatenate(xs, axis=0).astype(BF16)
    w = jnp.dot(x, jnp.ones((A_DK, A_DV), BF16), preferred_element_type=F32)
    o = o + jnp.sum(w.reshape(c, c, A_DV) * vv[None], axis=1)
    ke = kk * jnp.exp(last - cum)
    st_new = st * jnp.exp(last) + lax.dot_general(vv.astype(BF16), ke.astype(BF16), (((0,), (0,)), ((), ())),
                                                  preferred_element_type=F32)
    ms = jnp.mean(o * o, axis=-1, keepdims=True)
    y = o * lax.rsqrt(ms + EPS) * gn_ref[:, cols] * _silu(g_ref[rows, cols])
    return y, st_new


def _hgrn_prompt_kernel(q_ref, f_ref, v_ref, g_ref, lb_ref, gn_ref, o_ref, s_ref, st_ref, cum_s, kk_s, *, layer, c):
    i = pl.program_id(1)

    @pl.when(i == 0)
    def _():
        st_ref[...] = jnp.zeros_like(st_ref)

    _hgrn_prepare(f_ref, lb_ref, cum_s, kk_s, layer=layer, c=c)

    def body(n, carry):
        r0 = pl.multiple_of(n * c, c)
        for hh in range(HGRN_HEADS):
            y, st_new = _hgrn_subchunk(r0, c, st_ref[hh], hh, q_ref, v_ref, g_ref, gn_ref, cum_s, kk_s)
            st_ref[hh] = st_new
            o_ref[pl.ds(r0, c), hh * A_DV:(hh + 1) * A_DV] = y
        return carry

    lax.fori_loop(0, q_ref.shape[0] // c, body, 0)

    @pl.when(i == pl.num_programs(1) - 1)
    def _():
        for hh in range(HGRN_HEADS):
            s_ref[hh] = st_ref[hh].T


def _hgrn_sample_kernel(q_ref, f_ref, v_ref, g_ref, lb_ref, gn_ref, s0_ref, o_ref, s_ref, cum_s, kk_s, *, layer, c):
    _hgrn_prepare(f_ref, lb_ref, cum_s, kk_s, layer=layer, c=c)

    def body(n, carry):
        r0 = pl.multiple_of(n * c, c)
        for hh in range(HGRN_HEADS):
            y, st_new = _hgrn_subchunk(r0, c, s0_ref[n, hh].T, hh, q_ref, v_ref, g_ref, gn_ref, cum_s, kk_s)
            s_ref[n, hh] = st_new.T
            o_ref[pl.ds(r0, c), hh * A_DV:(hh + 1) * A_DV] = y
        return carry

    lax.fori_loop(0, q_ref.shape[0] // c, body, 0)


def _hgrn(u, lower_bound, norm_g2, layer, s0=None):
    m = u.shape[0]
    hb = HGRN_HEADS
    w = hb * A_DK
    nb = A_QK // w
    tc = HGRN_ROWS if s0 is None else HGRN_ROWS // 2

    def col(seg):
        return pl.BlockSpec((tc, w), lambda h, i, seg=seg: (i, seg * nb + h))

    in_specs = [col(0), col(1), col(2), col(3),
                pl.BlockSpec((DEPTH + 1, w), lambda h, i: (0, h)),
                pl.BlockSpec((1, w), lambda h, i: (0, h))]
    o_spec = pl.BlockSpec((tc, w), lambda h, i: (i, h))
    scratch = [pltpu.VMEM((tc, w), F32), pltpu.VMEM((tc, w), F32)]
    if s0 is None:
        return pl.pallas_call(
            functools.partial(_hgrn_prompt_kernel, layer=layer, c=16),
            grid=(nb, m // tc),
            in_specs=in_specs,
            out_specs=[o_spec, pl.BlockSpec((hb, A_DK, A_DV), lambda h, i: (h, 0, 0))],
            out_shape=[jax.ShapeDtypeStruct((m, A_WIDTH), F32), jax.ShapeDtypeStruct((A_HEADS, A_DK, A_DV), F32)],
            scratch_shapes=[pltpu.VMEM((hb, A_DV, A_DK), F32)] + scratch,
            compiler_params=_params("arbitrary", "arbitrary"),
            name="hgrn2_prompt",
        )(u, u, u, u, lower_bound, norm_g2)
    c = DEC_SEQ
    ns = tc // c
    s_spec = pl.BlockSpec((ns, hb, A_DK, A_DV), lambda h, i: (i, h, 0, 0))
    return pl.pallas_call(
        functools.partial(_hgrn_sample_kernel, layer=layer, c=c),
        grid=(nb, m // tc),
        in_specs=in_specs + [s_spec],
        out_specs=[o_spec, s_spec],
        out_shape=[jax.ShapeDtypeStruct((m, A_WIDTH), F32), jax.ShapeDtypeStruct(s0.shape, F32)],
        scratch_shapes=scratch,
        compiler_params=_params("arbitrary", "arbitrary"),
        name="hgrn2_sample",
    )(u, u, u, u, lower_bound, norm_g2, s0)


CONV_HALO = 32
CONV_ROWS = 256


def _ln_silu(y, g, b):
    mu = jnp.mean(y, axis=-1, keepdims=True)
    yc = y - mu
    var = jnp.mean(yc * yc, axis=-1, keepdims=True)
    return _silu(yc * lax.rsqrt(var + EPS) * g + b)


CONV_CHUNK_ROWS = 64
CONV_CHUNK_COLS = 512


def _conv_prompt_kernel(cur_ref, halo_ref, w_ref, b_ref, g_ref, lb_ref, o_ref, ext_ref, ph_ref, y_ref):
    i = pl.program_id(0)
    tt = cur_ref.shape[0]
    ext_ref[0:CONV_HALO, :] = jnp.where(i > 0, halo_ref[...], 0.0)
    ext_ref[CONV_HALO:, :] = cur_ref[...]
    off = CONV_HALO - (CONV_WIDTH - 1)
    span = ph_ref.shape[1]
    for r in range(1, SUBLANES):
        ph_ref[r - 1] = ext_ref[r:r + span, :]
    for c0 in range(0, CONV_DIM, CONV_CHUNK_COLS):
        cs = slice(c0, c0 + CONV_CHUNK_COLS)
        for r0 in range(0, tt, CONV_CHUNK_ROWS):
            y = jnp.zeros((CONV_CHUNK_ROWS, CONV_CHUNK_COLS), F32) + b_ref[:, cs]
            for w in range(CONV_WIDTH):
                a, r = divmod(off + w, SUBLANES)
                src = ext_ref if r == 0 else ph_ref.at[r - 1]
                lo = SUBLANES * a + r0
                y = y + src[lo:lo + CONV_CHUNK_ROWS, cs] * w_ref[w:w + 1, cs]
            y_ref[r0:r0 + CONV_CHUNK_ROWS, cs] = y
    o_ref[...] = _ln_silu(y_ref[...], g_ref[...], lb_ref[...]).astype(o_ref.dtype)


def _conv_prompt(glu, w_dw, b_dw, ln_g, ln_b):
    t = glu.shape[0]
    tt = CONV_ROWS
    r = tt // CONV_HALO
    vec = pl.BlockSpec((1, CONV_DIM), lambda i: (0, 0))
    return pl.pallas_call(
        _conv_prompt_kernel,
        grid=(t // tt,),
        in_specs=[
            pl.BlockSpec((tt, CONV_DIM), lambda i: (i, 0)),
            pl.BlockSpec((CONV_HALO, CONV_DIM), lambda i: (jnp.maximum(i * r - 1, 0), 0)),
            pl.BlockSpec((CONV_WIDTH, CONV_DIM), lambda i: (0, 0)),
            vec, vec, vec,
        ],
        out_specs=pl.BlockSpec((tt, CONV_DIM), lambda i: (i, 0)),
        out_shape=jax.ShapeDtypeStruct((t, CONV_DIM), BF16),
        scratch_shapes=[pltpu.VMEM((CONV_HALO + tt, CONV_DIM), F32),
                        pltpu.VMEM((SUBLANES - 1, CONV_HALO + tt - SUBLANES, CONV_DIM), F32),
                        pltpu.VMEM((tt, CONV_DIM), F32)],
        compiler_params=_params("arbitrary"),
        name="conv_prompt",
    )(glu, glu, w_dw, b_dw, ln_g, ln_b)


def _conv_sample_kernel(u_ref, buf_ref, w_ref, b_ref, g_ref, lb_ref, o_ref, nb_ref):
    hist = CONV_WIDTH - 1
    nt = u_ref.shape[0]

    def ext(j):
        return buf_ref[j] if j < hist else u_ref[j - hist]

    for t in range(nt):
        y = ext(t) * w_ref[0:1, :] + b_ref[...]
        for w in range(1, CONV_WIDTH):
            y = y + ext(t + w) * w_ref[w:w + 1, :]
        o_ref[t] = _ln_silu(y, g_ref[...], lb_ref[...]).astype(o_ref.dtype)
    for j in range(hist):
        nb_ref[j] = ext(j + nt)


def _conv_sample(glu_t, buf_t, w_dw, b_dw, ln_g, ln_b):
    nt, ns, _ = glu_t.shape
    hist = CONV_WIDTH - 1
    bs = 32
    vec = pl.BlockSpec((1, CONV_DIM), lambda i: (0, 0))
    return pl.pallas_call(
        _conv_sample_kernel,
        grid=(ns // bs,),
        in_specs=[
            pl.BlockSpec((nt, bs, CONV_DIM), lambda i: (0, i, 0)),
            pl.BlockSpec((hist, bs, CONV_DIM), lambda i: (0, i, 0)),
            pl.BlockSpec((CONV_WIDTH, CONV_DIM), lambda i: (0, 0)),
            vec, vec, vec,
        ],
        out_specs=[pl.BlockSpec((nt, bs, CONV_DIM), lambda i: (0, i, 0)),
                   pl.BlockSpec((hist, bs, CONV_DIM), lambda i: (0, i, 0))],
        out_shape=[jax.ShapeDtypeStruct((nt, ns, CONV_DIM), BF16), jax.ShapeDtypeStruct((hist, ns, CONV_DIM), F32)],
        compiler_params=_params("arbitrary"),
        name="conv_sample",
    )(glu_t, buf_t, w_dw, b_dw, ln_g, ln_b)


PAIR = 2 * B_DH
N_PAIRS = KV_WIDTH // PAIR
BLOCKS_PER_TOKEN = SEL_BLOCK // CMP_STRIDE
N_SUB = CMP_BLOCK // CMP_STRIDE
CMP_OFF = 120
CMP_ROWS = 640
KV_PAD = WINDOW
FAR_TILE = 512
LOWEST = -3.0e38
MASK_C = 2.0 ** 100


def _half(shape, half):
    lane = lax.broadcasted_iota(jnp.int32, shape, len(shape) - 1)
    return (lane % PAIR) // B_DH == half


def _group_q(q_ref, g):
    parts = []
    for r in range(B_GROUP):
        h = g * B_GROUP + r
        x = q_ref[:, (h // 2) * PAIR:(h // 2 + 1) * PAIR]
        x = jnp.where(_half(x.shape, h % 2), x, 0.0) * (B_DH ** -0.5)
        if h % 2 != g % 2:
            x = pltpu.roll(x, B_DH, axis=1)
        parts.append(x)
    return jnp.concatenate(parts, axis=0).astype(BF16)


def _gate_rows(gate_ref, branch, g):
    sig = jax.nn.sigmoid(gate_ref[...])
    c0 = branch * B_HEADS + g * B_GROUP
    return jnp.concatenate([sig[:, c0 + r:c0 + r + 1] for r in range(B_GROUP)], axis=0)


def _scatter_heads(o, g):
    nq = o.shape[0] // B_GROUP
    outs = []
    for jp in range(2):
        acc = None
        for e in range(2):
            r = 2 * jp + e
            x = jnp.where(_half((nq, PAIR), g % 2), o[r * nq:(r + 1) * nq], 0.0)
            if e != g % 2:
                x = pltpu.roll(x, B_DH, axis=1)
            acc = x if acc is None else acc + x
        outs.append(acc)
    return jnp.concatenate(outs, axis=1)


def _nt_dot(a, b):
    return lax.dot_general(a, b, (((1,), (1,)), ((), ())), preferred_element_type=F32)


def _piece(qg, k, v, mask, bias=None):
    s = _nt_dot(qg, k)
    if bias is not None:
        s = s + bias
    s = jnp.where(mask, s, NEG)
    m = jnp.max(s, axis=-1, keepdims=True)
    p = jnp.where(mask, jnp.exp(s - m), 0.0)
    return m, jnp.sum(p, axis=-1, keepdims=True), jnp.dot(p.astype(BF16), v, preferred_element_type=F32), p


def _merge(a, b):
    m = jnp.maximum(a[0], b[0])
    ea = jnp.exp(a[0] - m)
    eb = jnp.exp(b[0] - m)
    return m, a[1] * ea + b[1] * eb, a[2] * ea + b[2] * eb


def _inv_or_zero(l):
    return jnp.where(l > 0.0, 1.0 / l, 0.0)


def _dot01(p, o01):
    hi = p.astype(BF16)
    r1 = p - hi.astype(F32)
    mid = r1.astype(BF16)
    lo = (r1 - mid.astype(F32)).astype(BF16)
    d = lambda a: jnp.dot(a, o01, preferred_element_type=F32)
    return d(hi) + d(mid) + d(lo)


def _overlap01(tok, blk):
    return jnp.where((tok >= BLOCKS_PER_TOKEN * blk - (N_SUB - 1)) & (tok <= BLOCKS_PER_TOKEN * blk + BLOCKS_PER_TOKEN - 1),
                     1.0, 0.0).astype(BF16)


def _select_blocks(imp, qpos, n_cand=None):
    j = lax.broadcasted_iota(jnp.int32, imp.shape, 1)
    cur = qpos // SEL_BLOCK
    forced = (j == 0) | (j == cur) | (j == cur - 1)
    valid = j * SEL_BLOCK <= qpos
    imp = jnp.where(forced, FORCE, jnp.where(valid, imp, -FORCE))
    if n_cand is not None:
        ahead = jnp.zeros(imp.shape, F32)
        for i in range(n_cand):
            v = imp[:, i:i + 1]
            ahead = ahead + jnp.where((v > imp) | ((v == imp) & (j > i)), 1.0, 0.0)
        return jnp.where((ahead < N_SELECT) & (j < n_cand), 1.0, 0.0)
    sel = jnp.zeros(imp.shape, F32)
    jf = j.astype(F32)
    for _ in range(N_SELECT):
        m = jnp.max(imp, axis=-1, keepdims=True)
        first = jnp.min(jnp.where(imp == m, jf, float(imp.shape[1])), axis=-1, keepdims=True)
        pick = jf == first
        sel = jnp.where(pick, 1.0, sel)
        imp = jnp.where(pick, LOWEST, imp)
    return sel


def _expand_blocks(sel_rows, first_blk, n_keys):
    jj = lax.broadcasted_iota(jnp.int32, (sel_rows.shape[1], n_keys), 0)
    kk = lax.broadcasted_iota(jnp.int32, (sel_rows.shape[1], n_keys), 1)
    e = jnp.where(jj == first_blk + kk // SEL_BLOCK, 1.0, 0.0).astype(BF16)
    return jnp.dot(sel_rows, e, preferred_element_type=F32)


def _bias_tables(rel_bias):
    d = np.arange(MAX_DISTANCE + 1)
    exact = N_BUCKETS // 2
    large = exact + (np.log(np.maximum(d, 1).astype(np.float32) / exact) / math.log(MAX_DISTANCE / exact)
                     * (N_BUCKETS - exact)).astype(np.int32)
    bucket = np.where(d < exact, d, np.minimum(large, N_BUCKETS - 1))
    by_dist = rel_bias.astype(F32)[bucket]

    def tile(c, nq, n, step=1):
        n1 = step * (n - 1) + 1
        dmin, dmax = c - (n1 - 1), c + nq - 1
        n_neg = max(0, min(0, dmax + 1) - dmin)
        lo, hi = max(dmin, 0), min(dmax, MAX_DISTANCE)
        n_far = max(0, dmax - max(dmin, MAX_DISTANCE + 1) + 1)
        v = jnp.concatenate([jnp.broadcast_to(by_dist[:1], (n_neg, B_HEADS)), by_dist[lo:hi + 1],
                             jnp.broadcast_to(by_dist[MAX_DISTANCE:], (n_far, B_HEADS))], axis=0)
        lv = nq + n1 - 1
        p = jnp.concatenate([v[::-1], jnp.zeros((1, B_HEADS), F32)], axis=0)
        rows = jnp.tile(p, (nq, 1))[:nq * lv].reshape(nq, lv, B_HEADS)[:, nq - 1:nq - 1 + n1:step]
        t = rows.reshape(nq, n, B_KV_HEADS, B_GROUP).transpose(2, 3, 0, 1)
        return t.reshape(B_KV_HEADS, B_GROUP * nq, n)

    nq = PAIR
    far = by_dist[MAX_DISTANCE]
    shift = jnp.repeat(far.reshape(B_KV_HEADS, B_GROUP), nq, axis=1)[:, :, None]
    n_var = 32
    m0 = PAIR - n_var
    c_cmp = -CMP_STRIDE * (m0 - CMP_OFF) - (CMP_BLOCK - 1)
    assert c_cmp + CMP_STRIDE >= MAX_DISTANCE
    prompt_cmp = jnp.concatenate([jnp.zeros((B_KV_HEADS, B_GROUP * nq, m0), F32),
                                  tile(c_cmp, nq, n_var, CMP_STRIDE) - shift], axis=2)
    causal = np.tile(np.arange(nq)[:, None] + nq - np.arange(2 * nq)[None, :] >= 0, (B_GROUP, 1))[None]
    prompt_kv = jnp.where(causal, tile(nq, nq, 2 * nq) - shift, -MASK_C)
    s_cmp = tile(PAST_LEN - (CMP_BLOCK - 1), DEC_SEQ, PAIR, CMP_STRIDE)
    s_sel = tile(PAST_LEN, DEC_SEQ, PAST_LEN + PAIR)
    s_win = tile(WINDOW, DEC_SEQ, WINDOW + PAIR)
    flat = lambda a: a.reshape(B_KV_HEADS * B_GROUP * DEC_SEQ, a.shape[-1])
    return prompt_cmp, prompt_kv, flat(s_cmp), flat(s_sel), flat(s_win)


def _cmp_weights(pe, w1, b1, w2, b2):
    eye = jnp.eye(2, dtype=F32)
    w1r = w1.reshape(2, N_SUB, CMP_STRIDE, B_DH, CMP_HIDDEN)
    w1p = jnp.einsum('kmpdh,ef->kmpedfh', w1r, eye).reshape(2, N_SUB, CMP_STRIDE * PAIR, 2 * CMP_HIDDEN).astype(BF16)
    pep = jnp.tile(pe.reshape(2, N_SUB, CMP_STRIDE, 1, B_DH), (1, 1, 1, 2, 1)).reshape(2, N_SUB, 1, CMP_STRIDE * PAIR)
    b1p = jnp.tile(b1, (1, 2)).reshape(2, 1, 2 * CMP_HIDDEN)
    w2p = jnp.einsum('khd,ef->kehfd', w2, eye).reshape(2, 2 * CMP_HIDDEN, PAIR).astype(BF16)
    b2p = jnp.tile(b2, (1, 2)).reshape(2, 1, PAIR)
    return pep, w1p, b1p, w2p, b2p


def _compress_pair(src, pe_ref, w1_ref, b1_ref, w2_ref, b2_ref):
    x = jnp.concatenate([src(p) for p in range(CMP_STRIDE)], axis=1)
    m_rows = x.shape[0]
    h = b1_ref[...]
    for m in range(N_SUB):
        part = jnp.dot((x + pe_ref[m]).astype(BF16), w1_ref[m], preferred_element_type=F32)
        h = h + (part if m == 0 else pltpu.roll(part, m_rows - m, axis=0))
    return jnp.dot(_silu(h).astype(BF16), w2_ref[...], preferred_element_type=F32) + b2_ref[...]


def _cmp_weight_specs(kv_of):
    return [
        pl.BlockSpec((None, N_SUB, 1, CMP_STRIDE * PAIR), lambda *a: (kv_of(*a), 0, 0, 0)),
        pl.BlockSpec((None, N_SUB, CMP_STRIDE * PAIR, 2 * CMP_HIDDEN), lambda *a: (kv_of(*a), 0, 0, 0)),
        pl.BlockSpec((None, 1, 2 * CMP_HIDDEN), lambda *a: (kv_of(*a), 0, 0)),
        pl.BlockSpec((None, 2 * CMP_HIDDEN, PAIR), lambda *a: (kv_of(*a), 0, 0)),
        pl.BlockSpec((None, 1, PAIR), lambda *a: (kv_of(*a), 0, 0)),
    ]


def _compress_prompt_kernel(rows_ref, pe_ref, w1_ref, b1_ref, w2_ref, b2_ref, o_ref):
    n_blk = rows_ref.shape[0] // CMP_STRIDE
    tok = _compress_pair(lambda p: rows_ref[pl.ds(p, n_blk, stride=CMP_STRIDE), :], pe_ref, w1_ref, b1_ref, w2_ref, b2_ref)
    o_ref[...] = jnp.zeros_like(o_ref)
    o_ref[CMP_OFF:CMP_OFF + n_blk, :] = tok


def _compress_prompt(u, weights):
    t = u.shape[0]
    col0 = (2 * A_QK + 2 * A_WIDTH + B_WIDTH) // PAIR
    return pl.pallas_call(
        _compress_prompt_kernel,
        grid=(2 * N_PAIRS,),
        in_specs=[pl.BlockSpec((t, PAIR), lambda c: (0, col0 + c))] + _cmp_weight_specs(lambda c: c // N_PAIRS),
        out_specs=pl.BlockSpec((None, CMP_ROWS, PAIR), lambda c: (c, 0, 0)),
        out_shape=jax.ShapeDtypeStruct((2 * N_PAIRS, CMP_ROWS, PAIR), F32),
        compiler_params=_params("arbitrary"),
        name="nsa_compress_prompt",
    )(u, *weights)


CMP_SEQ_BLOCK = 8
N_PAGES = PAST_LEN // PAGE_SIZE


def _compress_sample_kernel(pt_ref, *refs):
    n_src = CMP_SEQ_BLOCK * N_PAGES
    pages = refs[:n_src]
    pe_ref, w1_ref, b1_ref, w2_ref, b2_ref, o_ref, x_ref = refs[n_src:]
    n_blk = PAGE_SIZE // CMP_STRIDE
    for idx, pg in enumerate(pages):
        x_ref[idx] = pg[...].T
    tok = _compress_pair(
        lambda p: jnp.concatenate([x_ref[idx, pl.ds(p, n_blk, stride=CMP_STRIDE), :] for idx in range(n_src)], axis=0),
        pe_ref, w1_ref, b1_ref, w2_ref, b2_ref)
    o_ref[...] = tok.reshape(o_ref.shape)


def _compress_sample(cache, page_table, weights):
    n_seq = page_table.shape[0]
    sb = CMP_SEQ_BLOCK
    n_tok = PAST_LEN // CMP_STRIDE
    page_specs = [pl.BlockSpec((None, PAIR, PAGE_SIZE), lambda c, i, pt, s=s, j=j: (pt[i * sb + s, j], c, 0))
                  for s in range(sb) for j in range(N_PAGES)]
    return pl.pallas_call(
        _compress_sample_kernel,
        grid_spec=pltpu.PrefetchScalarGridSpec(
            num_scalar_prefetch=1,
            grid=(2 * N_PAIRS, n_seq // sb),
            in_specs=page_specs + _cmp_weight_specs(lambda c, i, pt: c // N_PAIRS),
            out_specs=pl.BlockSpec((sb, None, n_tok, PAIR), lambda c, i, pt: (i, c, 0, 0)),
            scratch_shapes=[pltpu.VMEM((sb * N_PAGES, PAGE_SIZE, PAIR), F32)],
        ),
        out_shape=jax.ShapeDtypeStruct((n_seq, 2 * N_PAIRS, n_tok, PAIR), F32),
        compiler_params=_params("arbitrary", "arbitrary"),
        name="nsa_compress_sample",
    )(page_table, *([cache] * (sb * N_PAGES)), *weights)


Q_ROWS = 128


def _nsa_cmp_prompt_kernel(q_ref, gate_ref, cmp_ref, bias_ref, oc_ref, sel_ref):
    qb = pl.program_id(0)
    nq = q_ref.shape[0]
    rows = B_GROUP * nq
    n_tok = CMP_ROWS - PAIR
    near0 = pl.multiple_of(qb * (nq // CMP_STRIDE), SUBLANES)
    tok0 = near0 - CMP_OFF
    mask_far = lax.broadcasted_iota(jnp.int32, (rows, n_tok), 1) < tok0
    i = lax.broadcasted_iota(jnp.int32, (rows, PAIR), 0) % nq
    mn = lax.broadcasted_iota(jnp.int32, (rows, PAIR), 1)
    dist = i - CMP_STRIDE * (mn - CMP_OFF) - (CMP_BLOCK - 1)
    mask_near = (dist >= 0) & (tok0 + mn >= 0)
    o_far = _overlap01(lax.broadcasted_iota(jnp.int32, (n_tok, PAIR), 0), lax.broadcasted_iota(jnp.int32, (n_tok, PAIR), 1))
    o_near = _overlap01(tok0 + lax.broadcasted_iota(jnp.int32, (PAIR, PAIR), 0), lax.broadcasted_iota(jnp.int32, (PAIR, PAIR), 1))
    qpos = qb * nq + lax.broadcasted_iota(jnp.int32, (nq, PAIR), 0)
    for g in range(B_KV_HEADS):
        qg = _group_q(q_ref, g)
        kp, vp = g // 2, N_PAIRS + g // 2
        far = _piece(qg, cmp_ref[kp, CMP_OFF:CMP_OFF + n_tok, :].astype(BF16),
                     cmp_ref[vp, CMP_OFF:CMP_OFF + n_tok, :].astype(BF16), mask_far)
        near = _piece(qg, cmp_ref[kp, pl.ds(near0, PAIR), :].astype(BF16),
                      cmp_ref[vp, pl.ds(near0, PAIR), :].astype(BF16), mask_near, bias_ref[g])
        m, l, acc = _merge(far[:3], near[:3])
        linv = _inv_or_zero(l)
        oc_ref[:, g * 2 * PAIR:(g + 1) * 2 * PAIR] = _scatter_heads(acc * linv * _gate_rows(gate_ref, 0, g), g)
        pf = jnp.sum((far[3] * (jnp.exp(far[0] - m) * linv)).reshape(B_GROUP, nq, n_tok), axis=0)
        pn = jnp.sum((near[3] * (jnp.exp(near[0] - m) * linv)).reshape(B_GROUP, nq, PAIR), axis=0)
        imp = _dot01(pf, o_far) + _dot01(pn, o_near)
        sel_ref[:, g * PAIR:(g + 1) * PAIR] = _select_blocks(imp, qpos)


def _flash_step(st, qx, kx, v, bias=None):
    m_old, l, acc = st
    s = _nt_dot(qx, kx)
    if bias is not None:
        s = s + bias
    m = jnp.maximum(m_old, jnp.max(s, axis=-1, keepdims=True))
    p = jnp.exp(s - m)
    alpha = jnp.exp(m_old - m)
    return m, alpha * l + jnp.sum(p, axis=-1, keepdims=True), alpha * acc + jnp.dot(p.astype(BF16), v, preferred_element_type=F32)


def _nsa_selwin_prompt_kernel(q_ref, gate_ref, sel_ref, oc_ref, kv_ref, oh_ref, ohr_ref, bias_ref, wbias_ref, o_ref):
    qb = pl.program_id(0)
    nq = q_ref.shape[0]
    rows = B_GROUP * nq
    qs = qb * nq
    per_q = nq // SEL_BLOCK
    n_win = WINDOW + nq
    near_rows = pl.ds(pl.multiple_of(qs + KV_PAD - nq, nq), 2 * nq)
    win_rows = pl.ds(pl.multiple_of(qs, nq), n_win)
    lane = lax.broadcasted_iota(jnp.int32, (rows, PAIR), 1)
    first_near = per_q * (qb - 1)
    first_win = per_q * qb - WINDOW // SEL_BLOCK
    jj = lax.broadcasted_iota(jnp.int32, (PAIR, PAIR), 0)
    bb = lax.broadcasted_iota(jnp.int32, (PAIR, PAIR), 1)
    to_near = jnp.where(jj == first_near + bb, 1.0, 0.0).astype(BF16)
    exists_near = jnp.where(first_near + lane >= 0, 0.0, -MASK_C).astype(BF16)
    exists_win = jnp.where(first_win + lane >= 0, 0.0, -MASK_C).astype(BF16)
    n_far = (qb + 2) // (FAR_TILE // nq)
    init = (jnp.full((rows, 1), NEG, F32), jnp.zeros((rows, 1), F32), jnp.zeros((rows, PAIR), F32))
    for g in range(B_KV_HEADS):
        qg = _group_q(q_ref, g)
        cols = lambda base: slice(base + (g // 2) * PAIR, base + (g // 2 + 1) * PAIR)
        selm = jnp.concatenate([sel_ref[:, g * PAIR:(g + 1) * PAIR]] * B_GROUP, axis=0)
        far_vec = jnp.where((selm > 0.5) & (lane < first_near), 0.0, -MASK_C).astype(BF16)
        near_sel = jnp.dot(selm.astype(BF16), to_near, preferred_element_type=F32)
        near_vec = jnp.where(near_sel > 0.5, 0.0, -MASK_C).astype(BF16) + exists_near
        qx_far = jnp.concatenate([qg, far_vec], axis=1)

        def far_step(t, st):
            r = pl.ds(pl.multiple_of(KV_PAD + t * FAR_TILE, FAR_TILE), FAR_TILE)
            kx = jnp.concatenate([kv_ref[r, cols(0)], oh_ref[r, :]], axis=1)
            return _flash_step(st, qx_far, kx, kv_ref[r, cols(KV_WIDTH)])

        st = lax.fori_loop(0, n_far, far_step, init)
        st = _flash_step(st, jnp.concatenate([qg, near_vec], axis=1),
                         jnp.concatenate([kv_ref[near_rows, cols(0)], ohr_ref[:2 * nq, :]], axis=1),
                         kv_ref[near_rows, cols(KV_WIDTH)], bias_ref[g])
        o_s = st[2] * _inv_or_zero(st[1])

        s_w = _nt_dot(jnp.concatenate([qg, exists_win], axis=1),
                      jnp.concatenate([kv_ref[win_rows, cols(2 * KV_WIDTH)], ohr_ref[...]], axis=1)) + wbias_ref[g]
        p_w = jnp.exp(s_w - jnp.max(s_w, axis=-1, keepdims=True))
        o_w = (jnp.dot(p_w.astype(BF16), kv_ref[win_rows, cols(3 * KV_WIDTH)], preferred_element_type=F32)
               * _inv_or_zero(jnp.sum(p_w, axis=-1, keepdims=True)))
        o = o_s * _gate_rows(gate_ref, 1, g) + o_w * _gate_rows(gate_ref, 2, g)
        blk = slice(g * 2 * PAIR, (g + 1) * 2 * PAIR)
        o_ref[:, blk] = _scatter_heads(o, g) + oc_ref[:, blk]


def _block_onehots(t):
    pos = np.arange(KV_PAD + t) - KV_PAD
    absolute = (pos[:, None] // SEL_BLOCK == np.arange(PAIR)[None, :]) & (pos[:, None] >= 0)
    relative = np.arange(WINDOW + Q_ROWS)[:, None] // SEL_BLOCK == np.arange(PAIR)[None, :]
    i = np.arange(B_GROUP * Q_ROWS)[:, None] % Q_ROWS
    in_window = np.arange(WINDOW - Q_ROWS)[None, :] > i
    return (jnp.asarray(absolute, BF16), jnp.asarray(relative, BF16),
            jnp.asarray(np.where(in_window, 0.0, -MASK_C), F32))


def _nsa_prompt(u, kv_pad, cmp_tok, bias_cmp, bias_kv):
    t = u.shape[0]
    nq = Q_ROWS
    q_col = (2 * A_QK + 2 * A_WIDTH) // B_WIDTH
    g_col = (IN_DIM - 3 * B_HEADS) // PAIR
    q_spec = pl.BlockSpec((nq, B_WIDTH), lambda i: (i, q_col))
    gate_spec = pl.BlockSpec((nq, PAIR), lambda i: (i, g_col))
    oc, sel = pl.pallas_call(
        _nsa_cmp_prompt_kernel,
        grid=(t // nq,),
        in_specs=[q_spec, gate_spec,
                  pl.BlockSpec(cmp_tok.shape, lambda i: (0, 0, 0)),
                  pl.BlockSpec(bias_cmp.shape, lambda i: (0, 0, 0))],
        out_specs=[pl.BlockSpec((nq, B_WIDTH), lambda i: (i, 0)), pl.BlockSpec((nq, B_KV_HEADS * PAIR), lambda i: (i, 0))],
        out_shape=[jax.ShapeDtypeStruct((t, B_WIDTH), F32), jax.ShapeDtypeStruct((t, B_KV_HEADS * PAIR), F32)],
        compiler_params=_params("arbitrary"),
        name="nsa_cmp_select_prompt",
    )(u, u, cmp_tok, bias_cmp)
    oh_abs, oh_rel, win_mask = _block_onehots(t)
    win_bias = jnp.concatenate([jnp.broadcast_to(win_mask, (B_KV_HEADS,) + win_mask.shape), bias_kv], axis=2)
    whole = lambda a: pl.BlockSpec(a.shape, lambda i: (0,) * a.ndim, pipeline_mode=pl.Buffered(1))
    return pl.pallas_call(
        _nsa_selwin_prompt_kernel,
        grid=(t // nq,),
        in_specs=[q_spec, gate_spec,
                  pl.BlockSpec((nq, B_KV_HEADS * PAIR), lambda i: (i, 0)),
                  pl.BlockSpec((nq, B_WIDTH), lambda i: (i, 0)),
                  whole(kv_pad), whole(oh_abs), whole(oh_rel), whole(bias_kv), whole(win_bias)],
        out_specs=pl.BlockSpec((nq, B_WIDTH), lambda i: (i, 0)),
        out_shape=jax.ShapeDtypeStruct((t, B_WIDTH), F32),
        compiler_params=_params("arbitrary"),
        name="nsa_select_window_prompt",
    )(u, u, sel, oc, kv_pad, oh_abs, oh_rel, bias_kv, win_bias)


SEL_KEYS = PAST_LEN + PAIR
WIN_KEYS = WINDOW + PAIR


NSA_SEQ_BLOCK = 2


def _piece_t(qg, kt, vt, mask, bias):
    s = jnp.dot(qg, kt, preferred_element_type=F32) + bias
    s = jnp.where(mask, s, NEG)
    m = jnp.max(s, axis=-1, keepdims=True)
    p = jnp.where(mask, jnp.exp(s - m), 0.0)
    return m, jnp.sum(p, axis=-1, keepdims=True), _nt_dot(p.astype(BF16), vt), p


def _nsa_sample_kernel(pt_ref, *refs):
    n_pg = NSA_SEQ_BLOCK * N_PAGES
    pages = refs[:n_pg]
    (q_ref, gate_ref, cmp_ref, selnew_ref, winnew_ref, win_ref, bc_ref, bs_ref, bw_ref,
     o_ref, winout_ref, kt_ref, vt_ref, wkt_ref, wvt_ref) = refs[n_pg:]
    seqs = [_nsa_sample_one(pages[s * N_PAGES:(s + 1) * N_PAGES], q_ref.at[s], gate_ref.at[s], cmp_ref.at[s],
                            selnew_ref.at[s], winnew_ref.at[s], win_ref.at[s], bc_ref, bs_ref, bw_ref,
                            o_ref.at[s], winout_ref.at[s], kt_ref.at[s], vt_ref.at[s], wkt_ref.at[s], wvt_ref.at[s])
            for s in range(NSA_SEQ_BLOCK)]
    for _ in range(NSA_SAMPLE_STAGES):
        for seq in seqs:
            next(seq)


NSA_SAMPLE_STAGES = 5


def _nsa_sample_one(pages, q_ref, gate_ref, cmp_ref, selnew_ref, winnew_ref, win_ref, bc_ref, bs_ref, bw_ref,
                    o_ref, winout_ref, kt_ref, vt_ref, wkt_ref, wvt_ref):
    nt = q_ref.shape[0]
    grp = B_GROUP * nt
    rows = B_KV_HEADS * grp
    half = 2 * PAIR
    pad = jnp.zeros((PAIR - nt, 2 * half), F32)
    sel_new_t = jnp.concatenate([selnew_ref[...], pad], axis=0).T
    win_new_t = jnp.concatenate([winnew_ref[...], pad], axis=0).T
    for j, pg in enumerate(pages):
        kt_ref[:, j * PAGE_SIZE:(j + 1) * PAGE_SIZE] = pg[:half, :].astype(BF16)
        vt_ref[:, j * PAGE_SIZE:(j + 1) * PAGE_SIZE] = pg[half:, :].astype(BF16)
    kt_ref[:, PAST_LEN:] = sel_new_t[:half].astype(BF16)
    vt_ref[:, PAST_LEN:] = sel_new_t[half:].astype(BF16)
    buf = win_ref[...]
    wkt_ref[:, :WINDOW] = buf[:half].astype(BF16)
    wvt_ref[:, :WINDOW] = buf[half:].astype(BF16)
    wkt_ref[:, WINDOW:] = win_new_t[:half].astype(BF16)
    wvt_ref[:, WINDOW:] = win_new_t[half:].astype(BF16)
    shifted = pltpu.roll(buf, WINDOW - nt, axis=1)
    tail = pltpu.roll(win_new_t, PAIR - nt, axis=1)
    lane = lax.broadcasted_iota(jnp.int32, tail.shape, 1)
    winout_ref[:, :WINDOW - PAIR] = shifted[:, :WINDOW - PAIR]
    winout_ref[:, WINDOW - PAIR:] = jnp.where(lane >= PAIR - nt, tail, shifted[:, WINDOW - PAIR:])
    yield

    zero = jnp.zeros((grp, PAIR), BF16)
    qq = jnp.concatenate(
        [jnp.concatenate([_group_q(q_ref, g), zero] if g // 2 == 0 else [zero, _group_q(q_ref, g)], axis=1)
         for g in range(B_KV_HEADS)], axis=0)
    take = lambda acc, g: acc[g * grp:(g + 1) * grp, (g // 2) * PAIR:(g // 2 + 1) * PAIR]
    qpos = PAST_LEN + lax.broadcasted_iota(jnp.int32, (rows, 1), 0) % nt

    n_tok = cmp_ref.shape[1]
    ck = jnp.concatenate([cmp_ref[0], cmp_ref[1]], axis=1).astype(BF16)
    cv = jnp.concatenate([cmp_ref[2], cmp_ref[3]], axis=1).astype(BF16)
    mask_c = lax.broadcasted_iota(jnp.int32, (rows, n_tok), 1) < n_tok - (N_SUB - 1)
    mc, lc, acc_c, pc = _piece(qq, ck, cv, mask_c, bc_ref[...])
    pc = pc * _inv_or_zero(lc)
    p_all = jnp.concatenate([jnp.sum(pc[g * grp:(g + 1) * grp].reshape(B_GROUP, nt, n_tok), axis=0)
                             for g in range(B_KV_HEADS)], axis=0)
    yield
    o01 = _overlap01(lax.broadcasted_iota(jnp.int32, (n_tok, PAIR), 0), lax.broadcasted_iota(jnp.int32, (n_tok, PAIR), 1))
    qpos_gt = PAST_LEN + lax.broadcasted_iota(jnp.int32, (B_KV_HEADS * nt, PAIR), 0) % nt
    n_blocks = pl.cdiv(PAST_LEN + nt, SEL_BLOCK)
    sel = _select_blocks(_dot01(p_all, o01), qpos_gt, n_cand=n_blocks).astype(BF16)
    sel_rows = jnp.concatenate([sel[g * nt:(g + 1) * nt] for g in range(B_KV_HEADS) for _ in range(B_GROUP)], axis=0)
    yield

    ks = lax.broadcasted_iota(jnp.int32, (rows, SEL_KEYS), 1)
    mask_s = (_expand_blocks(sel_rows, 0, SEL_KEYS) > 0.5) & (ks <= qpos)
    ms, ls, acc_s, _ = _piece_t(qq, kt_ref[...], vt_ref[...], mask_s, bs_ref[...])
    yield
    kw = PAST_LEN - WINDOW + lax.broadcasted_iota(jnp.int32, (rows, WIN_KEYS), 1)
    mask_w = (kw <= qpos) & (qpos - kw < WINDOW)
    mw, lw, acc_w, _ = _piece_t(qq, wkt_ref[...], wvt_ref[...], mask_w, bw_ref[...])
    acc_c, acc_s, acc_w = acc_c * _inv_or_zero(lc), acc_s * _inv_or_zero(ls), acc_w * _inv_or_zero(lw)
    for g in range(B_KV_HEADS):
        o = (take(acc_c, g) * _gate_rows(gate_ref, 0, g) + take(acc_s, g) * _gate_rows(gate_ref, 1, g)
             + take(acc_w, g) * _gate_rows(gate_ref, 2, g))
        o_ref[:, g * 2 * PAIR:(g + 1) * 2 * PAIR] = _scatter_heads(o, g)
    yield


def _nsa_sample(u_s, cmp_tok, cache_sel, win_buf, page_table, bias_c, bias_s, bias_w):
    n_seq, nt, _ = u_s.shape
    sb = NSA_SEQ_BLOCK
    q_col = (2 * A_QK + 2 * A_WIDTH) // B_WIDTH
    kv_col = (2 * A_QK + 2 * A_WIDTH + B_WIDTH) // (4 * PAIR)
    g_col = (IN_DIM - 3 * B_HEADS) // PAIR
    const = lambda a: pl.BlockSpec(a.shape, lambda b, pt: (0, 0))
    in_specs = [pl.BlockSpec((None, 4 * PAIR, PAGE_SIZE), lambda b, pt, s=s, j=j: (pt[b * sb + s, j], 0, 0))
                for s in range(sb) for j in range(N_PAGES)]
    in_specs += [
        pl.BlockSpec((sb, nt, B_WIDTH), lambda b, pt: (b, 0, q_col)),
        pl.BlockSpec((sb, nt, PAIR), lambda b, pt: (b, 0, g_col)),
        pl.BlockSpec((sb,) + cmp_tok.shape[1:], lambda b, pt: (b, 0, 0, 0)),
        pl.BlockSpec((sb, nt, 4 * PAIR), lambda b, pt: (b, 0, kv_col + 1)),
        pl.BlockSpec((sb, nt, 4 * PAIR), lambda b, pt: (b, 0, kv_col + 2)),
        pl.BlockSpec((sb, 4 * PAIR, WINDOW), lambda b, pt: (b, 0, 0)),
        const(bias_c), const(bias_s), const(bias_w),
    ]
    return pl.pallas_call(
        _nsa_sample_kernel,
        grid_spec=pltpu.PrefetchScalarGridSpec(
            num_scalar_prefetch=1,
            grid=(n_seq // sb,),
            in_specs=in_specs,
            out_specs=[pl.BlockSpec((sb, nt, B_WIDTH), lambda b, pt: (b, 0, 0)),
                       pl.BlockSpec((sb, 4 * PAIR, WINDOW), lambda b, pt: (b, 0, 0))],
            scratch_shapes=[pltpu.VMEM((sb, 2 * PAIR, SEL_KEYS), BF16), pltpu.VMEM((sb, 2 * PAIR, SEL_KEYS), BF16),
                            pltpu.VMEM((sb, 2 * PAIR, WIN_KEYS), BF16), pltpu.VMEM((sb, 2 * PAIR, WIN_KEYS), BF16)],
        ),
        out_shape=[jax.ShapeDtypeStruct((n_seq, nt, B_WIDTH), F32), jax.ShapeDtypeStruct(win_buf.shape, F32)],
        compiler_params=_params("arbitrary"),
        name="nsa_sample",
    )(page_table, *([cache_sel] * (sb * N_PAGES)), u_s, u_s, cmp_tok, u_s, u_s, win_buf, bias_c, bias_s, bias_w)


PROMPT_TM = 1024
PROJ_TN = 512
COL_CMP = 2 * A_QK + 2 * A_WIDTH + B_WIDTH
COL_SEL = COL_CMP + 2 * KV_WIDTH
COL_WIN = COL_SEL + 2 * KV_WIDTH
COL_GATE = COL_WIN + 2 * KV_WIDTH


def _time_major(a):
    return a.transpose(1, 0, 2).reshape(a.shape[0] * a.shape[1], a.shape[2])


def _seq_major(a2d, n_seq):
    return a2d.reshape(a2d.shape[0] // n_seq, n_seq, a2d.shape[1]).transpose(1, 0, 2)


def _kv_rows(u3, col):
    return u3[..., col:col + 2 * KV_WIDTH].reshape(u3.shape[:-1] + (2, B_KV_HEADS, B_DH))


def kernel(x_prompt, x_sample, cache_cmp, cache_sel, state_win, state_hgrn, state_conv, page_table, c_prompt, c_sample, norm_g, ada_w, ada_b, ffn_w_gate, ffn_w_up, ffn_w_down, w_in_even, hgrn_lower_bound, hgrn_norm_g, cmp_pe, cmp_w1, cmp_b1, cmp_w2, cmp_b2, rel_bias, w_out_even, conv_w_pw1, conv_b_pw1, conv_w_dw, conv_b_dw, conv_ln_g, conv_ln_b, conv_w_pw2, conv_b_pw2, final_norm_g):
    P = {'norm_g': norm_g, 'ffn_w_gate': ffn_w_gate, 'ffn_w_up': ffn_w_up, 'ffn_w_down': ffn_w_down,
         'final_norm_g': final_norm_g}
    n_seq = x_sample.shape[0]
    n_pool = cache_cmp.shape[1]
    rows_p, rows_s = _prompt_rows(PROMPT_TM), _sample_rows()
    g4 = norm_g.reshape(DEPTH, 3, 1, D_MODEL)
    c_all = jnp.concatenate([c_sample, jnp.tile(c_prompt, (SUBLANES, 1))], axis=0)
    mod = _ada_mod(c_all, ada_w, ada_b)
    bias_pc, bias_pkv, bias_sc, bias_ss, bias_sw = _bias_tables(rel_bias)

    xp = x_prompt.reshape(SEQ, D_MODEL)
    xs = _time_major(x_sample)
    cmp_p, cmp_s, sel_p, sel_s, win_p, win_s, hgrn_p, hgrn_s, conv_p, conv_s = ([] for _ in range(10))
    for l in range(DEPTH):
        i = l // 2
        last = l == DEPTH - 1
        xs, w_bf16 = _ffn(xs, rows_s, mod, l, 0, P)
        xp, _ = _ffn(xp, rows_p, mod, l, 0, P, w_bf16=w_bf16)
        if l % 2 == 0:
            up = _proj(xp, rows_p, mod, l, g4, w_in_even[i:i + 1], IN_DIM, PROJ_TN)
            us = _seq_major(_proj(xs, rows_s, mod, l, g4, w_in_even[i:i + 1], IN_DIM, PROJ_TN), n_seq)
            gn = hgrn_norm_g[i:i + 1]
            oa_p, hp = _hgrn(up, hgrn_lower_bound, gn, l)
            oa_s, hs = _hgrn(us.reshape(n_seq * DEC_SEQ, IN_DIM), hgrn_lower_bound, gn, l, state_hgrn[i])
            weights = _cmp_weights(cmp_pe[i], cmp_w1[i], cmp_b1[i], cmp_w2[i], cmp_b2[i])
            kv_pad = jnp.pad(up[:, COL_SEL:COL_GATE].astype(BF16), ((KV_PAD, 0), (0, 0)))
            ob_p = _nsa_prompt(up, kv_pad, _compress_prompt(up, weights), bias_pc, bias_pkv)
            cmp_t = cache_cmp[i].transpose(0, 2, 3, 4, 1).reshape(n_pool, 4 * PAIR, PAGE_SIZE)
            cmp_tok_s = _compress_sample(cmp_t, page_table, weights)
            sel_t = cache_sel[i].transpose(0, 2, 3, 4, 1).reshape(n_pool, 4 * PAIR, PAGE_SIZE)
            win_t = state_win[i].transpose(0, 2, 3, 4, 1).reshape(n_seq, 4 * PAIR, WINDOW)
            ob_s, wn = _nsa_sample(us, cmp_tok_s, sel_t, win_t, page_table, bias_sc, bias_ss, bias_sw)
            xp = _out_proj([oa_p, ob_p], w_out_even[i:i + 1], [0, A_WIDTH], None, xp, rows_p, mod, l, PROJ_TN)
            xs = _out_proj([_time_major(oa_s.reshape(n_seq, DEC_SEQ, A_WIDTH)), _time_major(ob_s)],
                           w_out_even[i:i + 1], [0, A_WIDTH], None, xs, rows_s, mod, l, PROJ_TN)
            up3 = up[None]
            cmp_p.append(_kv_rows(up3, COL_CMP))
            sel_p.append(_kv_rows(up3, COL_SEL))
            win_p.append(_kv_rows(up3[:, SEQ - min(WINDOW, SEQ):], COL_WIN))
            cmp_s.append(_kv_rows(us, COL_CMP))
            sel_s.append(_kv_rows(us, COL_SEL))
            win_s.append(wn.reshape(n_seq, 2, B_KV_HEADS, B_DH, WINDOW).transpose(0, 4, 1, 2, 3))
            hgrn_p.append(hp[None])
            hgrn_s.append(hs)
        else:
            b_pw1 = conv_b_pw1[i].reshape(1, 1, 2 * CONV_DIM)
            b_pw2 = conv_b_pw2[i].reshape(1, 1, D_MODEL)
            vec = lambda a: a[i].reshape(1, CONV_DIM)
            glu_p = _glu_proj(xp, rows_p, mod, l, g4, conv_w_pw1[i:i + 1], b_pw1, PROJ_TN)
            act_p = _conv_prompt(glu_p, conv_w_dw[i], vec(conv_b_dw), vec(conv_ln_g), vec(conv_ln_b))
            xp = _out_proj([act_p], conv_w_pw2[i:i + 1], [0], b_pw2, xp, rows_p, mod, l, PROJ_TN)
            glu_s = _glu_proj(xs, rows_s, mod, l, g4, conv_w_pw1[i:i + 1], b_pw1, PROJ_TN)
            act_s, nb = _conv_sample(glu_s.reshape(DEC_SEQ, n_seq, CONV_DIM), state_conv[i].transpose(1, 0, 2),
                                     conv_w_dw[i], vec(conv_b_dw), vec(conv_ln_g), vec(conv_ln_b))
            xs = _out_proj([act_s.reshape(DEC_SEQ * n_seq, CONV_DIM)], conv_w_pw2[i:i + 1], [0], b_pw2, xs, rows_s, mod, l, PROJ_TN)
            conv_p.append(glu_p[None, SEQ - (CONV_WIDTH - 1):])
            conv_s.append(nb.transpose(1, 0, 2))
        xs, w_bf16 = _ffn(xs, rows_s, mod, l, 2, P, final_norm=last)
        xp, _ = _ffn(xp, rows_p, mod, l, 2, P, w_bf16=w_bf16, final_norm=last)
    y_prompt = xp.reshape(1, SEQ, D_MODEL)
    y_sample = _seq_major(xs, n_seq)
    st = jnp.stack
    return (y_prompt, y_sample, st(cmp_p), st(cmp_s), st(sel_p), st(sel_s), st(win_p), st(win_s),
            st(hgrn_p), st(hgrn_s), st(conv_p), st(conv_s))
```

```python
import functools
import math

import numpy as np
import jax
import jax.numpy as jnp
from jax import lax
from jax.experimental import pallas as pl
from jax.experimental.pallas import tpu as pltpu

F32 = jnp.float32
BF16 = jnp.bfloat16

D_MODEL = 2048
SEQ = 8192
DEPTH = 2
DEC_BATCH = 128
DEC_SEQ = 8
PAST_LEN = 2048
PAGE_SIZE = 128
N_MOD = 9
D_FF = 5504
EPS = 1e-6
A_HEADS = 8
A_DK = 128
A_DV = 128
A_QK = A_HEADS * A_DK
A_WIDTH = A_HEADS * A_DV
B_HEADS = 16
B_KV_HEADS = 4
B_DH = 64
B_GROUP = B_HEADS // B_KV_HEADS
B_WIDTH = B_HEADS * B_DH
KV_WIDTH = B_KV_HEADS * B_DH
CMP_BLOCK = 32
CMP_STRIDE = 16
CMP_HIDDEN = 256
SEL_BLOCK = 64
N_SELECT = 16
WINDOW = 512
N_BUCKETS = 32
MAX_DISTANCE = 128
MIX_WIDTH = A_WIDTH + B_WIDTH
IN_DIM = 2 * A_QK + 2 * A_WIDTH + B_WIDTH + 6 * KV_WIDTH + 3 * B_HEADS
CONV_WIDTH = 31
CONV_DIM = D_MODEL
NEG = -1e30
FORCE = 1e9

V7X_VMEM_LIMIT_BYTES = 60 * 1024 * 1024
SUBLANES = 8
LANES = 128

N_SEQ_ROWS = DEC_BATCH + SUBLANES
PROMPT_ROW_BLOCK = DEC_BATCH // SUBLANES


def _params(*sem):
    return pltpu.CompilerParams(dimension_semantics=sem, vmem_limit_bytes=V7X_VMEM_LIMIT_BYTES)


def _silu(x):
    return x * jax.nn.sigmoid(x)


def _bdot(a, b):
    return jnp.dot(a.astype(BF16), b.astype(BF16), preferred_element_type=F32)


def _ada_kernel(c_ref, w_ref, b_ref, o_ref):
    o_ref[...] = _bdot(_silu(c_ref[...]), w_ref[...]) + b_ref[...]


def _ada_mod(c_all, ada_w, ada_b):
    n = c_all.shape[0]
    return pl.pallas_call(
        _ada_kernel,
        grid=(DEPTH, N_MOD),
        in_specs=[
            pl.BlockSpec((n, D_MODEL), lambda l, k: (0, 0)),
            pl.BlockSpec((None, D_MODEL, D_MODEL), lambda l, k: (l, 0, k)),
            pl.BlockSpec((None, None, 1, D_MODEL), lambda l, k: (l, k, 0, 0)),
        ],
        out_specs=pl.BlockSpec((None, None, n, D_MODEL), lambda l, k: (l, k, 0, 0)),
        out_shape=jax.ShapeDtypeStruct((DEPTH, N_MOD, n, D_MODEL), F32),
        compiler_params=_params("arbitrary", "arbitrary"),
        name="ada_mod",
    )(c_all, ada_w, ada_b.reshape(DEPTH, N_MOD, 1, D_MODEL))


def _norm_mod(x, g, shift, scale):
    ms = jnp.mean(x * x, axis=-1, keepdims=True)
    y = x * lax.rsqrt(ms + EPS) * g
    h = y * (1.0 + scale) + shift
    return h.reshape(x.shape[0] * x.shape[1], x.shape[2]).astype(BF16)


class _Rows:
    def __init__(self, m, bs, nt, seq_block):
        assert m % (bs * nt) == 0
        self.m, self.bs, self.nt, self.seq_block = m, bs, nt, seq_block
        self.tm = bs * nt
        self.n_tiles = m // self.tm

    def view(self, x2d):
        return x2d.reshape(self.m // self.bs, self.bs, x2d.shape[-1])

    def x_spec(self, width, col=lambda j: 0, single_buffer=False):
        mode = dict(pipeline_mode=pl.Buffered(1)) if single_buffer else {}
        return pl.BlockSpec((self.nt, self.bs, width), lambda i, j: (i, 0, col(j)), **mode)

    def mod_spec(self, layer, k, width=D_MODEL, col=lambda j: 0):
        sb = self.seq_block
        return pl.BlockSpec((None, None, self.bs, width), lambda i, j: (layer, k, sb, col(j)))


def _prompt_rows(tm):
    return _Rows(SEQ, SUBLANES, tm // SUBLANES, PROMPT_ROW_BLOCK)


def _sample_rows():
    return _Rows(DEC_BATCH * DEC_SEQ, DEC_BATCH, DEC_SEQ, 0)


FFN_TF_F32 = 256
FFN_TF_BF16 = 512
FFN_ACC_CHUNKS = 4


def _ffn_kernel(x_ref, sh_ref, sc_ref, gt_ref, g_ref, wg_ref, wu_ref, wd_ref, fg_ref, o_ref, *rest, final_norm, emit):
    if emit:
        wg_o, wu_o, wd_o, h_ref = rest
    else:
        (h_ref,) = rest
    j = pl.program_id(1)
    nj = pl.num_programs(1)
    tf = wg_ref.shape[1]

    nt, bs = o_ref.shape[0], o_ref.shape[1]
    step = max(nt // FFN_ACC_CHUNKS, 1)

    @pl.when(j == 0)
    def _():
        for r in range(0, nt, step):
            h_ref[r * bs:(r + step) * bs, :] = _norm_mod(x_ref[r:r + step], g_ref[...], sh_ref[...], sc_ref[...])
        o_ref[...] = jnp.zeros_like(o_ref)

    wg, wu, wd = wg_ref[...].astype(BF16), wu_ref[...].astype(BF16), wd_ref[...].astype(BF16)
    if emit:
        wg_o[...], wu_o[...], wd_o[...] = wg, wu, wd
    valid = D_FF - j * tf
    h = h_ref[...]
    a = _silu(_bdot(h, wg)) * _bdot(h, wu)
    col = lax.broadcasted_iota(jnp.int32, a.shape, 1)
    a = jnp.where(col < valid, a, 0.0)
    row = lax.broadcasted_iota(jnp.int32, wd.shape, 0)
    wd = jnp.where(row < valid, wd, jnp.zeros_like(wd))
    a = a.astype(BF16)
    for r in range(0, nt, step):
        o_ref[r:r + step] += _bdot(a[r * bs:(r + step) * bs], wd).reshape(step, bs, o_ref.shape[2])

    @pl.when(j == nj - 1)
    def _():
        for r in range(0, nt, step):
            y = x_ref[r:r + step] + (0.5 * gt_ref[...]) * o_ref[r:r + step]
            if final_norm:
                ms = jnp.mean(y * y, axis=-1, keepdims=True)
                y = y * lax.rsqrt(ms + EPS) * fg_ref[...]
            o_ref[r:r + step] = y


def _ffn(x2d, rows, mod, layer, sub, P, w_bf16=None, final_norm=False):
    half = sub // 2
    emit = w_bf16 is None
    tf = FFN_TF_F32 if emit else FFN_TF_BF16
    g_norm = P['norm_g'].reshape(DEPTH, 3, 1, D_MODEL)
    fg = P['final_norm_g'].reshape(1, D_MODEL)
    if emit:
        weights = (P['ffn_w_gate'], P['ffn_w_up'], P['ffn_w_down'])
        w_specs = [pl.BlockSpec((None, None, D_MODEL, tf), lambda i, j: (layer, half, 0, j)),
                   pl.BlockSpec((None, None, D_MODEL, tf), lambda i, j: (layer, half, 0, j)),
                   pl.BlockSpec((None, None, tf, D_MODEL), lambda i, j: (layer, half, j, 0))]
    else:
        weights = w_bf16
        w_specs = [pl.BlockSpec((D_MODEL, tf), lambda i, j: (0, j)),
                   pl.BlockSpec((D_MODEL, tf), lambda i, j: (0, j)),
                   pl.BlockSpec((tf, D_MODEL), lambda i, j: (j, 0))]
    out_specs = [rows.x_spec(D_MODEL, single_buffer=emit)]
    out_shape = [jax.ShapeDtypeStruct((rows.m // rows.bs, rows.bs, D_MODEL), F32)]
    if emit:
        out_specs += [pl.BlockSpec((D_MODEL, tf), lambda i, j: (0, j)),
                      pl.BlockSpec((D_MODEL, tf), lambda i, j: (0, j)),
                      pl.BlockSpec((tf, D_MODEL), lambda i, j: (j, 0))]
        out_shape += [jax.ShapeDtypeStruct((D_MODEL, D_FF), BF16), jax.ShapeDtypeStruct((D_MODEL, D_FF), BF16),
                      jax.ShapeDtypeStruct((D_FF, D_MODEL), BF16)]
        assert rows.n_tiles == 1
    outs = pl.pallas_call(
        functools.partial(_ffn_kernel, final_norm=final_norm, emit=emit),
        grid=(rows.n_tiles, pl.cdiv(D_FF, tf)),
        in_specs=[
            rows.x_spec(D_MODEL, single_buffer=True),
            rows.mod_spec(layer, 3 * sub), rows.mod_spec(layer, 3 * sub + 1), rows.mod_spec(layer, 3 * sub + 2),
            pl.BlockSpec((None, None, 1, D_MODEL), lambda i, j: (layer, sub, 0, 0)),
            *w_specs,
            pl.BlockSpec((1, D_MODEL), lambda i, j: (0, 0)),
        ],
        out_specs=out_specs,
        out_shape=out_shape,
        scratch_shapes=[pltpu.VMEM((rows.tm, D_MODEL), BF16)],
        compiler_params=_params("arbitrary", "arbitrary"),
        name="ffn_half_step",
    )(rows.view(x2d), mod, mod, mod, g_norm, *weights, fg)
    return outs[0].reshape(rows.m, D_MODEL), tuple(outs[1:])


def _proj_kernel(x_ref, sh_ref, sc_ref, g_ref, w_ref, o_ref, h_ref):
    @pl.when(pl.program_id(1) == 0)
    def _():
        h_ref[...] = _norm_mod(x_ref[...], g_ref[...], sh_ref[...], sc_ref[...])

    o_ref[...] = _bdot(h_ref[...], w_ref[...])


def _proj(x2d, rows, mod, layer, g_norm4, w3, n_out, tn):
    return pl.pallas_call(
        _proj_kernel,
        grid=(rows.n_tiles, pl.cdiv(n_out, tn)),
        in_specs=[
            rows.x_spec(D_MODEL),
            rows.mod_spec(layer, 3), rows.mod_spec(layer, 4),
            pl.BlockSpec((None, None, 1, D_MODEL), lambda i, j: (layer, 1, 0, 0)),
            pl.BlockSpec((None, D_MODEL, tn), lambda i, j: (0, 0, j)),
        ],
        out_specs=pl.BlockSpec((rows.tm, tn), lambda i, j: (i, j)),
        out_shape=jax.ShapeDtypeStruct((rows.m, n_out), F32),
        scratch_shapes=[pltpu.VMEM((rows.tm, D_MODEL), BF16)],
        compiler_params=_params("arbitrary", "arbitrary"),
        name="prenorm_proj",
    )(rows.view(x2d), mod, mod, g_norm4, w3)


def _glu_proj_kernel(x_ref, sh_ref, sc_ref, g_ref, wa_ref, wg_ref, ba_ref, bg_ref, o_ref, h_ref):
    @pl.when(pl.program_id(1) == 0)
    def _():
        h_ref[...] = _norm_mod(x_ref[...], g_ref[...], sh_ref[...], sc_ref[...])

    h = h_ref[...]
    a = _bdot(h, wa_ref[...]) + ba_ref[...]
    gt = _bdot(h, wg_ref[...]) + bg_ref[...]
    o_ref[...] = a * jax.nn.sigmoid(gt)


def _glu_proj(x2d, rows, mod, layer, g_norm4, w3, b3, tn):
    nb = CONV_DIM // tn
    return pl.pallas_call(
        _glu_proj_kernel,
        grid=(rows.n_tiles, nb),
        in_specs=[
            rows.x_spec(D_MODEL),
            rows.mod_spec(layer, 3), rows.mod_spec(layer, 4),
            pl.BlockSpec((None, None, 1, D_MODEL), lambda i, j: (layer, 1, 0, 0)),
            pl.BlockSpec((None, D_MODEL, tn), lambda i, j: (0, 0, j)),
            pl.BlockSpec((None, D_MODEL, tn), lambda i, j: (0, 0, j + nb)),
            pl.BlockSpec((None, 1, tn), lambda i, j: (0, 0, j)),
            pl.BlockSpec((None, 1, tn), lambda i, j: (0, 0, j + nb)),
        ],
        out_specs=pl.BlockSpec((rows.tm, tn), lambda i, j: (i, j)),
        out_shape=jax.ShapeDtypeStruct((rows.m, CONV_DIM), F32),
        scratch_shapes=[pltpu.VMEM((rows.tm, D_MODEL), BF16)],
        compiler_params=_params("arbitrary", "arbitrary"),
        name="prenorm_glu_proj",
    )(rows.view(x2d), mod, mod, g_norm4, w3, w3, b3, b3)


def _out_kernel(*refs, n_in, has_bias):
    a_refs = refs[:n_in]
    w_refs = refs[n_in:2 * n_in]
    pos = 2 * n_in
    b_ref = refs[pos] if has_bias else None
    pos += int(has_bias)
    x_ref, gt_ref, o_ref = refs[pos:pos + 3]
    y = _bdot(a_refs[0][...], w_refs[0][...])
    for a_ref, w_ref in zip(a_refs[1:], w_refs[1:]):
        y += _bdot(a_ref[...], w_ref[...])
    if has_bias:
        y += b_ref[...]
    o_ref[...] = x_ref[...] + gt_ref[...] * y.reshape(o_ref.shape)


def _out_proj(acts, w3, k_offsets, bias3, x2d, rows, mod, layer, tn):
    n_in = len(acts)
    in_specs = [pl.BlockSpec((rows.tm, a.shape[1]), lambda i, j: (i, 0)) for a in acts]
    for a, off in zip(acts, k_offsets):
        kb = off // a.shape[1]
        in_specs.append(pl.BlockSpec((None, a.shape[1], tn), lambda i, j, kb=kb: (0, kb, j)))
    args = list(acts) + [w3] * n_in
    if bias3 is not None:
        in_specs.append(pl.BlockSpec((None, 1, tn), lambda i, j: (0, 0, j)))
        args.append(bias3)
    in_specs += [rows.x_spec(tn, col=lambda j: j), rows.mod_spec(layer, 5, width=tn, col=lambda j: j)]
    args += [rows.view(x2d), mod]
    out = pl.pallas_call(
        functools.partial(_out_kernel, n_in=n_in, has_bias=bias3 is not None),
        grid=(rows.n_tiles, D_MODEL // tn),
        in_specs=in_specs,
        out_specs=rows.x_spec(tn, col=lambda j: j),
        out_shape=jax.ShapeDtypeStruct((rows.m // rows.bs, rows.bs, D_MODEL), F32),
        compiler_params=_params("arbitrary", "arbitrary"),
        name="out_proj_residual",
    )(*args)
    return out.reshape(rows.m, D_MODEL)


HGRN_ROWS = 512
HGRN_HEADS = 4
HGRN_TRI = 128


def _hgrn_prepare(f_ref, lb_ref, cum_s, kk_s, *, layer, c):
    p = lb_ref[...]
    e = jnp.exp(p - jnp.max(p, axis=0, keepdims=True))
    sm = e / jnp.sum(e, axis=0, keepdims=True)
    lb = jnp.sum(sm[:layer + 1], axis=0, keepdims=True)
    f = lb + (1.0 - lb) * jax.nn.sigmoid(f_ref[...])
    lf = jnp.log(f)
    n = HGRN_TRI
    r = lax.broadcasted_iota(jnp.int32, (n, n), 0)
    s = lax.broadcasted_iota(jnp.int32, (n, n), 1)
    tri = jnp.where((s <= r) & (s // c == r // c), 1.0, 0.0).astype(F32)
    for b in range(f.shape[0] // n):
        cum_s[b * n:(b + 1) * n, :] = jnp.dot(tri, lf[b * n:(b + 1) * n], preferred_element_type=F32,
                                              precision=lax.Precision.HIGHEST)
    kk_s[...] = 1.0 - f


def _hgrn_subchunk(r0, c, st, hh, q_ref, v_ref, g_ref, gn_ref, cum_s, kk_s):
    rows = pl.ds(r0, c)
    cols = slice(hh * A_DK, (hh + 1) * A_DK)
    cum = cum_s[rows, cols]
    q = q_ref[rows, cols]
    kk = kk_s[rows, cols]
    vv = v_ref[rows, cols]
    last = cum[c - 1:c, :]
    o = lax.dot_general((q * jnp.exp(cum)).astype(BF16), st.astype(BF16), (((1,), (1,)), ((), ())),
                        preferred_element_type=F32)
    srow = lax.broadcasted_iota(jnp.int32, (c, A_DK), 0)
    xs = []
    for t in range(c):
        d = jnp.where(srow <= t, cum[t:t + 1, :] - cum, NEG)
        xs.append(jnp.exp(d) * (q[t:t + 1, :] * kk))
    x = jnp.concatenate(xs, axis=0).astype(BF16)
    w = jnp.dot(x, jnp.ones((A_DK, A_DV), BF16), preferred_element_type=F32)
    o = o + jnp.sum(w.reshape(c, c, A_DV) * vv[None], axis=1)
    ke = kk * jnp.exp(last - cum)
    st_new = st * jnp.exp(last) + lax.dot_general(vv.astype(BF16), ke.astype(BF16), (((0,), (0,)), ((), ())),
                                                  preferred_element_type=F32)
    ms = jnp.mean(o * o, axis=-1, keepdims=True)
    y = o * lax.rsqrt(ms + EPS) * gn_ref[:, cols] * _silu(g_ref[rows, cols])
    return y, st_new


def _hgrn_prompt_kernel(q_ref, f_ref, v_ref, g_ref, lb_ref, gn_ref, o_ref, s_ref, st_ref, cum_s, kk_s, *, layer, c):
    i = pl.program_id(1)

    @pl.when(i == 0)
    def _():
        st_ref[...] = jnp.zeros_like(st_ref)

    _hgrn_prepare(f_ref, lb_ref, cum_s, kk_s, layer=layer, c=c)

    def body(n, carry):
        r0 = pl.multiple_of(n * c, c)
        for hh in range(HGRN_HEADS):
            y, st_new = _hgrn_subchunk(r0, c, st_ref[hh], hh, q_ref, v_ref, g_ref, gn_ref, cum_s, kk_s)
            st_ref[hh] = st_new
            o_ref[pl.ds(r0, c), hh * A_DV:(hh + 1) * A_DV] = y
        return carry

    lax.fori_loop(0, q_ref.shape[0] // c, body, 0)

    @pl.when(i == pl.num_programs(1) - 1)
    def _():
        for hh in range(HGRN_HEADS):
            s_ref[hh] = st_ref[hh].T


def _hgrn_sample_kernel(q_ref, f_ref, v_ref, g_ref, lb_ref, gn_ref, s0_ref, o_ref, s_ref, cum_s, kk_s, *, layer, c):
    _hgrn_prepare(f_ref, lb_ref, cum_s, kk_s, layer=layer, c=c)

    def body(n, carry):
        r0 = pl.multiple_of(n * c, c)
        for hh in range(HGRN_HEADS):
            y, st_new = _hgrn_subchunk(r0, c, s0_ref[n, hh].T, hh, q_ref, v_ref, g_ref, gn_ref, cum_s, kk_s)
            s_ref[n, hh] = st_new.T
            o_ref[pl.ds(r0, c), hh * A_DV:(hh + 1) * A_DV] = y
        return carry

    lax.fori_loop(0, q_ref.shape[0] // c, body, 0)


def _hgrn(u, lower_bound, norm_g2, layer, s0=None):
    m = u.shape[0]
    hb = HGRN_HEADS
    w = hb * A_DK
    nb = A_QK // w
    tc = HGRN_ROWS if s0 is None else HGRN_ROWS // 2

    def col(seg):
        return pl.BlockSpec((tc, w), lambda h, i, seg=seg: (i, seg * nb + h))

    in_specs = [col(0), col(1), col(2), col(3),
                pl.BlockSpec((DEPTH + 1, w), lambda h, i: (0, h)),
                pl.BlockSpec((1, w), lambda h, i: (0, h))]
    o_spec = pl.BlockSpec((tc, w), lambda h, i: (i, h))
    scratch = [pltpu.VMEM((tc, w), F32), pltpu.VMEM((tc, w), F32)]
    if s0 is None:
        return pl.pallas_call(
            functools.partial(_hgrn_prompt_kernel, layer=layer, c=16),
            grid=(nb, m // tc),
            in_specs=in_specs,
            out_specs=[o_spec, pl.BlockSpec((hb, A_DK, A_DV), lambda h, i: (h, 0, 0))],
            out_shape=[jax.ShapeDtypeStruct((m, A_WIDTH), F32), jax.ShapeDtypeStruct((A_HEADS, A_DK, A_DV), F32)],
            scratch_shapes=[pltpu.VMEM((hb, A_DV, A_DK), F32)] + scratch,
            compiler_params=_params("arbitrary", "arbitrary"),
            name="hgrn2_prompt",
        )(u, u, u, u, lower_bound, norm_g2)
    c = DEC_SEQ
    ns = tc // c
    s_spec = pl.BlockSpec((ns, hb, A_DK, A_DV), lambda h, i: (i, h, 0, 0))
    return pl.pallas_call(
        functools.partial(_hgrn_sample_kernel, layer=layer, c=c),
        grid=(nb, m // tc),
        in_specs=in_specs + [s_spec],
        out_specs=[o_spec, s_spec],
        out_shape=[jax.ShapeDtypeStruct((m, A_WIDTH), F32), jax.ShapeDtypeStruct(s0.shape, F32)],
        scratch_shapes=scratch,
        compiler_params=_params("arbitrary", "arbitrary"),
        name="hgrn2_sample",
    )(u, u, u, u, lower_bound, norm_g2, s0)


CONV_HALO = 32
CONV_ROWS = 256


def _ln_silu(y, g, b):
    mu = jnp.mean(y, axis=-1, keepdims=True)
    yc = y - mu
    var = jnp.mean(yc * yc, axis=-1, keepdims=True)
    return _silu(yc * lax.rsqrt(var + EPS) * g + b)


CONV_CHUNK_ROWS = 64
CONV_CHUNK_COLS = 512


def _conv_prompt_kernel(cur_ref, halo_ref, w_ref, b_ref, g_ref, lb_ref, o_ref, ext_ref, ph_ref, y_ref):
    i = pl.program_id(0)
    tt = cur_ref.shape[0]
    ext_ref[0:CONV_HALO, :] = jnp.where(i > 0, halo_ref[...], 0.0)
    ext_ref[CONV_HALO:, :] = cur_ref[...]
    off = CONV_HALO - (CONV_WIDTH - 1)
    span = ph_ref.shape[1]
    for r in range(1, SUBLANES):
        ph_ref[r - 1] = ext_ref[r:r + span, :]
    for c0 in range(0, CONV_DIM, CONV_CHUNK_COLS):
        cs = slice(c0, c0 + CONV_CHUNK_COLS)
        for r0 in range(0, tt, CONV_CHUNK_ROWS):
            y = jnp.zeros((CONV_CHUNK_ROWS, CONV_CHUNK_COLS), F32) + b_ref[:, cs]
            for w in range(CONV_WIDTH):
                a, r = divmod(off + w, SUBLANES)
                src = ext_ref if r == 0 else ph_ref.at[r - 1]
                lo = SUBLANES * a + r0
                y = y + src[lo:lo + CONV_CHUNK_ROWS, cs] * w_ref[w:w + 1, cs]
            y_ref[r0:r0 + CONV_CHUNK_ROWS, cs] = y
    o_ref[...] = _ln_silu(y_ref[...], g_ref[...], lb_ref[...]).astype(o_ref.dtype)


def _conv_prompt(glu, w_dw, b_dw, ln_g, ln_b):
    t = glu.shape[0]
    tt = CONV_ROWS
    r = tt // CONV_HALO
    vec = pl.BlockSpec((1, CONV_DIM), lambda i: (0, 0))
    return pl.pallas_call(
        _conv_prompt_kernel,
        grid=(t // tt,),
        in_specs=[
            pl.BlockSpec((tt, CONV_DIM), lambda i: (i, 0)),
            pl.BlockSpec((CONV_HALO, CONV_DIM), lambda i: (jnp.maximum(i * r - 1, 0), 0)),
            pl.BlockSpec((CONV_WIDTH, CONV_DIM), lambda i: (0, 0)),
            vec, vec, vec,
        ],
        out_specs=pl.BlockSpec((tt, CONV_DIM), lambda i: (i, 0)),
        out_shape=jax.ShapeDtypeStruct((t, CONV_DIM), BF16),
        scratch_shapes=[pltpu.VMEM((CONV_HALO + tt, CONV_DIM), F32),
                        pltpu.VMEM((SUBLANES - 1, CONV_HALO + tt - SUBLANES, CONV_DIM), F32),
                        pltpu.VMEM((tt, CONV_DIM), F32)],
        compiler_params=_params("arbitrary"),
        name="conv_prompt",
    )(glu, glu, w_dw, b_dw, ln_g, ln_b)


def _conv_sample_kernel(u_ref, buf_ref, w_ref, b_ref, g_ref, lb_ref, o_ref, nb_ref):
    hist = CONV_WIDTH - 1
    nt = u_ref.shape[0]

    def ext(j):
        return buf_ref[j] if j < hist else u_ref[j - hist]

    for t in range(nt):
        y = ext(t) * w_ref[0:1, :] + b_ref[...]
        for w in range(1, CONV_WIDTH):
            y = y + ext(t + w) * w_ref[w:w + 1, :]
        o_ref[t] = _ln_silu(y, g_ref[...], lb_ref[...]).astype(o_ref.dtype)
    for j in range(hist):
        nb_ref[j] = ext(j + nt)


def _conv_sample(glu_t, buf_t, w_dw, b_dw, ln_g, ln_b):
    nt, ns, _ = glu_t.shape
    hist = CONV_WIDTH - 1
    bs = 32
    vec = pl.BlockSpec((1, CONV_DIM), lambda i: (0, 0))
    return pl.pallas_call(
        _conv_sample_kernel,
        grid=(ns // bs,),
        in_specs=[
            pl.BlockSpec((nt, bs, CONV_DIM), lambda i: (0, i, 0)),
            pl.BlockSpec((hist, bs, CONV_DIM), lambda i: (0, i, 0)),
            pl.BlockSpec((CONV_WIDTH, CONV_DIM), lambda i: (0, 0)),
            vec, vec, vec,
        ],
        out_specs=[pl.BlockSpec((nt, bs, CONV_DIM), lambda i: (0, i, 0)),
                   pl.BlockSpec((hist, bs, CONV_DIM), lambda i: (0, i, 0))],
        out_shape=[jax.ShapeDtypeStruct((nt, ns, CONV_DIM), BF16), jax.ShapeDtypeStruct((hist, ns, CONV_DIM), F32)],
        compiler_params=_params("arbitrary"),
        name="conv_sample",
    )(glu_t, buf_t, w_dw, b_dw, ln_g, ln_b)


PAIR = 2 * B_DH
N_PAIRS = KV_WIDTH // PAIR
BLOCKS_PER_TOKEN = SEL_BLOCK // CMP_STRIDE
N_SUB = CMP_BLOCK // CMP_STRIDE
CMP_OFF = 120
CMP_ROWS = 640
KV_PAD = WINDOW
FAR_TILE = 512
LOWEST = -3.0e38
MASK_C = 2.0 ** 100


def _half(shape, half):
    lane = lax.broadcasted_iota(jnp.int32, shape, len(shape) - 1)
    return (lane % PAIR) // B_DH == half


def _group_q(q_ref, g):
    parts = []
    for r in range(B_GROUP):
        h = g * B_GROUP + r
        x = q_ref[:, (h // 2) * PAIR:(h // 2 + 1) * PAIR]
        x = jnp.where(_half(x.shape, h % 2), x, 0.0) * (B_DH ** -0.5)
        if h % 2 != g % 2:
            x = pltpu.roll(x, B_DH, axis=1)
        parts.append(x)
    return jnp.concatenate(parts, axis=0).astype(BF16)


def _gate_rows(gate_ref, branch, g):
    sig = jax.nn.sigmoid(gate_ref[...])
    c0 = branch * B_HEADS + g * B_GROUP
    return jnp.concatenate([sig[:, c0 + r:c0 + r + 1] for r in range(B_GROUP)], axis=0)


def _scatter_heads(o, g):
    nq = o.shape[0] // B_GROUP
    outs = []
    for jp in range(2):
        acc = None
        for e in range(2):
            r = 2 * jp + e
            x = jnp.where(_half((nq, PAIR), g % 2), o[r * nq:(r + 1) * nq], 0.0)
            if e != g % 2:
                x = pltpu.roll(x, B_DH, axis=1)
            acc = x if acc is None else acc + x
        outs.append(acc)
    return jnp.concatenate(outs, axis=1)


def _nt_dot(a, b):
    return lax.dot_general(a, b, (((1,), (1,)), ((), ())), preferred_element_type=F32)


def _piece(qg, k, v, mask, bias=None):
    s = _nt_dot(qg, k)
    if bias is not None:
        s = s + bias
    s = jnp.where(mask, s, NEG)
    m = jnp.max(s, axis=-1, keepdims=True)
    p = jnp.where(mask, jnp.exp(s - m), 0.0)
    return m, jnp.sum(p, axis=-1, keepdims=True), jnp.dot(p.astype(BF16), v, preferred_element_type=F32), p


def _merge(a, b):
    m = jnp.maximum(a[0], b[0])
    ea = jnp.exp(a[0] - m)
    eb = jnp.exp(b[0] - m)
    return m, a[1] * ea + b[1] * eb, a[2] * ea + b[2] * eb


def _inv_or_zero(l):
    return jnp.where(l > 0.0, 1.0 / l, 0.0)


def _dot01(p, o01):
    hi = p.astype(BF16)
    r1 = p - hi.astype(F32)
    mid = r1.astype(BF16)
    lo = (r1 - mid.astype(F32)).astype(BF16)
    d = lambda a: jnp.dot(a, o01, preferred_element_type=F32)
    return d(hi) + d(mid) + d(lo)


def _dot01_t(o01_t, p):
    hi = p.astype(BF16)
    r1 = p - hi.astype(F32)
    mid = r1.astype(BF16)
    lo = (r1 - mid.astype(F32)).astype(BF16)
    return _nt_dot(o01_t, hi) + _nt_dot(o01_t, mid) + _nt_dot(o01_t, lo)


def _overlap01(tok, blk):
    return jnp.where((tok >= BLOCKS_PER_TOKEN * blk - (N_SUB - 1)) & (tok <= BLOCKS_PER_TOKEN * blk + BLOCKS_PER_TOKEN - 1),
                     1.0, 0.0).astype(BF16)


def _select_blocks(imp, qpos, n_cand=None, blocks_axis=1):
    j = lax.broadcasted_iota(jnp.int32, imp.shape, blocks_axis)
    cur = qpos // SEL_BLOCK
    forced = (j == 0) | (j == cur) | (j == cur - 1)
    valid = j * SEL_BLOCK <= qpos
    imp = jnp.where(forced, FORCE, jnp.where(valid, imp, -FORCE))
    if n_cand is not None:
        assert blocks_axis == 1
        ahead = jnp.zeros(imp.shape, F32)
        for i in range(n_cand):
            v = imp[:, i:i + 1]
            ahead = ahead + jnp.where((v > imp) | ((v == imp) & (j > i)), 1.0, 0.0)
        return jnp.where((ahead < N_SELECT) & (j < n_cand), 1.0, 0.0)
    sel = jnp.zeros(imp.shape, F32)
    jf = j.astype(F32)
    for _ in range(N_SELECT):
        m = jnp.max(imp, axis=blocks_axis, keepdims=True)
        first = jnp.min(jnp.where(imp == m, jf, float(imp.shape[blocks_axis])), axis=blocks_axis, keepdims=True)
        pick = jf == first
        sel = jnp.where(pick, 1.0, sel)
        imp = jnp.where(pick, LOWEST, imp)
    return sel


def _expand_blocks(sel_rows, first_blk, n_keys):
    jj = lax.broadcasted_iota(jnp.int32, (sel_rows.shape[1], n_keys), 0)
    kk = lax.broadcasted_iota(jnp.int32, (sel_rows.shape[1], n_keys), 1)
    e = jnp.where(jj == first_blk + kk // SEL_BLOCK, 1.0, 0.0).astype(BF16)
    return jnp.dot(sel_rows, e, preferred_element_type=F32)


def _bias_tables(rel_bias):
    d = np.arange(MAX_DISTANCE + 1)
    exact = N_BUCKETS // 2
    large = exact + (np.log(np.maximum(d, 1).astype(np.float32) / exact) / math.log(MAX_DISTANCE / exact)
                     * (N_BUCKETS - exact)).astype(np.int32)
    bucket = np.where(d < exact, d, np.minimum(large, N_BUCKETS - 1))
    by_dist = rel_bias.astype(F32)[bucket]

    def tile(c, nq, n, step=1):
        n1 = step * (n - 1) + 1
        dmin, dmax = c - (n1 - 1), c + nq - 1
        n_neg = max(0, min(0, dmax + 1) - dmin)
        lo, hi = max(dmin, 0), min(dmax, MAX_DISTANCE)
        n_far = max(0, dmax - max(dmin, MAX_DISTANCE + 1) + 1)
        v = jnp.concatenate([jnp.broadcast_to(by_dist[:1], (n_neg, B_HEADS)), by_dist[lo:hi + 1],
                             jnp.broadcast_to(by_dist[MAX_DISTANCE:], (n_far, B_HEADS))], axis=0)
        lv = nq + n1 - 1
        p = jnp.concatenate([v[::-1], jnp.zeros((1, B_HEADS), F32)], axis=0)
        rows = jnp.tile(p, (nq, 1))[:nq * lv].reshape(nq, lv, B_HEADS)[:, nq - 1:nq - 1 + n1:step]
        t = rows.reshape(nq, n, B_KV_HEADS, B_GROUP).transpose(2, 3, 0, 1)
        return t.reshape(B_KV_HEADS, B_GROUP * nq, n)

    nq = PAIR
    far = by_dist[MAX_DISTANCE]
    shift = jnp.repeat(far.reshape(B_KV_HEADS, B_GROUP), nq, axis=1)[:, :, None]
    n_var = 32
    m0 = PAIR - n_var
    c_cmp = -CMP_STRIDE * (m0 - CMP_OFF) - (CMP_BLOCK - 1)
    assert c_cmp + CMP_STRIDE >= MAX_DISTANCE
    prompt_cmp = jnp.concatenate([jnp.zeros((B_KV_HEADS, B_GROUP * nq, m0), F32),
                                  tile(c_cmp, nq, n_var, CMP_STRIDE) - shift], axis=2)
    causal = np.tile(np.arange(nq)[:, None] + nq - np.arange(2 * nq)[None, :] >= 0, (B_GROUP, 1))[None]
    prompt_kv = jnp.where(causal, tile(nq, nq, 2 * nq) - shift, -MASK_C)
    s_cmp = tile(PAST_LEN - (CMP_BLOCK - 1), DEC_SEQ, PAIR, CMP_STRIDE)
    s_sel = tile(PAST_LEN, DEC_SEQ, PAST_LEN + PAIR)
    s_win = tile(WINDOW, DEC_SEQ, WINDOW + PAIR)
    flat = lambda a: a.reshape(B_KV_HEADS * B_GROUP * DEC_SEQ, a.shape[-1])
    return prompt_cmp, prompt_kv, flat(s_cmp), flat(s_sel), flat(s_win)


def _cmp_weights(pe, w1, b1, w2, b2):
    k1 = CMP_STRIDE * B_DH
    w1f = w1.reshape(2, N_SUB, k1, CMP_HIDDEN).astype(BF16)
    pef = pe.reshape(2, N_SUB, 1, k1)
    b1f = b1.reshape(2, 1, CMP_HIDDEN)
    w2p = jnp.einsum('khd,ef->kehfd', w2, jnp.eye(2, dtype=F32)).reshape(2, 2, CMP_HIDDEN, PAIR).astype(BF16)
    b2p = jnp.tile(b2, (1, 2)).reshape(2, 1, PAIR)
    return pef, w1f, b1f, w2p, b2p


def _compress_pair(src, pe_ref, w1_ref, b1_ref, w2_ref, b2_ref):
    rows = [src(p) for p in range(CMP_STRIDE)]
    swapped = [pltpu.roll(r, B_DH, axis=1) for r in rows]
    low = _half(rows[0].shape, 0)
    m_rows = rows[0].shape[0]
    out = b2_ref[...]
    for e in range(2):
        x = jnp.concatenate([jnp.where(low, (rows, swapped)[e][p], (swapped, rows)[e][p + 1])
                             for p in range(0, CMP_STRIDE, 2)], axis=1)
        h = b1_ref[...]
        for m in range(N_SUB):
            part = jnp.dot((x + pe_ref[m]).astype(BF16), w1_ref[m], preferred_element_type=F32)
            h = h + (part if m == 0 else pltpu.roll(part, m_rows - m, axis=0))
        out = out + jnp.dot(_silu(h).astype(BF16), w2_ref[e], preferred_element_type=F32)
    return out


def _cmp_weight_specs(kv_of):
    k1 = CMP_STRIDE * B_DH
    return [
        pl.BlockSpec((None, N_SUB, 1, k1), lambda *a: (kv_of(*a), 0, 0, 0)),
        pl.BlockSpec((None, N_SUB, k1, CMP_HIDDEN), lambda *a: (kv_of(*a), 0, 0, 0)),
        pl.BlockSpec((None, 1, CMP_HIDDEN), lambda *a: (kv_of(*a), 0, 0)),
        pl.BlockSpec((None, 2, CMP_HIDDEN, PAIR), lambda *a: (kv_of(*a), 0, 0, 0)),
        pl.BlockSpec((None, 1, PAIR), lambda *a: (kv_of(*a), 0, 0)),
    ]


def _compress_prompt_kernel(rows_ref, pe_ref, w1_ref, b1_ref, w2_ref, b2_ref, o_ref):
    n_blk = rows_ref.shape[0] // CMP_STRIDE
    tok = _compress_pair(lambda p: rows_ref[pl.ds(p, n_blk, stride=CMP_STRIDE), :], pe_ref, w1_ref, b1_ref, w2_ref, b2_ref)
    o_ref[...] = jnp.zeros_like(o_ref)
    o_ref[CMP_OFF:CMP_OFF + n_blk, :] = tok


def _compress_prompt(u, weights):
    t = u.shape[0]
    col0 = (2 * A_QK + 2 * A_WIDTH + B_WIDTH) // PAIR
    return pl.pallas_call(
        _compress_prompt_kernel,
        grid=(2 * N_PAIRS,),
        in_specs=[pl.BlockSpec((t, PAIR), lambda c: (0, col0 + c))] + _cmp_weight_specs(lambda c: c // N_PAIRS),
        out_specs=pl.BlockSpec((None, CMP_ROWS, PAIR), lambda c: (c, 0, 0)),
        out_shape=jax.ShapeDtypeStruct((2 * N_PAIRS, CMP_ROWS, PAIR), F32),
        compiler_params=_params("arbitrary"),
        name="nsa_compress_prompt",
    )(u, *weights)


CMP_SEQ_BLOCK = 8
N_PAGES = PAST_LEN // PAGE_SIZE


def _compress_sample_kernel(pt_ref, *refs):
    n_src = CMP_SEQ_BLOCK * N_PAGES
    pages = refs[:n_src]
    pe_ref, w1_ref, b1_ref, w2_ref, b2_ref, o_ref, x_ref = refs[n_src:]
    n_blk = PAGE_SIZE // CMP_STRIDE
    for idx, pg in enumerate(pages):
        x_ref[idx] = pg[...].T
    tok = _compress_pair(
        lambda p: jnp.concatenate([x_ref[idx, pl.ds(p, n_blk, stride=CMP_STRIDE), :] for idx in range(n_src)], axis=0),
        pe_ref, w1_ref, b1_ref, w2_ref, b2_ref)
    o_ref[...] = tok.reshape(o_ref.shape)


def _compress_sample(cache, page_table, weights):
    n_seq = page_table.shape[0]
    sb = CMP_SEQ_BLOCK
    n_tok = PAST_LEN // CMP_STRIDE
    page_specs = [pl.BlockSpec((None, PAIR, PAGE_SIZE), lambda c, i, pt, s=s, j=j: (pt[i * sb + s, j], c, 0))
                  for s in range(sb) for j in range(N_PAGES)]
    return pl.pallas_call(
        _compress_sample_kernel,
        grid_spec=pltpu.PrefetchScalarGridSpec(
            num_scalar_prefetch=1,
            grid=(2 * N_PAIRS, n_seq // sb),
            in_specs=page_specs + _cmp_weight_specs(lambda c, i, pt: c // N_PAIRS),
            out_specs=pl.BlockSpec((sb, None, n_tok, PAIR), lambda c, i, pt: (i, c, 0, 0)),
            scratch_shapes=[pltpu.VMEM((sb * N_PAGES, PAGE_SIZE, PAIR), F32)],
        ),
        out_shape=jax.ShapeDtypeStruct((n_seq, 2 * N_PAIRS, n_tok, PAIR), F32),
        compiler_params=_params("arbitrary", "arbitrary"),
        name="nsa_compress_sample",
    )(page_table, *([cache] * (sb * N_PAGES)), *weights)


Q_ROWS = 128


def _nsa_cmp_prompt_kernel(q_ref, gate_ref, cmp_ref, bias_ref, oc_ref, sel_ref):
    qb = pl.program_id(0)
    nq = q_ref.shape[0]
    rows = B_GROUP * nq
    n_tok = CMP_ROWS - PAIR
    near0 = pl.multiple_of(qb * (nq // CMP_STRIDE), SUBLANES)
    tok0 = near0 - CMP_OFF
    mask_far = lax.broadcasted_iota(jnp.int32, (rows, n_tok), 1) < tok0
    i = lax.broadcasted_iota(jnp.int32, (rows, PAIR), 0) % nq
    mn = lax.broadcasted_iota(jnp.int32, (rows, PAIR), 1)
    dist = i - CMP_STRIDE * (mn - CMP_OFF) - (CMP_BLOCK - 1)
    mask_near = (dist >= 0) & (tok0 + mn >= 0)
    o_far = _overlap01(lax.broadcasted_iota(jnp.int32, (PAIR, n_tok), 1), lax.broadcasted_iota(jnp.int32, (PAIR, n_tok), 0))
    o_near = _overlap01(tok0 + lax.broadcasted_iota(jnp.int32, (PAIR, PAIR), 1), lax.broadcasted_iota(jnp.int32, (PAIR, PAIR), 0))
    qpos = qb * nq + lax.broadcasted_iota(jnp.int32, (PAIR, nq), 1)
    for g in range(B_KV_HEADS):
        qg = _group_q(q_ref, g)
        kp, vp = g // 2, N_PAIRS + g // 2
        far = _piece(qg, cmp_ref[kp, CMP_OFF:CMP_OFF + n_tok, :].astype(BF16),
                     cmp_ref[vp, CMP_OFF:CMP_OFF + n_tok, :].astype(BF16), mask_far)
        near = _piece(qg, cmp_ref[kp, pl.ds(near0, PAIR), :].astype(BF16),
                      cmp_ref[vp, pl.ds(near0, PAIR), :].astype(BF16), mask_near, bias_ref[g])
        m, l, acc = _merge(far[:3], near[:3])
        linv = _inv_or_zero(l)
        oc_ref[:, g * 2 * PAIR:(g + 1) * 2 * PAIR] = _scatter_heads(acc * linv * _gate_rows(gate_ref, 0, g), g)
        pf = jnp.sum((far[3] * (jnp.exp(far[0] - m) * linv)).reshape(B_GROUP, nq, n_tok), axis=0)
        pn = jnp.sum((near[3] * (jnp.exp(near[0] - m) * linv)).reshape(B_GROUP, nq, PAIR), axis=0)
        imp_t = _dot01_t(o_far, pf) + _dot01_t(o_near, pn)
        sel_ref[:, g * PAIR:(g + 1) * PAIR] = _select_blocks(imp_t, qpos, blocks_axis=0).T


def _flash_step(st, qx, kx, v, bias=None):
    m_old, l, acc = st
    s = _nt_dot(qx, kx)
    if bias is not None:
        s = s + bias
    m = jnp.maximum(m_old, jnp.max(s, axis=-1, keepdims=True))
    p = jnp.exp(s - m)
    alpha = jnp.exp(m_old - m)
    return m, alpha * l + jnp.sum(p, axis=-1, keepdims=True), alpha * acc + jnp.dot(p.astype(BF16), v, preferred_element_type=F32)


def _nsa_selwin_prompt_kernel(q_ref, gate_ref, sel_ref, oc_ref, kv_ref, oh_ref, ohr_ref, bias_ref, wbias_ref, o_ref):
    qb = pl.program_id(0)
    nq = q_ref.shape[0]
    rows = B_GROUP * nq
    qs = qb * nq
    per_q = nq // SEL_BLOCK
    n_win = WINDOW + nq
    near_rows = pl.ds(pl.multiple_of(qs + KV_PAD - nq, nq), 2 * nq)
    win_rows = pl.ds(pl.multiple_of(qs, nq), n_win)
    lane = lax.broadcasted_iota(jnp.int32, (rows, PAIR), 1)
    first_near = per_q * (qb - 1)
    first_win = per_q * qb - WINDOW // SEL_BLOCK
    jj = lax.broadcasted_iota(jnp.int32, (PAIR, PAIR), 0)
    bb = lax.broadcasted_iota(jnp.int32, (PAIR, PAIR), 1)
    to_near = jnp.where(jj == first_near + bb, 1.0, 0.0).astype(BF16)
    exists_near = jnp.where(first_near + lane >= 0, 0.0, -MASK_C).astype(BF16)
    exists_win = jnp.where(first_win + lane >= 0, 0.0, -MASK_C).astype(BF16)
    n_far = (qb + 2) // (FAR_TILE // nq)
    init = (jnp.full((rows, 1), NEG, F32), jnp.zeros((rows, 1), F32), jnp.zeros((rows, PAIR), F32))
    for g in range(B_KV_HEADS):
        qg = _group_q(q_ref, g)
        cols = lambda base: slice(base + (g // 2) * PAIR, base + (g // 2 + 1) * PAIR)
        selm = jnp.concatenate([sel_ref[:, g * PAIR:(g + 1) * PAIR]] * B_GROUP, axis=0)
        far_vec = jnp.where((selm > 0.5) & (lane < first_near), 0.0, -MASK_C).astype(BF16)
        near_sel = jnp.dot(selm.astype(BF16), to_near, preferred_element_type=F32)
        near_vec = jnp.where(near_sel > 0.5, 0.0, -MASK_C).astype(BF16) + exists_near
        qx_far = jnp.concatenate([qg, far_vec], axis=1)

        def far_step(t, st):
            r = pl.ds(pl.multiple_of(KV_PAD + t * FAR_TILE, FAR_TILE), FAR_TILE)
            kx = jnp.concatenate([kv_ref[r, cols(0)], oh_ref[r, :]], axis=1)
            return _flash_step(st, qx_far, kx, kv_ref[r, cols(KV_WIDTH)])

        st = lax.fori_loop(0, n_far, far_step, init)
        st = _flash_step(st, jnp.concatenate([qg, near_vec], axis=1),
                         jnp.concatenate([kv_ref[near_rows, cols(0)], ohr_ref[:2 * nq, :]], axis=1),
                         kv_ref[near_rows, cols(KV_WIDTH)], bias_ref[g])
        o_s = st[2] * _inv_or_zero(st[1])

        s_w = _nt_dot(jnp.concatenate([qg, exists_win], axis=1),
                      jnp.concatenate([kv_ref[win_rows, cols(2 * KV_WIDTH)], ohr_ref[...]], axis=1)) + wbias_ref[g]
        p_w = jnp.exp(s_w - jnp.max(s_w, axis=-1, keepdims=True))
        o_w = (jnp.dot(p_w.astype(BF16), kv_ref[win_rows, cols(3 * KV_WIDTH)], preferred_element_type=F32)
               * _inv_or_zero(jnp.sum(p_w, axis=-1, keepdims=True)))
        o = o_s * _gate_rows(gate_ref, 1, g) + o_w * _gate_rows(gate_ref, 2, g)
        blk = slice(g * 2 * PAIR, (g + 1) * 2 * PAIR)
        o_ref[:, blk] = _scatter_heads(o, g) + oc_ref[:, blk]


def _block_onehots(t):
    pos = np.arange(KV_PAD + t) - KV_PAD
    absolute = (pos[:, None] // SEL_BLOCK == np.arange(PAIR)[None, :]) & (pos[:, None] >= 0)
    relative = np.arange(WINDOW + Q_ROWS)[:, None] // SEL_BLOCK == np.arange(PAIR)[None, :]
    i = np.arange(B_GROUP * Q_ROWS)[:, None] % Q_ROWS
    in_window = np.arange(WINDOW - Q_ROWS)[None, :] > i
    return (jnp.asarray(absolute, BF16), jnp.asarray(relative, BF16),
            jnp.asarray(np.where(in_window, 0.0, -MASK_C), F32))


def _nsa_prompt(u, kv_pad, cmp_tok, bias_cmp, bias_kv):
    t = u.shape[0]
    nq = Q_ROWS
    q_col = (2 * A_QK + 2 * A_WIDTH) // B_WIDTH
    g_col = (IN_DIM - 3 * B_HEADS) // PAIR
    q_spec = pl.BlockSpec((nq, B_WIDTH), lambda i: (i, q_col))
    gate_spec = pl.BlockSpec((nq, PAIR), lambda i: (i, g_col))
    oc, sel = pl.pallas_call(
        _nsa_cmp_prompt_kernel,
        grid=(t // nq,),
        in_specs=[q_spec, gate_spec,
                  pl.BlockSpec(cmp_tok.shape, lambda i: (0, 0, 0)),
                  pl.BlockSpec(bias_cmp.shape, lambda i: (0, 0, 0))],
        out_specs=[pl.BlockSpec((nq, B_WIDTH), lambda i: (i, 0)), pl.BlockSpec((nq, B_KV_HEADS * PAIR), lambda i: (i, 0))],
        out_shape=[jax.ShapeDtypeStruct((t, B_WIDTH), F32), jax.ShapeDtypeStruct((t, B_KV_HEADS * PAIR), F32)],
        compiler_params=_params("arbitrary"),
        name="nsa_cmp_select_prompt",
    )(u, u, cmp_tok, bias_cmp)
    oh_abs, oh_rel, win_mask = _block_onehots(t)
    win_bias = jnp.concatenate([jnp.broadcast_to(win_mask, (B_KV_HEADS,) + win_mask.shape), bias_kv], axis=2)
    whole = lambda a: pl.BlockSpec(a.shape, lambda i: (0,) * a.ndim, pipeline_mode=pl.Buffered(1))
    return pl.pallas_call(
        _nsa_selwin_prompt_kernel,
        grid=(t // nq,),
        in_specs=[q_spec, gate_spec,
                  pl.BlockSpec((nq, B_KV_HEADS * PAIR), lambda i: (i, 0)),
                  pl.BlockSpec((nq, B_WIDTH), lambda i: (i, 0)),
                  whole(kv_pad), whole(oh_abs), whole(oh_rel), whole(bias_kv), whole(win_bias)],
        out_specs=pl.BlockSpec((nq, B_WIDTH), lambda i: (i, 0)),
        out_shape=jax.ShapeDtypeStruct((t, B_WIDTH), F32),
        compiler_params=_params("arbitrary"),
        name="nsa_select_window_prompt",
    )(u, u, sel, oc, kv_pad, oh_abs, oh_rel, bias_kv, win_bias)


SEL_KEYS = PAST_LEN + PAIR
WIN_KEYS = WINDOW + PAIR


NSA_SEQ_BLOCK = 2


def _piece_t(qg, kt, vt, mask, bias):
    s = jnp.dot(qg, kt, preferred_element_type=F32) + bias
    s = jnp.where(mask, s, NEG)
    m = jnp.max(s, axis=-1, keepdims=True)
    p = jnp.where(mask, jnp.exp(s - m), 0.0)
    return m, jnp.sum(p, axis=-1, keepdims=True), _nt_dot(p.astype(BF16), vt), p


def _nsa_sample_kernel(pt_ref, *refs):
    n_pg = NSA_SEQ_BLOCK * N_PAGES
    pages = refs[:n_pg]
    (q_ref, gate_ref, cmp_ref, selnew_ref, winnew_ref, win_ref, bc_ref, bs_ref, bw_ref,
     o_ref, winout_ref, kt_ref, vt_ref, wkt_ref, wvt_ref) = refs[n_pg:]
    seqs = [_nsa_sample_one(pages[s * N_PAGES:(s + 1) * N_PAGES], q_ref.at[s], gate_ref.at[s], cmp_ref.at[s],
                            selnew_ref.at[s], winnew_ref.at[s], win_ref.at[s], bc_ref, bs_ref, bw_ref,
                            o_ref.at[s], winout_ref.at[s], kt_ref.at[s], vt_ref.at[s], wkt_ref.at[s], wvt_ref.at[s])
            for s in range(NSA_SEQ_BLOCK)]
    for _ in range(NSA_SAMPLE_STAGES):
        for seq in seqs:
            next(seq)


NSA_SAMPLE_STAGES = 5


def _nsa_sample_one(pages, q_ref, gate_ref, cmp_ref, selnew_ref, winnew_ref, win_ref, bc_ref, bs_ref, bw_ref,
                    o_ref, winout_ref, kt_ref, vt_ref, wkt_ref, wvt_ref):
    nt = q_ref.shape[0]
    grp = B_GROUP * nt
    rows = B_KV_HEADS * grp
    half = 2 * PAIR
    pad = jnp.zeros((PAIR - nt, 2 * half), F32)
    sel_new_t = jnp.concatenate([selnew_ref[...], pad], axis=0).T
    win_new_t = jnp.concatenate([winnew_ref[...], pad], axis=0).T
    for j, pg in enumerate(pages):
        kt_ref[:, j * PAGE_SIZE:(j + 1) * PAGE_SIZE] = pg[:half, :].astype(BF16)
        vt_ref[:, j * PAGE_SIZE:(j + 1) * PAGE_SIZE] = pg[half:, :].astype(BF16)
    kt_ref[:, PAST_LEN:] = sel_new_t[:half].astype(BF16)
    vt_ref[:, PAST_LEN:] = sel_new_t[half:].astype(BF16)
    buf = win_ref[...]
    wkt_ref[:, :WINDOW] = buf[:half].astype(BF16)
    wvt_ref[:, :WINDOW] = buf[half:].astype(BF16)
    wkt_ref[:, WINDOW:] = win_new_t[:half].astype(BF16)
    wvt_ref[:, WINDOW:] = win_new_t[half:].astype(BF16)
    shifted = pltpu.roll(buf, WINDOW - nt, axis=1)
    tail = pltpu.roll(win_new_t, PAIR - nt, axis=1)
    lane = lax.broadcasted_iota(jnp.int32, tail.shape, 1)
    winout_ref[:, :WINDOW - PAIR] = shifted[:, :WINDOW - PAIR]
    winout_ref[:, WINDOW - PAIR:] = jnp.where(lane >= PAIR - nt, tail, shifted[:, WINDOW - PAIR:])
    yield

    zero = jnp.zeros((grp, PAIR), BF16)
    qq = jnp.concatenate(
        [jnp.concatenate([_group_q(q_ref, g), zero] if g // 2 == 0 else [zero, _group_q(q_ref, g)], axis=1)
         for g in range(B_KV_HEADS)], axis=0)
    take = lambda acc, g: acc[g * grp:(g + 1) * grp, (g // 2) * PAIR:(g // 2 + 1) * PAIR]
    qpos = PAST_LEN + lax.broadcasted_iota(jnp.int32, (rows, 1), 0) % nt

    n_tok = cmp_ref.shape[1]
    ck = jnp.concatenate([cmp_ref[0], cmp_ref[1]], axis=1).astype(BF16)
    cv = jnp.concatenate([cmp_ref[2], cmp_ref[3]], axis=1).astype(BF16)
    mask_c = lax.broadcasted_iota(jnp.int32, (rows, n_tok), 1) < n_tok - (N_SUB - 1)
    mc, lc, acc_c, pc = _piece(qq, ck, cv, mask_c, bc_ref[...])
    pc = pc * _inv_or_zero(lc)
    p_all = jnp.concatenate([jnp.sum(pc[g * grp:(g + 1) * grp].reshape(B_GROUP, nt, n_tok), axis=0)
                             for g in range(B_KV_HEADS)], axis=0)
    yield
    o01 = _overlap01(lax.broadcasted_iota(jnp.int32, (n_tok, PAIR), 0), lax.broadcasted_iota(jnp.int32, (n_tok, PAIR), 1))
    qpos_gt = PAST_LEN + lax.broadcasted_iota(jnp.int32, (B_KV_HEADS * nt, PAIR), 0) % nt
    n_blocks = pl.cdiv(PAST_LEN + nt, SEL_BLOCK)
    sel = _select_blocks(_dot01(p_all, o01), qpos_gt, n_cand=n_blocks).astype(BF16)
    sel_rows = jnp.concatenate([sel[g * nt:(g + 1) * nt] for g in range(B_KV_HEADS) for _ in range(B_GROUP)], axis=0)
    yield

    ks = lax.broadcasted_iota(jnp.int32, (rows, SEL_KEYS), 1)
    mask_s = (_expand_blocks(sel_rows, 0, SEL_KEYS) > 0.5) & (ks <= qpos)
    ms, ls, acc_s, _ = _piece_t(qq, kt_ref[...], vt_ref[...], mask_s, bs_ref[...])
    yield
    kw = PAST_LEN - WINDOW + lax.broadcasted_iota(jnp.int32, (rows, WIN_KEYS), 1)
    mask_w = (kw <= qpos) & (qpos - kw < WINDOW)
    mw, lw, acc_w, _ = _piece_t(qq, wkt_ref[...], wvt_ref[...], mask_w, bw_ref[...])
    acc_c, acc_s, acc_w = acc_c * _inv_or_zero(lc), acc_s * _inv_or_zero(ls), acc_w * _inv_or_zero(lw)
    for g in range(B_KV_HEADS):
        o = (take(acc_c, g) * _gate_rows(gate_ref, 0, g) + take(acc_s, g) * _gate_rows(gate_ref, 1, g)
             + take(acc_w, g) * _gate_rows(gate_ref, 2, g))
        o_ref[:, g * 2 * PAIR:(g + 1) * 2 * PAIR] = _scatter_heads(o, g)
    yield


def _nsa_sample(u_s, cmp_tok, cache_sel, win_buf, page_table, bias_c, bias_s, bias_w):
    n_seq, nt, _ = u_s.shape
    sb = NSA_SEQ_BLOCK
    q_col = (2 * A_QK + 2 * A_WIDTH) // B_WIDTH
    kv_col = (2 * A_QK + 2 * A_WIDTH + B_WIDTH) // (4 * PAIR)
    g_col = (IN_DIM - 3 * B_HEADS) // PAIR
    const = lambda a: pl.BlockSpec(a.shape, lambda b, pt: (0, 0))
    in_specs = [pl.BlockSpec((None, 4 * PAIR, PAGE_SIZE), lambda b, pt, s=s, j=j: (pt[b * sb + s, j], 0, 0))
                for s in range(sb) for j in range(N_PAGES)]
    in_specs += [
        pl.BlockSpec((sb, nt, B_WIDTH), lambda b, pt: (b, 0, q_col)),
        pl.BlockSpec((sb, nt, PAIR), lambda b, pt: (b, 0, g_col)),
        pl.BlockSpec((sb,) + cmp_tok.shape[1:], lambda b, pt: (b, 0, 0, 0)),
        pl.BlockSpec((sb, nt, 4 * PAIR), lambda b, pt: (b, 0, kv_col + 1)),
        pl.BlockSpec((sb, nt, 4 * PAIR), lambda b, pt: (b, 0, kv_col + 2)),
        pl.BlockSpec((sb, 4 * PAIR, WINDOW), lambda b, pt: (b, 0, 0)),
        const(bias_c), const(bias_s), const(bias_w),
    ]
    return pl.pallas_call(
        _nsa_sample_kernel,
        grid_spec=pltpu.PrefetchScalarGridSpec(
            num_scalar_prefetch=1,
            grid=(n_seq // sb,),
            in_specs=in_specs,
            out_specs=[pl.BlockSpec((sb, nt, B_WIDTH), lambda b, pt: (b, 0, 0)),
                       pl.BlockSpec((sb, 4 * PAIR, WINDOW), lambda b, pt: (b, 0, 0))],
            scratch_shapes=[pltpu.VMEM((sb, 2 * PAIR, SEL_KEYS), BF16), pltpu.VMEM((sb, 2 * PAIR, SEL_KEYS), BF16),
                            pltpu.VMEM((sb, 2 * PAIR, WIN_KEYS), BF16), pltpu.VMEM((sb, 2 * PAIR, WIN_KEYS), BF16)],
        ),
        out_shape=[jax.ShapeDtypeStruct((n_seq, nt, B_WIDTH), F32), jax.ShapeDtypeStruct(win_buf.shape, F32)],
        compiler_params=_params("arbitrary"),
        name="nsa_sample",
    )(page_table, *([cache_sel] * (sb * N_PAGES)), u_s, u_s, cmp_tok, u_s, u_s, win_buf, bias_c, bias_s, bias_w)


PROMPT_TM = 1024
PROJ_TN = 512
COL_CMP = 2 * A_QK + 2 * A_WIDTH + B_WIDTH
COL_SEL = COL_CMP + 2 * KV_WIDTH
COL_WIN = COL_SEL + 2 * KV_WIDTH
COL_GATE = COL_WIN + 2 * KV_WIDTH


def _time_major(a):
    return a.transpose(1, 0, 2).reshape(a.shape[0] * a.shape[1], a.shape[2])


def _seq_major(a2d, n_seq):
    return a2d.reshape(a2d.shape[0] // n_seq, n_seq, a2d.shape[1]).transpose(1, 0, 2)


def _kv_rows(u3, col):
    return u3[..., col:col + 2 * KV_WIDTH].reshape(u3.shape[:-1] + (2, B_KV_HEADS, B_DH))


def kernel(x_prompt, x_sample, cache_cmp, cache_sel, state_win, state_hgrn, state_conv, page_table, c_prompt, c_sample, norm_g, ada_w, ada_b, ffn_w_gate, ffn_w_up, ffn_w_down, w_in_even, hgrn_lower_bound, hgrn_norm_g, cmp_pe, cmp_w1, cmp_b1, cmp_w2, cmp_b2, rel_bias, w_out_even, conv_w_pw1, conv_b_pw1, conv_w_dw, conv_b_dw, conv_ln_g, conv_ln_b, conv_w_pw2, conv_b_pw2, final_norm_g):
    P = {'norm_g': norm_g, 'ffn_w_gate': ffn_w_gate, 'ffn_w_up': ffn_w_up, 'ffn_w_down': ffn_w_down,
         'final_norm_g': final_norm_g}
    n_seq = x_sample.shape[0]
    n_pool = cache_cmp.shape[1]
    rows_p, rows_s = _prompt_rows(PROMPT_TM), _sample_rows()
    g4 = norm_g.reshape(DEPTH, 3, 1, D_MODEL)
    c_all = jnp.concatenate([c_sample, jnp.tile(c_prompt, (SUBLANES, 1))], axis=0)
    mod = _ada_mod(c_all, ada_w, ada_b)
    bias_pc, bias_pkv, bias_sc, bias_ss, bias_sw = _bias_tables(rel_bias)

    xp = x_prompt.reshape(SEQ, D_MODEL)
    xs = _time_major(x_sample)
    cmp_p, cmp_s, sel_p, sel_s, win_p, win_s, hgrn_p, hgrn_s, conv_p, conv_s = ([] for _ in range(10))
    for l in range(DEPTH):
        i = l // 2
        last = l == DEPTH - 1
        xs, w_bf16 = _ffn(xs, rows_s, mod, l, 0, P)
        xp, _ = _ffn(xp, rows_p, mod, l, 0, P, w_bf16=w_bf16)
        if l % 2 == 0:
            up = _proj(xp, rows_p, mod, l, g4, w_in_even[i:i + 1], IN_DIM, PROJ_TN)
            us = _seq_major(_proj(xs, rows_s, mod, l, g4, w_in_even[i:i + 1], IN_DIM, PROJ_TN), n_seq)
            gn = hgrn_norm_g[i:i + 1]
            oa_p, hp = _hgrn(up, hgrn_lower_bound, gn, l)
            oa_s, hs = _hgrn(us.reshape(n_seq * DEC_SEQ, IN_DIM), hgrn_lower_bound, gn, l, state_hgrn[i])
            weights = _cmp_weights(cmp_pe[i], cmp_w1[i], cmp_b1[i], cmp_w2[i], cmp_b2[i])
            kv_pad = jnp.pad(up[:, COL_SEL:COL_GATE].astype(BF16), ((KV_PAD, 0), (0, 0)))
            ob_p = _nsa_prompt(up, kv_pad, _compress_prompt(up, weights), bias_pc, bias_pkv)
            cmp_t = cache_cmp[i].transpose(0, 2, 3, 4, 1).reshape(n_pool, 4 * PAIR, PAGE_SIZE)
            cmp_tok_s = _compress_sample(cmp_t, page_table, weights)
            sel_t = cache_sel[i].transpose(0, 2, 3, 4, 1).reshape(n_pool, 4 * PAIR, PAGE_SIZE)
            win_t = state_win[i].transpose(0, 2, 3, 4, 1).reshape(n_seq, 4 * PAIR, WINDOW)
            ob_s, wn = _nsa_sample(us, cmp_tok_s, sel_t, win_t, page_table, bias_sc, bias_ss, bias_sw)
            xp = _out_proj([oa_p, ob_p], w_out_even[i:i + 1], [0, A_WIDTH], None, xp, rows_p, mod, l, PROJ_TN)
            xs = _out_proj([_time_major(oa_s.reshape(n_seq, DEC_SEQ, A_WIDTH)), _time_major(ob_s)],
                           w_out_even[i:i + 1], [0, A_WIDTH], None, xs, rows_s, mod, l, PROJ_TN)
            up3 = up[None]
            cmp_p.append(_kv_rows(up3, COL_CMP))
            sel_p.append(_kv_rows(up3, COL_SEL))
            win_p.append(_kv_rows(up3[:, SEQ - min(WINDOW, SEQ):], COL_WIN))
            cmp_s.append(_kv_rows(us, COL_CMP))
            sel_s.append(_kv_rows(us, COL_SEL))
            win_s.append(wn.reshape(n_seq, 2, B_KV_HEADS, B_DH, WINDOW).transpose(0, 4, 1, 2, 3))
            hgrn_p.append(hp[None])
            hgrn_s.append(hs)
        else:
            b_pw1 = conv_b_pw1[i].reshape(1, 1, 2 * CONV_DIM)
            b_pw2 = conv_b_pw2[i].reshape(1, 1, D_MODEL)
            vec = lambda a: a[i].reshape(1, CONV_DIM)
            glu_p = _glu_proj(xp, rows_p, mod, l, g4, conv_w_pw1[i:i + 1], b_pw1, PROJ_TN)
            act_p = _conv_prompt(glu_p, conv_w_dw[i], vec(conv_b_dw), vec(conv_ln_g), vec(conv_ln_b))
            xp = _out_proj([act_p], conv_w_pw2[i:i + 1], [0], b_pw2, xp, rows_p, mod, l, PROJ_TN)
            glu_s = _glu_proj(xs, rows_s, mod, l, g4, conv_w_pw1[i:i + 1], b_pw1, PROJ_TN)
            act_s, nb = _conv_sample(glu_s.reshape(DEC_SEQ, n_seq, CONV_DIM), state_conv[i].transpose(1, 0, 2),
                                     conv_w_dw[i], vec(conv_b_dw), vec(conv_ln_g), vec(conv_ln_b))
            xs = _out_proj([act_s.reshape(DEC_SEQ * n_seq, CONV_DIM)], conv_w_pw2[i:i + 1], [0], b_pw2, xs, rows_s, mod, l, PROJ_TN)
            conv_p.append(glu_p[None, SEQ - (CONV_WIDTH - 1):])
            conv_s.append(nb.transpose(1, 0, 2))
        xs, w_bf16 = _ffn(xs, rows_s, mod, l, 2, P, final_norm=last)
        xp, _ = _ffn(xp, rows_p, mod, l, 2, P, w_bf16=w_bf16, final_norm=last)
    y_prompt = xp.reshape(1, SEQ, D_MODEL)
    y_sample = _seq_major(xs, n_seq)
    st = jnp.stack
    return (y_prompt, y_sample, st(cmp_p), st(cmp_s), st(sel_p), st(sel_s), st(win_p), st(win_s),
            st(hgrn_p), st(hgrn_s), st(conv_p), st(conv_s))
```

```python
import functools
import math

import numpy as np
import jax
import jax.numpy as jnp
from jax import lax
from jax.experimental import pallas as pl
from jax.experimental.pallas import tpu as pltpu

F32 = jnp.float32
BF16 = jnp.bfloat16

D_MODEL = 2048
SEQ = 8192
DEPTH = 2
DEC_BATCH = 128
DEC_SEQ = 8
PAST_LEN = 2048
PAGE_SIZE = 128
N_MOD = 9
D_FF = 5504
EPS = 1e-6
A_HEADS = 8
A_DK = 128
A_DV = 128
A_QK = A_HEADS * A_DK
A_WIDTH = A_HEADS * A_DV
B_HEADS = 16
B_KV_HEADS = 4
B_DH = 64
B_GROUP = B_HEADS // B_KV_HEADS
B_WIDTH = B_HEADS * B_DH
KV_WIDTH = B_KV_HEADS * B_DH
CMP_BLOCK = 32
CMP_STRIDE = 16
CMP_HIDDEN = 256
SEL_BLOCK = 64
N_SELECT = 16
WINDOW = 512
N_BUCKETS = 32
MAX_DISTANCE = 128
MIX_WIDTH = A_WIDTH + B_WIDTH
IN_DIM = 2 * A_QK + 2 * A_WIDTH + B_WIDTH + 6 * KV_WIDTH + 3 * B_HEADS
CONV_WIDTH = 31
CONV_DIM = D_MODEL
NEG = -1e30
FORCE = 1e9

V7X_VMEM_LIMIT_BYTES = 60 * 1024 * 1024
SUBLANES = 8
LANES = 128

N_SEQ_ROWS = DEC_BATCH + SUBLANES
PROMPT_ROW_BLOCK = DEC_BATCH // SUBLANES


def _params(*sem):
    return pltpu.CompilerParams(dimension_semantics=sem, vmem_limit_bytes=V7X_VMEM_LIMIT_BYTES)


def _silu(x):
    return x * jax.nn.sigmoid(x)


def _bdot(a, b):
    return jnp.dot(a.astype(BF16), b.astype(BF16), preferred_element_type=F32)


def _ada_kernel(c_ref, w_ref, b_ref, o_ref):
    o_ref[...] = _bdot(_silu(c_ref[...]), w_ref[...]) + b_ref[...]


def _ada_mod(c_all, ada_w, ada_b):
    n = c_all.shape[0]
    return pl.pallas_call(
        _ada_kernel,
        grid=(DEPTH, N_MOD),
        in_specs=[
            pl.BlockSpec((n, D_MODEL), lambda l, k: (0, 0)),
            pl.BlockSpec((None, D_MODEL, D_MODEL), lambda l, k: (l, 0, k)),
            pl.BlockSpec((None, None, 1, D_MODEL), lambda l, k: (l, k, 0, 0)),
        ],
        out_specs=pl.BlockSpec((None, None, n, D_MODEL), lambda l, k: (l, k, 0, 0)),
        out_shape=jax.ShapeDtypeStruct((DEPTH, N_MOD, n, D_MODEL), F32),
        compiler_params=_params("arbitrary", "arbitrary"),
        name="ada_mod",
    )(c_all, ada_w, ada_b.reshape(DEPTH, N_MOD, 1, D_MODEL))


def _norm_mod(x, g, shift, scale):
    ms = jnp.mean(x * x, axis=-1, keepdims=True)
    y = x * lax.rsqrt(ms + EPS) * g
    h = y * (1.0 + scale) + shift
    return h.reshape(x.shape[0] * x.shape[1], x.shape[2]).astype(BF16)


class _Rows:
    def __init__(self, m, bs, nt, seq_block):
        assert m % (bs * nt) == 0
        self.m, self.bs, self.nt, self.seq_block = m, bs, nt, seq_block
        self.tm = bs * nt
        self.n_tiles = m // self.tm

    def view(self, x2d):
        return x2d.reshape(self.m // self.bs, self.bs, x2d.shape[-1])

    def x_spec(self, width, col=lambda j: 0, single_buffer=False):
        mode = dict(pipeline_mode=pl.Buffered(1)) if single_buffer else {}
        return pl.BlockSpec((self.nt, self.bs, width), lambda i, j: (i, 0, col(j)), **mode)

    def mod_spec(self, layer, k, width=D_MODEL, col=lambda j: 0):
        sb = self.seq_block
        return pl.BlockSpec((None, None, self.bs, width), lambda i, j: (layer, k, sb, col(j)))


def _prompt_rows(tm):
    return _Rows(SEQ, SUBLANES, tm // SUBLANES, PROMPT_ROW_BLOCK)


def _sample_rows():
    return _Rows(DEC_BATCH * DEC_SEQ, DEC_BATCH, DEC_SEQ, 0)


FFN_TF_F32 = 256
FFN_TF_BF16 = 512
FFN_ACC_CHUNKS = 4


def _ffn_kernel(x_ref, sh_ref, sc_ref, gt_ref, g_ref, wg_ref, wu_ref, wd_ref, fg_ref, o_ref, *rest, final_norm, emit):
    if emit:
        wg_o, wu_o, wd_o, h_ref = rest
    else:
        (h_ref,) = rest
    j = pl.program_id(1)
    nj = pl.num_programs(1)
    tf = wg_ref.shape[1]

    nt, bs = o_ref.shape[0], o_ref.shape[1]
    step = max(nt // FFN_ACC_CHUNKS, 1)

    @pl.when(j == 0)
    def _():
        for r in range(0, nt, step):
            h_ref[r * bs:(r + step) * bs, :] = _norm_mod(x_ref[r:r + step], g_ref[...], sh_ref[...], sc_ref[...])
        o_ref[...] = jnp.zeros_like(o_ref)

    wg, wu, wd = wg_ref[...].astype(BF16), wu_ref[...].astype(BF16), wd_ref[...].astype(BF16)
    if emit:
        wg_o[...], wu_o[...], wd_o[...] = wg, wu, wd
    valid = D_FF - j * tf
    h = h_ref[...]
    a = _silu(_bdot(h, wg)) * _bdot(h, wu)
    col = lax.broadcasted_iota(jnp.int32, a.shape, 1)
    a = jnp.where(col < valid, a, 0.0)
    row = lax.broadcasted_iota(jnp.int32, wd.shape, 0)
    wd = jnp.where(row < valid, wd, jnp.zeros_like(wd))
    a = a.astype(BF16)
    for r in range(0, nt, step):
        o_ref[r:r + step] += _bdot(a[r * bs:(r + step) * bs], wd).reshape(step, bs, o_ref.shape[2])

    @pl.when(j == nj - 1)
    def _():
        for r in range(0, nt, step):
            y = x_ref[r:r + step] + (0.5 * gt_ref[...]) * o_ref[r:r + step]
            if final_norm:
                ms = jnp.mean(y * y, axis=-1, keepdims=True)
                y = y * lax.rsqrt(ms + EPS) * fg_ref[...]
            o_ref[r:r + step] = y


def _ffn(x2d, rows, mod, layer, sub, P, w_bf16=None, final_norm=False):
    half = sub // 2
    emit = w_bf16 is None
    tf = FFN_TF_F32 if emit else FFN_TF_BF16
    g_norm = P['norm_g'].reshape(DEPTH, 3, 1, D_MODEL)
    fg = P['final_norm_g'].reshape(1, D_MODEL)
    if emit:
        weights = (P['ffn_w_gate'], P['ffn_w_up'], P['ffn_w_down'])
        w_specs = [pl.BlockSpec((None, None, D_MODEL, tf), lambda i, j: (layer, half, 0, j)),
                   pl.BlockSpec((None, None, D_MODEL, tf), lambda i, j: (layer, half, 0, j)),
                   pl.BlockSpec((None, None, tf, D_MODEL), lambda i, j: (layer, half, j, 0))]
    else:
        weights = w_bf16
        w_specs = [pl.BlockSpec((D_MODEL, tf), lambda i, j: (0, j)),
                   pl.BlockSpec((D_MODEL, tf), lambda i, j: (0, j)),
                   pl.BlockSpec((tf, D_MODEL), lambda i, j: (j, 0))]
    out_specs = [rows.x_spec(D_MODEL, single_buffer=emit)]
    out_shape = [jax.ShapeDtypeStruct((rows.m // rows.bs, rows.bs, D_MODEL), F32)]
    if emit:
        out_specs += [pl.BlockSpec((D_MODEL, tf), lambda i, j: (0, j)),
                      pl.BlockSpec((D_MODEL, tf), lambda i, j: (0, j)),
                      pl.BlockSpec((tf, D_MODEL), lambda i, j: (j, 0))]
        out_shape += [jax.ShapeDtypeStruct((D_MODEL, D_FF), BF16), jax.ShapeDtypeStruct((D_MODEL, D_FF), BF16),
                      jax.ShapeDtypeStruct((D_FF, D_MODEL), BF16)]
        assert rows.n_tiles == 1
    outs = pl.pallas_call(
        functools.partial(_ffn_kernel, final_norm=final_norm, emit=emit),
        grid=(rows.n_tiles, pl.cdiv(D_FF, tf)),
        in_specs=[
            rows.x_spec(D_MODEL, single_buffer=True),
            rows.mod_spec(layer, 3 * sub), rows.mod_spec(layer, 3 * sub + 1), rows.mod_spec(layer, 3 * sub + 2),
            pl.BlockSpec((None, None, 1, D_MODEL), lambda i, j: (layer, sub, 0, 0)),
            *w_specs,
            pl.BlockSpec((1, D_MODEL), lambda i, j: (0, 0)),
        ],
        out_specs=out_specs,
        out_shape=out_shape,
        scratch_shapes=[pltpu.VMEM((rows.tm, D_MODEL), BF16)],
        compiler_params=_params("arbitrary", "arbitrary"),
        name="ffn_half_step",
    )(rows.view(x2d), mod, mod, mod, g_norm, *weights, fg)
    return outs[0].reshape(rows.m, D_MODEL), tuple(outs[1:])


def _proj_kernel(x_ref, sh_ref, sc_ref, g_ref, w_ref, o_ref, h_ref):
    @pl.when(pl.program_id(1) == 0)
    def _():
        h_ref[...] = _norm_mod(x_ref[...], g_ref[...], sh_ref[...], sc_ref[...])

    o_ref[...] = _bdot(h_ref[...], w_ref[...])


def _proj(x2d, rows, mod, layer, g_norm4, w3, n_out, tn):
    return pl.pallas_call(
        _proj_kernel,
        grid=(rows.n_tiles, pl.cdiv(n_out, tn)),
        in_specs=[
            rows.x_spec(D_MODEL),
            rows.mod_spec(layer, 3), rows.mod_spec(layer, 4),
            pl.BlockSpec((None, None, 1, D_MODEL), lambda i, j: (layer, 1, 0, 0)),
            pl.BlockSpec((None, D_MODEL, tn), lambda i, j: (0, 0, j)),
        ],
        out_specs=pl.BlockSpec((rows.tm, tn), lambda i, j: (i, j)),
        out_shape=jax.ShapeDtypeStruct((rows.m, n_out), F32),
        scratch_shapes=[pltpu.VMEM((rows.tm, D_MODEL), BF16)],
        compiler_params=_params("arbitrary", "arbitrary"),
        name="prenorm_proj",
    )(rows.view(x2d), mod, mod, g_norm4, w3)


def _glu_proj_kernel(x_ref, sh_ref, sc_ref, g_ref, wa_ref, wg_ref, ba_ref, bg_ref, o_ref, h_ref):
    @pl.when(pl.program_id(1) == 0)
    def _():
        h_ref[...] = _norm_mod(x_ref[...], g_ref[...], sh_ref[...], sc_ref[...])

    h = h_ref[...]
    a = _bdot(h, wa_ref[...]) + ba_ref[...]
    gt = _bdot(h, wg_ref[...]) + bg_ref[...]
    o_ref[...] = a * jax.nn.sigmoid(gt)


def _glu_proj(x2d, rows, mod, layer, g_norm4, w3, b3, tn):
    nb = CONV_DIM // tn
    return pl.pallas_call(
        _glu_proj_kernel,
        grid=(rows.n_tiles, nb),
        in_specs=[
            rows.x_spec(D_MODEL),
            rows.mod_spec(layer, 3), rows.mod_spec(layer, 4),
            pl.BlockSpec((None, None, 1, D_MODEL), lambda i, j: (layer, 1, 0, 0)),
            pl.BlockSpec((None, D_MODEL, tn), lambda i, j: (0, 0, j)),
            pl.BlockSpec((None, D_MODEL, tn), lambda i, j: (0, 0, j + nb)),
            pl.BlockSpec((None, 1, tn), lambda i, j: (0, 0, j)),
            pl.BlockSpec((None, 1, tn), lambda i, j: (0, 0, j + nb)),
        ],
        out_specs=pl.BlockSpec((rows.tm, tn), lambda i, j: (i, j)),
        out_shape=jax.ShapeDtypeStruct((rows.m, CONV_DIM), F32),
        scratch_shapes=[pltpu.VMEM((rows.tm, D_MODEL), BF16)],
        compiler_params=_params("arbitrary", "arbitrary"),
        name="prenorm_glu_proj",
    )(rows.view(x2d), mod, mod, g_norm4, w3, w3, b3, b3)


def _out_kernel(*refs, n_in, has_bias):
    a_refs = refs[:n_in]
    w_refs = refs[n_in:2 * n_in]
    pos = 2 * n_in
    b_ref = refs[pos] if has_bias else None
    pos += int(has_bias)
    x_ref, gt_ref, o_ref = refs[pos:pos + 3]
    y = _bdot(a_refs[0][...], w_refs[0][...])
    for a_ref, w_ref in zip(a_refs[1:], w_refs[1:]):
        y += _bdot(a_ref[...], w_ref[...])
    if has_bias:
        y += b_ref[...]
    o_ref[...] = x_ref[...] + gt_ref[...] * y.reshape(o_ref.shape)


def _out_proj(acts, w3, k_offsets, bias3, x2d, rows, mod, layer, tn):
    n_in = len(acts)
    in_specs = [pl.BlockSpec((rows.tm, a.shape[1]), lambda i, j: (i, 0)) for a in acts]
    for a, off in zip(acts, k_offsets):
        kb = off // a.shape[1]
        in_specs.append(pl.BlockSpec((None, a.shape[1], tn), lambda i, j, kb=kb: (0, kb, j)))
    args = list(acts) + [w3] * n_in
    if bias3 is not None:
        in_specs.append(pl.BlockSpec((None, 1, tn), lambda i, j: (0, 0, j)))
        args.append(bias3)
    in_specs += [rows.x_spec(tn, col=lambda j: j), rows.mod_spec(layer, 5, width=tn, col=lambda j: j)]
    args += [rows.view(x2d), mod]
    out = pl.pallas_call(
        functools.partial(_out_kernel, n_in=n_in, has_bias=bias3 is not None),
        grid=(rows.n_tiles, D_MODEL // tn),
        in_specs=in_specs,
        out_specs=rows.x_spec(tn, col=lambda j: j),
        out_shape=jax.ShapeDtypeStruct((rows.m // rows.bs, rows.bs, D_MODEL), F32),
        compiler_params=_params("arbitrary", "arbitrary"),
        name="out_proj_residual",
    )(*args)
    return out.reshape(rows.m, D_MODEL)


HGRN_ROWS = 512
HGRN_HEADS = 4
HGRN_TRI = 128


def _hgrn_prepare(f_ref, lb_ref, cum_s, kk_s, *, layer, c):
    p = lb_ref[...]
    e = jnp.exp(p - jnp.max(p, axis=0, keepdims=True))
    sm = e / jnp.sum(e, axis=0, keepdims=True)
    lb = jnp.sum(sm[:layer + 1], axis=0, keepdims=True)
    f = lb + (1.0 - lb) * jax.nn.sigmoid(f_ref[...])
    lf = jnp.log(f)
    n = HGRN_TRI
    r = lax.broadcasted_iota(jnp.int32, (n, n), 0)
    s = lax.broadcasted_iota(jnp.int32, (n, n), 1)
    tri = jnp.where((s <= r) & (s // c == r // c), 1.0, 0.0).astype(F32)
    for b in range(f.shape[0] // n):
        cum_s[b * n:(b + 1) * n, :] = jnp.dot(tri, lf[b * n:(b + 1) * n], preferred_element_type=F32,
                                              precision=lax.Precision.HIGHEST)
    kk_s[...] = 1.0 - f


def _hgrn_subchunk(r0, c, st, hh, q_ref, v_ref, g_ref, gn_ref, cum_s, kk_s):
    rows = pl.ds(r0, c)
    cols = slice(hh * A_DK, (hh + 1) * A_DK)
    cum = cum_s[rows, cols]
    q = q_ref[rows, cols]
    kk = kk_s[rows, cols]
    vv = v_ref[rows, cols]
    last = cum[c - 1:c, :]
    o = lax.dot_general((q * jnp.exp(cum)).astype(BF16), st.astype(BF16), (((1,), (1,)), ((), ())),
                        preferred_element_type=F32)
    srow = lax.broadcasted_iota(jnp.int32, (c, A_DK), 0)
    xs = []
    for t in range(c):
        d = jnp.where(srow <= t, cum[t:t + 1, :] - cum, NEG)
        xs.append(jnp.exp(d) * (q[t:t + 1, :] * kk))
    x = jnp.concatenate(xs, axis=0).astype(BF16)
    w = jnp.dot(x, jnp.ones((A_DK, A_DV), BF16), preferred_element_type=F32)
    o = o + jnp.sum(w.reshape(c, c, A_DV) * vv[None], axis=1)
    ke = kk * jnp.exp(last - cum)
    st_new = st * jnp.exp(last) + lax.dot_general(vv.astype(BF16), ke.astype(BF16), (((0,), (0,)), ((), ())),
                                                  preferred_element_type=F32)
    ms = jnp.mean(o * o, axis=-1, keepdims=True)
    y = o * lax.rsqrt(ms + EPS) * gn_ref[:, cols] * _silu(g_ref[rows, cols])
    return y, st_new


def _hgrn_prompt_kernel(q_ref, f_ref, v_ref, g_ref, lb_ref, gn_ref, o_ref, s_ref, st_ref, cum_s, kk_s, *, layer, c):
    i = pl.program_id(1)

    @pl.when(i == 0)
    def _():
        st_ref[...] = jnp.zeros_like(st_ref)

    _hgrn_prepare(f_ref, lb_ref, cum_s, kk_s, layer=layer, c=c)

    def body(n, carry):
        r0 = pl.multiple_of(n * c, c)
        for hh in range(HGRN_HEADS):
            y, st_new = _hgrn_subchunk(r0, c, st_ref[hh], hh, q_ref, v_ref, g_ref, gn_ref, cum_s, kk_s)
            st_ref[hh] = st_new
            o_ref[pl.ds(r0, c), hh * A_DV:(hh + 1) * A_DV] = y
        return carry

    lax.fori_loop(0, q_ref.shape[0] // c, body, 0)

    @pl.when(i == pl.num_programs(1) - 1)
    def _():
        for hh in range(HGRN_HEADS):
            s_ref[hh] = st_ref[hh].T


def _hgrn_sample_kernel(q_ref, f_ref, v_ref, g_ref, lb_ref, gn_ref, s0_ref, o_ref, s_ref, cum_s, kk_s, *, layer, c):
    _hgrn_prepare(f_ref, lb_ref, cum_s, kk_s, layer=layer, c=c)

    def body(n, carry):
        r0 = pl.multiple_of(n * c, c)
        for hh in range(HGRN_HEADS):
            y, st_new = _hgrn_subchunk(r0, c, s0_ref[n, hh].T, hh, q_ref, v_ref, g_ref, gn_ref, cum_s, kk_s)
            s_ref[n, hh] = st_new.T
            o_ref[pl.ds(r0, c), hh * A_DV:(hh + 1) * A_DV] = y
        return carry

    lax.fori_loop(0, q_ref.shape[0] // c, body, 0)


def _hgrn(u, lower_bound, norm_g2, layer, s0=None):
    m = u.shape[0]
    hb = HGRN_HEADS
    w = hb * A_DK
    nb = A_QK // w
    tc = HGRN_ROWS if s0 is None else HGRN_ROWS // 2

    def col(seg):
        return pl.BlockSpec((tc, w), lambda h, i, seg=seg: (i, seg * nb + h))

    in_specs = [col(0), col(1), col(2), col(3),
                pl.BlockSpec((DEPTH + 1, w), lambda h, i: (0, h)),
                pl.BlockSpec((1, w), lambda h, i: (0, h))]
    o_spec = pl.BlockSpec((tc, w), lambda h, i: (i, h))
    scratch = [pltpu.VMEM((tc, w), F32), pltpu.VMEM((tc, w), F32)]
    if s0 is None:
        return pl.pallas_call(
            functools.partial(_hgrn_prompt_kernel, layer=layer, c=16),
            grid=(nb, m // tc),
            in_specs=in_specs,
            out_specs=[o_spec, pl.BlockSpec((hb, A_DK, A_DV), lambda h, i: (h, 0, 0))],
            out_shape=[jax.ShapeDtypeStruct((m, A_WIDTH), F32), jax.ShapeDtypeStruct((A_HEADS, A_DK, A_DV), F32)],
            scratch_shapes=[pltpu.VMEM((hb, A_DV, A_DK), F32)] + scratch,
            compiler_params=_params("arbitrary", "arbitrary"),
            name="hgrn2_prompt",
        )(u, u, u, u, lower_bound, norm_g2)
    c = DEC_SEQ
    ns = tc // c
    s_spec = pl.BlockSpec((ns, hb, A_DK, A_DV), lambda h, i: (i, h, 0, 0))
    return pl.pallas_call(
        functools.partial(_hgrn_sample_kernel, layer=layer, c=c),
        grid=(nb, m // tc),
        in_specs=in_specs + [s_spec],
        out_specs=[o_spec, s_spec],
        out_shape=[jax.ShapeDtypeStruct((m, A_WIDTH), F32), jax.ShapeDtypeStruct(s0.shape, F32)],
        scratch_shapes=scratch,
        compiler_params=_params("arbitrary", "arbitrary"),
        name="hgrn2_sample",
    )(u, u, u, u, lower_bound, norm_g2, s0)


CONV_HALO = 32
CONV_ROWS = 256


def _ln_silu(y, g, b):
    mu = jnp.mean(y, axis=-1, keepdims=True)
    yc = y - mu
    var = jnp.mean(yc * yc, axis=-1, keepdims=True)
    return _silu(yc * lax.rsqrt(var + EPS) * g + b)


CONV_CHUNK_ROWS = 64
CONV_CHUNK_COLS = 512


def _conv_prompt_kernel(cur_ref, halo_ref, w_ref, b_ref, g_ref, lb_ref, o_ref, ext_ref, ph_ref, y_ref):
    i = pl.program_id(0)
    tt = cur_ref.shape[0]
    ext_ref[0:CONV_HALO, :] = jnp.where(i > 0, halo_ref[...], 0.0)
    ext_ref[CONV_HALO:, :] = cur_ref[...]
    off = CONV_HALO - (CONV_WIDTH - 1)
    span = ph_ref.shape[1]
    for r in range(1, SUBLANES):
        ph_ref[r - 1] = ext_ref[r:r + span, :]
    for c0 in range(0, CONV_DIM, CONV_CHUNK_COLS):
        cs = slice(c0, c0 + CONV_CHUNK_COLS)
        for r0 in range(0, tt, CONV_CHUNK_ROWS):
            y = jnp.zeros((CONV_CHUNK_ROWS, CONV_CHUNK_COLS), F32) + b_ref[:, cs]
            for w in range(CONV_WIDTH):
                a, r = divmod(off + w, SUBLANES)
                src = ext_ref if r == 0 else ph_ref.at[r - 1]
                lo = SUBLANES * a + r0
                y = y + src[lo:lo + CONV_CHUNK_ROWS, cs] * w_ref[w:w + 1, cs]
            y_ref[r0:r0 + CONV_CHUNK_ROWS, cs] = y
    o_ref[...] = _ln_silu(y_ref[...], g_ref[...], lb_ref[...]).astype(o_ref.dtype)


def _conv_prompt(glu, w_dw, b_dw, ln_g, ln_b):
    t = glu.shape[0]
    tt = CONV_ROWS
    r = tt // CONV_HALO
    vec = pl.BlockSpec((1, CONV_DIM), lambda i: (0, 0))
    return pl.pallas_call(
        _conv_prompt_kernel,
        grid=(t // tt,),
        in_specs=[
            pl.BlockSpec((tt, CONV_DIM), lambda i: (i, 0)),
            pl.BlockSpec((CONV_HALO, CONV_DIM), lambda i: (jnp.maximum(i * r - 1, 0), 0)),
            pl.BlockSpec((CONV_WIDTH, CONV_DIM), lambda i: (0, 0)),
            vec, vec, vec,
        ],
        out_specs=pl.BlockSpec((tt, CONV_DIM), lambda i: (i, 0)),
        out_shape=jax.ShapeDtypeStruct((t, CONV_DIM), BF16),
        scratch_shapes=[pltpu.VMEM((CONV_HALO + tt, CONV_DIM), F32),
                        pltpu.VMEM((SUBLANES - 1, CONV_HALO + tt - SUBLANES, CONV_DIM), F32),
                        pltpu.VMEM((tt, CONV_DIM), F32)],
        compiler_params=_params("arbitrary"),
        name="conv_prompt",
    )(glu, glu, w_dw, b_dw, ln_g, ln_b)


def _conv_sample_kernel(u_ref, buf_ref, w_ref, b_ref, g_ref, lb_ref, o_ref, nb_ref):
    hist = CONV_WIDTH - 1
    nt = u_ref.shape[0]

    def ext(j):
        return buf_ref[j] if j < hist else u_ref[j - hist]

    for t in range(nt):
        y = ext(t) * w_ref[0:1, :] + b_ref[...]
        for w in range(1, CONV_WIDTH):
            y = y + ext(t + w) * w_ref[w:w + 1, :]
        o_ref[t] = _ln_silu(y, g_ref[...], lb_ref[...]).astype(o_ref.dtype)
    for j in range(hist):
        nb_ref[j] = ext(j + nt)


def _conv_sample(glu_t, buf_t, w_dw, b_dw, ln_g, ln_b):
    nt, ns, _ = glu_t.shape
    hist = CONV_WIDTH - 1
    bs = 32
    vec = pl.BlockSpec((1, CONV_DIM), lambda i: (0, 0))
    return pl.pallas_call(
        _conv_sample_kernel,
        grid=(ns // bs,),
        in_specs=[
            pl.BlockSpec((nt, bs, CONV_DIM), lambda i: (0, i, 0)),
            pl.BlockSpec((hist, bs, CONV_DIM), lambda i: (0, i, 0)),
            pl.BlockSpec((CONV_WIDTH, CONV_DIM), lambda i: (0, 0)),
            vec, vec, vec,
        ],
        out_specs=[pl.BlockSpec((nt, bs, CONV_DIM), lambda i: (0, i, 0)),
                   pl.BlockSpec((hist, bs, CONV_DIM), lambda i: (0, i, 0))],
        out_shape=[jax.ShapeDtypeStruct((nt, ns, CONV_DIM), BF16), jax.ShapeDtypeStruct((hist, ns, CONV_DIM), F32)],
        compiler_params=_params("arbitrary"),
        name="conv_sample",
    )(glu_t, buf_t, w_dw, b_dw, ln_g, ln_b)


PAIR = 2 * B_DH
N_PAIRS = KV_WIDTH // PAIR
BLOCKS_PER_TOKEN = SEL_BLOCK // CMP_STRIDE
N_SUB = CMP_BLOCK // CMP_STRIDE
CMP_OFF = 120
CMP_ROWS = 640
KV_PAD = WINDOW
FAR_TILE = 1024
LOWEST = -3.0e38
MASK_C = 2.0 ** 100


def _half(shape, half):
    lane = lax.broadcasted_iota(jnp.int32, shape, len(shape) - 1)
    return (lane % PAIR) // B_DH == half


def _group_q(q_ref, g):
    parts = []
    for r in range(B_GROUP):
        h = g * B_GROUP + r
        x = q_ref[:, (h // 2) * PAIR:(h // 2 + 1) * PAIR]
        x = jnp.where(_half(x.shape, h % 2), x, 0.0) * (B_DH ** -0.5)
        if h % 2 != g % 2:
            x = pltpu.roll(x, B_DH, axis=1)
        parts.append(x)
    return jnp.concatenate(parts, axis=0).astype(BF16)


def _gate_rows(gate_ref, branch, g):
    sig = jax.nn.sigmoid(gate_ref[...])
    c0 = branch * B_HEADS + g * B_GROUP
    return jnp.concatenate([sig[:, c0 + r:c0 + r + 1] for r in range(B_GROUP)], axis=0)


def _scatter_heads(o, g):
    nq = o.shape[0] // B_GROUP
    outs = []
    for jp in range(2):
        acc = None
        for e in range(2):
            r = 2 * jp + e
            x = jnp.where(_half((nq, PAIR), g % 2), o[r * nq:(r + 1) * nq], 0.0)
            if e != g % 2:
                x = pltpu.roll(x, B_DH, axis=1)
            acc = x if acc is None else acc + x
        outs.append(acc)
    return jnp.concatenate(outs, axis=1)


def _nt_dot(a, b):
    return lax.dot_general(a, b, (((1,), (1,)), ((), ())), preferred_element_type=F32)


def _piece(qg, k, v, mask, bias=None):
    s = _nt_dot(qg, k)
    if bias is not None:
        s = s + bias
    s = jnp.where(mask, s, NEG)
    m = jnp.max(s, axis=-1, keepdims=True)
    p = jnp.where(mask, jnp.exp(s - m), 0.0)
    return m, jnp.sum(p, axis=-1, keepdims=True), jnp.dot(p.astype(BF16), v, preferred_element_type=F32), p


def _merge(a, b):
    m = jnp.maximum(a[0], b[0])
    ea = jnp.exp(a[0] - m)
    eb = jnp.exp(b[0] - m)
    return m, a[1] * ea + b[1] * eb, a[2] * ea + b[2] * eb


def _inv_or_zero(l):
    return jnp.where(l > 0.0, 1.0 / l, 0.0)


def _dot01(p, o01):
    hi = p.astype(BF16)
    r1 = p - hi.astype(F32)
    mid = r1.astype(BF16)
    lo = (r1 - mid.astype(F32)).astype(BF16)
    d = lambda a: jnp.dot(a, o01, preferred_element_type=F32)
    return d(hi) + d(mid) + d(lo)


def _dot01_t(o01_t, p):
    hi = p.astype(BF16)
    r1 = p - hi.astype(F32)
    mid = r1.astype(BF16)
    lo = (r1 - mid.astype(F32)).astype(BF16)
    return _nt_dot(o01_t, hi) + _nt_dot(o01_t, mid) + _nt_dot(o01_t, lo)


def _overlap01(tok, blk):
    return jnp.where((tok >= BLOCKS_PER_TOKEN * blk - (N_SUB - 1)) & (tok <= BLOCKS_PER_TOKEN * blk + BLOCKS_PER_TOKEN - 1),
                     1.0, 0.0).astype(BF16)


def _select_blocks(imp, qpos, n_cand=None, blocks_axis=1):
    j = lax.broadcasted_iota(jnp.int32, imp.shape, blocks_axis)
    cur = qpos // SEL_BLOCK
    forced = (j == 0) | (j == cur) | (j == cur - 1)
    valid = j * SEL_BLOCK <= qpos
    imp = jnp.where(forced, FORCE, jnp.where(valid, imp, -FORCE))
    if n_cand is not None:
        assert blocks_axis == 1
        ahead = jnp.zeros(imp.shape, F32)
        for i in range(n_cand):
            v = imp[:, i:i + 1]
            ahead = ahead + jnp.where((v > imp) | ((v == imp) & (j > i)), 1.0, 0.0)
        return jnp.where((ahead < N_SELECT) & (j < n_cand), 1.0, 0.0)
    sel = jnp.zeros(imp.shape, F32)
    jf = j.astype(F32)
    for _ in range(N_SELECT):
        m = jnp.max(imp, axis=blocks_axis, keepdims=True)
        first = jnp.min(jnp.where(imp == m, jf, float(imp.shape[blocks_axis])), axis=blocks_axis, keepdims=True)
        pick = jf == first
        sel = jnp.where(pick, 1.0, sel)
        imp = jnp.where(pick, LOWEST, imp)
    return sel


def _expand_blocks(sel_rows, first_blk, n_keys):
    jj = lax.broadcasted_iota(jnp.int32, (sel_rows.shape[1], n_keys), 0)
    kk = lax.broadcasted_iota(jnp.int32, (sel_rows.shape[1], n_keys), 1)
    e = jnp.where(jj == first_blk + kk // SEL_BLOCK, 1.0, 0.0).astype(BF16)
    return jnp.dot(sel_rows, e, preferred_element_type=F32)


def _bias_tables(rel_bias):
    d = np.arange(MAX_DISTANCE + 1)
    exact = N_BUCKETS // 2
    large = exact + (np.log(np.maximum(d, 1).astype(np.float32) / exact) / math.log(MAX_DISTANCE / exact)
                     * (N_BUCKETS - exact)).astype(np.int32)
    bucket = np.where(d < exact, d, np.minimum(large, N_BUCKETS - 1))
    by_dist = rel_bias.astype(F32)[bucket]

    def tile(c, nq, n, step=1):
        n1 = step * (n - 1) + 1
        dmin, dmax = c - (n1 - 1), c + nq - 1
        n_neg = max(0, min(0, dmax + 1) - dmin)
        lo, hi = max(dmin, 0), min(dmax, MAX_DISTANCE)
        n_far = max(0, dmax - max(dmin, MAX_DISTANCE + 1) + 1)
        v = jnp.concatenate([jnp.broadcast_to(by_dist[:1], (n_neg, B_HEADS)), by_dist[lo:hi + 1],
                             jnp.broadcast_to(by_dist[MAX_DISTANCE:], (n_far, B_HEADS))], axis=0)
        lv = nq + n1 - 1
        p = jnp.concatenate([v[::-1], jnp.zeros((1, B_HEADS), F32)], axis=0)
        rows = jnp.tile(p, (nq, 1))[:nq * lv].reshape(nq, lv, B_HEADS)[:, nq - 1:nq - 1 + n1:step]
        t = rows.reshape(nq, n, B_KV_HEADS, B_GROUP).transpose(2, 3, 0, 1)
        return t.reshape(B_KV_HEADS, B_GROUP * nq, n)

    nq = PAIR
    far = by_dist[MAX_DISTANCE]
    shift = jnp.repeat(far.reshape(B_KV_HEADS, B_GROUP), nq, axis=1)[:, :, None]
    n_var = 32
    m0 = PAIR - n_var
    c_cmp = -CMP_STRIDE * (m0 - CMP_OFF) - (CMP_BLOCK - 1)
    assert c_cmp + CMP_STRIDE >= MAX_DISTANCE
    prompt_cmp = jnp.concatenate([jnp.zeros((B_KV_HEADS, B_GROUP * nq, m0), F32),
                                  tile(c_cmp, nq, n_var, CMP_STRIDE) - shift], axis=2)
    causal = np.tile(np.arange(nq)[:, None] + nq - np.arange(2 * nq)[None, :] >= 0, (B_GROUP, 1))[None]
    prompt_kv = jnp.where(causal, tile(nq, nq, 2 * nq) - shift, -MASK_C)
    s_cmp = tile(PAST_LEN - (CMP_BLOCK - 1), DEC_SEQ, PAIR, CMP_STRIDE)
    s_sel = tile(PAST_LEN, DEC_SEQ, PAST_LEN + PAIR)
    s_win = tile(WINDOW, DEC_SEQ, WINDOW + PAIR)
    flat = lambda a: a.reshape(B_KV_HEADS * B_GROUP * DEC_SEQ, a.shape[-1])
    return prompt_cmp, prompt_kv, flat(s_cmp), flat(s_sel), flat(s_win)


def _cmp_weights(pe, w1, b1, w2, b2):
    k1 = CMP_STRIDE * B_DH
    w1f = w1.reshape(2, N_SUB, k1, CMP_HIDDEN).astype(BF16)
    pef = pe.reshape(2, N_SUB, 1, k1)
    b1f = b1.reshape(2, 1, CMP_HIDDEN)
    w2p = jnp.einsum('khd,ef->kehfd', w2, jnp.eye(2, dtype=F32)).reshape(2, 2, CMP_HIDDEN, PAIR).astype(BF16)
    b2p = jnp.tile(b2, (1, 2)).reshape(2, 1, PAIR)
    return pef, w1f, b1f, w2p, b2p


def _compress_pair(src, pe_ref, w1_ref, b1_ref, w2_ref, b2_ref):
    rows = [src(p) for p in range(CMP_STRIDE)]
    swapped = [pltpu.roll(r, B_DH, axis=1) for r in rows]
    low = _half(rows[0].shape, 0)
    m_rows = rows[0].shape[0]
    out = b2_ref[...]
    for e in range(2):
        x = jnp.concatenate([jnp.where(low, (rows, swapped)[e][p], (swapped, rows)[e][p + 1])
                             for p in range(0, CMP_STRIDE, 2)], axis=1)
        h = b1_ref[...]
        for m in range(N_SUB):
            part = jnp.dot((x + pe_ref[m]).astype(BF16), w1_ref[m], preferred_element_type=F32)
            h = h + (part if m == 0 else pltpu.roll(part, m_rows - m, axis=0))
        out = out + jnp.dot(_silu(h).astype(BF16), w2_ref[e], preferred_element_type=F32)
    return out


def _cmp_weight_specs(kv_of):
    k1 = CMP_STRIDE * B_DH
    return [
        pl.BlockSpec((None, N_SUB, 1, k1), lambda *a: (kv_of(*a), 0, 0, 0)),
        pl.BlockSpec((None, N_SUB, k1, CMP_HIDDEN), lambda *a: (kv_of(*a), 0, 0, 0)),
        pl.BlockSpec((None, 1, CMP_HIDDEN), lambda *a: (kv_of(*a), 0, 0)),
        pl.BlockSpec((None, 2, CMP_HIDDEN, PAIR), lambda *a: (kv_of(*a), 0, 0, 0)),
        pl.BlockSpec((None, 1, PAIR), lambda *a: (kv_of(*a), 0, 0)),
    ]


def _compress_prompt_kernel(rows_ref, pe_ref, w1_ref, b1_ref, w2_ref, b2_ref, o_ref):
    n_blk = rows_ref.shape[0] // CMP_STRIDE
    tok = _compress_pair(lambda p: rows_ref[pl.ds(p, n_blk, stride=CMP_STRIDE), :], pe_ref, w1_ref, b1_ref, w2_ref, b2_ref)
    o_ref[...] = jnp.zeros_like(o_ref)
    o_ref[CMP_OFF:CMP_OFF + n_blk, :] = tok


def _compress_prompt(u, weights):
    t = u.shape[0]
    col0 = (2 * A_QK + 2 * A_WIDTH + B_WIDTH) // PAIR
    return pl.pallas_call(
        _compress_prompt_kernel,
        grid=(2 * N_PAIRS,),
        in_specs=[pl.BlockSpec((t, PAIR), lambda c: (0, col0 + c))] + _cmp_weight_specs(lambda c: c // N_PAIRS),
        out_specs=pl.BlockSpec((None, CMP_ROWS, PAIR), lambda c: (c, 0, 0)),
        out_shape=jax.ShapeDtypeStruct((2 * N_PAIRS, CMP_ROWS, PAIR), F32),
        compiler_params=_params("arbitrary"),
        name="nsa_compress_prompt",
    )(u, *weights)


CMP_SEQ_BLOCK = 8
N_PAGES = PAST_LEN // PAGE_SIZE


def _compress_sample_kernel(pt_ref, *refs):
    n_src = CMP_SEQ_BLOCK * N_PAGES
    pages = refs[:n_src]
    pe_ref, w1_ref, b1_ref, w2_ref, b2_ref, o_ref, x_ref = refs[n_src:]
    n_blk = PAGE_SIZE // CMP_STRIDE
    for idx, pg in enumerate(pages):
        x_ref[idx] = pg[...].T
    tok = _compress_pair(
        lambda p: jnp.concatenate([x_ref[idx, pl.ds(p, n_blk, stride=CMP_STRIDE), :] for idx in range(n_src)], axis=0),
        pe_ref, w1_ref, b1_ref, w2_ref, b2_ref)
    o_ref[...] = tok.reshape(o_ref.shape)


def _compress_sample(cache, page_table, weights):
    n_seq = page_table.shape[0]
    sb = CMP_SEQ_BLOCK
    n_tok = PAST_LEN // CMP_STRIDE
    page_specs = [pl.BlockSpec((None, PAIR, PAGE_SIZE), lambda c, i, pt, s=s, j=j: (pt[i * sb + s, j], c, 0))
                  for s in range(sb) for j in range(N_PAGES)]
    return pl.pallas_call(
        _compress_sample_kernel,
        grid_spec=pltpu.PrefetchScalarGridSpec(
            num_scalar_prefetch=1,
            grid=(2 * N_PAIRS, n_seq // sb),
            in_specs=page_specs + _cmp_weight_specs(lambda c, i, pt: c // N_PAIRS),
            out_specs=pl.BlockSpec((sb, None, n_tok, PAIR), lambda c, i, pt: (i, c, 0, 0)),
            scratch_shapes=[pltpu.VMEM((sb * N_PAGES, PAGE_SIZE, PAIR), F32)],
        ),
        out_shape=jax.ShapeDtypeStruct((n_seq, 2 * N_PAIRS, n_tok, PAIR), F32),
        compiler_params=_params("arbitrary", "arbitrary"),
        name="nsa_compress_sample",
    )(page_table, *([cache] * (sb * N_PAGES)), *weights)


Q_ROWS = 128


def _nsa_cmp_prompt_kernel(q_ref, gate_ref, cmp_ref, bias_ref, oc_ref, sel_ref):
    qb = pl.program_id(0)
    nq = q_ref.shape[0]
    rows = B_GROUP * nq
    n_tok = CMP_ROWS - PAIR
    near0 = pl.multiple_of(qb * (nq // CMP_STRIDE), SUBLANES)
    tok0 = near0 - CMP_OFF
    mask_far = lax.broadcasted_iota(jnp.int32, (rows, n_tok), 1) < tok0
    i = lax.broadcasted_iota(jnp.int32, (rows, PAIR), 0) % nq
    mn = lax.broadcasted_iota(jnp.int32, (rows, PAIR), 1)
    dist = i - CMP_STRIDE * (mn - CMP_OFF) - (CMP_BLOCK - 1)
    mask_near = (dist >= 0) & (tok0 + mn >= 0)
    o_far = _overlap01(lax.broadcasted_iota(jnp.int32, (PAIR, n_tok), 1), lax.broadcasted_iota(jnp.int32, (PAIR, n_tok), 0))
    o_near = _overlap01(tok0 + lax.broadcasted_iota(jnp.int32, (PAIR, PAIR), 1), lax.broadcasted_iota(jnp.int32, (PAIR, PAIR), 0))
    qpos = qb * nq + lax.broadcasted_iota(jnp.int32, (PAIR, nq), 1)
    for g in range(B_KV_HEADS):
        qg = _group_q(q_ref, g)
        kp, vp = g // 2, N_PAIRS + g // 2
        far = _piece(qg, cmp_ref[kp, CMP_OFF:CMP_OFF + n_tok, :].astype(BF16),
                     cmp_ref[vp, CMP_OFF:CMP_OFF + n_tok, :].astype(BF16), mask_far)
        near = _piece(qg, cmp_ref[kp, pl.ds(near0, PAIR), :].astype(BF16),
                      cmp_ref[vp, pl.ds(near0, PAIR), :].astype(BF16), mask_near, bias_ref[g])
        m, l, acc = _merge(far[:3], near[:3])
        linv = _inv_or_zero(l)
        oc_ref[:, g * 2 * PAIR:(g + 1) * 2 * PAIR] = _scatter_heads(acc * linv * _gate_rows(gate_ref, 0, g), g)
        pf = jnp.sum((far[3] * (jnp.exp(far[0] - m) * linv)).reshape(B_GROUP, nq, n_tok), axis=0)
        pn = jnp.sum((near[3] * (jnp.exp(near[0] - m) * linv)).reshape(B_GROUP, nq, PAIR), axis=0)
        imp_t = _dot01_t(o_far, pf) + _dot01_t(o_near, pn)
        sel_ref[:, g * PAIR:(g + 1) * PAIR] = _select_blocks(imp_t, qpos, blocks_axis=0).T


def _flash_step_t(st, kx, qxt, vt, bias_t=None):
    m_old, l, acc = st
    s = jnp.dot(kx, qxt, preferred_element_type=F32)
    if bias_t is not None:
        s = s + bias_t
    m = jnp.maximum(m_old, jnp.max(s, axis=0, keepdims=True))
    p = jnp.exp(s - m)
    alpha = jnp.exp(m_old - m)
    return (m, alpha * l + jnp.sum(p, axis=0, keepdims=True),
            alpha * acc + jnp.dot(vt, p.astype(BF16), preferred_element_type=F32))


def _group_q_t(q_ref, g):
    parts = []
    for r in range(B_GROUP):
        h = g * B_GROUP + r
        x = q_ref[:, (h // 2) * PAIR:(h // 2 + 1) * PAIR]
        x = jnp.where(_half(x.shape, h % 2), x, 0.0) * (B_DH ** -0.5)
        if h % 2 != g % 2:
            x = pltpu.roll(x, B_DH, axis=1)
        parts.append(x.T)
    return jnp.concatenate(parts, axis=1).astype(BF16)


def _nsa_selwin_prompt_kernel(q_ref, gate_ref, sel_ref, oc_ref, k_ref, vt_ref, oh_ref, ohr_ref, bias_ref, wbias_ref, o_ref):
    qb = pl.program_id(0)
    nq = q_ref.shape[0]
    rows = B_GROUP * nq
    qs = qb * nq
    per_q = nq // SEL_BLOCK
    n_win = WINDOW + nq
    near_keys = pl.ds(pl.multiple_of(qs + KV_PAD - nq, nq), 2 * nq)
    win_keys = pl.ds(pl.multiple_of(qs, nq), n_win)
    blk = lax.broadcasted_iota(jnp.int32, (PAIR, rows), 0)
    first_near = per_q * (qb - 1)
    first_win = per_q * qb - WINDOW // SEL_BLOCK
    bb = lax.broadcasted_iota(jnp.int32, (PAIR, PAIR), 0)
    jj = lax.broadcasted_iota(jnp.int32, (PAIR, PAIR), 1)
    to_near = jnp.where(jj == first_near + bb, 1.0, 0.0).astype(BF16)
    exists_near = jnp.where(first_near + blk >= 0, 0.0, -MASK_C).astype(BF16)
    exists_win = jnp.where(first_win + blk >= 0, 0.0, -MASK_C).astype(BF16)
    per_tile = FAR_TILE // nq
    n_far = (qb + per_tile - 2) // per_tile
    init = (jnp.full((1, rows), NEG, F32), jnp.zeros((1, rows), F32), jnp.zeros((PAIR, rows), F32))
    gates_t = jax.nn.sigmoid(gate_ref[...]).T
    tile4 = lambda a: jnp.concatenate([a] * B_GROUP, axis=1)
    for g in range(B_KV_HEADS):
        qgt = _group_q_t(q_ref, g)
        pair = lambda base: slice(base + (g // 2) * PAIR, base + (g // 2 + 1) * PAIR)
        sel_t = sel_ref[:, g * PAIR:(g + 1) * PAIR].T
        far_vec = jnp.where((tile4(sel_t) > 0.5) & (blk < first_near), 0.0, -MASK_C).astype(BF16)
        near_sel = tile4(jnp.dot(to_near, sel_t.astype(BF16), preferred_element_type=F32))
        near_vec = jnp.where(near_sel > 0.5, 0.0, -MASK_C).astype(BF16) + exists_near
        qxt_far = jnp.concatenate([qgt, far_vec], axis=0)

        def far_step(t, st):
            keys = pl.ds(pl.multiple_of(KV_PAD + t * FAR_TILE, KV_PAD), FAR_TILE)
            kx = jnp.concatenate([k_ref[keys, pair(0)], oh_ref[keys, :]], axis=1)
            return _flash_step_t(st, kx, qxt_far, vt_ref[pair(0), keys])

        st = lax.fori_loop(0, n_far, far_step, init)
        st = _flash_step_t(st, jnp.concatenate([k_ref[near_keys, pair(0)], ohr_ref[:2 * nq, :]], axis=1),
                           jnp.concatenate([qgt, near_vec], axis=0), vt_ref[pair(0), near_keys], bias_ref[g])
        o_s = st[2] * _inv_or_zero(st[1])
        sw = _flash_step_t(init, jnp.concatenate([k_ref[win_keys, pair(KV_WIDTH)], ohr_ref[...]], axis=1),
                           jnp.concatenate([qgt, exists_win], axis=0), vt_ref[pair(KV_WIDTH), win_keys], wbias_ref[g])
        o_w = sw[2] * _inv_or_zero(sw[1])
        gate = lambda branch: jnp.concatenate(
            [gates_t[branch * B_HEADS + g * B_GROUP + r:branch * B_HEADS + g * B_GROUP + r + 1, :] for r in range(B_GROUP)],
            axis=1)
        o = (o_s * gate(1) + o_w * gate(2)).T
        cols = slice(g * 2 * PAIR, (g + 1) * 2 * PAIR)
        o_ref[:, cols] = _scatter_heads(o, g) + oc_ref[:, cols]


def _block_onehots(t):
    pos = np.arange(KV_PAD + t) - KV_PAD
    absolute = (pos[:, None] // SEL_BLOCK == np.arange(PAIR)[None, :]) & (pos[:, None] >= 0)
    relative = np.arange(WINDOW + Q_ROWS)[:, None] // SEL_BLOCK == np.arange(PAIR)[None, :]
    i = np.arange(B_GROUP * Q_ROWS)[:, None] % Q_ROWS
    in_window = np.arange(WINDOW - Q_ROWS)[None, :] > i
    return (jnp.asarray(absolute, BF16), jnp.asarray(relative, BF16),
            jnp.asarray(np.where(in_window, 0.0, -MASK_C), F32))


def _nsa_prompt(u, kv_pad, cmp_tok, bias_cmp, bias_kv):
    t = u.shape[0]
    nq = Q_ROWS
    q_col = (2 * A_QK + 2 * A_WIDTH) // B_WIDTH
    g_col = (IN_DIM - 3 * B_HEADS) // PAIR
    q_spec = pl.BlockSpec((nq, B_WIDTH), lambda i: (i, q_col))
    gate_spec = pl.BlockSpec((nq, PAIR), lambda i: (i, g_col))
    oc, sel = pl.pallas_call(
        _nsa_cmp_prompt_kernel,
        grid=(t // nq,),
        in_specs=[q_spec, gate_spec,
                  pl.BlockSpec(cmp_tok.shape, lambda i: (0, 0, 0)),
                  pl.BlockSpec(bias_cmp.shape, lambda i: (0, 0, 0))],
        out_specs=[pl.BlockSpec((nq, B_WIDTH), lambda i: (i, 0)), pl.BlockSpec((nq, B_KV_HEADS * PAIR), lambda i: (i, 0))],
        out_shape=[jax.ShapeDtypeStruct((t, B_WIDTH), F32), jax.ShapeDtypeStruct((t, B_KV_HEADS * PAIR), F32)],
        compiler_params=_params("arbitrary"),
        name="nsa_cmp_select_prompt",
    )(u, u, cmp_tok, bias_cmp)
    oh_abs, oh_rel, win_mask = _block_onehots(t)
    win_bias = jnp.concatenate([jnp.broadcast_to(win_mask, (B_KV_HEADS,) + win_mask.shape), bias_kv], axis=2)
    kw = KV_WIDTH
    k_pad = jnp.concatenate([kv_pad[:, :kw], kv_pad[:, 2 * kw:3 * kw]], axis=1)
    vt_pad = jnp.concatenate([kv_pad[:, kw:2 * kw], kv_pad[:, 3 * kw:]], axis=1).T
    bias_kv_t, win_bias_t = bias_kv.transpose(0, 2, 1), win_bias.transpose(0, 2, 1)
    whole = lambda a: pl.BlockSpec(a.shape, lambda i: (0,) * a.ndim, pipeline_mode=pl.Buffered(1))
    return pl.pallas_call(
        _nsa_selwin_prompt_kernel,
        grid=(t // nq,),
        in_specs=[q_spec, gate_spec,
                  pl.BlockSpec((nq, B_KV_HEADS * PAIR), lambda i: (i, 0)),
                  pl.BlockSpec((nq, B_WIDTH), lambda i: (i, 0)),
                  whole(k_pad), whole(vt_pad), whole(oh_abs), whole(oh_rel), whole(bias_kv_t), whole(win_bias_t)],
        out_specs=pl.BlockSpec((nq, B_WIDTH), lambda i: (i, 0)),
        out_shape=jax.ShapeDtypeStruct((t, B_WIDTH), F32),
        compiler_params=_params("arbitrary"),
        name="nsa_select_window_prompt",
    )(u, u, sel, oc, k_pad, vt_pad, oh_abs, oh_rel, bias_kv_t, win_bias_t)


SEL_KEYS = PAST_LEN + PAIR
WIN_KEYS = WINDOW + PAIR


NSA_SEQ_BLOCK = 2


def _piece_t(qg, kt, vt, mask, bias):
    s = jnp.dot(qg, kt, preferred_element_type=F32) + bias
    s = jnp.where(mask, s, NEG)
    m = jnp.max(s, axis=-1, keepdims=True)
    p = jnp.where(mask, jnp.exp(s - m), 0.0)
    return m, jnp.sum(p, axis=-1, keepdims=True), _nt_dot(p.astype(BF16), vt), p


def _nsa_sample_kernel(pt_ref, *refs):
    n_pg = NSA_SEQ_BLOCK * N_PAGES
    pages = refs[:n_pg]
    (q_ref, gate_ref, cmp_ref, selnew_ref, winnew_ref, win_ref, bc_ref, bs_ref, bw_ref,
     o_ref, winout_ref, kt_ref, vt_ref, wkt_ref, wvt_ref) = refs[n_pg:]
    seqs = [_nsa_sample_one(pages[s * N_PAGES:(s + 1) * N_PAGES], q_ref.at[s], gate_ref.at[s], cmp_ref.at[s],
                            selnew_ref.at[s], winnew_ref.at[s], win_ref.at[s], bc_ref, bs_ref, bw_ref,
                            o_ref.at[s], winout_ref.at[s], kt_ref.at[s], vt_ref.at[s], wkt_ref.at[s], wvt_ref.at[s])
            for s in range(NSA_SEQ_BLOCK)]
    for _ in range(NSA_SAMPLE_STAGES):
        for seq in seqs:
            next(seq)


NSA_SAMPLE_STAGES = 5


def _nsa_sample_one(pages, q_ref, gate_ref, cmp_ref, selnew_ref, winnew_ref, win_ref, bc_ref, bs_ref, bw_ref,
                    o_ref, winout_ref, kt_ref, vt_ref, wkt_ref, wvt_ref):
    nt = q_ref.shape[0]
    grp = B_GROUP * nt
    rows = B_KV_HEADS * grp
    half = 2 * PAIR
    pad = jnp.zeros((PAIR - nt, 2 * half), F32)
    sel_new_t = jnp.concatenate([selnew_ref[...], pad], axis=0).T
    win_new_t = jnp.concatenate([winnew_ref[...], pad], axis=0).T
    for j, pg in enumerate(pages):
        kt_ref[:, j * PAGE_SIZE:(j + 1) * PAGE_SIZE] = pg[:half, :].astype(BF16)
        vt_ref[:, j * PAGE_SIZE:(j + 1) * PAGE_SIZE] = pg[half:, :].astype(BF16)
    kt_ref[:, PAST_LEN:] = sel_new_t[:half].astype(BF16)
    vt_ref[:, PAST_LEN:] = sel_new_t[half:].astype(BF16)
    buf = win_ref[...]
    wkt_ref[:, :WINDOW] = buf[:half].astype(BF16)
    wvt_ref[:, :WINDOW] = buf[half:].astype(BF16)
    wkt_ref[:, WINDOW:] = win_new_t[:half].astype(BF16)
    wvt_ref[:, WINDOW:] = win_new_t[half:].astype(BF16)
    shifted = pltpu.roll(buf, WINDOW - nt, axis=1)
    tail = pltpu.roll(win_new_t, PAIR - nt, axis=1)
    lane = lax.broadcasted_iota(jnp.int32, tail.shape, 1)
    winout_ref[:, :WINDOW - PAIR] = shifted[:, :WINDOW - PAIR]
    winout_ref[:, WINDOW - PAIR:] = jnp.where(lane >= PAIR - nt, tail, shifted[:, WINDOW - PAIR:])
    yield

    zero = jnp.zeros((grp, PAIR), BF16)
    qq = jnp.concatenate(
        [jnp.concatenate([_group_q(q_ref, g), zero] if g // 2 == 0 else [zero, _group_q(q_ref, g)], axis=1)
         for g in range(B_KV_HEADS)], axis=0)
    take = lambda acc, g: acc[g * grp:(g + 1) * grp, (g // 2) * PAIR:(g // 2 + 1) * PAIR]
    qpos = PAST_LEN + lax.broadcasted_iota(jnp.int32, (rows, 1), 0) % nt

    n_tok = cmp_ref.shape[1]
    ck = jnp.concatenate([cmp_ref[0], cmp_ref[1]], axis=1).astype(BF16)
    cv = jnp.concatenate([cmp_ref[2], cmp_ref[3]], axis=1).astype(BF16)
    mask_c = lax.broadcasted_iota(jnp.int32, (rows, n_tok), 1) < n_tok - (N_SUB - 1)
    mc, lc, acc_c, pc = _piece(qq, ck, cv, mask_c, bc_ref[...])
    pc = pc * _inv_or_zero(lc)
    p_all = jnp.concatenate([jnp.sum(pc[g * grp:(g + 1) * grp].reshape(B_GROUP, nt, n_tok), axis=0)
                             for g in range(B_KV_HEADS)], axis=0)
    yield
    o01 = _overlap01(lax.broadcasted_iota(jnp.int32, (n_tok, PAIR), 0), lax.broadcasted_iota(jnp.int32, (n_tok, PAIR), 1))
    qpos_gt = PAST_LEN + lax.broadcasted_iota(jnp.int32, (B_KV_HEADS * nt, PAIR), 0) % nt
    n_blocks = pl.cdiv(PAST_LEN + nt, SEL_BLOCK)
    sel = _select_blocks(_dot01(p_all, o01), qpos_gt, n_cand=n_blocks).astype(BF16)
    sel_rows = jnp.concatenate([sel[g * nt:(g + 1) * nt] for g in range(B_KV_HEADS) for _ in range(B_GROUP)], axis=0)
    yield

    ks = lax.broadcasted_iota(jnp.int32, (rows, SEL_KEYS), 1)
    mask_s = (_expand_blocks(sel_rows, 0, SEL_KEYS) > 0.5) & (ks <= qpos)
    ms, ls, acc_s, _ = _piece_t(qq, kt_ref[...], vt_ref[...], mask_s, bs_ref[...])
    yield
    kw = PAST_LEN - WINDOW + lax.broadcasted_iota(jnp.int32, (rows, WIN_KEYS), 1)
    mask_w = (kw <= qpos) & (qpos - kw < WINDOW)
    mw, lw, acc_w, _ = _piece_t(qq, wkt_ref[...], wvt_ref[...], mask_w, bw_ref[...])
    acc_c, acc_s, acc_w = acc_c * _inv_or_zero(lc), acc_s * _inv_or_zero(ls), acc_w * _inv_or_zero(lw)
    for g in range(B_KV_HEADS):
        o = (take(acc_c, g) * _gate_rows(gate_ref, 0, g) + take(acc_s, g) * _gate_rows(gate_ref, 1, g)
             + take(acc_w, g) * _gate_rows(gate_ref, 2, g))
        o_ref[:, g * 2 * PAIR:(g + 1) * 2 * PAIR] = _scatter_heads(o, g)
    yield


def _nsa_sample(u_s, cmp_tok, cache_sel, win_buf, page_table, bias_c, bias_s, bias_w):
    n_seq, nt, _ = u_s.shape
    sb = NSA_SEQ_BLOCK
    q_col = (2 * A_QK + 2 * A_WIDTH) // B_WIDTH
    kv_col = (2 * A_QK + 2 * A_WIDTH + B_WIDTH) // (4 * PAIR)
    g_col = (IN_DIM - 3 * B_HEADS) // PAIR
    const = lambda a: pl.BlockSpec(a.shape, lambda b, pt: (0, 0))
    in_specs = [pl.BlockSpec((None, 4 * PAIR, PAGE_SIZE), lambda b, pt, s=s, j=j: (pt[b * sb + s, j], 0, 0))
                for s in range(sb) for j in range(N_PAGES)]
    in_specs += [
        pl.BlockSpec((sb, nt, B_WIDTH), lambda b, pt: (b, 0, q_col)),
        pl.BlockSpec((sb, nt, PAIR), lambda b, pt: (b, 0, g_col)),
        pl.BlockSpec((sb,) + cmp_tok.shape[1:], lambda b, pt: (b, 0, 0, 0)),
        pl.BlockSpec((sb, nt, 4 * PAIR), lambda b, pt: (b, 0, kv_col + 1)),
        pl.BlockSpec((sb, nt, 4 * PAIR), lambda b, pt: (b, 0, kv_col + 2)),
        pl.BlockSpec((sb, 4 * PAIR, WINDOW), lambda b, pt: (b, 0, 0)),
        const(bias_c), const(bias_s), const(bias_w),
    ]
    return pl.pallas_call(
        _nsa_sample_kernel,
        grid_spec=pltpu.PrefetchScalarGridSpec(
            num_scalar_prefetch=1,
            grid=(n_seq // sb,),
            in_specs=in_specs,
            out_specs=[pl.BlockSpec((sb, nt, B_WIDTH), lambda b, pt: (b, 0, 0)),
                       pl.BlockSpec((sb, 4 * PAIR, WINDOW), lambda b, pt: (b, 0, 0))],
            scratch_shapes=[pltpu.VMEM((sb, 2 * PAIR, SEL_KEYS), BF16), pltpu.VMEM((sb, 2 * PAIR, SEL_KEYS), BF16),
                            pltpu.VMEM((sb, 2 * PAIR, WIN_KEYS), BF16), pltpu.VMEM((sb, 2 * PAIR, WIN_KEYS), BF16)],
        ),
        out_shape=[jax.ShapeDtypeStruct((n_seq, nt, B_WIDTH), F32), jax.ShapeDtypeStruct(win_buf.shape, F32)],
        compiler_params=_params("arbitrary"),
        name="nsa_sample",
    )(page_table, *([cache_sel] * (sb * N_PAGES)), u_s, u_s, cmp_tok, u_s, u_s, win_buf, bias_c, bias_s, bias_w)


PROMPT_TM = 1024
PROJ_TN = 512
COL_CMP = 2 * A_QK + 2 * A_WIDTH + B_WIDTH
COL_SEL = COL_CMP + 2 * KV_WIDTH
COL_WIN = COL_SEL + 2 * KV_WIDTH
COL_GATE = COL_WIN + 2 * KV_WIDTH


def _time_major(a):
    return a.transpose(1, 0, 2).reshape(a.shape[0] * a.shape[1], a.shape[2])


def _seq_major(a2d, n_seq):
    return a2d.reshape(a2d.shape[0] // n_seq, n_seq, a2d.shape[1]).transpose(1, 0, 2)


def _kv_rows(u3, col):
    return u3[..., col:col + 2 * KV_WIDTH].reshape(u3.shape[:-1] + (2, B_KV_HEADS, B_DH))


def kernel(x_prompt, x_sample, cache_cmp, cache_sel, state_win, state_hgrn, state_conv, page_table, c_prompt, c_sample, norm_g, ada_w, ada_b, ffn_w_gate, ffn_w_up, ffn_w_down, w_in_even, hgrn_lower_bound, hgrn_norm_g, cmp_pe, cmp_w1, cmp_b1, cmp_w2, cmp_b2, rel_bias, w_out_even, conv_w_pw1, conv_b_pw1, conv_w_dw, conv_b_dw, conv_ln_g, conv_ln_b, conv_w_pw2, conv_b_pw2, final_norm_g):
    P = {'norm_g': norm_g, 'ffn_w_gate': ffn_w_gate, 'ffn_w_up': ffn_w_up, 'ffn_w_down': ffn_w_down,
         'final_norm_g': final_norm_g}
    n_seq = x_sample.shape[0]
    n_pool = cache_cmp.shape[1]
    rows_p, rows_s = _prompt_rows(PROMPT_TM), _sample_rows()
    g4 = norm_g.reshape(DEPTH, 3, 1, D_MODEL)
    c_all = jnp.concatenate([c_sample, jnp.tile(c_prompt, (SUBLANES, 1))], axis=0)
    mod = _ada_mod(c_all, ada_w, ada_b)
    bias_pc, bias_pkv, bias_sc, bias_ss, bias_sw = _bias_tables(rel_bias)

    xp = x_prompt.reshape(SEQ, D_MODEL)
    xs = _time_major(x_sample)
    cmp_p, cmp_s, sel_p, sel_s, win_p, win_s, hgrn_p, hgrn_s, conv_p, conv_s = ([] for _ in range(10))
    for l in range(DEPTH):
        i = l // 2
        last = l == DEPTH - 1
        xs, w_bf16 = _ffn(xs, rows_s, mod, l, 0, P)
        xp, _ = _ffn(xp, rows_p, mod, l, 0, P, w_bf16=w_bf16)
        if l % 2 == 0:
            up = _proj(xp, rows_p, mod, l, g4, w_in_even[i:i + 1], IN_DIM, PROJ_TN)
            us = _seq_major(_proj(xs, rows_s, mod, l, g4, w_in_even[i:i + 1], IN_DIM, PROJ_TN), n_seq)
            gn = hgrn_norm_g[i:i + 1]
            oa_p, hp = _hgrn(up, hgrn_lower_bound, gn, l)
            oa_s, hs = _hgrn(us.reshape(n_seq * DEC_SEQ, IN_DIM), hgrn_lower_bound, gn, l, state_hgrn[i])
            weights = _cmp_weights(cmp_pe[i], cmp_w1[i], cmp_b1[i], cmp_w2[i], cmp_b2[i])
            kv_pad = jnp.pad(up[:, COL_SEL:COL_GATE].astype(BF16), ((KV_PAD, 0), (0, 0)))
            ob_p = _nsa_prompt(up, kv_pad, _compress_prompt(up, weights), bias_pc, bias_pkv)
            cmp_t = cache_cmp[i].transpose(0, 2, 3, 4, 1).reshape(n_pool, 4 * PAIR, PAGE_SIZE)
            cmp_tok_s = _compress_sample(cmp_t, page_table, weights)
            sel_t = cache_sel[i].transpose(0, 2, 3, 4, 1).reshape(n_pool, 4 * PAIR, PAGE_SIZE)
            win_t = state_win[i].transpose(0, 2, 3, 4, 1).reshape(n_seq, 4 * PAIR, WINDOW)
            ob_s, wn = _nsa_sample(us, cmp_tok_s, sel_t, win_t, page_table, bias_sc, bias_ss, bias_sw)
            xp = _out_proj([oa_p, ob_p], w_out_even[i:i + 1], [0, A_WIDTH], None, xp, rows_p, mod, l, PROJ_TN)
            xs = _out_proj([_time_major(oa_s.reshape(n_seq, DEC_SEQ, A_WIDTH)), _time_major(ob_s)],
                           w_out_even[i:i + 1], [0, A_WIDTH], None, xs, rows_s, mod, l, PROJ_TN)
            up3 = up[None]
            cmp_p.append(_kv_rows(up3, COL_CMP))
            sel_p.append(_kv_rows(up3, COL_SEL))
            win_p.append(_kv_rows(up3[:, SEQ - min(WINDOW, SEQ):], COL_WIN))
            cmp_s.append(_kv_rows(us, COL_CMP))
            sel_s.append(_kv_rows(us, COL_SEL))
            win_s.append(wn.reshape(n_seq, 2, B_KV_HEADS, B_DH, WINDOW).transpose(0, 4, 1, 2, 3))
            hgrn_p.append(hp[None])
            hgrn_s.append(hs)
        else:
            b_pw1 = conv_b_pw1[i].reshape(1, 1, 2 * CONV_DIM)
            b_pw2 = conv_b_pw2[i].reshape(1, 1, D_MODEL)
            vec = lambda a: a[i].reshape(1, CONV_DIM)
            glu_p = _glu_proj(xp, rows_p, mod, l, g4, conv_w_pw1[i:i + 1], b_pw1, PROJ_TN)
            act_p = _conv_prompt(glu_p, conv_w_dw[i], vec(conv_b_dw), vec(conv_ln_g), vec(conv_ln_b))
            xp = _out_proj([act_p], conv_w_pw2[i:i + 1], [0], b_pw2, xp, rows_p, mod, l, PROJ_TN)
            glu_s = _glu_proj(xs, rows_s, mod, l, g4, conv_w_pw1[i:i + 1], b_pw1, PROJ_TN)
            act_s, nb = _conv_sample(glu_s.reshape(DEC_SEQ, n_seq, CONV_DIM), state_conv[i].transpose(1, 0, 2),
                                     conv_w_dw[i], vec(conv_b_dw), vec(conv_ln_g), vec(conv_ln_b))
            xs = _out_proj([act_s.reshape(DEC_SEQ * n_seq, CONV_DIM)], conv_w_pw2[i:i + 1], [0], b_pw2, xs, rows_s, mod, l, PROJ_TN)
            conv_p.append(glu_p[None, SEQ - (CONV_WIDTH - 1):])
            conv_s.append(nb.transpose(1, 0, 2))
        xs, w_bf16 = _ffn(xs, rows_s, mod, l, 2, P, final_norm=last)
        xp, _ = _ffn(xp, rows_p, mod, l, 2, P, w_bf16=w_bf16, final_norm=last)
    y_prompt = xp.reshape(1, SEQ, D_MODEL)
    y_sample = _seq_major(xs, n_seq)
    st = jnp.stack
    return (y_prompt, y_sample, st(cmp_p), st(cmp_s), st(sel_p), st(sel_s), st(win_p), st(win_s),
            st(hgrn_p), st(hgrn_s), st(conv_p), st(conv_s))
```

```python
import functools
import math

import numpy as np
import jax
import jax.numpy as jnp
from jax import lax
from jax.experimental import pallas as pl
from jax.experimental.pallas import tpu as pltpu

F32 = jnp.float32
BF16 = jnp.bfloat16

D_MODEL = 2048
SEQ = 8192
DEPTH = 2
DEC_BATCH = 128
DEC_SEQ = 8
PAST_LEN = 2048
PAGE_SIZE = 128
N_MOD = 9
D_FF = 5504
EPS = 1e-6
A_HEADS = 8
A_DK = 128
A_DV = 128
A_QK = A_HEADS * A_DK
A_WIDTH = A_HEADS * A_DV
B_HEADS = 16
B_KV_HEADS = 4
B_DH = 64
B_GROUP = B_HEADS // B_KV_HEADS
B_WIDTH = B_HEADS * B_DH
KV_WIDTH = B_KV_HEADS * B_DH
CMP_BLOCK = 32
CMP_STRIDE = 16
CMP_HIDDEN = 256
SEL_BLOCK = 64
N_SELECT = 16
WINDOW = 512
N_BUCKETS = 32
MAX_DISTANCE = 128
MIX_WIDTH = A_WIDTH + B_WIDTH
IN_DIM = 2 * A_QK + 2 * A_WIDTH + B_WIDTH + 6 * KV_WIDTH + 3 * B_HEADS
CONV_WIDTH = 31
CONV_DIM = D_MODEL
NEG = -1e30
FORCE = 1e9

V7X_VMEM_LIMIT_BYTES = 60 * 1024 * 1024
SUBLANES = 8
LANES = 128

N_SEQ_ROWS = DEC_BATCH + SUBLANES
PROMPT_ROW_BLOCK = DEC_BATCH // SUBLANES


def _params(*sem):
    return pltpu.CompilerParams(dimension_semantics=sem, vmem_limit_bytes=V7X_VMEM_LIMIT_BYTES)


def _silu(x):
    return x * jax.nn.sigmoid(x)


def _bdot(a, b):
    return jnp.dot(a.astype(BF16), b.astype(BF16), preferred_element_type=F32)


def _ada_kernel(c_ref, w_ref, b_ref, o_ref):
    o_ref[...] = _bdot(_silu(c_ref[...]), w_ref[...]) + b_ref[...]


def _ada_mod(c_all, ada_w, ada_b):
    n = c_all.shape[0]
    return pl.pallas_call(
        _ada_kernel,
        grid=(DEPTH, N_MOD),
        in_specs=[
            pl.BlockSpec((n, D_MODEL), lambda l, k: (0, 0)),
            pl.BlockSpec((None, D_MODEL, D_MODEL), lambda l, k: (l, 0, k)),
            pl.BlockSpec((None, None, 1, D_MODEL), lambda l, k: (l, k, 0, 0)),
        ],
        out_specs=pl.BlockSpec((None, None, n, D_MODEL), lambda l, k: (l, k, 0, 0)),
        out_shape=jax.ShapeDtypeStruct((DEPTH, N_MOD, n, D_MODEL), F32),
        compiler_params=_params("arbitrary", "arbitrary"),
        name="ada_mod",
    )(c_all, ada_w, ada_b.reshape(DEPTH, N_MOD, 1, D_MODEL))


def _norm_mod(x, g, shift, scale):
    ms = jnp.mean(x * x, axis=-1, keepdims=True)
    y = x * lax.rsqrt(ms + EPS) * g
    h = y * (1.0 + scale) + shift
    return h.reshape(x.shape[0] * x.shape[1], x.shape[2]).astype(BF16)


class _Rows:
    def __init__(self, m, bs, nt, seq_block):
        assert m % (bs * nt) == 0
        self.m, self.bs, self.nt, self.seq_block = m, bs, nt, seq_block
        self.tm = bs * nt
        self.n_tiles = m // self.tm

    def view(self, x2d):
        return x2d.reshape(self.m // self.bs, self.bs, x2d.shape[-1])

    def x_spec(self, width, col=lambda j: 0, single_buffer=False):
        mode = dict(pipeline_mode=pl.Buffered(1)) if single_buffer else {}
        return pl.BlockSpec((self.nt, self.bs, width), lambda i, j: (i, 0, col(j)), **mode)

    def mod_spec(self, layer, k, width=D_MODEL, col=lambda j: 0):
        sb = self.seq_block
        return pl.BlockSpec((None, None, self.bs, width), lambda i, j: (layer, k, sb, col(j)))


def _prompt_rows(tm):
    return _Rows(SEQ, SUBLANES, tm // SUBLANES, PROMPT_ROW_BLOCK)


def _sample_rows():
    return _Rows(DEC_BATCH * DEC_SEQ, DEC_BATCH, DEC_SEQ, 0)


FFN_TF_F32 = 256
FFN_TF_BF16 = 512
FFN_ACC_CHUNKS = 4


def _ffn_kernel(x_ref, sh_ref, sc_ref, gt_ref, g_ref, wg_ref, wu_ref, wd_ref, fg_ref, o_ref, *rest, final_norm, emit):
    if emit:
        wg_o, wu_o, wd_o, h_ref = rest
    else:
        (h_ref,) = rest
    j = pl.program_id(1)
    nj = pl.num_programs(1)
    tf = wg_ref.shape[1]

    nt, bs = o_ref.shape[0], o_ref.shape[1]
    step = max(nt // FFN_ACC_CHUNKS, 1)

    @pl.when(j == 0)
    def _():
        for r in range(0, nt, step):
            h_ref[r * bs:(r + step) * bs, :] = _norm_mod(x_ref[r:r + step], g_ref[...], sh_ref[...], sc_ref[...])
        o_ref[...] = jnp.zeros_like(o_ref)

    wg, wu, wd = wg_ref[...].astype(BF16), wu_ref[...].astype(BF16), wd_ref[...].astype(BF16)
    if emit:
        wg_o[...], wu_o[...], wd_o[...] = wg, wu, wd
    valid = D_FF - j * tf
    h = h_ref[...]
    a = _silu(_bdot(h, wg)) * _bdot(h, wu)
    col = lax.broadcasted_iota(jnp.int32, a.shape, 1)
    a = jnp.where(col < valid, a, 0.0)
    row = lax.broadcasted_iota(jnp.int32, wd.shape, 0)
    wd = jnp.where(row < valid, wd, jnp.zeros_like(wd))
    a = a.astype(BF16)
    for r in range(0, nt, step):
        o_ref[r:r + step] += _bdot(a[r * bs:(r + step) * bs], wd).reshape(step, bs, o_ref.shape[2])

    @pl.when(j == nj - 1)
    def _():
        for r in range(0, nt, step):
            y = x_ref[r:r + step] + (0.5 * gt_ref[...]) * o_ref[r:r + step]
            if final_norm:
                ms = jnp.mean(y * y, axis=-1, keepdims=True)
                y = y * lax.rsqrt(ms + EPS) * fg_ref[...]
            o_ref[r:r + step] = y


def _ffn(x2d, rows, mod, layer, sub, P, w_bf16=None, final_norm=False):
    half = sub // 2
    emit = w_bf16 is None
    tf = FFN_TF_F32 if emit else FFN_TF_BF16
    g_norm = P['norm_g'].reshape(DEPTH, 3, 1, D_MODEL)
    fg = P['final_norm_g'].reshape(1, D_MODEL)
    if emit:
        weights = (P['ffn_w_gate'], P['ffn_w_up'], P['ffn_w_down'])
        w_specs = [pl.BlockSpec((None, None, D_MODEL, tf), lambda i, j: (layer, half, 0, j)),
                   pl.BlockSpec((None, None, D_MODEL, tf), lambda i, j: (layer, half, 0, j)),
                   pl.BlockSpec((None, None, tf, D_MODEL), lambda i, j: (layer, half, j, 0))]
    else:
        weights = w_bf16
        w_specs = [pl.BlockSpec((D_MODEL, tf), lambda i, j: (0, j)),
                   pl.BlockSpec((D_MODEL, tf), lambda i, j: (0, j)),
                   pl.BlockSpec((tf, D_MODEL), lambda i, j: (j, 0))]
    out_specs = [rows.x_spec(D_MODEL, single_buffer=emit)]
    out_shape = [jax.ShapeDtypeStruct((rows.m // rows.bs, rows.bs, D_MODEL), F32)]
    if emit:
        out_specs += [pl.BlockSpec((D_MODEL, tf), lambda i, j: (0, j)),
                      pl.BlockSpec((D_MODEL, tf), lambda i, j: (0, j)),
                      pl.BlockSpec((tf, D_MODEL), lambda i, j: (j, 0))]
        out_shape += [jax.ShapeDtypeStruct((D_MODEL, D_FF), BF16), jax.ShapeDtypeStruct((D_MODEL, D_FF), BF16),
                      jax.ShapeDtypeStruct((D_FF, D_MODEL), BF16)]
        assert rows.n_tiles == 1
    outs = pl.pallas_call(
        functools.partial(_ffn_kernel, final_norm=final_norm, emit=emit),
        grid=(rows.n_tiles, pl.cdiv(D_FF, tf)),
        in_specs=[
            rows.x_spec(D_MODEL, single_buffer=True),
            rows.mod_spec(layer, 3 * sub), rows.mod_spec(layer, 3 * sub + 1), rows.mod_spec(layer, 3 * sub + 2),
            pl.BlockSpec((None, None, 1, D_MODEL), lambda i, j: (layer, sub, 0, 0)),
            *w_specs,
            pl.BlockSpec((1, D_MODEL), lambda i, j: (0, 0)),
        ],
        out_specs=out_specs,
        out_shape=out_shape,
        scratch_shapes=[pltpu.VMEM((rows.tm, D_MODEL), BF16)],
        compiler_params=_params("arbitrary", "arbitrary"),
        name="ffn_half_step",
    )(rows.view(x2d), mod, mod, mod, g_norm, *weights, fg)
    return outs[0].reshape(rows.m, D_MODEL), tuple(outs[1:])


def _proj_kernel(x_ref, sh_ref, sc_ref, g_ref, w_ref, o_ref, h_ref):
    @pl.when(pl.program_id(1) == 0)
    def _():
        h_ref[...] = _norm_mod(x_ref[...], g_ref[...], sh_ref[...], sc_ref[...])

    o_ref[...] = _nt_dot(h_ref[...], w_ref[...].astype(BF16))


def _proj(x2d, rows, mod, layer, g_norm4, w3t, n_out, tn):
    return pl.pallas_call(
        _proj_kernel,
        grid=(rows.n_tiles, pl.cdiv(n_out, tn)),
        in_specs=[
            rows.x_spec(D_MODEL),
            rows.mod_spec(layer, 3), rows.mod_spec(layer, 4),
            pl.BlockSpec((None, None, 1, D_MODEL), lambda i, j: (layer, 1, 0, 0)),
            pl.BlockSpec((None, tn, D_MODEL), lambda i, j: (0, j, 0)),
        ],
        out_specs=pl.BlockSpec((rows.tm, tn), lambda i, j: (i, j)),
        out_shape=jax.ShapeDtypeStruct((rows.m, n_out), F32),
        scratch_shapes=[pltpu.VMEM((rows.tm, D_MODEL), BF16)],
        compiler_params=_params("arbitrary", "arbitrary"),
        name="prenorm_proj",
    )(rows.view(x2d), mod, mod, g_norm4, w3t)


def _glu_proj_kernel(x_ref, sh_ref, sc_ref, g_ref, wa_ref, wg_ref, ba_ref, bg_ref, o_ref, h_ref):
    @pl.when(pl.program_id(1) == 0)
    def _():
        h_ref[...] = _norm_mod(x_ref[...], g_ref[...], sh_ref[...], sc_ref[...])

    h = h_ref[...]
    a = _bdot(h, wa_ref[...]) + ba_ref[...]
    gt = _bdot(h, wg_ref[...]) + bg_ref[...]
    o_ref[...] = a * jax.nn.sigmoid(gt)


def _glu_proj(x2d, rows, mod, layer, g_norm4, w3, b3, tn):
    nb = CONV_DIM // tn
    return pl.pallas_call(
        _glu_proj_kernel,
        grid=(rows.n_tiles, nb),
        in_specs=[
            rows.x_spec(D_MODEL),
            rows.mod_spec(layer, 3), rows.mod_spec(layer, 4),
            pl.BlockSpec((None, None, 1, D_MODEL), lambda i, j: (layer, 1, 0, 0)),
            pl.BlockSpec((None, D_MODEL, tn), lambda i, j: (0, 0, j)),
            pl.BlockSpec((None, D_MODEL, tn), lambda i, j: (0, 0, j + nb)),
            pl.BlockSpec((None, 1, tn), lambda i, j: (0, 0, j)),
            pl.BlockSpec((None, 1, tn), lambda i, j: (0, 0, j + nb)),
        ],
        out_specs=pl.BlockSpec((rows.tm, tn), lambda i, j: (i, j)),
        out_shape=jax.ShapeDtypeStruct((rows.m, CONV_DIM), F32),
        scratch_shapes=[pltpu.VMEM((rows.tm, D_MODEL), BF16)],
        compiler_params=_params("arbitrary", "arbitrary"),
        name="prenorm_glu_proj",
    )(rows.view(x2d), mod, mod, g_norm4, w3, w3, b3, b3)


def _out_kernel(*refs, n_in, has_bias):
    a_refs = refs[:n_in]
    w_refs = refs[n_in:2 * n_in]
    pos = 2 * n_in
    b_ref = refs[pos] if has_bias else None
    pos += int(has_bias)
    x_ref, gt_ref, o_ref = refs[pos:pos + 3]
    y = _bdot(a_refs[0][...], w_refs[0][...])
    for a_ref, w_ref in zip(a_refs[1:], w_refs[1:]):
        y += _bdot(a_ref[...], w_ref[...])
    if has_bias:
        y += b_ref[...]
    o_ref[...] = x_ref[...] + gt_ref[...] * y.reshape(o_ref.shape)


def _out_proj(acts, w3, k_offsets, bias3, x2d, rows, mod, layer, tn):
    n_in = len(acts)
    in_specs = [pl.BlockSpec((rows.tm, a.shape[1]), lambda i, j: (i, 0)) for a in acts]
    for a, off in zip(acts, k_offsets):
        kb = off // a.shape[1]
        in_specs.append(pl.BlockSpec((None, a.shape[1], tn), lambda i, j, kb=kb: (0, kb, j)))
    args = list(acts) + [w3] * n_in
    if bias3 is not None:
        in_specs.append(pl.BlockSpec((None, 1, tn), lambda i, j: (0, 0, j)))
        args.append(bias3)
    in_specs += [rows.x_spec(tn, col=lambda j: j), rows.mod_spec(layer, 5, width=tn, col=lambda j: j)]
    args += [rows.view(x2d), mod]
    out = pl.pallas_call(
        functools.partial(_out_kernel, n_in=n_in, has_bias=bias3 is not None),
        grid=(rows.n_tiles, D_MODEL // tn),
        in_specs=in_specs,
        out_specs=rows.x_spec(tn, col=lambda j: j),
        out_shape=jax.ShapeDtypeStruct((rows.m // rows.bs, rows.bs, D_MODEL), F32),
        compiler_params=_params("arbitrary", "arbitrary"),
        name="out_proj_residual",
    )(*args)
    return out.reshape(rows.m, D_MODEL)


HGRN_ROWS = 512
HGRN_HEADS = 8
HGRN_TRI = 128


def _hgrn_prepare(f_ref, lb_ref, cum_s, kk_s, *, layer, c):
    p = lb_ref[...]
    e = jnp.exp(p - jnp.max(p, axis=0, keepdims=True))
    sm = e / jnp.sum(e, axis=0, keepdims=True)
    lb = jnp.sum(sm[:layer + 1], axis=0, keepdims=True)
    f = lb + (1.0 - lb) * jax.nn.sigmoid(f_ref[...])
    lf = jnp.log(f)
    n = HGRN_TRI
    r = lax.broadcasted_iota(jnp.int32, (n, n), 0)
    s = lax.broadcasted_iota(jnp.int32, (n, n), 1)
    tri = jnp.where((s <= r) & (s // c == r // c), 1.0, 0.0).astype(F32)
    for b in range(f.shape[0] // n):
        cum_s[b * n:(b + 1) * n, :] = jnp.dot(tri, lf[b * n:(b + 1) * n], preferred_element_type=F32,
                                              precision=lax.Precision.HIGHEST)
    kk_s[...] = 1.0 - f


def _hgrn_subchunk(r0, c, st, hh, q_ref, v_ref, g_ref, gn_ref, cum_s, kk_s):
    rows = pl.ds(r0, c)
    cols = slice(hh * A_DK, (hh + 1) * A_DK)
    cum = cum_s[rows, cols]
    q = q_ref[rows, cols]
    kk = kk_s[rows, cols]
    vv = v_ref[rows, cols]
    last = cum[c - 1:c, :]
    o = lax.dot_general((q * jnp.exp(cum)).astype(BF16), st.astype(BF16), (((1,), (1,)), ((), ())),
                        preferred_element_type=F32)
    srow = lax.broadcasted_iota(jnp.int32, (c, A_DK), 0)
    xs = []
    for t in range(c):
        d = jnp.where(srow <= t, cum[t:t + 1, :] - cum, NEG)
        xs.append(jnp.exp(d) * (q[t:t + 1, :] * kk))
    x = jnp.concatenate(xs, axis=0).astype(BF16)
    w = jnp.dot(x, jnp.ones((A_DK, A_DV), BF16), preferred_element_type=F32)
    o = o + jnp.sum(w.reshape(c, c, A_DV) * vv[None], axis=1)
    ke = kk * jnp.exp(last - cum)
    st_new = st * jnp.exp(last) + lax.dot_general(vv.astype(BF16), ke.astype(BF16), (((0,), (0,)), ((), ())),
                                                  preferred_element_type=F32)
    ms = jnp.mean(o * o, axis=-1, keepdims=True)
    y = o * lax.rsqrt(ms + EPS) * gn_ref[:, cols] * _silu(g_ref[rows, cols])
    return y, st_new


def _hgrn_prompt_kernel(q_ref, f_ref, v_ref, g_ref, lb_ref, gn_ref, o_ref, s_ref, st_ref, cum_s, kk_s, *, layer, c):
    i = pl.program_id(1)

    @pl.when(i == 0)
    def _():
        st_ref[...] = jnp.zeros_like(st_ref)

    _hgrn_prepare(f_ref, lb_ref, cum_s, kk_s, layer=layer, c=c)

    def body(n, carry):
        r0 = pl.multiple_of(n * c, c)
        for hh in range(HGRN_HEADS):
            y, st_new = _hgrn_subchunk(r0, c, st_ref[hh], hh, q_ref, v_ref, g_ref, gn_ref, cum_s, kk_s)
            st_ref[hh] = st_new
            o_ref[pl.ds(r0, c), hh * A_DV:(hh + 1) * A_DV] = y
        return carry

    lax.fori_loop(0, q_ref.shape[0] // c, body, 0)

    @pl.when(i == pl.num_programs(1) - 1)
    def _():
        for hh in range(HGRN_HEADS):
            s_ref[hh] = st_ref[hh].T


def _hgrn_sample_kernel(q_ref, f_ref, v_ref, g_ref, lb_ref, gn_ref, s0_ref, o_ref, s_ref, cum_s, kk_s, *, layer, c):
    _hgrn_prepare(f_ref, lb_ref, cum_s, kk_s, layer=layer, c=c)

    def body(n, carry):
        r0 = pl.multiple_of(n * c, c)
        for hh in range(HGRN_HEADS):
            y, st_new = _hgrn_subchunk(r0, c, s0_ref[n, hh].T, hh, q_ref, v_ref, g_ref, gn_ref, cum_s, kk_s)
            s_ref[n, hh] = st_new.T
            o_ref[pl.ds(r0, c), hh * A_DV:(hh + 1) * A_DV] = y
        return carry

    lax.fori_loop(0, q_ref.shape[0] // c, body, 0)


def _hgrn(u, lower_bound, norm_g2, layer, s0=None):
    m = u.shape[0]
    hb = HGRN_HEADS
    w = hb * A_DK
    nb = A_QK // w
    tc = HGRN_ROWS if s0 is None else HGRN_ROWS // 4

    def col(seg):
        return pl.BlockSpec((tc, w), lambda h, i, seg=seg: (i, seg * nb + h))

    in_specs = [col(0), col(1), col(2), col(3),
                pl.BlockSpec((DEPTH + 1, w), lambda h, i: (0, h)),
                pl.BlockSpec((1, w), lambda h, i: (0, h))]
    o_spec = pl.BlockSpec((tc, w), lambda h, i: (i, h))
    scratch = [pltpu.VMEM((tc, w), F32), pltpu.VMEM((tc, w), F32)]
    if s0 is None:
        return pl.pallas_call(
            functools.partial(_hgrn_prompt_kernel, layer=layer, c=16),
            grid=(nb, m // tc),
            in_specs=in_specs,
            out_specs=[o_spec, pl.BlockSpec((hb, A_DK, A_DV), lambda h, i: (h, 0, 0))],
            out_shape=[jax.ShapeDtypeStruct((m, A_WIDTH), F32), jax.ShapeDtypeStruct((A_HEADS, A_DK, A_DV), F32)],
            scratch_shapes=[pltpu.VMEM((hb, A_DV, A_DK), F32)] + scratch,
            compiler_params=_params("arbitrary", "arbitrary"),
            name="hgrn2_prompt",
        )(u, u, u, u, lower_bound, norm_g2)
    c = DEC_SEQ
    ns = tc // c
    s_spec = pl.BlockSpec((ns, hb, A_DK, A_DV), lambda h, i: (i, h, 0, 0))
    return pl.pallas_call(
        functools.partial(_hgrn_sample_kernel, layer=layer, c=c),
        grid=(nb, m // tc),
        in_specs=in_specs + [s_spec],
        out_specs=[o_spec, s_spec],
        out_shape=[jax.ShapeDtypeStruct((m, A_WIDTH), F32), jax.ShapeDtypeStruct(s0.shape, F32)],
        scratch_shapes=scratch,
        compiler_params=_params("arbitrary", "arbitrary"),
        name="hgrn2_sample",
    )(u, u, u, u, lower_bound, norm_g2, s0)


CONV_HALO = 32
CONV_ROWS = 256


def _ln_silu(y, g, b):
    mu = jnp.mean(y, axis=-1, keepdims=True)
    yc = y - mu
    var = jnp.mean(yc * yc, axis=-1, keepdims=True)
    return _silu(yc * lax.rsqrt(var + EPS) * g + b)


CONV_CHUNK_ROWS = 64
CONV_CHUNK_COLS = 512


def _conv_prompt_kernel(cur_ref, halo_ref, w_ref, b_ref, g_ref, lb_ref, o_ref, ext_ref, ph_ref, y_ref):
    i = pl.program_id(0)
    tt = cur_ref.shape[0]
    ext_ref[0:CONV_HALO, :] = jnp.where(i > 0, halo_ref[...], 0.0)
    ext_ref[CONV_HALO:, :] = cur_ref[...]
    off = CONV_HALO - (CONV_WIDTH - 1)
    span = ph_ref.shape[1]
    for r in range(1, SUBLANES):
        ph_ref[r - 1] = ext_ref[r:r + span, :]
    for c0 in range(0, CONV_DIM, CONV_CHUNK_COLS):
        cs = slice(c0, c0 + CONV_CHUNK_COLS)
        for r0 in range(0, tt, CONV_CHUNK_ROWS):
            y = jnp.zeros((CONV_CHUNK_ROWS, CONV_CHUNK_COLS), F32) + b_ref[:, cs]
            for w in range(CONV_WIDTH):
                a, r = divmod(off + w, SUBLANES)
                src = ext_ref if r == 0 else ph_ref.at[r - 1]
                lo = SUBLANES * a + r0
                y = y + src[lo:lo + CONV_CHUNK_ROWS, cs] * w_ref[w:w + 1, cs]
            y_ref[r0:r0 + CONV_CHUNK_ROWS, cs] = y
    o_ref[...] = _ln_silu(y_ref[...], g_ref[...], lb_ref[...]).astype(o_ref.dtype)


def _conv_prompt(glu, w_dw, b_dw, ln_g, ln_b):
    t = glu.shape[0]
    tt = CONV_ROWS
    r = tt // CONV_HALO
    vec = pl.BlockSpec((1, CONV_DIM), lambda i: (0, 0))
    return pl.pallas_call(
        _conv_prompt_kernel,
        grid=(t // tt,),
        in_specs=[
            pl.BlockSpec((tt, CONV_DIM), lambda i: (i, 0)),
            pl.BlockSpec((CONV_HALO, CONV_DIM), lambda i: (jnp.maximum(i * r - 1, 0), 0)),
            pl.BlockSpec((CONV_WIDTH, CONV_DIM), lambda i: (0, 0)),
            vec, vec, vec,
        ],
        out_specs=pl.BlockSpec((tt, CONV_DIM), lambda i: (i, 0)),
        out_shape=jax.ShapeDtypeStruct((t, CONV_DIM), BF16),
        scratch_shapes=[pltpu.VMEM((CONV_HALO + tt, CONV_DIM), F32),
                        pltpu.VMEM((SUBLANES - 1, CONV_HALO + tt - SUBLANES, CONV_DIM), F32),
                        pltpu.VMEM((tt, CONV_DIM), F32)],
        compiler_params=_params("arbitrary"),
        name="conv_prompt",
    )(glu, glu, w_dw, b_dw, ln_g, ln_b)


def _conv_sample_kernel(u_ref, buf_ref, w_ref, b_ref, g_ref, lb_ref, o_ref, nb_ref):
    hist = CONV_WIDTH - 1
    nt = u_ref.shape[0]

    def ext(j):
        return buf_ref[j] if j < hist else u_ref[j - hist]

    for t in range(nt):
        y = ext(t) * w_ref[0:1, :] + b_ref[...]
        for w in range(1, CONV_WIDTH):
            y = y + ext(t + w) * w_ref[w:w + 1, :]
        o_ref[t] = _ln_silu(y, g_ref[...], lb_ref[...]).astype(o_ref.dtype)
    for j in range(hist):
        nb_ref[j] = ext(j + nt)


def _conv_sample(glu_t, buf_t, w_dw, b_dw, ln_g, ln_b):
    nt, ns, _ = glu_t.shape
    hist = CONV_WIDTH - 1
    bs = 32
    vec = pl.BlockSpec((1, CONV_DIM), lambda i: (0, 0))
    return pl.pallas_call(
        _conv_sample_kernel,
        grid=(ns // bs,),
        in_specs=[
            pl.BlockSpec((nt, bs, CONV_DIM), lambda i: (0, i, 0)),
            pl.BlockSpec((hist, bs, CONV_DIM), lambda i: (0, i, 0)),
            pl.BlockSpec((CONV_WIDTH, CONV_DIM), lambda i: (0, 0)),
            vec, vec, vec,
        ],
        out_specs=[pl.BlockSpec((nt, bs, CONV_DIM), lambda i: (0, i, 0)),
                   pl.BlockSpec((hist, bs, CONV_DIM), lambda i: (0, i, 0))],
        out_shape=[jax.ShapeDtypeStruct((nt, ns, CONV_DIM), BF16), jax.ShapeDtypeStruct((hist, ns, CONV_DIM), F32)],
        compiler_params=_params("arbitrary"),
        name="conv_sample",
    )(glu_t, buf_t, w_dw, b_dw, ln_g, ln_b)


PAIR = 2 * B_DH
N_PAIRS = KV_WIDTH // PAIR
BLOCKS_PER_TOKEN = SEL_BLOCK // CMP_STRIDE
N_SUB = CMP_BLOCK // CMP_STRIDE
CMP_OFF = 120
CMP_ROWS = 640
KV_PAD = WINDOW
FAR_TILE = 1024
LOWEST = -3.0e38
MASK_C = 2.0 ** 100


def _half(shape, half):
    lane = lax.broadcasted_iota(jnp.int32, shape, len(shape) - 1)
    return (lane % PAIR) // B_DH == half


def _group_q(q_ref, g):
    parts = []
    for r in range(B_GROUP):
        h = g * B_GROUP + r
        x = q_ref[:, (h // 2) * PAIR:(h // 2 + 1) * PAIR]
        x = jnp.where(_half(x.shape, h % 2), x, 0.0) * (B_DH ** -0.5)
        if h % 2 != g % 2:
            x = pltpu.roll(x, B_DH, axis=1)
        parts.append(x)
    return jnp.concatenate(parts, axis=0).astype(BF16)


def _gate_rows(gate_ref, branch, g):
    sig = jax.nn.sigmoid(gate_ref[...])
    c0 = branch * B_HEADS + g * B_GROUP
    return jnp.concatenate([sig[:, c0 + r:c0 + r + 1] for r in range(B_GROUP)], axis=0)


def _scatter_heads(o, g):
    nq = o.shape[0] // B_GROUP
    outs = []
    for jp in range(2):
        acc = None
        for e in range(2):
            r = 2 * jp + e
            x = jnp.where(_half((nq, PAIR), g % 2), o[r * nq:(r + 1) * nq], 0.0)
            if e != g % 2:
                x = pltpu.roll(x, B_DH, axis=1)
            acc = x if acc is None else acc + x
        outs.append(acc)
    return jnp.concatenate(outs, axis=1)


def _nt_dot(a, b):
    return lax.dot_general(a, b, (((1,), (1,)), ((), ())), preferred_element_type=F32)


def _piece(qg, k, v, mask, bias=None):
    s = _nt_dot(qg, k)
    if bias is not None:
        s = s + bias
    s = jnp.where(mask, s, NEG)
    m = jnp.max(s, axis=-1, keepdims=True)
    p = jnp.where(mask, jnp.exp(s - m), 0.0)
    return m, jnp.sum(p, axis=-1, keepdims=True), jnp.dot(p.astype(BF16), v, preferred_element_type=F32), p


def _merge(a, b):
    m = jnp.maximum(a[0], b[0])
    ea = jnp.exp(a[0] - m)
    eb = jnp.exp(b[0] - m)
    return m, a[1] * ea + b[1] * eb, a[2] * ea + b[2] * eb


def _inv_or_zero(l):
    return jnp.where(l > 0.0, 1.0 / l, 0.0)


def _dot01(p, o01):
    hi = p.astype(BF16)
    r1 = p - hi.astype(F32)
    mid = r1.astype(BF16)
    lo = (r1 - mid.astype(F32)).astype(BF16)
    d = lambda a: jnp.dot(a, o01, preferred_element_type=F32)
    return d(hi) + d(mid) + d(lo)


def _dot01_t(o01_t, p):
    hi = p.astype(BF16)
    r1 = p - hi.astype(F32)
    mid = r1.astype(BF16)
    lo = (r1 - mid.astype(F32)).astype(BF16)
    return _nt_dot(o01_t, hi) + _nt_dot(o01_t, mid) + _nt_dot(o01_t, lo)


def _overlap01(tok, blk):
    return jnp.where((tok >= BLOCKS_PER_TOKEN * blk - (N_SUB - 1)) & (tok <= BLOCKS_PER_TOKEN * blk + BLOCKS_PER_TOKEN - 1),
                     1.0, 0.0).astype(BF16)


def _select_blocks(imp, qpos, n_cand=None, blocks_axis=1):
    j = lax.broadcasted_iota(jnp.int32, imp.shape, blocks_axis)
    cur = qpos // SEL_BLOCK
    forced = (j == 0) | (j == cur) | (j == cur - 1)
    valid = j * SEL_BLOCK <= qpos
    imp = jnp.where(forced, FORCE, jnp.where(valid, imp, -FORCE))
    if n_cand is not None:
        assert blocks_axis == 1
        ahead = jnp.zeros(imp.shape, F32)
        for i in range(n_cand):
            v = imp[:, i:i + 1]
            ahead = ahead + jnp.where((v > imp) | ((v == imp) & (j > i)), 1.0, 0.0)
        return jnp.where((ahead < N_SELECT) & (j < n_cand), 1.0, 0.0)
    sel = jnp.zeros(imp.shape, F32)
    jf = j.astype(F32)
    for _ in range(N_SELECT):
        m = jnp.max(imp, axis=blocks_axis, keepdims=True)
        first = jnp.min(jnp.where(imp == m, jf, float(imp.shape[blocks_axis])), axis=blocks_axis, keepdims=True)
        pick = jf == first
        sel = jnp.where(pick, 1.0, sel)
        imp = jnp.where(pick, LOWEST, imp)
    return sel


def _expand_blocks(sel_rows, first_blk, n_keys):
    jj = lax.broadcasted_iota(jnp.int32, (sel_rows.shape[1], n_keys), 0)
    kk = lax.broadcasted_iota(jnp.int32, (sel_rows.shape[1], n_keys), 1)
    e = jnp.where(jj == first_blk + kk // SEL_BLOCK, 1.0, 0.0).astype(BF16)
    return jnp.dot(sel_rows, e, preferred_element_type=F32)


def _bias_tables(rel_bias):
    d = np.arange(MAX_DISTANCE + 1)
    exact = N_BUCKETS // 2
    large = exact + (np.log(np.maximum(d, 1).astype(np.float32) / exact) / math.log(MAX_DISTANCE / exact)
                     * (N_BUCKETS - exact)).astype(np.int32)
    bucket = np.where(d < exact, d, np.minimum(large, N_BUCKETS - 1))
    by_dist = rel_bias.astype(F32)[bucket]

    def tile(c, nq, n, step=1):
        n1 = step * (n - 1) + 1
        dmin, dmax = c - (n1 - 1), c + nq - 1
        n_neg = max(0, min(0, dmax + 1) - dmin)
        lo, hi = max(dmin, 0), min(dmax, MAX_DISTANCE)
        n_far = max(0, dmax - max(dmin, MAX_DISTANCE + 1) + 1)
        v = jnp.concatenate([jnp.broadcast_to(by_dist[:1], (n_neg, B_HEADS)), by_dist[lo:hi + 1],
                             jnp.broadcast_to(by_dist[MAX_DISTANCE:], (n_far, B_HEADS))], axis=0)
        lv = nq + n1 - 1
        p = jnp.concatenate([v[::-1], jnp.zeros((1, B_HEADS), F32)], axis=0)
        rows = jnp.tile(p, (nq, 1))[:nq * lv].reshape(nq, lv, B_HEADS)[:, nq - 1:nq - 1 + n1:step]
        t = rows.reshape(nq, n, B_KV_HEADS, B_GROUP).transpose(2, 3, 0, 1)
        return t.reshape(B_KV_HEADS, B_GROUP * nq, n)

    nq = PAIR
    far = by_dist[MAX_DISTANCE]
    shift = jnp.repeat(far.reshape(B_KV_HEADS, B_GROUP), nq, axis=1)[:, :, None]
    n_var = 32
    m0 = PAIR - n_var
    c_cmp = -CMP_STRIDE * (m0 - CMP_OFF) - (CMP_BLOCK - 1)
    assert c_cmp + CMP_STRIDE >= MAX_DISTANCE
    prompt_cmp = jnp.concatenate([jnp.zeros((B_KV_HEADS, B_GROUP * nq, m0), F32),
                                  tile(c_cmp, nq, n_var, CMP_STRIDE) - shift], axis=2)
    causal = np.tile(np.arange(nq)[:, None] + nq - np.arange(2 * nq)[None, :] >= 0, (B_GROUP, 1))[None]
    prompt_kv = jnp.where(causal, tile(nq, nq, 2 * nq) - shift, -MASK_C)
    s_cmp = tile(PAST_LEN - (CMP_BLOCK - 1), DEC_SEQ, PAIR, CMP_STRIDE)
    s_sel = tile(PAST_LEN, DEC_SEQ, PAST_LEN + PAIR)
    s_win = tile(WINDOW, DEC_SEQ, WINDOW + PAIR)
    flat = lambda a: a.reshape(B_KV_HEADS * B_GROUP * DEC_SEQ, a.shape[-1])
    return prompt_cmp, prompt_kv, flat(s_cmp), flat(s_sel), flat(s_win)


def _cmp_weights(pe, w1, b1, w2, b2):
    k1 = CMP_STRIDE * B_DH
    w1f = w1.reshape(2, N_SUB, k1, CMP_HIDDEN).astype(BF16)
    pef = pe.reshape(2, N_SUB, 1, k1)
    b1f = b1.reshape(2, 1, CMP_HIDDEN)
    w2p = jnp.einsum('khd,ef->kehfd', w2, jnp.eye(2, dtype=F32)).reshape(2, 2, CMP_HIDDEN, PAIR).astype(BF16)
    b2p = jnp.tile(b2, (1, 2)).reshape(2, 1, PAIR)
    return pef, w1f, b1f, w2p, b2p


def _compress_pair(src, pe_ref, w1_ref, b1_ref, w2_ref, b2_ref):
    rows = [src(p) for p in range(CMP_STRIDE)]
    swapped = [pltpu.roll(r, B_DH, axis=1) for r in rows]
    low = _half(rows[0].shape, 0)
    m_rows = rows[0].shape[0]
    out = b2_ref[...]
    for e in range(2):
        x = jnp.concatenate([jnp.where(low, (rows, swapped)[e][p], (swapped, rows)[e][p + 1])
                             for p in range(0, CMP_STRIDE, 2)], axis=1)
        h = b1_ref[...]
        for m in range(N_SUB):
            part = jnp.dot((x + pe_ref[m]).astype(BF16), w1_ref[m], preferred_element_type=F32)
            h = h + (part if m == 0 else pltpu.roll(part, m_rows - m, axis=0))
        out = out + jnp.dot(_silu(h).astype(BF16), w2_ref[e], preferred_element_type=F32)
    return out


def _cmp_weight_specs(kv_of):
    k1 = CMP_STRIDE * B_DH
    return [
        pl.BlockSpec((None, N_SUB, 1, k1), lambda *a: (kv_of(*a), 0, 0, 0)),
        pl.BlockSpec((None, N_SUB, k1, CMP_HIDDEN), lambda *a: (kv_of(*a), 0, 0, 0)),
        pl.BlockSpec((None, 1, CMP_HIDDEN), lambda *a: (kv_of(*a), 0, 0)),
        pl.BlockSpec((None, 2, CMP_HIDDEN, PAIR), lambda *a: (kv_of(*a), 0, 0, 0)),
        pl.BlockSpec((None, 1, PAIR), lambda *a: (kv_of(*a), 0, 0)),
    ]


def _compress_prompt_kernel(rows_ref, pe_ref, w1_ref, b1_ref, w2_ref, b2_ref, o_ref):
    n_blk = rows_ref.shape[0] // CMP_STRIDE
    tok = _compress_pair(lambda p: rows_ref[pl.ds(p, n_blk, stride=CMP_STRIDE), :], pe_ref, w1_ref, b1_ref, w2_ref, b2_ref)
    o_ref[...] = jnp.zeros_like(o_ref)
    o_ref[CMP_OFF:CMP_OFF + n_blk, :] = tok


def _compress_prompt(u, weights):
    t = u.shape[0]
    col0 = (2 * A_QK + 2 * A_WIDTH + B_WIDTH) // PAIR
    return pl.pallas_call(
        _compress_prompt_kernel,
        grid=(2 * N_PAIRS,),
        in_specs=[pl.BlockSpec((t, PAIR), lambda c: (0, col0 + c))] + _cmp_weight_specs(lambda c: c // N_PAIRS),
        out_specs=pl.BlockSpec((None, CMP_ROWS, PAIR), lambda c: (c, 0, 0)),
        out_shape=jax.ShapeDtypeStruct((2 * N_PAIRS, CMP_ROWS, PAIR), F32),
        compiler_params=_params("arbitrary"),
        name="nsa_compress_prompt",
    )(u, *weights)


CMP_SEQ_BLOCK = 8
N_PAGES = PAST_LEN // PAGE_SIZE


def _compress_sample_kernel(pt_ref, *refs):
    n_src = CMP_SEQ_BLOCK * N_PAGES
    pages = refs[:n_src]
    pe_ref, w1_ref, b1_ref, w2_ref, b2_ref, o_ref, x_ref = refs[n_src:]
    n_blk = PAGE_SIZE // CMP_STRIDE
    for idx, pg in enumerate(pages):
        x_ref[idx] = pg[...].T
    tok = _compress_pair(
        lambda p: jnp.concatenate([x_ref[idx, pl.ds(p, n_blk, stride=CMP_STRIDE), :] for idx in range(n_src)], axis=0),
        pe_ref, w1_ref, b1_ref, w2_ref, b2_ref)
    o_ref[...] = tok.reshape(o_ref.shape)


def _compress_sample(cache, page_table, weights):
    n_seq = page_table.shape[0]
    sb = CMP_SEQ_BLOCK
    n_tok = PAST_LEN // CMP_STRIDE
    page_specs = [pl.BlockSpec((None, PAIR, PAGE_SIZE), lambda c, i, pt, s=s, j=j: (pt[i * sb + s, j], c, 0))
                  for s in range(sb) for j in range(N_PAGES)]
    return pl.pallas_call(
        _compress_sample_kernel,
        grid_spec=pltpu.PrefetchScalarGridSpec(
            num_scalar_prefetch=1,
            grid=(2 * N_PAIRS, n_seq // sb),
            in_specs=page_specs + _cmp_weight_specs(lambda c, i, pt: c // N_PAIRS),
            out_specs=pl.BlockSpec((sb, None, n_tok, PAIR), lambda c, i, pt: (i, c, 0, 0)),
            scratch_shapes=[pltpu.VMEM((sb * N_PAGES, PAGE_SIZE, PAIR), F32)],
        ),
        out_shape=jax.ShapeDtypeStruct((n_seq, 2 * N_PAIRS, n_tok, PAIR), F32),
        compiler_params=_params("arbitrary", "arbitrary"),
        name="nsa_compress_sample",
    )(page_table, *([cache] * (sb * N_PAGES)), *weights)


Q_ROWS = 128


def _nsa_cmp_prompt_kernel(q_ref, gate_ref, cmp_ref, bias_ref, oc_ref, sel_ref):
    qb = pl.program_id(0)
    nq = q_ref.shape[0]
    rows = B_GROUP * nq
    n_tok = CMP_ROWS - PAIR
    near0 = pl.multiple_of(qb * (nq // CMP_STRIDE), SUBLANES)
    tok0 = near0 - CMP_OFF
    mask_far = lax.broadcasted_iota(jnp.int32, (rows, n_tok), 1) < tok0
    i = lax.broadcasted_iota(jnp.int32, (rows, PAIR), 0) % nq
    mn = lax.broadcasted_iota(jnp.int32, (rows, PAIR), 1)
    dist = i - CMP_STRIDE * (mn - CMP_OFF) - (CMP_BLOCK - 1)
    mask_near = (dist >= 0) & (tok0 + mn >= 0)
    o_far = _overlap01(lax.broadcasted_iota(jnp.int32, (PAIR, n_tok), 1), lax.broadcasted_iota(jnp.int32, (PAIR, n_tok), 0))
    o_near = _overlap01(tok0 + lax.broadcasted_iota(jnp.int32, (PAIR, PAIR), 1), lax.broadcasted_iota(jnp.int32, (PAIR, PAIR), 0))
    qpos = qb * nq + lax.broadcasted_iota(jnp.int32, (PAIR, nq), 1)
    for g in range(B_KV_HEADS):
        qg = _group_q(q_ref, g)
        kp, vp = g // 2, N_PAIRS + g // 2
        far = _piece(qg, cmp_ref[kp, CMP_OFF:CMP_OFF + n_tok, :].astype(BF16),
                     cmp_ref[vp, CMP_OFF:CMP_OFF + n_tok, :].astype(BF16), mask_far)
        near = _piece(qg, cmp_ref[kp, pl.ds(near0, PAIR), :].astype(BF16),
                      cmp_ref[vp, pl.ds(near0, PAIR), :].astype(BF16), mask_near, bias_ref[g])
        m, l, acc = _merge(far[:3], near[:3])
        linv = _inv_or_zero(l)
        oc_ref[:, g * 2 * PAIR:(g + 1) * 2 * PAIR] = _scatter_heads(acc * linv * _gate_rows(gate_ref, 0, g), g)
        pf = jnp.sum((far[3] * (jnp.exp(far[0] - m) * linv)).reshape(B_GROUP, nq, n_tok), axis=0)
        pn = jnp.sum((near[3] * (jnp.exp(near[0] - m) * linv)).reshape(B_GROUP, nq, PAIR), axis=0)
        imp_t = _dot01_t(o_far, pf) + _dot01_t(o_near, pn)
        sel_ref[:, g * PAIR:(g + 1) * PAIR] = _select_blocks(imp_t, qpos, blocks_axis=0).T


def _flash_step_t(st, kx, qxt, vt, bias_t=None):
    m_old, l, acc = st
    s = jnp.dot(kx, qxt, preferred_element_type=F32)
    if bias_t is not None:
        s = s + bias_t
    m = jnp.maximum(m_old, jnp.max(s, axis=0, keepdims=True))
    p = jnp.exp(s - m)
    alpha = jnp.exp(m_old - m)
    return (m, alpha * l + jnp.sum(p, axis=0, keepdims=True),
            alpha * acc + jnp.dot(vt, p.astype(BF16), preferred_element_type=F32))


def _group_q_t(q_ref, g):
    parts = []
    for r in range(B_GROUP):
        h = g * B_GROUP + r
        x = q_ref[:, (h // 2) * PAIR:(h // 2 + 1) * PAIR]
        x = jnp.where(_half(x.shape, h % 2), x, 0.0) * (B_DH ** -0.5)
        if h % 2 != g % 2:
            x = pltpu.roll(x, B_DH, axis=1)
        parts.append(x.T)
    return jnp.concatenate(parts, axis=1).astype(BF16)


def _nsa_selwin_prompt_kernel(q_ref, gate_ref, sel_ref, oc_ref, k_ref, vt_ref, oh_ref, ohr_ref, bias_ref, wbias_ref, o_ref):
    qb = pl.program_id(0)
    nq = q_ref.shape[0]
    rows = B_GROUP * nq
    qs = qb * nq
    per_q = nq // SEL_BLOCK
    n_win = WINDOW + nq
    near_keys = pl.ds(pl.multiple_of(qs + KV_PAD - nq, nq), 2 * nq)
    win_keys = pl.ds(pl.multiple_of(qs, nq), n_win)
    blk = lax.broadcasted_iota(jnp.int32, (PAIR, rows), 0)
    first_near = per_q * (qb - 1)
    first_win = per_q * qb - WINDOW // SEL_BLOCK
    bb = lax.broadcasted_iota(jnp.int32, (PAIR, PAIR), 0)
    jj = lax.broadcasted_iota(jnp.int32, (PAIR, PAIR), 1)
    to_near = jnp.where(jj == first_near + bb, 1.0, 0.0).astype(BF16)
    exists_near = jnp.where(first_near + blk >= 0, 0.0, -MASK_C).astype(BF16)
    exists_win = jnp.where(first_win + blk >= 0, 0.0, -MASK_C).astype(BF16)
    per_tile = FAR_TILE // nq
    n_far = (qb + per_tile - 2) // per_tile
    init = (jnp.full((1, rows), NEG, F32), jnp.zeros((1, rows), F32), jnp.zeros((PAIR, rows), F32))
    gates_t = jax.nn.sigmoid(gate_ref[...]).T
    tile4 = lambda a: jnp.concatenate([a] * B_GROUP, axis=1)
    for g in range(B_KV_HEADS):
        qgt = _group_q_t(q_ref, g)
        pair = lambda base: slice(base + (g // 2) * PAIR, base + (g // 2 + 1) * PAIR)
        sel_t = sel_ref[:, g * PAIR:(g + 1) * PAIR].T
        far_vec = jnp.where((tile4(sel_t) > 0.5) & (blk < first_near), 0.0, -MASK_C).astype(BF16)
        near_sel = tile4(jnp.dot(to_near, sel_t.astype(BF16), preferred_element_type=F32))
        near_vec = jnp.where(near_sel > 0.5, 0.0, -MASK_C).astype(BF16) + exists_near
        qxt_far = jnp.concatenate([qgt, far_vec], axis=0)

        def far_step(t, st):
            keys = pl.ds(pl.multiple_of(KV_PAD + t * FAR_TILE, KV_PAD), FAR_TILE)
            kx = jnp.concatenate([k_ref[keys, pair(0)], oh_ref[keys, :]], axis=1)
            return _flash_step_t(st, kx, qxt_far, vt_ref[pair(0), keys])

        st = lax.fori_loop(0, n_far, far_step, init)
        st = _flash_step_t(st, jnp.concatenate([k_ref[near_keys, pair(0)], ohr_ref[:2 * nq, :]], axis=1),
                           jnp.concatenate([qgt, near_vec], axis=0), vt_ref[pair(0), near_keys], bias_ref[g])
        o_s = st[2] * _inv_or_zero(st[1])
        sw = _flash_step_t(init, jnp.concatenate([k_ref[win_keys, pair(KV_WIDTH)], ohr_ref[...]], axis=1),
                           jnp.concatenate([qgt, exists_win], axis=0), vt_ref[pair(KV_WIDTH), win_keys], wbias_ref[g])
        o_w = sw[2] * _inv_or_zero(sw[1])
        gate = lambda branch: jnp.concatenate(
            [gates_t[branch * B_HEADS + g * B_GROUP + r:branch * B_HEADS + g * B_GROUP + r + 1, :] for r in range(B_GROUP)],
            axis=1)
        o = (o_s * gate(1) + o_w * gate(2)).T
        cols = slice(g * 2 * PAIR, (g + 1) * 2 * PAIR)
        o_ref[:, cols] = _scatter_heads(o, g) + oc_ref[:, cols]


def _block_onehots(t):
    pos = np.arange(KV_PAD + t) - KV_PAD
    absolute = (pos[:, None] // SEL_BLOCK == np.arange(PAIR)[None, :]) & (pos[:, None] >= 0)
    relative = np.arange(WINDOW + Q_ROWS)[:, None] // SEL_BLOCK == np.arange(PAIR)[None, :]
    i = np.arange(B_GROUP * Q_ROWS)[:, None] % Q_ROWS
    in_window = np.arange(WINDOW - Q_ROWS)[None, :] > i
    return (jnp.asarray(absolute, BF16), jnp.asarray(relative, BF16),
            jnp.asarray(np.where(in_window, 0.0, -MASK_C), F32))


def _nsa_prompt(u, kv_pad, cmp_tok, bias_cmp, bias_kv):
    t = u.shape[0]
    nq = Q_ROWS
    q_col = (2 * A_QK + 2 * A_WIDTH) // B_WIDTH
    g_col = (IN_DIM - 3 * B_HEADS) // PAIR
    q_spec = pl.BlockSpec((nq, B_WIDTH), lambda i: (i, q_col))
    gate_spec = pl.BlockSpec((nq, PAIR), lambda i: (i, g_col))
    oc, sel = pl.pallas_call(
        _nsa_cmp_prompt_kernel,
        grid=(t // nq,),
        in_specs=[q_spec, gate_spec,
                  pl.BlockSpec(cmp_tok.shape, lambda i: (0, 0, 0)),
                  pl.BlockSpec(bias_cmp.shape, lambda i: (0, 0, 0))],
        out_specs=[pl.BlockSpec((nq, B_WIDTH), lambda i: (i, 0)), pl.BlockSpec((nq, B_KV_HEADS * PAIR), lambda i: (i, 0))],
        out_shape=[jax.ShapeDtypeStruct((t, B_WIDTH), F32), jax.ShapeDtypeStruct((t, B_KV_HEADS * PAIR), F32)],
        compiler_params=_params("arbitrary"),
        name="nsa_cmp_select_prompt",
    )(u, u, cmp_tok, bias_cmp)
    oh_abs, oh_rel, win_mask = _block_onehots(t)
    win_bias = jnp.concatenate([jnp.broadcast_to(win_mask, (B_KV_HEADS,) + win_mask.shape), bias_kv], axis=2)
    kw = KV_WIDTH
    k_pad = jnp.concatenate([kv_pad[:, :kw], kv_pad[:, 2 * kw:3 * kw]], axis=1)
    vt_pad = jnp.concatenate([kv_pad[:, kw:2 * kw], kv_pad[:, 3 * kw:]], axis=1).T
    bias_kv_t, win_bias_t = bias_kv.transpose(0, 2, 1), win_bias.transpose(0, 2, 1)
    whole = lambda a: pl.BlockSpec(a.shape, lambda i: (0,) * a.ndim, pipeline_mode=pl.Buffered(1))
    return pl.pallas_call(
        _nsa_selwin_prompt_kernel,
        grid=(t // nq,),
        in_specs=[q_spec, gate_spec,
                  pl.BlockSpec((nq, B_KV_HEADS * PAIR), lambda i: (i, 0)),
                  pl.BlockSpec((nq, B_WIDTH), lambda i: (i, 0)),
                  whole(k_pad), whole(vt_pad), whole(oh_abs), whole(oh_rel), whole(bias_kv_t), whole(win_bias_t)],
        out_specs=pl.BlockSpec((nq, B_WIDTH), lambda i: (i, 0)),
        out_shape=jax.ShapeDtypeStruct((t, B_WIDTH), F32),
        compiler_params=_params("arbitrary"),
        name="nsa_select_window_prompt",
    )(u, u, sel, oc, k_pad, vt_pad, oh_abs, oh_rel, bias_kv_t, win_bias_t)


SEL_KEYS = PAST_LEN + PAIR
WIN_KEYS = WINDOW + PAIR


NSA_SEQ_BLOCK = 2


def _piece_t(qg, kt, vt, mask, bias):
    s = jnp.dot(qg, kt, preferred_element_type=F32) + bias
    s = jnp.where(mask, s, NEG)
    m = jnp.max(s, axis=-1, keepdims=True)
    p = jnp.where(mask, jnp.exp(s - m), 0.0)
    return m, jnp.sum(p, axis=-1, keepdims=True), _nt_dot(p.astype(BF16), vt), p


def _nsa_sample_kernel(pt_ref, *refs):
    n_pg = NSA_SEQ_BLOCK * N_PAGES
    pages = refs[:n_pg]
    (q_ref, gate_ref, cmp_ref, selnew_ref, winnew_ref, win_ref, bc_ref, bs_ref, bw_ref,
     o_ref, winout_ref, kt_ref, vt_ref, wkt_ref, wvt_ref) = refs[n_pg:]
    seqs = [_nsa_sample_one(pages[s * N_PAGES:(s + 1) * N_PAGES], q_ref.at[s], gate_ref.at[s], cmp_ref.at[s],
                            selnew_ref.at[s], winnew_ref.at[s], win_ref.at[s], bc_ref, bs_ref, bw_ref,
                            o_ref.at[s], winout_ref.at[s], kt_ref.at[s], vt_ref.at[s], wkt_ref.at[s], wvt_ref.at[s])
            for s in range(NSA_SEQ_BLOCK)]
    for _ in range(NSA_SAMPLE_STAGES):
        for seq in seqs:
            next(seq)


NSA_SAMPLE_STAGES = 5


def _nsa_sample_one(pages, q_ref, gate_ref, cmp_ref, selnew_ref, winnew_ref, win_ref, bc_ref, bs_ref, bw_ref,
                    o_ref, winout_ref, kt_ref, vt_ref, wkt_ref, wvt_ref):
    nt = q_ref.shape[0]
    grp = B_GROUP * nt
    rows = B_KV_HEADS * grp
    half = 2 * PAIR
    pad = jnp.zeros((PAIR - nt, 2 * half), F32)
    sel_new_t = jnp.concatenate([selnew_ref[...], pad], axis=0).T
    win_new_t = jnp.concatenate([winnew_ref[...], pad], axis=0).T
    for j, pg in enumerate(pages):
        kt_ref[:, j * PAGE_SIZE:(j + 1) * PAGE_SIZE] = pg[:half, :].astype(BF16)
        vt_ref[:, j * PAGE_SIZE:(j + 1) * PAGE_SIZE] = pg[half:, :].astype(BF16)
    kt_ref[:, PAST_LEN:] = sel_new_t[:half].astype(BF16)
    vt_ref[:, PAST_LEN:] = sel_new_t[half:].astype(BF16)
    buf = win_ref[...]
    wkt_ref[:, :WINDOW] = buf[:half].astype(BF16)
    wvt_ref[:, :WINDOW] = buf[half:].astype(BF16)
    wkt_ref[:, WINDOW:] = win_new_t[:half].astype(BF16)
    wvt_ref[:, WINDOW:] = win_new_t[half:].astype(BF16)
    shifted = pltpu.roll(buf, WINDOW - nt, axis=1)
    tail = pltpu.roll(win_new_t, PAIR - nt, axis=1)
    lane = lax.broadcasted_iota(jnp.int32, tail.shape, 1)
    winout_ref[:, :WINDOW - PAIR] = shifted[:, :WINDOW - PAIR]
    winout_ref[:, WINDOW - PAIR:] = jnp.where(lane >= PAIR - nt, tail, shifted[:, WINDOW - PAIR:])
    yield

    zero = jnp.zeros((grp, PAIR), BF16)
    qq = jnp.concatenate(
        [jnp.concatenate([_group_q(q_ref, g), zero] if g // 2 == 0 else [zero, _group_q(q_ref, g)], axis=1)
         for g in range(B_KV_HEADS)], axis=0)
    take = lambda acc, g: acc[g * grp:(g + 1) * grp, (g // 2) * PAIR:(g // 2 + 1) * PAIR]
    qpos = PAST_LEN + lax.broadcasted_iota(jnp.int32, (rows, 1), 0) % nt

    n_tok = cmp_ref.shape[1]
    ck = jnp.concatenate([cmp_ref[0], cmp_ref[1]], axis=1).astype(BF16)
    cv = jnp.concatenate([cmp_ref[2], cmp_ref[3]], axis=1).astype(BF16)
    mask_c = lax.broadcasted_iota(jnp.int32, (rows, n_tok), 1) < n_tok - (N_SUB - 1)
    mc, lc, acc_c, pc = _piece(qq, ck, cv, mask_c, bc_ref[...])
    pc = pc * _inv_or_zero(lc)
    p_all = jnp.concatenate([jnp.sum(pc[g * grp:(g + 1) * grp].reshape(B_GROUP, nt, n_tok), axis=0)
                             for g in range(B_KV_HEADS)], axis=0)
    yield
    o01 = _overlap01(lax.broadcasted_iota(jnp.int32, (n_tok, PAIR), 0), lax.broadcasted_iota(jnp.int32, (n_tok, PAIR), 1))
    qpos_gt = PAST_LEN + lax.broadcasted_iota(jnp.int32, (B_KV_HEADS * nt, PAIR), 0) % nt
    n_blocks = pl.cdiv(PAST_LEN + nt, SEL_BLOCK)
    sel = _select_blocks(_dot01(p_all, o01), qpos_gt, n_cand=n_blocks).astype(BF16)
    sel_rows = jnp.concatenate([sel[g * nt:(g + 1) * nt] for g in range(B_KV_HEADS) for _ in range(B_GROUP)], axis=0)
    yield

    ks = lax.broadcasted_iota(jnp.int32, (rows, SEL_KEYS), 1)
    mask_s = (_expand_blocks(sel_rows, 0, SEL_KEYS) > 0.5) & (ks <= qpos)
    ms, ls, acc_s, _ = _piece_t(qq, kt_ref[...], vt_ref[...], mask_s, bs_ref[...])
    yield
    kw = PAST_LEN - WINDOW + lax.broadcasted_iota(jnp.int32, (rows, WIN_KEYS), 1)
    mask_w = (kw <= qpos) & (qpos - kw < WINDOW)
    mw, lw, acc_w, _ = _piece_t(qq, wkt_ref[...], wvt_ref[...], mask_w, bw_ref[...])
    acc_c, acc_s, acc_w = acc_c * _inv_or_zero(lc), acc_s * _inv_or_zero(ls), acc_w * _inv_or_zero(lw)
    for g in range(B_KV_HEADS):
        o = (take(acc_c, g) * _gate_rows(gate_ref, 0, g) + take(acc_s, g) * _gate_rows(gate_ref, 1, g)
             + take(acc_w, g) * _gate_rows(gate_ref, 2, g))
        o_ref[:, g * 2 * PAIR:(g + 1) * 2 * PAIR] = _scatter_heads(o, g)
    yield


def _nsa_sample(u_s, cmp_tok, cache_sel, win_buf, page_table, bias_c, bias_s, bias_w):
    n_seq, nt, _ = u_s.shape
    sb = NSA_SEQ_BLOCK
    q_col = (2 * A_QK + 2 * A_WIDTH) // B_WIDTH
    kv_col = (2 * A_QK + 2 * A_WIDTH + B_WIDTH) // (4 * PAIR)
    g_col = (IN_DIM - 3 * B_HEADS) // PAIR
    const = lambda a: pl.BlockSpec(a.shape, lambda b, pt: (0, 0))
    in_specs = [pl.BlockSpec((None, 4 * PAIR, PAGE_SIZE), lambda b, pt, s=s, j=j: (pt[b * sb + s, j], 0, 0))
                for s in range(sb) for j in range(N_PAGES)]
    in_specs += [
        pl.BlockSpec((sb, nt, B_WIDTH), lambda b, pt: (b, 0, q_col)),
        pl.BlockSpec((sb, nt, PAIR), lambda b, pt: (b, 0, g_col)),
        pl.BlockSpec((sb,) + cmp_tok.shape[1:], lambda b, pt: (b, 0, 0, 0)),
        pl.BlockSpec((sb, nt, 4 * PAIR), lambda b, pt: (b, 0, kv_col + 1)),
        pl.BlockSpec((sb, nt, 4 * PAIR), lambda b, pt: (b, 0, kv_col + 2)),
        pl.BlockSpec((sb, 4 * PAIR, WINDOW), lambda b, pt: (b, 0, 0)),
        const(bias_c), const(bias_s), const(bias_w),
    ]
    return pl.pallas_call(
        _nsa_sample_kernel,
        grid_spec=pltpu.PrefetchScalarGridSpec(
            num_scalar_prefetch=1,
            grid=(n_seq // sb,),
            in_specs=in_specs,
            out_specs=[pl.BlockSpec((sb, nt, B_WIDTH), lambda b, pt: (b, 0, 0)),
                       pl.BlockSpec((sb, 4 * PAIR, WINDOW), lambda b, pt: (b, 0, 0))],
            scratch_shapes=[pltpu.VMEM((sb, 2 * PAIR, SEL_KEYS), BF16), pltpu.VMEM((sb, 2 * PAIR, SEL_KEYS), BF16),
                            pltpu.VMEM((sb, 2 * PAIR, WIN_KEYS), BF16), pltpu.VMEM((sb, 2 * PAIR, WIN_KEYS), BF16)],
        ),
        out_shape=[jax.ShapeDtypeStruct((n_seq, nt, B_WIDTH), F32), jax.ShapeDtypeStruct(win_buf.shape, F32)],
        compiler_params=_params("arbitrary"),
        name="nsa_sample",
    )(page_table, *([cache_sel] * (sb * N_PAGES)), u_s, u_s, cmp_tok, u_s, u_s, win_buf, bias_c, bias_s, bias_w)


PROMPT_TM = 1024
PROJ_TN = 512
COL_CMP = 2 * A_QK + 2 * A_WIDTH + B_WIDTH
COL_SEL = COL_CMP + 2 * KV_WIDTH
COL_WIN = COL_SEL + 2 * KV_WIDTH
COL_GATE = COL_WIN + 2 * KV_WIDTH


def _time_major(a):
    return a.transpose(1, 0, 2).reshape(a.shape[0] * a.shape[1], a.shape[2])


def _seq_major(a2d, n_seq):
    return a2d.reshape(a2d.shape[0] // n_seq, n_seq, a2d.shape[1]).transpose(1, 0, 2)


def _kv_rows(u3, col):
    return u3[..., col:col + 2 * KV_WIDTH].reshape(u3.shape[:-1] + (2, B_KV_HEADS, B_DH))


def kernel(x_prompt, x_sample, cache_cmp, cache_sel, state_win, state_hgrn, state_conv, page_table, c_prompt, c_sample, norm_g, ada_w, ada_b, ffn_w_gate, ffn_w_up, ffn_w_down, w_in_even, hgrn_lower_bound, hgrn_norm_g, cmp_pe, cmp_w1, cmp_b1, cmp_w2, cmp_b2, rel_bias, w_out_even, conv_w_pw1, conv_b_pw1, conv_w_dw, conv_b_dw, conv_ln_g, conv_ln_b, conv_w_pw2, conv_b_pw2, final_norm_g):
    P = {'norm_g': norm_g, 'ffn_w_gate': ffn_w_gate, 'ffn_w_up': ffn_w_up, 'ffn_w_down': ffn_w_down,
         'final_norm_g': final_norm_g}
    n_seq = x_sample.shape[0]
    n_pool = cache_cmp.shape[1]
    rows_p, rows_s = _prompt_rows(PROMPT_TM), _sample_rows()
    g4 = norm_g.reshape(DEPTH, 3, 1, D_MODEL)
    c_all = jnp.concatenate([c_sample, jnp.tile(c_prompt, (SUBLANES, 1))], axis=0)
    mod = _ada_mod(c_all, ada_w, ada_b)
    bias_pc, bias_pkv, bias_sc, bias_ss, bias_sw = _bias_tables(rel_bias)

    xp = x_prompt.reshape(SEQ, D_MODEL)
    xs = _time_major(x_sample)
    cmp_p, cmp_s, sel_p, sel_s, win_p, win_s, hgrn_p, hgrn_s, conv_p, conv_s = ([] for _ in range(10))
    for l in range(DEPTH):
        i = l // 2
        last = l == DEPTH - 1
        xs, w_bf16 = _ffn(xs, rows_s, mod, l, 0, P)
        xp, _ = _ffn(xp, rows_p, mod, l, 0, P, w_bf16=w_bf16)
        if l % 2 == 0:
            w_in_t = w_in_even[i:i + 1].transpose(0, 2, 1)
            up = _proj(xp, rows_p, mod, l, g4, w_in_t, IN_DIM, PROJ_TN)
            us = _seq_major(_proj(xs, rows_s, mod, l, g4, w_in_t, IN_DIM, PROJ_TN), n_seq)
            gn = hgrn_norm_g[i:i + 1]
            oa_p, hp = _hgrn(up, hgrn_lower_bound, gn, l)
            oa_s, hs = _hgrn(us.reshape(n_seq * DEC_SEQ, IN_DIM), hgrn_lower_bound, gn, l, state_hgrn[i])
            weights = _cmp_weights(cmp_pe[i], cmp_w1[i], cmp_b1[i], cmp_w2[i], cmp_b2[i])
            kv_pad = jnp.pad(up[:, COL_SEL:COL_GATE].astype(BF16), ((KV_PAD, 0), (0, 0)))
            ob_p = _nsa_prompt(up, kv_pad, _compress_prompt(up, weights), bias_pc, bias_pkv)
            cmp_t = cache_cmp[i].transpose(0, 2, 3, 4, 1).reshape(n_pool, 4 * PAIR, PAGE_SIZE)
            cmp_tok_s = _compress_sample(cmp_t, page_table, weights)
            sel_t = cache_sel[i].transpose(0, 2, 3, 4, 1).reshape(n_pool, 4 * PAIR, PAGE_SIZE)
            win_t = state_win[i].transpose(0, 2, 3, 4, 1).reshape(n_seq, 4 * PAIR, WINDOW)
            ob_s, wn = _nsa_sample(us, cmp_tok_s, sel_t, win_t, page_table, bias_sc, bias_ss, bias_sw)
            xp = _out_proj([oa_p, ob_p], w_out_even[i:i + 1], [0, A_WIDTH], None, xp, rows_p, mod, l, PROJ_TN)
            xs = _out_proj([_time_major(oa_s.reshape(n_seq, DEC_SEQ, A_WIDTH)), _time_major(ob_s)],
                           w_out_even[i:i + 1], [0, A_WIDTH], None, xs, rows_s, mod, l, PROJ_TN)
            up3 = up[None]
            cmp_p.append(_kv_rows(up3, COL_CMP))
            sel_p.append(_kv_rows(up3, COL_SEL))
            win_p.append(_kv_rows(up3[:, SEQ - min(WINDOW, SEQ):], COL_WIN))
            cmp_s.append(_kv_rows(us, COL_CMP))
            sel_s.append(_kv_rows(us, COL_SEL))
            win_s.append(wn.reshape(n_seq, 2, B_KV_HEADS, B_DH, WINDOW).transpose(0, 4, 1, 2, 3))
            hgrn_p.append(hp[None])
            hgrn_s.append(hs)
        else:
            b_pw1 = conv_b_pw1[i].reshape(1, 1, 2 * CONV_DIM)
            b_pw2 = conv_b_pw2[i].reshape(1, 1, D_MODEL)
            vec = lambda a: a[i].reshape(1, CONV_DIM)
            glu_p = _glu_proj(xp, rows_p, mod, l, g4, conv_w_pw1[i:i + 1], b_pw1, PROJ_TN)
            act_p = _conv_prompt(glu_p, conv_w_dw[i], vec(conv_b_dw), vec(conv_ln_g), vec(conv_ln_b))
            xp = _out_proj([act_p], conv_w_pw2[i:i + 1], [0], b_pw2, xp, rows_p, mod, l, PROJ_TN)
            glu_s = _glu_proj(xs, rows_s, mod, l, g4, conv_w_pw1[i:i + 1], b_pw1, PROJ_TN)
            act_s, nb = _conv_sample(glu_s.reshape(DEC_SEQ, n_seq, CONV_DIM), state_conv[i].transpose(1, 0, 2),
                                     conv_w_dw[i], vec(conv_b_dw), vec(conv_ln_g), vec(conv_ln_b))
            xs = _out_proj([act_s.reshape(DEC_SEQ * n_seq, CONV_DIM)], conv_w_pw2[i:i + 1], [0], b_pw2, xs, rows_s, mod, l, PROJ_TN)
            conv_p.append(glu_p[None, SEQ - (CONV_WIDTH - 1):])
            conv_s.append(nb.transpose(1, 0, 2))
        xs, w_bf16 = _ffn(xs, rows_s, mod, l, 2, P, final_norm=last)
        xp, _ = _ffn(xp, rows_p, mod, l, 2, P, w_bf16=w_bf16, final_norm=last)
    y_prompt = xp.reshape(1, SEQ, D_MODEL)
    y_sample = _seq_major(xs, n_seq)
    st = jnp.stack
    return (y_prompt, y_sample, st(cmp_p), st(cmp_s), st(sel_p), st(sel_s), st(win_p), st(win_s),
            st(hgrn_p), st(hgrn_s), st(conv_p), st(conv_s))
```

```python
import functools
import math

import numpy as np
import jax
import jax.numpy as jnp
from jax import lax
from jax.experimental import pallas as pl
from jax.experimental.pallas import tpu as pltpu

F32 = jnp.float32
BF16 = jnp.bfloat16

D_MODEL = 2048
SEQ = 8192
DEPTH = 2
DEC_BATCH = 128
DEC_SEQ = 8
PAST_LEN = 2048
PAGE_SIZE = 128
N_MOD = 9
D_FF = 5504
EPS = 1e-6
A_HEADS = 8
A_DK = 128
A_DV = 128
A_QK = A_HEADS * A_DK
A_WIDTH = A_HEADS * A_DV
B_HEADS = 16
B_KV_HEADS = 4
B_DH = 64
B_GROUP = B_HEADS // B_KV_HEADS
B_WIDTH = B_HEADS * B_DH
KV_WIDTH = B_KV_HEADS * B_DH
CMP_BLOCK = 32
CMP_STRIDE = 16
CMP_HIDDEN = 256
SEL_BLOCK = 64
N_SELECT = 16
WINDOW = 512
N_BUCKETS = 32
MAX_DISTANCE = 128
MIX_WIDTH = A_WIDTH + B_WIDTH
IN_DIM = 2 * A_QK + 2 * A_WIDTH + B_WIDTH + 6 * KV_WIDTH + 3 * B_HEADS
CONV_WIDTH = 31
CONV_DIM = D_MODEL
NEG = -1e30
FORCE = 1e9

V7X_VMEM_LIMIT_BYTES = 60 * 1024 * 1024
SUBLANES = 8
LANES = 128

N_SEQ_ROWS = DEC_BATCH + SUBLANES
PROMPT_ROW_BLOCK = DEC_BATCH // SUBLANES


def _params(*sem):
    return pltpu.CompilerParams(dimension_semantics=sem, vmem_limit_bytes=V7X_VMEM_LIMIT_BYTES)


def _silu(x):
    return x * jax.nn.sigmoid(x)


def _bdot(a, b):
    return jnp.dot(a.astype(BF16), b.astype(BF16), preferred_element_type=F32)


def _ada_kernel(c_ref, w_ref, b_ref, o_ref):
    o_ref[...] = _bdot(_silu(c_ref[...]), w_ref[...]) + b_ref[...]


def _ada_mod(c_all, ada_w, ada_b):
    n = c_all.shape[0]
    return pl.pallas_call(
        _ada_kernel,
        grid=(DEPTH, N_MOD),
        in_specs=[
            pl.BlockSpec((n, D_MODEL), lambda l, k: (0, 0)),
            pl.BlockSpec((None, D_MODEL, D_MODEL), lambda l, k: (l, 0, k)),
            pl.BlockSpec((None, None, 1, D_MODEL), lambda l, k: (l, k, 0, 0)),
        ],
        out_specs=pl.BlockSpec((None, None, n, D_MODEL), lambda l, k: (l, k, 0, 0)),
        out_shape=jax.ShapeDtypeStruct((DEPTH, N_MOD, n, D_MODEL), F32),
        compiler_params=_params("arbitrary", "arbitrary"),
        name="ada_mod",
    )(c_all, ada_w, ada_b.reshape(DEPTH, N_MOD, 1, D_MODEL))


def _norm_mod(x, g, shift, scale):
    ms = jnp.mean(x * x, axis=-1, keepdims=True)
    y = x * lax.rsqrt(ms + EPS) * g
    h = y * (1.0 + scale) + shift
    return h.reshape(x.shape[0] * x.shape[1], x.shape[2]).astype(BF16)


class _Rows:
    def __init__(self, m, bs, nt, seq_block):
        assert m % (bs * nt) == 0
        self.m, self.bs, self.nt, self.seq_block = m, bs, nt, seq_block
        self.tm = bs * nt
        self.n_tiles = m // self.tm

    def view(self, x2d):
        return x2d.reshape(self.m // self.bs, self.bs, x2d.shape[-1])

    def x_spec(self, width, col=lambda j: 0, single_buffer=False):
        mode = dict(pipeline_mode=pl.Buffered(1)) if single_buffer else {}
        return pl.BlockSpec((self.nt, self.bs, width), lambda i, j: (i, 0, col(j)), **mode)

    def mod_spec(self, layer, k, width=D_MODEL, col=lambda j: 0):
        sb = self.seq_block
        return pl.BlockSpec((None, None, self.bs, width), lambda i, j: (layer, k, sb, col(j)))


def _prompt_rows(tm):
    return _Rows(SEQ, SUBLANES, tm // SUBLANES, PROMPT_ROW_BLOCK)


def _sample_rows():
    return _Rows(DEC_BATCH * DEC_SEQ, DEC_BATCH, DEC_SEQ, 0)


FFN_TF_F32 = 256
FFN_TF_BF16 = 512
FFN_ACC_CHUNKS = 4


def _ffn_kernel(x_ref, sh_ref, sc_ref, gt_ref, g_ref, wg_ref, wu_ref, wd_ref, fg_ref, o_ref, *rest, final_norm, emit):
    if emit:
        wg_o, wu_o, wd_o, h_ref = rest
    else:
        (h_ref,) = rest
    j = pl.program_id(1)
    nj = pl.num_programs(1)
    tf = wg_ref.shape[1]

    nt, bs = o_ref.shape[0], o_ref.shape[1]
    step = max(nt // FFN_ACC_CHUNKS, 1)

    @pl.when(j == 0)
    def _():
        for r in range(0, nt, step):
            h_ref[r * bs:(r + step) * bs, :] = _norm_mod(x_ref[r:r + step], g_ref[...], sh_ref[...], sc_ref[...])
        o_ref[...] = jnp.zeros_like(o_ref)

    wg, wu, wd = wg_ref[...].astype(BF16), wu_ref[...].astype(BF16), wd_ref[...].astype(BF16)
    if emit:
        wg_o[...], wu_o[...], wd_o[...] = wg, wu, wd
    valid = D_FF - j * tf
    h = h_ref[...]
    a = _silu(_bdot(h, wg)) * _bdot(h, wu)
    col = lax.broadcasted_iota(jnp.int32, a.shape, 1)
    a = jnp.where(col < valid, a, 0.0)
    row = lax.broadcasted_iota(jnp.int32, wd.shape, 0)
    wd = jnp.where(row < valid, wd, jnp.zeros_like(wd))
    a = a.astype(BF16)
    for r in range(0, nt, step):
        o_ref[r:r + step] += _bdot(a[r * bs:(r + step) * bs], wd).reshape(step, bs, o_ref.shape[2])

    @pl.when(j == nj - 1)
    def _():
        for r in range(0, nt, step):
            y = x_ref[r:r + step] + (0.5 * gt_ref[...]) * o_ref[r:r + step]
            if final_norm:
                ms = jnp.mean(y * y, axis=-1, keepdims=True)
                y = y * lax.rsqrt(ms + EPS) * fg_ref[...]
            o_ref[r:r + step] = y


def _ffn(x2d, rows, mod, layer, sub, P, w_bf16=None, final_norm=False):
    half = sub // 2
    emit = w_bf16 is None
    tf = FFN_TF_F32 if emit else FFN_TF_BF16
    g_norm = P['norm_g'].reshape(DEPTH, 3, 1, D_MODEL)
    fg = P['final_norm_g'].reshape(1, D_MODEL)
    if emit:
        weights = (P['ffn_w_gate'], P['ffn_w_up'], P['ffn_w_down'])
        w_specs = [pl.BlockSpec((None, None, D_MODEL, tf), lambda i, j: (layer, half, 0, j)),
                   pl.BlockSpec((None, None, D_MODEL, tf), lambda i, j: (layer, half, 0, j)),
                   pl.BlockSpec((None, None, tf, D_MODEL), lambda i, j: (layer, half, j, 0))]
    else:
        weights = w_bf16
        w_specs = [pl.BlockSpec((D_MODEL, tf), lambda i, j: (0, j)),
                   pl.BlockSpec((D_MODEL, tf), lambda i, j: (0, j)),
                   pl.BlockSpec((tf, D_MODEL), lambda i, j: (j, 0))]
    out_specs = [rows.x_spec(D_MODEL, single_buffer=emit)]
    out_shape = [jax.ShapeDtypeStruct((rows.m // rows.bs, rows.bs, D_MODEL), F32)]
    if emit:
        out_specs += [pl.BlockSpec((D_MODEL, tf), lambda i, j: (0, j)),
                      pl.BlockSpec((D_MODEL, tf), lambda i, j: (0, j)),
                      pl.BlockSpec((tf, D_MODEL), lambda i, j: (j, 0))]
        out_shape += [jax.ShapeDtypeStruct((D_MODEL, D_FF), BF16), jax.ShapeDtypeStruct((D_MODEL, D_FF), BF16),
                      jax.ShapeDtypeStruct((D_FF, D_MODEL), BF16)]
        assert rows.n_tiles == 1
    outs = pl.pallas_call(
        functools.partial(_ffn_kernel, final_norm=final_norm, emit=emit),
        grid=(rows.n_tiles, pl.cdiv(D_FF, tf)),
        in_specs=[
            rows.x_spec(D_MODEL, single_buffer=True),
            rows.mod_spec(layer, 3 * sub), rows.mod_spec(layer, 3 * sub + 1), rows.mod_spec(layer, 3 * sub + 2),
            pl.BlockSpec((None, None, 1, D_MODEL), lambda i, j: (layer, sub, 0, 0)),
            *w_specs,
            pl.BlockSpec((1, D_MODEL), lambda i, j: (0, 0)),
        ],
        out_specs=out_specs,
        out_shape=out_shape,
        scratch_shapes=[pltpu.VMEM((rows.tm, D_MODEL), BF16)],
        compiler_params=_params("arbitrary", "arbitrary"),
        name="ffn_half_step",
    )(rows.view(x2d), mod, mod, mod, g_norm, *weights, fg)
    return outs[0].reshape(rows.m, D_MODEL), tuple(outs[1:])


def _proj_kernel(x_ref, sh_ref, sc_ref, g_ref, w_ref, o_ref, h_ref):
    @pl.when(pl.program_id(1) == 0)
    def _():
        h_ref[...] = _norm_mod(x_ref[...], g_ref[...], sh_ref[...], sc_ref[...])

    o_ref[...] = _nt_dot(h_ref[...], w_ref[...].astype(BF16))


def _proj(x2d, rows, mod, layer, g_norm4, w3t, n_out, tn):
    return pl.pallas_call(
        _proj_kernel,
        grid=(rows.n_tiles, pl.cdiv(n_out, tn)),
        in_specs=[
            rows.x_spec(D_MODEL),
            rows.mod_spec(layer, 3), rows.mod_spec(layer, 4),
            pl.BlockSpec((None, None, 1, D_MODEL), lambda i, j: (layer, 1, 0, 0)),
            pl.BlockSpec((None, tn, D_MODEL), lambda i, j: (0, j, 0)),
        ],
        out_specs=pl.BlockSpec((rows.tm, tn), lambda i, j: (i, j)),
        out_shape=jax.ShapeDtypeStruct((rows.m, n_out), F32),
        scratch_shapes=[pltpu.VMEM((rows.tm, D_MODEL), BF16)],
        compiler_params=_params("arbitrary", "arbitrary"),
        name="prenorm_proj",
    )(rows.view(x2d), mod, mod, g_norm4, w3t)


def _glu_proj_kernel(x_ref, sh_ref, sc_ref, g_ref, wa_ref, wg_ref, ba_ref, bg_ref, o_ref, h_ref):
    @pl.when(pl.program_id(1) == 0)
    def _():
        h_ref[...] = _norm_mod(x_ref[...], g_ref[...], sh_ref[...], sc_ref[...])

    h = h_ref[...]
    a = _bdot(h, wa_ref[...]) + ba_ref[...]
    gt = _bdot(h, wg_ref[...]) + bg_ref[...]
    o_ref[...] = a * jax.nn.sigmoid(gt)


def _glu_proj(x2d, rows, mod, layer, g_norm4, w3, b3, tn):
    nb = CONV_DIM // tn
    return pl.pallas_call(
        _glu_proj_kernel,
        grid=(rows.n_tiles, nb),
        in_specs=[
            rows.x_spec(D_MODEL),
            rows.mod_spec(layer, 3), rows.mod_spec(layer, 4),
            pl.BlockSpec((None, None, 1, D_MODEL), lambda i, j: (layer, 1, 0, 0)),
            pl.BlockSpec((None, D_MODEL, tn), lambda i, j: (0, 0, j)),
            pl.BlockSpec((None, D_MODEL, tn), lambda i, j: (0, 0, j + nb)),
            pl.BlockSpec((None, 1, tn), lambda i, j: (0, 0, j)),
            pl.BlockSpec((None, 1, tn), lambda i, j: (0, 0, j + nb)),
        ],
        out_specs=pl.BlockSpec((rows.tm, tn), lambda i, j: (i, j)),
        out_shape=jax.ShapeDtypeStruct((rows.m, CONV_DIM), F32),
        scratch_shapes=[pltpu.VMEM((rows.tm, D_MODEL), BF16)],
        compiler_params=_params("arbitrary", "arbitrary"),
        name="prenorm_glu_proj",
    )(rows.view(x2d), mod, mod, g_norm4, w3, w3, b3, b3)


def _out_kernel(*refs, n_in, has_bias):
    a_refs = refs[:n_in]
    w_refs = refs[n_in:2 * n_in]
    pos = 2 * n_in
    b_ref = refs[pos] if has_bias else None
    pos += int(has_bias)
    x_ref, gt_ref, o_ref = refs[pos:pos + 3]
    y = _bdot(a_refs[0][...], w_refs[0][...])
    for a_ref, w_ref in zip(a_refs[1:], w_refs[1:]):
        y += _bdot(a_ref[...], w_ref[...])
    if has_bias:
        y += b_ref[...]
    o_ref[...] = x_ref[...] + gt_ref[...] * y.reshape(o_ref.shape)


def _out_proj(acts, w3, k_offsets, bias3, x2d, rows, mod, layer, tn):
    n_in = len(acts)
    in_specs = [pl.BlockSpec((rows.tm, a.shape[1]), lambda i, j: (i, 0)) for a in acts]
    for a, off in zip(acts, k_offsets):
        kb = off // a.shape[1]
        in_specs.append(pl.BlockSpec((None, a.shape[1], tn), lambda i, j, kb=kb: (0, kb, j)))
    args = list(acts) + [w3] * n_in
    if bias3 is not None:
        in_specs.append(pl.BlockSpec((None, 1, tn), lambda i, j: (0, 0, j)))
        args.append(bias3)
    in_specs += [rows.x_spec(tn, col=lambda j: j), rows.mod_spec(layer, 5, width=tn, col=lambda j: j)]
    args += [rows.view(x2d), mod]
    out = pl.pallas_call(
        functools.partial(_out_kernel, n_in=n_in, has_bias=bias3 is not None),
        grid=(rows.n_tiles, D_MODEL // tn),
        in_specs=in_specs,
        out_specs=rows.x_spec(tn, col=lambda j: j),
        out_shape=jax.ShapeDtypeStruct((rows.m // rows.bs, rows.bs, D_MODEL), F32),
        compiler_params=_params("arbitrary", "arbitrary"),
        name="out_proj_residual",
    )(*args)
    return out.reshape(rows.m, D_MODEL)


HGRN_ROWS = 512
HGRN_HEADS = 8
HGRN_TRI = 128


def _hgrn_prepare(f_ref, lb_ref, cum_s, kk_s, *, layer, c):
    p = lb_ref[...]
    e = jnp.exp(p - jnp.max(p, axis=0, keepdims=True))
    sm = e / jnp.sum(e, axis=0, keepdims=True)
    lb = jnp.sum(sm[:layer + 1], axis=0, keepdims=True)
    f = lb + (1.0 - lb) * jax.nn.sigmoid(f_ref[...])
    lf = jnp.log(f)
    n = HGRN_TRI
    r = lax.broadcasted_iota(jnp.int32, (n, n), 0)
    s = lax.broadcasted_iota(jnp.int32, (n, n), 1)
    tri = jnp.where((s <= r) & (s // c == r // c), 1.0, 0.0).astype(F32)
    for b in range(f.shape[0] // n):
        cum_s[b * n:(b + 1) * n, :] = jnp.dot(tri, lf[b * n:(b + 1) * n], preferred_element_type=F32,
                                              precision=lax.Precision.HIGHEST)
    kk_s[...] = 1.0 - f


def _hgrn_subchunk(r0, c, st, hh, q_ref, v_ref, g_ref, gn_ref, cum_s, kk_s):
    rows = pl.ds(r0, c)
    cols = slice(hh * A_DK, (hh + 1) * A_DK)
    cum = cum_s[rows, cols]
    q = q_ref[rows, cols]
    kk = kk_s[rows, cols]
    vv = v_ref[rows, cols]
    last = cum[c - 1:c, :]
    o = lax.dot_general((q * jnp.exp(cum)).astype(BF16), st.astype(BF16), (((1,), (1,)), ((), ())),
                        preferred_element_type=F32)
    srow = lax.broadcasted_iota(jnp.int32, (c, A_DK), 0)
    xs = []
    for t in range(c):
        d = jnp.where(srow <= t, cum[t:t + 1, :] - cum, NEG)
        xs.append(jnp.exp(d) * (q[t:t + 1, :] * kk))
    x = jnp.concatenate(xs, axis=0).astype(BF16)
    w = jnp.dot(x, jnp.ones((A_DK, A_DV), BF16), preferred_element_type=F32)
    o = o + jnp.sum(w.reshape(c, c, A_DV) * vv[None], axis=1)
    ke = kk * jnp.exp(last - cum)
    st_new = st * jnp.exp(last) + lax.dot_general(vv.astype(BF16), ke.astype(BF16), (((0,), (0,)), ((), ())),
                                                  preferred_element_type=F32)
    ms = jnp.mean(o * o, axis=-1, keepdims=True)
    y = o * lax.rsqrt(ms + EPS) * gn_ref[:, cols] * _silu(g_ref[rows, cols])
    return y, st_new


def _hgrn_prompt_kernel(q_ref, f_ref, v_ref, g_ref, lb_ref, gn_ref, o_ref, s_ref, st_ref, cum_s, kk_s, *, layer, c):
    i = pl.program_id(1)

    @pl.when(i == 0)
    def _():
        st_ref[...] = jnp.zeros_like(st_ref)

    _hgrn_prepare(f_ref, lb_ref, cum_s, kk_s, layer=layer, c=c)

    def body(n, carry):
        r0 = pl.multiple_of(n * c, c)
        for hh in range(HGRN_HEADS):
            y, st_new = _hgrn_subchunk(r0, c, st_ref[hh], hh, q_ref, v_ref, g_ref, gn_ref, cum_s, kk_s)
            st_ref[hh] = st_new
            o_ref[pl.ds(r0, c), hh * A_DV:(hh + 1) * A_DV] = y
        return carry

    lax.fori_loop(0, q_ref.shape[0] // c, body, 0)

    @pl.when(i == pl.num_programs(1) - 1)
    def _():
        for hh in range(HGRN_HEADS):
            s_ref[hh] = st_ref[hh].T


def _hgrn_sample_kernel(q_ref, f_ref, v_ref, g_ref, lb_ref, gn_ref, s0_ref, o_ref, s_ref, cum_s, kk_s, *, layer, c):
    _hgrn_prepare(f_ref, lb_ref, cum_s, kk_s, layer=layer, c=c)

    def body(n, carry):
        r0 = pl.multiple_of(n * c, c)
        for hh in range(HGRN_HEADS):
            y, st_new = _hgrn_subchunk(r0, c, s0_ref[n, hh].T, hh, q_ref, v_ref, g_ref, gn_ref, cum_s, kk_s)
            s_ref[n, hh] = st_new.T
            o_ref[pl.ds(r0, c), hh * A_DV:(hh + 1) * A_DV] = y
        return carry

    lax.fori_loop(0, q_ref.shape[0] // c, body, 0)


def _hgrn(u, lower_bound, norm_g2, layer, s0=None):
    m = u.shape[0]
    hb = HGRN_HEADS
    w = hb * A_DK
    nb = A_QK // w
    tc = HGRN_ROWS if s0 is None else HGRN_ROWS // 4

    def col(seg):
        return pl.BlockSpec((tc, w), lambda h, i, seg=seg: (i, seg * nb + h))

    in_specs = [col(0), col(1), col(2), col(3),
                pl.BlockSpec((DEPTH + 1, w), lambda h, i: (0, h)),
                pl.BlockSpec((1, w), lambda h, i: (0, h))]
    o_spec = pl.BlockSpec((tc, w), lambda h, i: (i, h))
    scratch = [pltpu.VMEM((tc, w), F32), pltpu.VMEM((tc, w), F32)]
    if s0 is None:
        return pl.pallas_call(
            functools.partial(_hgrn_prompt_kernel, layer=layer, c=16),
            grid=(nb, m // tc),
            in_specs=in_specs,
            out_specs=[o_spec, pl.BlockSpec((hb, A_DK, A_DV), lambda h, i: (h, 0, 0))],
            out_shape=[jax.ShapeDtypeStruct((m, A_WIDTH), F32), jax.ShapeDtypeStruct((A_HEADS, A_DK, A_DV), F32)],
            scratch_shapes=[pltpu.VMEM((hb, A_DV, A_DK), F32)] + scratch,
            compiler_params=_params("arbitrary", "arbitrary"),
            name="hgrn2_prompt",
        )(u, u, u, u, lower_bound, norm_g2)
    c = DEC_SEQ
    ns = tc // c
    s_spec = pl.BlockSpec((ns, hb, A_DK, A_DV), lambda h, i: (i, h, 0, 0))
    return pl.pallas_call(
        functools.partial(_hgrn_sample_kernel, layer=layer, c=c),
        grid=(nb, m // tc),
        in_specs=in_specs + [s_spec],
        out_specs=[o_spec, s_spec],
        out_shape=[jax.ShapeDtypeStruct((m, A_WIDTH), F32), jax.ShapeDtypeStruct(s0.shape, F32)],
        scratch_shapes=scratch,
        compiler_params=_params("arbitrary", "arbitrary"),
        name="hgrn2_sample",
    )(u, u, u, u, lower_bound, norm_g2, s0)


CONV_HALO = 32
CONV_ROWS = 256


def _ln_silu(y, g, b):
    mu = jnp.mean(y, axis=-1, keepdims=True)
    yc = y - mu
    var = jnp.mean(yc * yc, axis=-1, keepdims=True)
    return _silu(yc * lax.rsqrt(var + EPS) * g + b)


CONV_CHUNK_ROWS = 64
CONV_CHUNK_COLS = 512


def _conv_prompt_kernel(cur_ref, halo_ref, w_ref, b_ref, g_ref, lb_ref, o_ref, ext_ref, ph_ref, y_ref):
    i = pl.program_id(0)
    tt = cur_ref.shape[0]
    ext_ref[0:CONV_HALO, :] = jnp.where(i > 0, halo_ref[...], 0.0)
    ext_ref[CONV_HALO:, :] = cur_ref[...]
    off = CONV_HALO - (CONV_WIDTH - 1)
    span = ph_ref.shape[1]
    for r in range(1, SUBLANES):
        ph_ref[r - 1] = ext_ref[r:r + span, :]
    for c0 in range(0, CONV_DIM, CONV_CHUNK_COLS):
        cs = slice(c0, c0 + CONV_CHUNK_COLS)
        for r0 in range(0, tt, CONV_CHUNK_ROWS):
            y = jnp.zeros((CONV_CHUNK_ROWS, CONV_CHUNK_COLS), F32) + b_ref[:, cs]
            for w in range(CONV_WIDTH):
                a, r = divmod(off + w, SUBLANES)
                src = ext_ref if r == 0 else ph_ref.at[r - 1]
                lo = SUBLANES * a + r0
                y = y + src[lo:lo + CONV_CHUNK_ROWS, cs] * w_ref[w:w + 1, cs]
            y_ref[r0:r0 + CONV_CHUNK_ROWS, cs] = y
    o_ref[...] = _ln_silu(y_ref[...], g_ref[...], lb_ref[...]).astype(o_ref.dtype)


def _conv_prompt(glu, w_dw, b_dw, ln_g, ln_b):
    t = glu.shape[0]
    tt = CONV_ROWS
    r = tt // CONV_HALO
    vec = pl.BlockSpec((1, CONV_DIM), lambda i: (0, 0))
    return pl.pallas_call(
        _conv_prompt_kernel,
        grid=(t // tt,),
        in_specs=[
            pl.BlockSpec((tt, CONV_DIM), lambda i: (i, 0)),
            pl.BlockSpec((CONV_HALO, CONV_DIM), lambda i: (jnp.maximum(i * r - 1, 0), 0)),
            pl.BlockSpec((CONV_WIDTH, CONV_DIM), lambda i: (0, 0)),
            vec, vec, vec,
        ],
        out_specs=pl.BlockSpec((tt, CONV_DIM), lambda i: (i, 0)),
        out_shape=jax.ShapeDtypeStruct((t, CONV_DIM), BF16),
        scratch_shapes=[pltpu.VMEM((CONV_HALO + tt, CONV_DIM), F32),
                        pltpu.VMEM((SUBLANES - 1, CONV_HALO + tt - SUBLANES, CONV_DIM), F32),
                        pltpu.VMEM((tt, CONV_DIM), F32)],
        compiler_params=_params("arbitrary"),
        name="conv_prompt",
    )(glu, glu, w_dw, b_dw, ln_g, ln_b)


def _conv_sample_kernel(u_ref, buf_ref, w_ref, b_ref, g_ref, lb_ref, o_ref, nb_ref):
    hist = CONV_WIDTH - 1
    nt = u_ref.shape[0]

    def ext(j):
        return buf_ref[j] if j < hist else u_ref[j - hist]

    for t in range(nt):
        y = ext(t) * w_ref[0:1, :] + b_ref[...]
        for w in range(1, CONV_WIDTH):
            y = y + ext(t + w) * w_ref[w:w + 1, :]
        o_ref[t] = _ln_silu(y, g_ref[...], lb_ref[...]).astype(o_ref.dtype)
    for j in range(hist):
        nb_ref[j] = ext(j + nt)


def _conv_sample(glu_t, buf_t, w_dw, b_dw, ln_g, ln_b):
    nt, ns, _ = glu_t.shape
    hist = CONV_WIDTH - 1
    bs = 32
    vec = pl.BlockSpec((1, CONV_DIM), lambda i: (0, 0))
    return pl.pallas_call(
        _conv_sample_kernel,
        grid=(ns // bs,),
        in_specs=[
            pl.BlockSpec((nt, bs, CONV_DIM), lambda i: (0, i, 0)),
            pl.BlockSpec((hist, bs, CONV_DIM), lambda i: (0, i, 0)),
            pl.BlockSpec((CONV_WIDTH, CONV_DIM), lambda i: (0, 0)),
            vec, vec, vec,
        ],
        out_specs=[pl.BlockSpec((nt, bs, CONV_DIM), lambda i: (0, i, 0)),
                   pl.BlockSpec((hist, bs, CONV_DIM), lambda i: (0, i, 0))],
        out_shape=[jax.ShapeDtypeStruct((nt, ns, CONV_DIM), BF16), jax.ShapeDtypeStruct((hist, ns, CONV_DIM), F32)],
        compiler_params=_params("arbitrary"),
        name="conv_sample",
    )(glu_t, buf_t, w_dw, b_dw, ln_g, ln_b)


PAIR = 2 * B_DH
N_PAIRS = KV_WIDTH // PAIR
BLOCKS_PER_TOKEN = SEL_BLOCK // CMP_STRIDE
N_SUB = CMP_BLOCK // CMP_STRIDE
CMP_OFF = 120
CMP_ROWS = 640
KV_PAD = WINDOW
FAR_TILE = 1024
LOWEST = -3.0e38
MASK_C = 2.0 ** 100


def _half(shape, half):
    lane = lax.broadcasted_iota(jnp.int32, shape, len(shape) - 1)
    return (lane % PAIR) // B_DH == half


def _group_q(q_ref, g):
    parts = []
    for r in range(B_GROUP):
        h = g * B_GROUP + r
        x = q_ref[:, (h // 2) * PAIR:(h // 2 + 1) * PAIR]
        x = jnp.where(_half(x.shape, h % 2), x, 0.0) * (B_DH ** -0.5)
        if h % 2 != g % 2:
            x = pltpu.roll(x, B_DH, axis=1)
        parts.append(x)
    return jnp.concatenate(parts, axis=0).astype(BF16)


def _gate_rows(gate_ref, branch, g):
    sig = jax.nn.sigmoid(gate_ref[...])
    c0 = branch * B_HEADS + g * B_GROUP
    return jnp.concatenate([sig[:, c0 + r:c0 + r + 1] for r in range(B_GROUP)], axis=0)


def _scatter_heads(o, g):
    nq = o.shape[0] // B_GROUP
    outs = []
    for jp in range(2):
        acc = None
        for e in range(2):
            r = 2 * jp + e
            x = jnp.where(_half((nq, PAIR), g % 2), o[r * nq:(r + 1) * nq], 0.0)
            if e != g % 2:
                x = pltpu.roll(x, B_DH, axis=1)
            acc = x if acc is None else acc + x
        outs.append(acc)
    return jnp.concatenate(outs, axis=1)


def _nt_dot(a, b):
    return lax.dot_general(a, b, (((1,), (1,)), ((), ())), preferred_element_type=F32)


def _piece(qg, k, v, mask, bias=None):
    s = _nt_dot(qg, k)
    if bias is not None:
        s = s + bias
    s = jnp.where(mask, s, NEG)
    m = jnp.max(s, axis=-1, keepdims=True)
    p = jnp.where(mask, jnp.exp(s - m), 0.0)
    return m, jnp.sum(p, axis=-1, keepdims=True), jnp.dot(p.astype(BF16), v, preferred_element_type=F32), p


def _merge(a, b):
    m = jnp.maximum(a[0], b[0])
    ea = jnp.exp(a[0] - m)
    eb = jnp.exp(b[0] - m)
    return m, a[1] * ea + b[1] * eb, a[2] * ea + b[2] * eb


def _inv_or_zero(l):
    return jnp.where(l > 0.0, 1.0 / l, 0.0)


def _dot01(p, o01):
    hi = p.astype(BF16)
    r1 = p - hi.astype(F32)
    mid = r1.astype(BF16)
    lo = (r1 - mid.astype(F32)).astype(BF16)
    d = lambda a: jnp.dot(a, o01, preferred_element_type=F32)
    return d(hi) + d(mid) + d(lo)


def _dot01_t(o01_t, p):
    hi = p.astype(BF16)
    r1 = p - hi.astype(F32)
    mid = r1.astype(BF16)
    lo = (r1 - mid.astype(F32)).astype(BF16)
    return _nt_dot(o01_t, hi) + _nt_dot(o01_t, mid) + _nt_dot(o01_t, lo)


def _overlap01(tok, blk):
    return jnp.where((tok >= BLOCKS_PER_TOKEN * blk - (N_SUB - 1)) & (tok <= BLOCKS_PER_TOKEN * blk + BLOCKS_PER_TOKEN - 1),
                     1.0, 0.0).astype(BF16)


def _select_blocks(imp, qpos, n_cand=None, blocks_axis=1):
    j = lax.broadcasted_iota(jnp.int32, imp.shape, blocks_axis)
    cur = qpos // SEL_BLOCK
    forced = (j == 0) | (j == cur) | (j == cur - 1)
    valid = j * SEL_BLOCK <= qpos
    imp = jnp.where(forced, FORCE, jnp.where(valid, imp, -FORCE))
    if n_cand is not None:
        assert blocks_axis == 1
        ahead = jnp.zeros(imp.shape, F32)
        for i in range(n_cand):
            v = imp[:, i:i + 1]
            ahead = ahead + jnp.where((v > imp) | ((v == imp) & (j > i)), 1.0, 0.0)
        return jnp.where((ahead < N_SELECT) & (j < n_cand), 1.0, 0.0)
    sel = jnp.zeros(imp.shape, F32)
    jf = j.astype(F32)
    for _ in range(N_SELECT):
        m = jnp.max(imp, axis=blocks_axis, keepdims=True)
        first = jnp.min(jnp.where(imp == m, jf, float(imp.shape[blocks_axis])), axis=blocks_axis, keepdims=True)
        pick = jf == first
        sel = jnp.where(pick, 1.0, sel)
        imp = jnp.where(pick, LOWEST, imp)
    return sel


def _expand_blocks(sel_rows, first_blk, n_keys):
    jj = lax.broadcasted_iota(jnp.int32, (sel_rows.shape[1], n_keys), 0)
    kk = lax.broadcasted_iota(jnp.int32, (sel_rows.shape[1], n_keys), 1)
    e = jnp.where(jj == first_blk + kk // SEL_BLOCK, 1.0, 0.0).astype(BF16)
    return jnp.dot(sel_rows, e, preferred_element_type=F32)


def _bias_tables(rel_bias):
    d = np.arange(MAX_DISTANCE + 1)
    exact = N_BUCKETS // 2
    large = exact + (np.log(np.maximum(d, 1).astype(np.float32) / exact) / math.log(MAX_DISTANCE / exact)
                     * (N_BUCKETS - exact)).astype(np.int32)
    bucket = np.where(d < exact, d, np.minimum(large, N_BUCKETS - 1))
    by_dist = rel_bias.astype(F32)[bucket]

    def tile(c, nq, n, step=1):
        n1 = step * (n - 1) + 1
        dmin, dmax = c - (n1 - 1), c + nq - 1
        n_neg = max(0, min(0, dmax + 1) - dmin)
        lo, hi = max(dmin, 0), min(dmax, MAX_DISTANCE)
        n_far = max(0, dmax - max(dmin, MAX_DISTANCE + 1) + 1)
        v = jnp.concatenate([jnp.broadcast_to(by_dist[:1], (n_neg, B_HEADS)), by_dist[lo:hi + 1],
                             jnp.broadcast_to(by_dist[MAX_DISTANCE:], (n_far, B_HEADS))], axis=0)
        lv = nq + n1 - 1
        p = jnp.concatenate([v[::-1], jnp.zeros((1, B_HEADS), F32)], axis=0)
        rows = jnp.tile(p, (nq, 1))[:nq * lv].reshape(nq, lv, B_HEADS)[:, nq - 1:nq - 1 + n1:step]
        t = rows.reshape(nq, n, B_KV_HEADS, B_GROUP).transpose(2, 3, 0, 1)
        return t.reshape(B_KV_HEADS, B_GROUP * nq, n)

    nq = PAIR
    far = by_dist[MAX_DISTANCE]
    shift = jnp.repeat(far.reshape(B_KV_HEADS, B_GROUP), nq, axis=1)[:, :, None]
    n_var = 32
    m0 = PAIR - n_var
    c_cmp = -CMP_STRIDE * (m0 - CMP_OFF) - (CMP_BLOCK - 1)
    assert c_cmp + CMP_STRIDE >= MAX_DISTANCE
    prompt_cmp = jnp.concatenate([jnp.zeros((B_KV_HEADS, B_GROUP * nq, m0), F32),
                                  tile(c_cmp, nq, n_var, CMP_STRIDE) - shift], axis=2)
    causal = np.tile(np.arange(nq)[:, None] + nq - np.arange(2 * nq)[None, :] >= 0, (B_GROUP, 1))[None]
    prompt_kv = jnp.where(causal, tile(nq, nq, 2 * nq) - shift, -MASK_C)
    s_cmp = tile(PAST_LEN - (CMP_BLOCK - 1), DEC_SEQ, PAIR, CMP_STRIDE)
    s_sel = tile(PAST_LEN, DEC_SEQ, PAST_LEN + PAIR)
    s_win = tile(WINDOW, DEC_SEQ, WINDOW + PAIR)
    flat = lambda a: a.reshape(B_KV_HEADS * B_GROUP * DEC_SEQ, a.shape[-1])
    return prompt_cmp, prompt_kv, flat(s_cmp), flat(s_sel), flat(s_win)


def _cmp_weights(pe, w1, b1, w2, b2):
    k1 = CMP_STRIDE * B_DH
    w1f = w1.reshape(2, N_SUB, k1, CMP_HIDDEN).astype(BF16)
    pef = pe.reshape(2, N_SUB, 1, k1)
    b1f = b1.reshape(2, 1, CMP_HIDDEN)
    w2p = jnp.einsum('khd,ef->kehfd', w2, jnp.eye(2, dtype=F32)).reshape(2, 2, CMP_HIDDEN, PAIR).astype(BF16)
    b2p = jnp.tile(b2, (1, 2)).reshape(2, 1, PAIR)
    return pef, w1f, b1f, w2p, b2p


def _compress_pair(src, pe_ref, w1_ref, b1_ref, w2_ref, b2_ref):
    rows = [src(p) for p in range(CMP_STRIDE)]
    swapped = [pltpu.roll(r, B_DH, axis=1) for r in rows]
    low = _half(rows[0].shape, 0)
    m_rows = rows[0].shape[0]
    out = b2_ref[...]
    for e in range(2):
        x = jnp.concatenate([jnp.where(low, (rows, swapped)[e][p], (swapped, rows)[e][p + 1])
                             for p in range(0, CMP_STRIDE, 2)], axis=1)
        h = b1_ref[...]
        for m in range(N_SUB):
            part = jnp.dot((x + pe_ref[m]).astype(BF16), w1_ref[m], preferred_element_type=F32)
            h = h + (part if m == 0 else pltpu.roll(part, m_rows - m, axis=0))
        out = out + jnp.dot(_silu(h).astype(BF16), w2_ref[e], preferred_element_type=F32)
    return out


def _cmp_weight_specs(kv_of):
    k1 = CMP_STRIDE * B_DH
    return [
        pl.BlockSpec((None, N_SUB, 1, k1), lambda *a: (kv_of(*a), 0, 0, 0)),
        pl.BlockSpec((None, N_SUB, k1, CMP_HIDDEN), lambda *a: (kv_of(*a), 0, 0, 0)),
        pl.BlockSpec((None, 1, CMP_HIDDEN), lambda *a: (kv_of(*a), 0, 0)),
        pl.BlockSpec((None, 2, CMP_HIDDEN, PAIR), lambda *a: (kv_of(*a), 0, 0, 0)),
        pl.BlockSpec((None, 1, PAIR), lambda *a: (kv_of(*a), 0, 0)),
    ]


def _compress_prompt_kernel(rows_ref, pe_ref, w1_ref, b1_ref, w2_ref, b2_ref, o_ref):
    n_blk = rows_ref.shape[0] // CMP_STRIDE
    tok = _compress_pair(lambda p: rows_ref[pl.ds(p, n_blk, stride=CMP_STRIDE), :], pe_ref, w1_ref, b1_ref, w2_ref, b2_ref)
    o_ref[...] = jnp.zeros_like(o_ref)
    o_ref[CMP_OFF:CMP_OFF + n_blk, :] = tok


def _compress_prompt(u, weights):
    t = u.shape[0]
    col0 = (2 * A_QK + 2 * A_WIDTH + B_WIDTH) // PAIR
    return pl.pallas_call(
        _compress_prompt_kernel,
        grid=(2 * N_PAIRS,),
        in_specs=[pl.BlockSpec((t, PAIR), lambda c: (0, col0 + c))] + _cmp_weight_specs(lambda c: c // N_PAIRS),
        out_specs=pl.BlockSpec((None, CMP_ROWS, PAIR), lambda c: (c, 0, 0)),
        out_shape=jax.ShapeDtypeStruct((2 * N_PAIRS, CMP_ROWS, PAIR), F32),
        compiler_params=_params("arbitrary"),
        name="nsa_compress_prompt",
    )(u, *weights)


CMP_SEQ_BLOCK = 8
N_PAGES = PAST_LEN // PAGE_SIZE


def _compress_sample_kernel(pt_ref, *refs):
    n_src = CMP_SEQ_BLOCK * N_PAGES
    pages = refs[:n_src]
    pe_ref, w1_ref, b1_ref, w2_ref, b2_ref, o_ref, x_ref = refs[n_src:]
    n_blk = PAGE_SIZE // CMP_STRIDE
    for idx, pg in enumerate(pages):
        x_ref[idx] = pg[...].T
    tok = _compress_pair(
        lambda p: jnp.concatenate([x_ref[idx, pl.ds(p, n_blk, stride=CMP_STRIDE), :] for idx in range(n_src)], axis=0),
        pe_ref, w1_ref, b1_ref, w2_ref, b2_ref)
    o_ref[...] = tok.reshape(o_ref.shape)


def _compress_sample(cache, page_table, weights):
    n_seq = page_table.shape[0]
    sb = CMP_SEQ_BLOCK
    n_tok = PAST_LEN // CMP_STRIDE
    page_specs = [pl.BlockSpec((None, PAIR, PAGE_SIZE), lambda c, i, pt, s=s, j=j: (pt[i * sb + s, j], c, 0))
                  for s in range(sb) for j in range(N_PAGES)]
    return pl.pallas_call(
        _compress_sample_kernel,
        grid_spec=pltpu.PrefetchScalarGridSpec(
            num_scalar_prefetch=1,
            grid=(2 * N_PAIRS, n_seq // sb),
            in_specs=page_specs + _cmp_weight_specs(lambda c, i, pt: c // N_PAIRS),
            out_specs=pl.BlockSpec((sb, None, n_tok, PAIR), lambda c, i, pt: (i, c, 0, 0)),
            scratch_shapes=[pltpu.VMEM((sb * N_PAGES, PAGE_SIZE, PAIR), F32)],
        ),
        out_shape=jax.ShapeDtypeStruct((n_seq, 2 * N_PAIRS, n_tok, PAIR), F32),
        compiler_params=_params("arbitrary", "arbitrary"),
        name="nsa_compress_sample",
    )(page_table, *([cache] * (sb * N_PAGES)), *weights)


Q_ROWS = 128


def _nsa_cmp_prompt_kernel(q_ref, gate_ref, cmp_ref, bias_ref, oc_ref, sel_ref):
    qb = pl.program_id(0)
    nq = q_ref.shape[0]
    rows = B_GROUP * nq
    n_tok = CMP_ROWS - PAIR
    near0 = pl.multiple_of(qb * (nq // CMP_STRIDE), SUBLANES)
    tok0 = near0 - CMP_OFF
    mask_far = lax.broadcasted_iota(jnp.int32, (rows, n_tok), 1) < tok0
    i = lax.broadcasted_iota(jnp.int32, (rows, PAIR), 0) % nq
    mn = lax.broadcasted_iota(jnp.int32, (rows, PAIR), 1)
    dist = i - CMP_STRIDE * (mn - CMP_OFF) - (CMP_BLOCK - 1)
    mask_near = (dist >= 0) & (tok0 + mn >= 0)
    o_far = _overlap01(lax.broadcasted_iota(jnp.int32, (PAIR, n_tok), 1), lax.broadcasted_iota(jnp.int32, (PAIR, n_tok), 0))
    o_near = _overlap01(tok0 + lax.broadcasted_iota(jnp.int32, (PAIR, PAIR), 1), lax.broadcasted_iota(jnp.int32, (PAIR, PAIR), 0))
    qpos = qb * nq + lax.broadcasted_iota(jnp.int32, (PAIR, nq), 1)
    for g in range(B_KV_HEADS):
        qg = _group_q(q_ref, g)
        kp, vp = g // 2, N_PAIRS + g // 2
        far = _piece(qg, cmp_ref[kp, CMP_OFF:CMP_OFF + n_tok, :].astype(BF16),
                     cmp_ref[vp, CMP_OFF:CMP_OFF + n_tok, :].astype(BF16), mask_far)
        near = _piece(qg, cmp_ref[kp, pl.ds(near0, PAIR), :].astype(BF16),
                      cmp_ref[vp, pl.ds(near0, PAIR), :].astype(BF16), mask_near, bias_ref[g])
        m, l, acc = _merge(far[:3], near[:3])
        linv = _inv_or_zero(l)
        oc_ref[:, g * 2 * PAIR:(g + 1) * 2 * PAIR] = _scatter_heads(acc * linv * _gate_rows(gate_ref, 0, g), g)
        pf = jnp.sum((far[3] * (jnp.exp(far[0] - m) * linv)).reshape(B_GROUP, nq, n_tok), axis=0)
        pn = jnp.sum((near[3] * (jnp.exp(near[0] - m) * linv)).reshape(B_GROUP, nq, PAIR), axis=0)
        imp_t = _dot01_t(o_far, pf) + _dot01_t(o_near, pn)
        sel_ref[:, g * PAIR:(g + 1) * PAIR] = _select_blocks(imp_t, qpos, blocks_axis=0).T


def _flash_step_t(st, kx, qxt, vt, bias_t=None):
    m_old, l, acc = st
    s = jnp.dot(kx, qxt, preferred_element_type=F32)
    if bias_t is not None:
        s = s + bias_t
    m = jnp.maximum(m_old, jnp.max(s, axis=0, keepdims=True))
    p = jnp.exp(s - m)
    alpha = jnp.exp(m_old - m)
    return (m, alpha * l + jnp.sum(p, axis=0, keepdims=True),
            alpha * acc + jnp.dot(vt, p.astype(BF16), preferred_element_type=F32))


def _group_q_t(q_ref, g):
    parts = []
    for r in range(B_GROUP):
        h = g * B_GROUP + r
        x = q_ref[:, (h // 2) * PAIR:(h // 2 + 1) * PAIR]
        x = jnp.where(_half(x.shape, h % 2), x, 0.0) * (B_DH ** -0.5)
        if h % 2 != g % 2:
            x = pltpu.roll(x, B_DH, axis=1)
        parts.append(x.T)
    return jnp.concatenate(parts, axis=1).astype(BF16)


def _nsa_selwin_prompt_kernel(q_ref, gate_ref, sel_ref, oc_ref, k_ref, vt_ref, oh_ref, ohr_ref, bias_ref, wbias_ref, o_ref):
    qb = pl.program_id(0)
    nq = q_ref.shape[0]
    rows = B_GROUP * nq
    qs = qb * nq
    per_q = nq // SEL_BLOCK
    n_win = WINDOW + nq
    near_keys = pl.ds(pl.multiple_of(qs + KV_PAD - nq, nq), 2 * nq)
    win_keys = pl.ds(pl.multiple_of(qs, nq), n_win)
    blk = lax.broadcasted_iota(jnp.int32, (PAIR, rows), 0)
    first_near = per_q * (qb - 1)
    first_win = per_q * qb - WINDOW // SEL_BLOCK
    bb = lax.broadcasted_iota(jnp.int32, (PAIR, PAIR), 0)
    jj = lax.broadcasted_iota(jnp.int32, (PAIR, PAIR), 1)
    to_near = jnp.where(jj == first_near + bb, 1.0, 0.0).astype(BF16)
    exists_near = jnp.where(first_near + blk >= 0, 0.0, -MASK_C).astype(BF16)
    exists_win = jnp.where(first_win + blk >= 0, 0.0, -MASK_C).astype(BF16)
    per_tile = FAR_TILE // nq
    n_far = (qb + per_tile - 2) // per_tile
    init = (jnp.full((1, rows), NEG, F32), jnp.zeros((1, rows), F32), jnp.zeros((PAIR, rows), F32))
    gates_t = jax.nn.sigmoid(gate_ref[...]).T
    tile4 = lambda a: jnp.concatenate([a] * B_GROUP, axis=1)
    for g in range(B_KV_HEADS):
        qgt = _group_q_t(q_ref, g)
        pair = lambda base: slice(base + (g // 2) * PAIR, base + (g // 2 + 1) * PAIR)
        sel_t = sel_ref[:, g * PAIR:(g + 1) * PAIR].T
        far_vec = jnp.where((tile4(sel_t) > 0.5) & (blk < first_near), 0.0, -MASK_C).astype(BF16)
        near_sel = tile4(jnp.dot(to_near, sel_t.astype(BF16), preferred_element_type=F32))
        near_vec = jnp.where(near_sel > 0.5, 0.0, -MASK_C).astype(BF16) + exists_near
        qxt_far = jnp.concatenate([qgt, far_vec], axis=0)

        def far_step(t, st):
            keys = pl.ds(pl.multiple_of(KV_PAD + t * FAR_TILE, KV_PAD), FAR_TILE)
            kx = jnp.concatenate([k_ref[keys, pair(0)], oh_ref[keys, :]], axis=1)
            return _flash_step_t(st, kx, qxt_far, vt_ref[pair(0), keys])

        st = lax.fori_loop(0, n_far, far_step, init)
        st = _flash_step_t(st, jnp.concatenate([k_ref[near_keys, pair(0)], ohr_ref[:2 * nq, :]], axis=1),
                           jnp.concatenate([qgt, near_vec], axis=0), vt_ref[pair(0), near_keys], bias_ref[g])
        o_s = st[2] * _inv_or_zero(st[1])
        sw = _flash_step_t(init, jnp.concatenate([k_ref[win_keys, pair(KV_WIDTH)], ohr_ref[...]], axis=1),
                           jnp.concatenate([qgt, exists_win], axis=0), vt_ref[pair(KV_WIDTH), win_keys], wbias_ref[g])
        o_w = sw[2] * _inv_or_zero(sw[1])
        gate = lambda branch: jnp.concatenate(
            [gates_t[branch * B_HEADS + g * B_GROUP + r:branch * B_HEADS + g * B_GROUP + r + 1, :] for r in range(B_GROUP)],
            axis=1)
        o = (o_s * gate(1) + o_w * gate(2)).T
        cols = slice(g * 2 * PAIR, (g + 1) * 2 * PAIR)
        o_ref[:, cols] = _scatter_heads(o, g) + oc_ref[:, cols]


def _block_onehots(t):
    pos = np.arange(KV_PAD + t) - KV_PAD
    absolute = (pos[:, None] // SEL_BLOCK == np.arange(PAIR)[None, :]) & (pos[:, None] >= 0)
    relative = np.arange(WINDOW + Q_ROWS)[:, None] // SEL_BLOCK == np.arange(PAIR)[None, :]
    i = np.arange(B_GROUP * Q_ROWS)[:, None] % Q_ROWS
    in_window = np.arange(WINDOW - Q_ROWS)[None, :] > i
    return (jnp.asarray(absolute, BF16), jnp.asarray(relative, BF16),
            jnp.asarray(np.where(in_window, 0.0, -MASK_C), F32))


def _nsa_prompt(u, kv_pad, cmp_tok, bias_cmp, bias_kv):
    t = u.shape[0]
    nq = Q_ROWS
    q_col = (2 * A_QK + 2 * A_WIDTH) // B_WIDTH
    g_col = (IN_DIM - 3 * B_HEADS) // PAIR
    q_spec = pl.BlockSpec((nq, B_WIDTH), lambda i: (i, q_col))
    gate_spec = pl.BlockSpec((nq, PAIR), lambda i: (i, g_col))
    oc, sel = pl.pallas_call(
        _nsa_cmp_prompt_kernel,
        grid=(t // nq,),
        in_specs=[q_spec, gate_spec,
                  pl.BlockSpec(cmp_tok.shape, lambda i: (0, 0, 0)),
                  pl.BlockSpec(bias_cmp.shape, lambda i: (0, 0, 0))],
        out_specs=[pl.BlockSpec((nq, B_WIDTH), lambda i: (i, 0)), pl.BlockSpec((nq, B_KV_HEADS * PAIR), lambda i: (i, 0))],
        out_shape=[jax.ShapeDtypeStruct((t, B_WIDTH), F32), jax.ShapeDtypeStruct((t, B_KV_HEADS * PAIR), F32)],
        compiler_params=_params("arbitrary"),
        name="nsa_cmp_select_prompt",
    )(u, u, cmp_tok, bias_cmp)
    oh_abs, oh_rel, win_mask = _block_onehots(t)
    win_bias = jnp.concatenate([jnp.broadcast_to(win_mask, (B_KV_HEADS,) + win_mask.shape), bias_kv], axis=2)
    kw = KV_WIDTH
    k_pad = jnp.concatenate([kv_pad[:, :kw], kv_pad[:, 2 * kw:3 * kw]], axis=1)
    vt_pad = jnp.concatenate([kv_pad[:, kw:2 * kw], kv_pad[:, 3 * kw:]], axis=1).T
    bias_kv_t, win_bias_t = bias_kv.transpose(0, 2, 1), win_bias.transpose(0, 2, 1)
    whole = lambda a: pl.BlockSpec(a.shape, lambda i: (0,) * a.ndim, pipeline_mode=pl.Buffered(1))
    return pl.pallas_call(
        _nsa_selwin_prompt_kernel,
        grid=(t // nq,),
        in_specs=[q_spec, gate_spec,
                  pl.BlockSpec((nq, B_KV_HEADS * PAIR), lambda i: (i, 0)),
                  pl.BlockSpec((nq, B_WIDTH), lambda i: (i, 0)),
                  whole(k_pad), whole(vt_pad), whole(oh_abs), whole(oh_rel), whole(bias_kv_t), whole(win_bias_t)],
        out_specs=pl.BlockSpec((nq, B_WIDTH), lambda i: (i, 0)),
        out_shape=jax.ShapeDtypeStruct((t, B_WIDTH), F32),
        compiler_params=_params("arbitrary"),
        name="nsa_select_window_prompt",
    )(u, u, sel, oc, k_pad, vt_pad, oh_abs, oh_rel, bias_kv_t, win_bias_t)


SEL_KEYS = PAST_LEN + PAIR
WIN_KEYS = WINDOW + PAIR


NSA_SEQ_BLOCK = 2


def _piece_t(qg, kt, vt, mask, bias):
    s = jnp.dot(qg, kt, preferred_element_type=F32) + bias
    s = jnp.where(mask, s, NEG)
    m = jnp.max(s, axis=-1, keepdims=True)
    p = jnp.where(mask, jnp.exp(s - m), 0.0)
    return m, jnp.sum(p, axis=-1, keepdims=True), _nt_dot(p.astype(BF16), vt), p


def _nsa_sample_kernel(pt_ref, *refs):
    n_pg = NSA_SEQ_BLOCK * N_PAGES
    pages = refs[:n_pg]
    (q_ref, gate_ref, cmp_ref, selnew_ref, winnew_ref, win_ref, bc_ref, bs_ref, bw_ref,
     o_ref, winout_ref, kt_ref, vt_ref, wkt_ref, wvt_ref) = refs[n_pg:]
    seqs = [_nsa_sample_one(pages[s * N_PAGES:(s + 1) * N_PAGES], q_ref.at[s], gate_ref.at[s], cmp_ref.at[s],
                            selnew_ref.at[s], winnew_ref.at[s], win_ref.at[s], bc_ref, bs_ref, bw_ref,
                            o_ref.at[s], winout_ref.at[s], kt_ref.at[s], vt_ref.at[s], wkt_ref.at[s], wvt_ref.at[s])
            for s in range(NSA_SEQ_BLOCK)]
    for _ in range(NSA_SAMPLE_STAGES):
        for seq in seqs:
            next(seq)


NSA_SAMPLE_STAGES = 5


def _nsa_sample_one(pages, q_ref, gate_ref, cmp_ref, selnew_ref, winnew_ref, win_ref, bc_ref, bs_ref, bw_ref,
                    o_ref, winout_ref, kt_ref, vt_ref, wkt_ref, wvt_ref):
    nt = q_ref.shape[0]
    grp = B_GROUP * nt
    rows = B_KV_HEADS * grp
    half = 2 * PAIR
    pad = jnp.zeros((PAIR - nt, 2 * half), F32)
    sel_new_t = jnp.concatenate([selnew_ref[...], pad], axis=0).T
    win_new_t = jnp.concatenate([winnew_ref[...], pad], axis=0).T
    for j, pg in enumerate(pages):
        kt_ref[:, j * PAGE_SIZE:(j + 1) * PAGE_SIZE] = pg[:half, :].astype(BF16)
        vt_ref[:, j * PAGE_SIZE:(j + 1) * PAGE_SIZE] = pg[half:, :].astype(BF16)
    kt_ref[:, PAST_LEN:] = sel_new_t[:half].astype(BF16)
    vt_ref[:, PAST_LEN:] = sel_new_t[half:].astype(BF16)
    buf = win_ref[...]
    wkt_ref[:, :WINDOW] = buf[:half].astype(BF16)
    wvt_ref[:, :WINDOW] = buf[half:].astype(BF16)
    wkt_ref[:, WINDOW:] = win_new_t[:half].astype(BF16)
    wvt_ref[:, WINDOW:] = win_new_t[half:].astype(BF16)
    shifted = pltpu.roll(buf, WINDOW - nt, axis=1)
    tail = pltpu.roll(win_new_t, PAIR - nt, axis=1)
    lane = lax.broadcasted_iota(jnp.int32, tail.shape, 1)
    winout_ref[:, :WINDOW - PAIR] = shifted[:, :WINDOW - PAIR]
    winout_ref[:, WINDOW - PAIR:] = jnp.where(lane >= PAIR - nt, tail, shifted[:, WINDOW - PAIR:])
    yield

    zero = jnp.zeros((grp, PAIR), BF16)
    qq = jnp.concatenate(
        [jnp.concatenate([_group_q(q_ref, g), zero] if g // 2 == 0 else [zero, _group_q(q_ref, g)], axis=1)
         for g in range(B_KV_HEADS)], axis=0)
    take = lambda acc, g: acc[g * grp:(g + 1) * grp, (g // 2) * PAIR:(g // 2 + 1) * PAIR]
    qpos = PAST_LEN + lax.broadcasted_iota(jnp.int32, (rows, 1), 0) % nt

    n_tok = cmp_ref.shape[1]
    ck = jnp.concatenate([cmp_ref[0], cmp_ref[1]], axis=1).astype(BF16)
    cv = jnp.concatenate([cmp_ref[2], cmp_ref[3]], axis=1).astype(BF16)
    mask_c = lax.broadcasted_iota(jnp.int32, (rows, n_tok), 1) < n_tok - (N_SUB - 1)
    mc, lc, acc_c, pc = _piece(qq, ck, cv, mask_c, bc_ref[...])
    pc = pc * _inv_or_zero(lc)
    p_all = jnp.concatenate([jnp.sum(pc[g * grp:(g + 1) * grp].reshape(B_GROUP, nt, n_tok), axis=0)
                             for g in range(B_KV_HEADS)], axis=0)
    yield
    o01 = _overlap01(lax.broadcasted_iota(jnp.int32, (n_tok, PAIR), 0), lax.broadcasted_iota(jnp.int32, (n_tok, PAIR), 1))
    qpos_gt = PAST_LEN + lax.broadcasted_iota(jnp.int32, (B_KV_HEADS * nt, PAIR), 0) % nt
    n_blocks = pl.cdiv(PAST_LEN + nt, SEL_BLOCK)
    sel = _select_blocks(_dot01(p_all, o01), qpos_gt, n_cand=n_blocks).astype(BF16)
    sel_rows = jnp.concatenate([sel[g * nt:(g + 1) * nt] for g in range(B_KV_HEADS) for _ in range(B_GROUP)], axis=0)
    yield

    ks = lax.broadcasted_iota(jnp.int32, (rows, SEL_KEYS), 1)
    mask_s = (_expand_blocks(sel_rows, 0, SEL_KEYS) > 0.5) & (ks <= qpos)
    ms, ls, acc_s, _ = _piece_t(qq, kt_ref[...], vt_ref[...], mask_s, bs_ref[...])
    yield
    kw = PAST_LEN - WINDOW + lax.broadcasted_iota(jnp.int32, (rows, WIN_KEYS), 1)
    mask_w = (kw <= qpos) & (qpos - kw < WINDOW)
    mw, lw, acc_w, _ = _piece_t(qq, wkt_ref[...], wvt_ref[...], mask_w, bw_ref[...])
    acc_c, acc_s, acc_w = acc_c * _inv_or_zero(lc), acc_s * _inv_or_zero(ls), acc_w * _inv_or_zero(lw)
    for g in range(B_KV_HEADS):
        o = (take(acc_c, g) * _gate_rows(gate_ref, 0, g) + take(acc_s, g) * _gate_rows(gate_ref, 1, g)
             + take(acc_w, g) * _gate_rows(gate_ref, 2, g))
        o_ref[:, g * 2 * PAIR:(g + 1) * 2 * PAIR] = _scatter_heads(o, g)
    yield


def _nsa_sample(u_s, cmp_tok, cache_sel, win_buf, page_table, bias_c, bias_s, bias_w):
    n_seq, nt, _ = u_s.shape
    sb = NSA_SEQ_BLOCK
    q_col = (2 * A_QK + 2 * A_WIDTH) // B_WIDTH
    kv_col = (2 * A_QK + 2 * A_WIDTH + B_WIDTH) // (4 * PAIR)
    g_col = (IN_DIM - 3 * B_HEADS) // PAIR
    const = lambda a: pl.BlockSpec(a.shape, lambda b, pt: (0, 0))
    in_specs = [pl.BlockSpec((None, 4 * PAIR, PAGE_SIZE), lambda b, pt, s=s, j=j: (pt[b * sb + s, j], 0, 0))
                for s in range(sb) for j in range(N_PAGES)]
    in_specs += [
        pl.BlockSpec((sb, nt, B_WIDTH), lambda b, pt: (b, 0, q_col)),
        pl.BlockSpec((sb, nt, PAIR), lambda b, pt: (b, 0, g_col)),
        pl.BlockSpec((sb,) + cmp_tok.shape[1:], lambda b, pt: (b, 0, 0, 0)),
        pl.BlockSpec((sb, nt, 4 * PAIR), lambda b, pt: (b, 0, kv_col + 1)),
        pl.BlockSpec((sb, nt, 4 * PAIR), lambda b, pt: (b, 0, kv_col + 2)),
        pl.BlockSpec((sb, 4 * PAIR, WINDOW), lambda b, pt: (b, 0, 0)),
        const(bias_c), const(bias_s), const(bias_w),
    ]
    return pl.pallas_call(
        _nsa_sample_kernel,
        grid_spec=pltpu.PrefetchScalarGridSpec(
            num_scalar_prefetch=1,
            grid=(n_seq // sb,),
            in_specs=in_specs,
            out_specs=[pl.BlockSpec((sb, nt, B_WIDTH), lambda b, pt: (b, 0, 0)),
                       pl.BlockSpec((sb, 4 * PAIR, WINDOW), lambda b, pt: (b, 0, 0))],
            scratch_shapes=[pltpu.VMEM((sb, 2 * PAIR, SEL_KEYS), BF16), pltpu.VMEM((sb, 2 * PAIR, SEL_KEYS), BF16),
                            pltpu.VMEM((sb, 2 * PAIR, WIN_KEYS), BF16), pltpu.VMEM((sb, 2 * PAIR, WIN_KEYS), BF16)],
        ),
        out_shape=[jax.ShapeDtypeStruct((n_seq, nt, B_WIDTH), F32), jax.ShapeDtypeStruct(win_buf.shape, F32)],
        compiler_params=_params("arbitrary"),
        name="nsa_sample",
    )(page_table, *([cache_sel] * (sb * N_PAGES)), u_s, u_s, cmp_tok, u_s, u_s, win_buf, bias_c, bias_s, bias_w)


PROMPT_TM = 1024
PROJ_TN = 512
IN_PROJ_TN = 1024
COL_CMP = 2 * A_QK + 2 * A_WIDTH + B_WIDTH
COL_SEL = COL_CMP + 2 * KV_WIDTH
COL_WIN = COL_SEL + 2 * KV_WIDTH
COL_GATE = COL_WIN + 2 * KV_WIDTH


def _time_major(a):
    return a.transpose(1, 0, 2).reshape(a.shape[0] * a.shape[1], a.shape[2])


def _seq_major(a2d, n_seq):
    return a2d.reshape(a2d.shape[0] // n_seq, n_seq, a2d.shape[1]).transpose(1, 0, 2)


def _kv_rows(u3, col):
    return u3[..., col:col + 2 * KV_WIDTH].reshape(u3.shape[:-1] + (2, B_KV_HEADS, B_DH))


def kernel(x_prompt, x_sample, cache_cmp, cache_sel, state_win, state_hgrn, state_conv, page_table, c_prompt, c_sample, norm_g, ada_w, ada_b, ffn_w_gate, ffn_w_up, ffn_w_down, w_in_even, hgrn_lower_bound, hgrn_norm_g, cmp_pe, cmp_w1, cmp_b1, cmp_w2, cmp_b2, rel_bias, w_out_even, conv_w_pw1, conv_b_pw1, conv_w_dw, conv_b_dw, conv_ln_g, conv_ln_b, conv_w_pw2, conv_b_pw2, final_norm_g):
    P = {'norm_g': norm_g, 'ffn_w_gate': ffn_w_gate, 'ffn_w_up': ffn_w_up, 'ffn_w_down': ffn_w_down,
         'final_norm_g': final_norm_g}
    n_seq = x_sample.shape[0]
    n_pool = cache_cmp.shape[1]
    rows_p, rows_s = _prompt_rows(PROMPT_TM), _sample_rows()
    g4 = norm_g.reshape(DEPTH, 3, 1, D_MODEL)
    c_all = jnp.concatenate([c_sample, jnp.tile(c_prompt, (SUBLANES, 1))], axis=0)
    mod = _ada_mod(c_all, ada_w, ada_b)
    bias_pc, bias_pkv, bias_sc, bias_ss, bias_sw = _bias_tables(rel_bias)

    xp = x_prompt.reshape(SEQ, D_MODEL)
    xs = _time_major(x_sample)
    cmp_p, cmp_s, sel_p, sel_s, win_p, win_s, hgrn_p, hgrn_s, conv_p, conv_s = ([] for _ in range(10))
    for l in range(DEPTH):
        i = l // 2
        last = l == DEPTH - 1
        xs, w_bf16 = _ffn(xs, rows_s, mod, l, 0, P)
        xp, _ = _ffn(xp, rows_p, mod, l, 0, P, w_bf16=w_bf16)
        if l % 2 == 0:
            w_in_t = w_in_even[i:i + 1].transpose(0, 2, 1)
            up = _proj(xp, rows_p, mod, l, g4, w_in_t, IN_DIM, IN_PROJ_TN)
            us = _seq_major(_proj(xs, rows_s, mod, l, g4, w_in_t, IN_DIM, IN_PROJ_TN), n_seq)
            gn = hgrn_norm_g[i:i + 1]
            oa_p, hp = _hgrn(up, hgrn_lower_bound, gn, l)
            oa_s, hs = _hgrn(us.reshape(n_seq * DEC_SEQ, IN_DIM), hgrn_lower_bound, gn, l, state_hgrn[i])
            weights = _cmp_weights(cmp_pe[i], cmp_w1[i], cmp_b1[i], cmp_w2[i], cmp_b2[i])
            kv_pad = jnp.pad(up[:, COL_SEL:COL_GATE].astype(BF16), ((KV_PAD, 0), (0, 0)))
            ob_p = _nsa_prompt(up, kv_pad, _compress_prompt(up, weights), bias_pc, bias_pkv)
            cmp_t = cache_cmp[i].transpose(0, 2, 3, 4, 1).reshape(n_pool, 4 * PAIR, PAGE_SIZE)
            cmp_tok_s = _compress_sample(cmp_t, page_table, weights)
            sel_t = cache_sel[i].transpose(0, 2, 3, 4, 1).reshape(n_pool, 4 * PAIR, PAGE_SIZE)
            win_t = state_win[i].transpose(0, 2, 3, 4, 1).reshape(n_seq, 4 * PAIR, WINDOW)
            ob_s, wn = _nsa_sample(us, cmp_tok_s, sel_t, win_t, page_table, bias_sc, bias_ss, bias_sw)
            xp = _out_proj([oa_p, ob_p], w_out_even[i:i + 1], [0, A_WIDTH], None, xp, rows_p, mod, l, PROJ_TN)
            xs = _out_proj([_time_major(oa_s.reshape(n_seq, DEC_SEQ, A_WIDTH)), _time_major(ob_s)],
                           w_out_even[i:i + 1], [0, A_WIDTH], None, xs, rows_s, mod, l, PROJ_TN)
            up3 = up[None]
            cmp_p.append(_kv_rows(up3, COL_CMP))
            sel_p.append(_kv_rows(up3, COL_SEL))
            win_p.append(_kv_rows(up3[:, SEQ - min(WINDOW, SEQ):], COL_WIN))
            cmp_s.append(_kv_rows(us, COL_CMP))
            sel_s.append(_kv_rows(us, COL_SEL))
            win_s.append(wn.reshape(n_seq, 2, B_KV_HEADS, B_DH, WINDOW).transpose(0, 4, 1, 2, 3))
            hgrn_p.append(hp[None])
            hgrn_s.append(hs)
        else:
            b_pw1 = conv_b_pw1[i].reshape(1, 1, 2 * CONV_DIM)
            b_pw2 = conv_b_pw2[i].reshape(1, 1, D_MODEL)
            vec = lambda a: a[i].reshape(1, CONV_DIM)
            glu_p = _glu_proj(xp, rows_p, mod, l, g4, conv_w_pw1[i:i + 1], b_pw1, PROJ_TN)
            act_p = _conv_prompt(glu_p, conv_w_dw[i], vec(conv_b_dw), vec(conv_ln_g), vec(conv_ln_b))
            xp = _out_proj([act_p], conv_w_pw2[i:i + 1], [0], b_pw2, xp, rows_p, mod, l, PROJ_TN)
            glu_s = _glu_proj(xs, rows_s, mod, l, g4, conv_w_pw1[i:i + 1], b_pw1, PROJ_TN)
            act_s, nb = _conv_sample(glu_s.reshape(DEC_SEQ, n_seq, CONV_DIM), state_conv[i].transpose(1, 0, 2),
                                     conv_w_dw[i], vec(conv_b_dw), vec(conv_ln_g), vec(conv_ln_b))
            xs = _out_proj([act_s.reshape(DEC_SEQ * n_seq, CONV_DIM)], conv_w_pw2[i:i + 1], [0], b_pw2, xs, rows_s, mod, l, PROJ_TN)
            conv_p.append(glu_p[None, SEQ - (CONV_WIDTH - 1):])
            conv_s.append(nb.transpose(1, 0, 2))
        xs, w_bf16 = _ffn(xs, rows_s, mod, l, 2, P, final_norm=last)
        xp, _ = _ffn(xp, rows_p, mod, l, 2, P, w_bf16=w_bf16, final_norm=last)
    y_prompt = xp.reshape(1, SEQ, D_MODEL)
    y_sample = _seq_major(xs, n_seq)
    st = jnp.stack
    return (y_prompt, y_sample, st(cmp_p), st(cmp_s), st(sel_p), st(sel_s), st(win_p), st(win_s),
            st(hgrn_p), st(hgrn_s), st(conv_p), st(conv_s))
```

```python
import functools
import math

import numpy as np
import jax
import jax.numpy as jnp
from jax import lax
from jax.experimental import pallas as pl
from jax.experimental.pallas import tpu as pltpu

F32 = jnp.float32
BF16 = jnp.bfloat16

D_MODEL = 2048
SEQ = 8192
DEPTH = 2
DEC_BATCH = 128
DEC_SEQ = 8
PAST_LEN = 2048
PAGE_SIZE = 128
N_MOD = 9
D_FF = 5504
EPS = 1e-6
A_HEADS = 8
A_DK = 128
A_DV = 128
A_QK = A_HEADS * A_DK
A_WIDTH = A_HEADS * A_DV
B_HEADS = 16
B_KV_HEADS = 4
B_DH = 64
B_GROUP = B_HEADS // B_KV_HEADS
B_WIDTH = B_HEADS * B_DH
KV_WIDTH = B_KV_HEADS * B_DH
CMP_BLOCK = 32
CMP_STRIDE = 16
CMP_HIDDEN = 256
SEL_BLOCK = 64
N_SELECT = 16
WINDOW = 512
N_BUCKETS = 32
MAX_DISTANCE = 128
MIX_WIDTH = A_WIDTH + B_WIDTH
IN_DIM = 2 * A_QK + 2 * A_WIDTH + B_WIDTH + 6 * KV_WIDTH + 3 * B_HEADS
CONV_WIDTH = 31
CONV_DIM = D_MODEL
NEG = -1e30
FORCE = 1e9

V7X_VMEM_LIMIT_BYTES = 60 * 1024 * 1024
SUBLANES = 8
LANES = 128

N_SEQ_ROWS = DEC_BATCH + SUBLANES
PROMPT_ROW_BLOCK = DEC_BATCH // SUBLANES


def _params(*sem):
    return pltpu.CompilerParams(dimension_semantics=sem, vmem_limit_bytes=V7X_VMEM_LIMIT_BYTES)


def _silu(x):
    return x * jax.nn.sigmoid(x)


def _bdot(a, b):
    return jnp.dot(a.astype(BF16), b.astype(BF16), preferred_element_type=F32)


def _ada_kernel(c_ref, w_ref, b_ref, o_ref):
    o_ref[...] = _bdot(_silu(c_ref[...]), w_ref[...]) + b_ref[...]


def _ada_mod(c_all, ada_w, ada_b):
    n = c_all.shape[0]
    return pl.pallas_call(
        _ada_kernel,
        grid=(DEPTH, N_MOD),
        in_specs=[
            pl.BlockSpec((n, D_MODEL), lambda l, k: (0, 0)),
            pl.BlockSpec((None, D_MODEL, D_MODEL), lambda l, k: (l, 0, k)),
            pl.BlockSpec((None, None, 1, D_MODEL), lambda l, k: (l, k, 0, 0)),
        ],
        out_specs=pl.BlockSpec((None, None, n, D_MODEL), lambda l, k: (l, k, 0, 0)),
        out_shape=jax.ShapeDtypeStruct((DEPTH, N_MOD, n, D_MODEL), F32),
        compiler_params=_params("arbitrary", "arbitrary"),
        name="ada_mod",
    )(c_all, ada_w, ada_b.reshape(DEPTH, N_MOD, 1, D_MODEL))


def _norm_mod(x, g, shift, scale):
    ms = jnp.mean(x * x, axis=-1, keepdims=True)
    y = x * lax.rsqrt(ms + EPS) * g
    h = y * (1.0 + scale) + shift
    return h.reshape(x.shape[0] * x.shape[1], x.shape[2]).astype(BF16)


class _Rows:
    def __init__(self, m, bs, nt, seq_block):
        assert m % (bs * nt) == 0
        self.m, self.bs, self.nt, self.seq_block = m, bs, nt, seq_block
        self.tm = bs * nt
        self.n_tiles = m // self.tm

    def view(self, x2d):
        return x2d.reshape(self.m // self.bs, self.bs, x2d.shape[-1])

    def x_spec(self, width, col=lambda j: 0, single_buffer=False):
        mode = dict(pipeline_mode=pl.Buffered(1)) if single_buffer else {}
        return pl.BlockSpec((self.nt, self.bs, width), lambda i, j: (i, 0, col(j)), **mode)

    def mod_spec(self, layer, k, width=D_MODEL, col=lambda j: 0):
        sb = self.seq_block
        return pl.BlockSpec((None, None, self.bs, width), lambda i, j: (layer, k, sb, col(j)))


def _prompt_rows(tm):
    return _Rows(SEQ, SUBLANES, tm // SUBLANES, PROMPT_ROW_BLOCK)


def _sample_rows():
    return _Rows(DEC_BATCH * DEC_SEQ, DEC_BATCH, DEC_SEQ, 0)


FFN_TF_F32 = 256
FFN_TF_BF16 = 512
FFN_ACC_CHUNKS = 4


def _ffn_kernel(x_ref, sh_ref, sc_ref, gt_ref, g_ref, wg_ref, wu_ref, wd_ref, fg_ref, o_ref, *rest, final_norm, emit):
    if emit:
        wg_o, wu_o, wd_o, h_ref = rest
    else:
        (h_ref,) = rest
    j = pl.program_id(1)
    nj = pl.num_programs(1)
    tf = wg_ref.shape[1]

    nt, bs = o_ref.shape[0], o_ref.shape[1]
    step = max(nt // FFN_ACC_CHUNKS, 1)

    @pl.when(j == 0)
    def _():
        for r in range(0, nt, step):
            h_ref[r * bs:(r + step) * bs, :] = _norm_mod(x_ref[r:r + step], g_ref[...], sh_ref[...], sc_ref[...])
        o_ref[...] = jnp.zeros_like(o_ref)

    wg, wu, wd = wg_ref[...].astype(BF16), wu_ref[...].astype(BF16), wd_ref[...].astype(BF16)
    if emit:
        wg_o[...], wu_o[...], wd_o[...] = wg, wu, wd
    valid = D_FF - j * tf
    h = h_ref[...]
    a = _silu(_bdot(h, wg)) * _bdot(h, wu)
    col = lax.broadcasted_iota(jnp.int32, a.shape, 1)
    a = jnp.where(col < valid, a, 0.0)
    row = lax.broadcasted_iota(jnp.int32, wd.shape, 0)
    wd = jnp.where(row < valid, wd, jnp.zeros_like(wd))
    a = a.astype(BF16)
    for r in range(0, nt, step):
        o_ref[r:r + step] += _bdot(a[r * bs:(r + step) * bs], wd).reshape(step, bs, o_ref.shape[2])

    @pl.when(j == nj - 1)
    def _():
        for r in range(0, nt, step):
            y = x_ref[r:r + step] + (0.5 * gt_ref[...]) * o_ref[r:r + step]
            if final_norm:
                ms = jnp.mean(y * y, axis=-1, keepdims=True)
                y = y * lax.rsqrt(ms + EPS) * fg_ref[...]
            o_ref[r:r + step] = y


def _ffn(x2d, rows, mod, layer, sub, P, w_bf16=None, final_norm=False):
    half = sub // 2
    emit = w_bf16 is None
    tf = FFN_TF_F32 if emit else FFN_TF_BF16
    g_norm = P['norm_g'].reshape(DEPTH, 3, 1, D_MODEL)
    fg = P['final_norm_g'].reshape(1, D_MODEL)
    if emit:
        weights = (P['ffn_w_gate'], P['ffn_w_up'], P['ffn_w_down'])
        w_specs = [pl.BlockSpec((None, None, D_MODEL, tf), lambda i, j: (layer, half, 0, j)),
                   pl.BlockSpec((None, None, D_MODEL, tf), lambda i, j: (layer, half, 0, j)),
                   pl.BlockSpec((None, None, tf, D_MODEL), lambda i, j: (layer, half, j, 0))]
    else:
        weights = w_bf16
        w_specs = [pl.BlockSpec((D_MODEL, tf), lambda i, j: (0, j)),
                   pl.BlockSpec((D_MODEL, tf), lambda i, j: (0, j)),
                   pl.BlockSpec((tf, D_MODEL), lambda i, j: (j, 0))]
    out_specs = [rows.x_spec(D_MODEL, single_buffer=emit)]
    out_shape = [jax.ShapeDtypeStruct((rows.m // rows.bs, rows.bs, D_MODEL), F32)]
    if emit:
        out_specs += [pl.BlockSpec((D_MODEL, tf), lambda i, j: (0, j)),
                      pl.BlockSpec((D_MODEL, tf), lambda i, j: (0, j)),
                      pl.BlockSpec((tf, D_MODEL), lambda i, j: (j, 0))]
        out_shape += [jax.ShapeDtypeStruct((D_MODEL, D_FF), BF16), jax.ShapeDtypeStruct((D_MODEL, D_FF), BF16),
                      jax.ShapeDtypeStruct((D_FF, D_MODEL), BF16)]
        assert rows.n_tiles == 1
    outs = pl.pallas_call(
        functools.partial(_ffn_kernel, final_norm=final_norm, emit=emit),
        grid=(rows.n_tiles, pl.cdiv(D_FF, tf)),
        in_specs=[
            rows.x_spec(D_MODEL, single_buffer=True),
            rows.mod_spec(layer, 3 * sub), rows.mod_spec(layer, 3 * sub + 1), rows.mod_spec(layer, 3 * sub + 2),
            pl.BlockSpec((None, None, 1, D_MODEL), lambda i, j: (layer, sub, 0, 0)),
            *w_specs,
            pl.BlockSpec((1, D_MODEL), lambda i, j: (0, 0)),
        ],
        out_specs=out_specs,
        out_shape=out_shape,
        scratch_shapes=[pltpu.VMEM((rows.tm, D_MODEL), BF16)],
        compiler_params=_params("arbitrary", "arbitrary"),
        name="ffn_half_step",
    )(rows.view(x2d), mod, mod, mod, g_norm, *weights, fg)
    return outs[0].reshape(rows.m, D_MODEL), tuple(outs[1:])


def _proj_kernel(x_ref, sh_ref, sc_ref, g_ref, w_ref, o_ref, h_ref):
    @pl.when(pl.program_id(1) == 0)
    def _():
        h_ref[...] = _norm_mod(x_ref[...], g_ref[...], sh_ref[...], sc_ref[...])

    o_ref[...] = _nt_dot(h_ref[...], w_ref[...].astype(BF16))


def _proj(x2d, rows, mod, layer, g_norm4, w3t, n_out, tn):
    return pl.pallas_call(
        _proj_kernel,
        grid=(rows.n_tiles, pl.cdiv(n_out, tn)),
        in_specs=[
            rows.x_spec(D_MODEL),
            rows.mod_spec(layer, 3), rows.mod_spec(layer, 4),
            pl.BlockSpec((None, None, 1, D_MODEL), lambda i, j: (layer, 1, 0, 0)),
            pl.BlockSpec((None, tn, D_MODEL), lambda i, j: (0, j, 0)),
        ],
        out_specs=pl.BlockSpec((rows.tm, tn), lambda i, j: (i, j)),
        out_shape=jax.ShapeDtypeStruct((rows.m, n_out), F32),
        scratch_shapes=[pltpu.VMEM((rows.tm, D_MODEL), BF16)],
        compiler_params=_params("arbitrary", "arbitrary"),
        name="prenorm_proj",
    )(rows.view(x2d), mod, mod, g_norm4, w3t)


def _glu_proj_kernel(x_ref, sh_ref, sc_ref, g_ref, wa_ref, wg_ref, ba_ref, bg_ref, o_ref, h_ref):
    @pl.when(pl.program_id(1) == 0)
    def _():
        h_ref[...] = _norm_mod(x_ref[...], g_ref[...], sh_ref[...], sc_ref[...])

    h = h_ref[...]
    a = _bdot(h, wa_ref[...]) + ba_ref[...]
    gt = _bdot(h, wg_ref[...]) + bg_ref[...]
    o_ref[...] = a * jax.nn.sigmoid(gt)


def _glu_proj(x2d, rows, mod, layer, g_norm4, w3, b3, tn):
    nb = CONV_DIM // tn
    return pl.pallas_call(
        _glu_proj_kernel,
        grid=(rows.n_tiles, nb),
        in_specs=[
            rows.x_spec(D_MODEL),
            rows.mod_spec(layer, 3), rows.mod_spec(layer, 4),
            pl.BlockSpec((None, None, 1, D_MODEL), lambda i, j: (layer, 1, 0, 0)),
            pl.BlockSpec((None, D_MODEL, tn), lambda i, j: (0, 0, j)),
            pl.BlockSpec((None, D_MODEL, tn), lambda i, j: (0, 0, j + nb)),
            pl.BlockSpec((None, 1, tn), lambda i, j: (0, 0, j)),
            pl.BlockSpec((None, 1, tn), lambda i, j: (0, 0, j + nb)),
        ],
        out_specs=pl.BlockSpec((rows.tm, tn), lambda i, j: (i, j)),
        out_shape=jax.ShapeDtypeStruct((rows.m, CONV_DIM), F32),
        scratch_shapes=[pltpu.VMEM((rows.tm, D_MODEL), BF16)],
        compiler_params=_params("arbitrary", "arbitrary"),
        name="prenorm_glu_proj",
    )(rows.view(x2d), mod, mod, g_norm4, w3, w3, b3, b3)


def _out_kernel(*refs, n_in, has_bias):
    a_refs = refs[:n_in]
    w_refs = refs[n_in:2 * n_in]
    pos = 2 * n_in
    b_ref = refs[pos] if has_bias else None
    pos += int(has_bias)
    x_ref, gt_ref, o_ref = refs[pos:pos + 3]
    y = _bdot(a_refs[0][...], w_refs[0][...])
    for a_ref, w_ref in zip(a_refs[1:], w_refs[1:]):
        y += _bdot(a_ref[...], w_ref[...])
    if has_bias:
        y += b_ref[...]
    o_ref[...] = x_ref[...] + gt_ref[...] * y.reshape(o_ref.shape)


def _out_proj(acts, w3, k_offsets, bias3, x2d, rows, mod, layer, tn):
    n_in = len(acts)
    in_specs = [pl.BlockSpec((rows.tm, a.shape[1]), lambda i, j: (i, 0)) for a in acts]
    for a, off in zip(acts, k_offsets):
        kb = off // a.shape[1]
        in_specs.append(pl.BlockSpec((None, a.shape[1], tn), lambda i, j, kb=kb: (0, kb, j)))
    args = list(acts) + [w3] * n_in
    if bias3 is not None:
        in_specs.append(pl.BlockSpec((None, 1, tn), lambda i, j: (0, 0, j)))
        args.append(bias3)
    in_specs += [rows.x_spec(tn, col=lambda j: j), rows.mod_spec(layer, 5, width=tn, col=lambda j: j)]
    args += [rows.view(x2d), mod]
    out = pl.pallas_call(
        functools.partial(_out_kernel, n_in=n_in, has_bias=bias3 is not None),
        grid=(rows.n_tiles, D_MODEL // tn),
        in_specs=in_specs,
        out_specs=rows.x_spec(tn, col=lambda j: j),
        out_shape=jax.ShapeDtypeStruct((rows.m // rows.bs, rows.bs, D_MODEL), F32),
        compiler_params=_params("arbitrary", "arbitrary"),
        name="out_proj_residual",
    )(*args)
    return out.reshape(rows.m, D_MODEL)


HGRN_ROWS = 512
HGRN_HEADS = 8
HGRN_TRI = 128


def _hgrn_prepare(f_ref, lb_ref, cum_s, kk_s, *, layer, c):
    p = lb_ref[...]
    e = jnp.exp(p - jnp.max(p, axis=0, keepdims=True))
    sm = e / jnp.sum(e, axis=0, keepdims=True)
    lb = jnp.sum(sm[:layer + 1], axis=0, keepdims=True)
    f = lb + (1.0 - lb) * jax.nn.sigmoid(f_ref[...])
    lf = jnp.log(f)
    n = HGRN_TRI
    r = lax.broadcasted_iota(jnp.int32, (n, n), 0)
    s = lax.broadcasted_iota(jnp.int32, (n, n), 1)
    tri = jnp.where((s <= r) & (s // c == r // c), 1.0, 0.0).astype(F32)
    for b in range(f.shape[0] // n):
        cum_s[b * n:(b + 1) * n, :] = jnp.dot(tri, lf[b * n:(b + 1) * n], preferred_element_type=F32,
                                              precision=lax.Precision.HIGHEST)
    kk_s[...] = 1.0 - f


def _hgrn_subchunk(r0, c, st, hh, q_ref, v_ref, g_ref, gn_ref, cum_s, kk_s):
    rows = pl.ds(r0, c)
    cols = slice(hh * A_DK, (hh + 1) * A_DK)
    cum = cum_s[rows, cols]
    q = q_ref[rows, cols]
    kk = kk_s[rows, cols]
    vv = v_ref[rows, cols]
    last = cum[c - 1:c, :]
    o = lax.dot_general((q * jnp.exp(cum)).astype(BF16), st.astype(BF16), (((1,), (1,)), ((), ())),
                        preferred_element_type=F32)
    srow = lax.broadcasted_iota(jnp.int32, (c, A_DK), 0)
    xs = []
    for t in range(c):
        d = jnp.where(srow <= t, cum[t:t + 1, :] - cum, NEG)
        xs.append(jnp.exp(d) * (q[t:t + 1, :] * kk))
    x = jnp.concatenate(xs, axis=0).astype(BF16)
    w = jnp.dot(x, jnp.ones((A_DK, A_DV), BF16), preferred_element_type=F32)
    o = o + jnp.sum(w.reshape(c, c, A_DV) * vv[None], axis=1)
    ke = kk * jnp.exp(last - cum)
    st_new = st * jnp.exp(last) + lax.dot_general(vv.astype(BF16), ke.astype(BF16), (((0,), (0,)), ((), ())),
                                                  preferred_element_type=F32)
    ms = jnp.mean(o * o, axis=-1, keepdims=True)
    y = o * lax.rsqrt(ms + EPS) * gn_ref[:, cols] * _silu(g_ref[rows, cols])
    return y, st_new


def _hgrn_prompt_kernel(q_ref, f_ref, v_ref, g_ref, lb_ref, gn_ref, o_ref, s_ref, st_ref, cum_s, kk_s, *, layer, c):
    i = pl.program_id(1)

    @pl.when(i == 0)
    def _():
        st_ref[...] = jnp.zeros_like(st_ref)

    _hgrn_prepare(f_ref, lb_ref, cum_s, kk_s, layer=layer, c=c)

    def body(n, carry):
        r0 = pl.multiple_of(n * c, c)
        for hh in range(HGRN_HEADS):
            y, st_new = _hgrn_subchunk(r0, c, st_ref[hh], hh, q_ref, v_ref, g_ref, gn_ref, cum_s, kk_s)
            st_ref[hh] = st_new
            o_ref[pl.ds(r0, c), hh * A_DV:(hh + 1) * A_DV] = y
        return carry

    lax.fori_loop(0, q_ref.shape[0] // c, body, 0)

    @pl.when(i == pl.num_programs(1) - 1)
    def _():
        for hh in range(HGRN_HEADS):
            s_ref[hh] = st_ref[hh].T


def _hgrn_sample_kernel(q_ref, f_ref, v_ref, g_ref, lb_ref, gn_ref, s0_ref, o_ref, s_ref, cum_s, kk_s, *, layer, c):
    _hgrn_prepare(f_ref, lb_ref, cum_s, kk_s, layer=layer, c=c)

    def body(n, carry):
        r0 = pl.multiple_of(n * c, c)
        for hh in range(HGRN_HEADS):
            y, st_new = _hgrn_subchunk(r0, c, s0_ref[n, hh].T, hh, q_ref, v_ref, g_ref, gn_ref, cum_s, kk_s)
            s_ref[n, hh] = st_new.T
            o_ref[pl.ds(r0, c), hh * A_DV:(hh + 1) * A_DV] = y
        return carry

    lax.fori_loop(0, q_ref.shape[0] // c, body, 0)


def _hgrn(u, lower_bound, norm_g2, layer, s0=None):
    m = u.shape[0]
    hb = HGRN_HEADS
    w = hb * A_DK
    nb = A_QK // w
    tc = HGRN_ROWS if s0 is None else HGRN_ROWS // 4

    def col(seg):
        return pl.BlockSpec((tc, w), lambda h, i, seg=seg: (i, seg * nb + h))

    in_specs = [col(0), col(1), col(2), col(3),
                pl.BlockSpec((DEPTH + 1, w), lambda h, i: (0, h)),
                pl.BlockSpec((1, w), lambda h, i: (0, h))]
    o_spec = pl.BlockSpec((tc, w), lambda h, i: (i, h))
    scratch = [pltpu.VMEM((tc, w), F32), pltpu.VMEM((tc, w), F32)]
    if s0 is None:
        return pl.pallas_call(
            functools.partial(_hgrn_prompt_kernel, layer=layer, c=16),
            grid=(nb, m // tc),
            in_specs=in_specs,
            out_specs=[o_spec, pl.BlockSpec((hb, A_DK, A_DV), lambda h, i: (h, 0, 0))],
            out_shape=[jax.ShapeDtypeStruct((m, A_WIDTH), F32), jax.ShapeDtypeStruct((A_HEADS, A_DK, A_DV), F32)],
            scratch_shapes=[pltpu.VMEM((hb, A_DV, A_DK), F32)] + scratch,
            compiler_params=_params("arbitrary", "arbitrary"),
            name="hgrn2_prompt",
        )(u, u, u, u, lower_bound, norm_g2)
    c = DEC_SEQ
    ns = tc // c
    s_spec = pl.BlockSpec((ns, hb, A_DK, A_DV), lambda h, i: (i, h, 0, 0))
    return pl.pallas_call(
        functools.partial(_hgrn_sample_kernel, layer=layer, c=c),
        grid=(nb, m // tc),
        in_specs=in_specs + [s_spec],
        out_specs=[o_spec, s_spec],
        out_shape=[jax.ShapeDtypeStruct((m, A_WIDTH), F32), jax.ShapeDtypeStruct(s0.shape, F32)],
        scratch_shapes=scratch,
        compiler_params=_params("arbitrary", "arbitrary"),
        name="hgrn2_sample",
    )(u, u, u, u, lower_bound, norm_g2, s0)


CONV_HALO = 32
CONV_ROWS = 256


def _ln_silu(y, g, b):
    mu = jnp.mean(y, axis=-1, keepdims=True)
    yc = y - mu
    var = jnp.mean(yc * yc, axis=-1, keepdims=True)
    return _silu(yc * lax.rsqrt(var + EPS) * g + b)


CONV_CHUNK_ROWS = 64
CONV_CHUNK_COLS = 512


def _conv_prompt_kernel(cur_ref, halo_ref, w_ref, b_ref, g_ref, lb_ref, o_ref, ext_ref, ph_ref, y_ref):
    i = pl.program_id(0)
    tt = cur_ref.shape[0]
    ext_ref[0:CONV_HALO, :] = jnp.where(i > 0, halo_ref[...], 0.0)
    ext_ref[CONV_HALO:, :] = cur_ref[...]
    off = CONV_HALO - (CONV_WIDTH - 1)
    span = ph_ref.shape[1]
    for r in range(1, SUBLANES):
        ph_ref[r - 1] = ext_ref[r:r + span, :]
    for c0 in range(0, CONV_DIM, CONV_CHUNK_COLS):
        cs = slice(c0, c0 + CONV_CHUNK_COLS)
        for r0 in range(0, tt, CONV_CHUNK_ROWS):
            y = jnp.zeros((CONV_CHUNK_ROWS, CONV_CHUNK_COLS), F32) + b_ref[:, cs]
            for w in range(CONV_WIDTH):
                a, r = divmod(off + w, SUBLANES)
                src = ext_ref if r == 0 else ph_ref.at[r - 1]
                lo = SUBLANES * a + r0
                y = y + src[lo:lo + CONV_CHUNK_ROWS, cs] * w_ref[w:w + 1, cs]
            y_ref[r0:r0 + CONV_CHUNK_ROWS, cs] = y
    o_ref[...] = _ln_silu(y_ref[...], g_ref[...], lb_ref[...]).astype(o_ref.dtype)


def _conv_prompt(glu, w_dw, b_dw, ln_g, ln_b):
    t = glu.shape[0]
    tt = CONV_ROWS
    r = tt // CONV_HALO
    vec = pl.BlockSpec((1, CONV_DIM), lambda i: (0, 0))
    return pl.pallas_call(
        _conv_prompt_kernel,
        grid=(t // tt,),
        in_specs=[
            pl.BlockSpec((tt, CONV_DIM), lambda i: (i, 0)),
            pl.BlockSpec((CONV_HALO, CONV_DIM), lambda i: (jnp.maximum(i * r - 1, 0), 0)),
            pl.BlockSpec((CONV_WIDTH, CONV_DIM), lambda i: (0, 0)),
            vec, vec, vec,
        ],
        out_specs=pl.BlockSpec((tt, CONV_DIM), lambda i: (i, 0)),
        out_shape=jax.ShapeDtypeStruct((t, CONV_DIM), BF16),
        scratch_shapes=[pltpu.VMEM((CONV_HALO + tt, CONV_DIM), F32),
                        pltpu.VMEM((SUBLANES - 1, CONV_HALO + tt - SUBLANES, CONV_DIM), F32),
                        pltpu.VMEM((tt, CONV_DIM), F32)],
        compiler_params=_params("arbitrary"),
        name="conv_prompt",
    )(glu, glu, w_dw, b_dw, ln_g, ln_b)


def _conv_sample_kernel(u_ref, buf_ref, w_ref, b_ref, g_ref, lb_ref, o_ref, nb_ref):
    hist = CONV_WIDTH - 1
    nt = u_ref.shape[0]

    def ext(j):
        return buf_ref[j] if j < hist else u_ref[j - hist]

    for t in range(nt):
        y = ext(t) * w_ref[0:1, :] + b_ref[...]
        for w in range(1, CONV_WIDTH):
            y = y + ext(t + w) * w_ref[w:w + 1, :]
        o_ref[t] = _ln_silu(y, g_ref[...], lb_ref[...]).astype(o_ref.dtype)
    for j in range(hist):
        nb_ref[j] = ext(j + nt)


def _conv_sample(glu_t, buf_t, w_dw, b_dw, ln_g, ln_b):
    nt, ns, _ = glu_t.shape
    hist = CONV_WIDTH - 1
    bs = 32
    vec = pl.BlockSpec((1, CONV_DIM), lambda i: (0, 0))
    return pl.pallas_call(
        _conv_sample_kernel,
        grid=(ns // bs,),
        in_specs=[
            pl.BlockSpec((nt, bs, CONV_DIM), lambda i: (0, i, 0)),
            pl.BlockSpec((hist, bs, CONV_DIM), lambda i: (0, i, 0)),
            pl.BlockSpec((CONV_WIDTH, CONV_DIM), lambda i: (0, 0)),
            vec, vec, vec,
        ],
        out_specs=[pl.BlockSpec((nt, bs, CONV_DIM), lambda i: (0, i, 0)),
                   pl.BlockSpec((hist, bs, CONV_DIM), lambda i: (0, i, 0))],
        out_shape=[jax.ShapeDtypeStruct((nt, ns, CONV_DIM), BF16), jax.ShapeDtypeStruct((hist, ns, CONV_DIM), F32)],
        compiler_params=_params("arbitrary"),
        name="conv_sample",
    )(glu_t, buf_t, w_dw, b_dw, ln_g, ln_b)


PAIR = 2 * B_DH
N_PAIRS = KV_WIDTH // PAIR
BLOCKS_PER_TOKEN = SEL_BLOCK // CMP_STRIDE
N_SUB = CMP_BLOCK // CMP_STRIDE
CMP_OFF = 120
CMP_ROWS = 640
KV_PAD = WINDOW
FAR_TILE = 1024
LOWEST = -3.0e38
MASK_C = 2.0 ** 100


def _half(shape, half):
    lane = lax.broadcasted_iota(jnp.int32, shape, len(shape) - 1)
    return (lane % PAIR) // B_DH == half


def _group_q(q_ref, g):
    parts = []
    for r in range(B_GROUP):
        h = g * B_GROUP + r
        x = q_ref[:, (h // 2) * PAIR:(h // 2 + 1) * PAIR]
        x = jnp.where(_half(x.shape, h % 2), x, 0.0) * (B_DH ** -0.5)
        if h % 2 != g % 2:
            x = pltpu.roll(x, B_DH, axis=1)
        parts.append(x)
    return jnp.concatenate(parts, axis=0).astype(BF16)


def _gate_rows(gate_ref, branch, g):
    sig = jax.nn.sigmoid(gate_ref[...])
    c0 = branch * B_HEADS + g * B_GROUP
    return jnp.concatenate([sig[:, c0 + r:c0 + r + 1] for r in range(B_GROUP)], axis=0)


def _scatter_heads(o, g):
    nq = o.shape[0] // B_GROUP
    outs = []
    for jp in range(2):
        acc = None
        for e in range(2):
            r = 2 * jp + e
            x = jnp.where(_half((nq, PAIR), g % 2), o[r * nq:(r + 1) * nq], 0.0)
            if e != g % 2:
                x = pltpu.roll(x, B_DH, axis=1)
            acc = x if acc is None else acc + x
        outs.append(acc)
    return jnp.concatenate(outs, axis=1)


def _nt_dot(a, b):
    return lax.dot_general(a, b, (((1,), (1,)), ((), ())), preferred_element_type=F32)


def _piece(qg, k, v, mask, bias=None):
    s = _nt_dot(qg, k)
    if bias is not None:
        s = s + bias
    s = jnp.where(mask, s, NEG)
    m = jnp.max(s, axis=-1, keepdims=True)
    p = jnp.where(mask, jnp.exp(s - m), 0.0)
    return m, jnp.sum(p, axis=-1, keepdims=True), jnp.dot(p.astype(BF16), v, preferred_element_type=F32), p


def _merge(a, b):
    m = jnp.maximum(a[0], b[0])
    ea = jnp.exp(a[0] - m)
    eb = jnp.exp(b[0] - m)
    return m, a[1] * ea + b[1] * eb, a[2] * ea + b[2] * eb


def _inv_or_zero(l):
    return jnp.where(l > 0.0, 1.0 / l, 0.0)


def _dot01(p, o01):
    hi = p.astype(BF16)
    r1 = p - hi.astype(F32)
    mid = r1.astype(BF16)
    lo = (r1 - mid.astype(F32)).astype(BF16)
    d = lambda a: jnp.dot(a, o01, preferred_element_type=F32)
    return d(hi) + d(mid) + d(lo)


def _dot01_t(o01_t, p):
    hi = p.astype(BF16)
    r1 = p - hi.astype(F32)
    mid = r1.astype(BF16)
    lo = (r1 - mid.astype(F32)).astype(BF16)
    return _nt_dot(o01_t, hi) + _nt_dot(o01_t, mid) + _nt_dot(o01_t, lo)


def _overlap01(tok, blk):
    return jnp.where((tok >= BLOCKS_PER_TOKEN * blk - (N_SUB - 1)) & (tok <= BLOCKS_PER_TOKEN * blk + BLOCKS_PER_TOKEN - 1),
                     1.0, 0.0).astype(BF16)


def _select_blocks(imp, qpos, n_cand=None, blocks_axis=1):
    j = lax.broadcasted_iota(jnp.int32, imp.shape, blocks_axis)
    cur = qpos // SEL_BLOCK
    forced = (j == 0) | (j == cur) | (j == cur - 1)
    valid = j * SEL_BLOCK <= qpos
    imp = jnp.where(forced, FORCE, jnp.where(valid, imp, -FORCE))
    if n_cand is not None:
        assert blocks_axis == 1
        ahead = jnp.zeros(imp.shape, F32)
        for i in range(n_cand):
            v = imp[:, i:i + 1]
            ahead = ahead + jnp.where((v > imp) | ((v == imp) & (j > i)), 1.0, 0.0)
        return jnp.where((ahead < N_SELECT) & (j < n_cand), 1.0, 0.0)
    sel = jnp.zeros(imp.shape, F32)
    jf = j.astype(F32)
    for _ in range(N_SELECT):
        m = jnp.max(imp, axis=blocks_axis, keepdims=True)
        first = jnp.min(jnp.where(imp == m, jf, float(imp.shape[blocks_axis])), axis=blocks_axis, keepdims=True)
        pick = jf == first
        sel = jnp.where(pick, 1.0, sel)
        imp = jnp.where(pick, LOWEST, imp)
    return sel


def _expand_blocks(sel_rows, first_blk, n_keys):
    jj = lax.broadcasted_iota(jnp.int32, (sel_rows.shape[1], n_keys), 0)
    kk = lax.broadcasted_iota(jnp.int32, (sel_rows.shape[1], n_keys), 1)
    e = jnp.where(jj == first_blk + kk // SEL_BLOCK, 1.0, 0.0).astype(BF16)
    return jnp.dot(sel_rows, e, preferred_element_type=F32)


def _bias_tables(rel_bias):
    d = np.arange(MAX_DISTANCE + 1)
    exact = N_BUCKETS // 2
    large = exact + (np.log(np.maximum(d, 1).astype(np.float32) / exact) / math.log(MAX_DISTANCE / exact)
                     * (N_BUCKETS - exact)).astype(np.int32)
    bucket = np.where(d < exact, d, np.minimum(large, N_BUCKETS - 1))
    by_dist = rel_bias.astype(F32)[bucket]

    def tile(c, nq, n, step=1):
        n1 = step * (n - 1) + 1
        dmin, dmax = c - (n1 - 1), c + nq - 1
        n_neg = max(0, min(0, dmax + 1) - dmin)
        lo, hi = max(dmin, 0), min(dmax, MAX_DISTANCE)
        n_far = max(0, dmax - max(dmin, MAX_DISTANCE + 1) + 1)
        v = jnp.concatenate([jnp.broadcast_to(by_dist[:1], (n_neg, B_HEADS)), by_dist[lo:hi + 1],
                             jnp.broadcast_to(by_dist[MAX_DISTANCE:], (n_far, B_HEADS))], axis=0)
        lv = nq + n1 - 1
        p = jnp.concatenate([v[::-1], jnp.zeros((1, B_HEADS), F32)], axis=0)
        rows = jnp.tile(p, (nq, 1))[:nq * lv].reshape(nq, lv, B_HEADS)[:, nq - 1:nq - 1 + n1:step]
        t = rows.reshape(nq, n, B_KV_HEADS, B_GROUP).transpose(2, 3, 0, 1)
        return t.reshape(B_KV_HEADS, B_GROUP * nq, n)

    nq = PAIR
    far = by_dist[MAX_DISTANCE]
    shift = jnp.repeat(far.reshape(B_KV_HEADS, B_GROUP), nq, axis=1)[:, :, None]
    n_var = 32
    m0 = PAIR - n_var
    c_cmp = -CMP_STRIDE * (m0 - CMP_OFF) - (CMP_BLOCK - 1)
    assert c_cmp + CMP_STRIDE >= MAX_DISTANCE
    prompt_cmp = jnp.concatenate([jnp.zeros((B_KV_HEADS, B_GROUP * nq, m0), F32),
                                  tile(c_cmp, nq, n_var, CMP_STRIDE) - shift], axis=2)
    causal = np.tile(np.arange(nq)[:, None] + nq - np.arange(2 * nq)[None, :] >= 0, (B_GROUP, 1))[None]
    prompt_kv = jnp.where(causal, tile(nq, nq, 2 * nq) - shift, -MASK_C)
    s_cmp = tile(PAST_LEN - (CMP_BLOCK - 1), DEC_SEQ, PAIR, CMP_STRIDE)
    s_sel = tile(PAST_LEN, DEC_SEQ, PAST_LEN + PAIR)
    s_win = tile(WINDOW, DEC_SEQ, WINDOW + PAIR)
    flat = lambda a: a.reshape(B_KV_HEADS * B_GROUP * DEC_SEQ, a.shape[-1])
    return prompt_cmp, prompt_kv, flat(s_cmp), flat(s_sel), flat(s_win)


def _cmp_weights(pe, w1, b1, w2, b2):
    k1 = CMP_STRIDE * B_DH
    w1f = w1.reshape(2, N_SUB, k1, CMP_HIDDEN).astype(BF16)
    pef = pe.reshape(2, N_SUB, 1, k1)
    b1f = b1.reshape(2, 1, CMP_HIDDEN)
    w2p = jnp.einsum('khd,ef->kehfd', w2, jnp.eye(2, dtype=F32)).reshape(2, 2, CMP_HIDDEN, PAIR).astype(BF16)
    b2p = jnp.tile(b2, (1, 2)).reshape(2, 1, PAIR)
    return pef, w1f, b1f, w2p, b2p


def _compress_pair(src, pe_ref, w1_ref, b1_ref, w2_ref, b2_ref):
    rows = [src(p) for p in range(CMP_STRIDE)]
    swapped = [pltpu.roll(r, B_DH, axis=1) for r in rows]
    low = _half(rows[0].shape, 0)
    m_rows = rows[0].shape[0]
    out = b2_ref[...]
    for e in range(2):
        x = jnp.concatenate([jnp.where(low, (rows, swapped)[e][p], (swapped, rows)[e][p + 1])
                             for p in range(0, CMP_STRIDE, 2)], axis=1)
        h = b1_ref[...]
        for m in range(N_SUB):
            part = jnp.dot((x + pe_ref[m]).astype(BF16), w1_ref[m], preferred_element_type=F32)
            h = h + (part if m == 0 else pltpu.roll(part, m_rows - m, axis=0))
        out = out + jnp.dot(_silu(h).astype(BF16), w2_ref[e], preferred_element_type=F32)
    return out


def _cmp_weight_specs(kv_of):
    k1 = CMP_STRIDE * B_DH
    return [
        pl.BlockSpec((None, N_SUB, 1, k1), lambda *a: (kv_of(*a), 0, 0, 0)),
        pl.BlockSpec((None, N_SUB, k1, CMP_HIDDEN), lambda *a: (kv_of(*a), 0, 0, 0)),
        pl.BlockSpec((None, 1, CMP_HIDDEN), lambda *a: (kv_of(*a), 0, 0)),
        pl.BlockSpec((None, 2, CMP_HIDDEN, PAIR), lambda *a: (kv_of(*a), 0, 0, 0)),
        pl.BlockSpec((None, 1, PAIR), lambda *a: (kv_of(*a), 0, 0)),
    ]


def _compress_prompt_kernel(rows_ref, pe_ref, w1_ref, b1_ref, w2_ref, b2_ref, o_ref):
    n_blk = rows_ref.shape[0] // CMP_STRIDE
    tok = _compress_pair(lambda p: rows_ref[pl.ds(p, n_blk, stride=CMP_STRIDE), :], pe_ref, w1_ref, b1_ref, w2_ref, b2_ref)
    o_ref[...] = jnp.zeros_like(o_ref)
    o_ref[CMP_OFF:CMP_OFF + n_blk, :] = tok


def _compress_prompt(u, weights):
    t = u.shape[0]
    col0 = (2 * A_QK + 2 * A_WIDTH + B_WIDTH) // PAIR
    return pl.pallas_call(
        _compress_prompt_kernel,
        grid=(2 * N_PAIRS,),
        in_specs=[pl.BlockSpec((t, PAIR), lambda c: (0, col0 + c))] + _cmp_weight_specs(lambda c: c // N_PAIRS),
        out_specs=pl.BlockSpec((None, CMP_ROWS, PAIR), lambda c: (c, 0, 0)),
        out_shape=jax.ShapeDtypeStruct((2 * N_PAIRS, CMP_ROWS, PAIR), F32),
        compiler_params=_params("arbitrary"),
        name="nsa_compress_prompt",
    )(u, *weights)


CMP_SEQ_BLOCK = 8
N_PAGES = PAST_LEN // PAGE_SIZE


def _compress_sample_kernel(pt_ref, *refs):
    n_src = CMP_SEQ_BLOCK * N_PAGES
    pages = refs[:n_src]
    pe_ref, w1_ref, b1_ref, w2_ref, b2_ref, o_ref, x_ref = refs[n_src:]
    n_blk = PAGE_SIZE // CMP_STRIDE
    for idx, pg in enumerate(pages):
        x_ref[idx] = pg[...].T
    tok = _compress_pair(
        lambda p: jnp.concatenate([x_ref[idx, pl.ds(p, n_blk, stride=CMP_STRIDE), :] for idx in range(n_src)], axis=0),
        pe_ref, w1_ref, b1_ref, w2_ref, b2_ref)
    o_ref[...] = tok.reshape(o_ref.shape)


def _compress_sample(cache, page_table, weights):
    n_seq = page_table.shape[0]
    sb = CMP_SEQ_BLOCK
    n_tok = PAST_LEN // CMP_STRIDE
    page_specs = [pl.BlockSpec((None, PAIR, PAGE_SIZE), lambda c, i, pt, s=s, j=j: (pt[i * sb + s, j], c, 0))
                  for s in range(sb) for j in range(N_PAGES)]
    return pl.pallas_call(
        _compress_sample_kernel,
        grid_spec=pltpu.PrefetchScalarGridSpec(
            num_scalar_prefetch=1,
            grid=(2 * N_PAIRS, n_seq // sb),
            in_specs=page_specs + _cmp_weight_specs(lambda c, i, pt: c // N_PAIRS),
            out_specs=pl.BlockSpec((sb, None, n_tok, PAIR), lambda c, i, pt: (i, c, 0, 0)),
            scratch_shapes=[pltpu.VMEM((sb * N_PAGES, PAGE_SIZE, PAIR), F32)],
        ),
        out_shape=jax.ShapeDtypeStruct((n_seq, 2 * N_PAIRS, n_tok, PAIR), F32),
        compiler_params=_params("arbitrary", "arbitrary"),
        name="nsa_compress_sample",
    )(page_table, *([cache] * (sb * N_PAGES)), *weights)


Q_ROWS = 128


def _nsa_cmp_prompt_kernel(q_ref, gate_ref, cmp_ref, bias_ref, oc_ref, sel_ref):
    qb = pl.program_id(0)
    nq = q_ref.shape[0]
    rows = B_GROUP * nq
    n_tok = CMP_ROWS - PAIR
    near0 = pl.multiple_of(qb * (nq // CMP_STRIDE), SUBLANES)
    tok0 = near0 - CMP_OFF
    mask_far = lax.broadcasted_iota(jnp.int32, (rows, n_tok), 1) < tok0
    i = lax.broadcasted_iota(jnp.int32, (rows, PAIR), 0) % nq
    mn = lax.broadcasted_iota(jnp.int32, (rows, PAIR), 1)
    dist = i - CMP_STRIDE * (mn - CMP_OFF) - (CMP_BLOCK - 1)
    mask_near = (dist >= 0) & (tok0 + mn >= 0)
    o_far = _overlap01(lax.broadcasted_iota(jnp.int32, (PAIR, n_tok), 1), lax.broadcasted_iota(jnp.int32, (PAIR, n_tok), 0))
    o_near = _overlap01(tok0 + lax.broadcasted_iota(jnp.int32, (PAIR, PAIR), 1), lax.broadcasted_iota(jnp.int32, (PAIR, PAIR), 0))
    qpos = qb * nq + lax.broadcasted_iota(jnp.int32, (PAIR, nq), 1)
    for g in range(B_KV_HEADS):
        qg = _group_q(q_ref, g)
        kp, vp = g // 2, N_PAIRS + g // 2
        far = _piece(qg, cmp_ref[kp, CMP_OFF:CMP_OFF + n_tok, :].astype(BF16),
                     cmp_ref[vp, CMP_OFF:CMP_OFF + n_tok, :].astype(BF16), mask_far)
        near = _piece(qg, cmp_ref[kp, pl.ds(near0, PAIR), :].astype(BF16),
                      cmp_ref[vp, pl.ds(near0, PAIR), :].astype(BF16), mask_near, bias_ref[g])
        m, l, acc = _merge(far[:3], near[:3])
        linv = _inv_or_zero(l)
        oc_ref[:, g * 2 * PAIR:(g + 1) * 2 * PAIR] = _scatter_heads(acc * linv * _gate_rows(gate_ref, 0, g), g)
        pf = jnp.sum((far[3] * (jnp.exp(far[0] - m) * linv)).reshape(B_GROUP, nq, n_tok), axis=0)
        pn = jnp.sum((near[3] * (jnp.exp(near[0] - m) * linv)).reshape(B_GROUP, nq, PAIR), axis=0)
        imp_t = _dot01_t(o_far, pf) + _dot01_t(o_near, pn)
        sel_ref[:, g * PAIR:(g + 1) * PAIR] = _select_blocks(imp_t, qpos, blocks_axis=0).T


def _flash_step_t(st, kx, qxt, vt, bias_t=None):
    m_old, l, acc = st
    s = jnp.dot(kx, qxt, preferred_element_type=F32)
    if bias_t is not None:
        s = s + bias_t
    m = jnp.maximum(m_old, jnp.max(s, axis=0, keepdims=True))
    p = jnp.exp(s - m)
    alpha = jnp.exp(m_old - m)
    return (m, alpha * l + jnp.sum(p, axis=0, keepdims=True),
            alpha * acc + jnp.dot(vt, p.astype(BF16), preferred_element_type=F32))


def _group_q_t(q_ref, g):
    parts = []
    for r in range(B_GROUP):
        h = g * B_GROUP + r
        x = q_ref[:, (h // 2) * PAIR:(h // 2 + 1) * PAIR]
        x = jnp.where(_half(x.shape, h % 2), x, 0.0) * (B_DH ** -0.5)
        if h % 2 != g % 2:
            x = pltpu.roll(x, B_DH, axis=1)
        parts.append(x.T)
    return jnp.concatenate(parts, axis=1).astype(BF16)


def _nsa_selwin_prompt_kernel(q_ref, gate_ref, sel_ref, oc_ref, k_ref, vt_ref, oh_ref, ohr_ref, bias_ref, wbias_ref, o_ref):
    qb = pl.program_id(0)
    nq = q_ref.shape[0]
    rows = B_GROUP * nq
    qs = qb * nq
    per_q = nq // SEL_BLOCK
    n_win = WINDOW + nq
    near_keys = pl.ds(pl.multiple_of(qs + KV_PAD - nq, nq), 2 * nq)
    win_keys = pl.ds(pl.multiple_of(qs, nq), n_win)
    blk = lax.broadcasted_iota(jnp.int32, (PAIR, rows), 0)
    first_near = per_q * (qb - 1)
    first_win = per_q * qb - WINDOW // SEL_BLOCK
    bb = lax.broadcasted_iota(jnp.int32, (PAIR, PAIR), 0)
    jj = lax.broadcasted_iota(jnp.int32, (PAIR, PAIR), 1)
    to_near = jnp.where(jj == first_near + bb, 1.0, 0.0).astype(BF16)
    exists_near = jnp.where(first_near + blk >= 0, 0.0, -MASK_C).astype(BF16)
    exists_win = jnp.where(first_win + blk >= 0, 0.0, -MASK_C).astype(BF16)
    per_tile = FAR_TILE // nq
    n_far = (qb + per_tile - 2) // per_tile
    init = (jnp.full((1, rows), NEG, F32), jnp.zeros((1, rows), F32), jnp.zeros((PAIR, rows), F32))
    gates_t = jax.nn.sigmoid(gate_ref[...]).T
    tile4 = lambda a: jnp.concatenate([a] * B_GROUP, axis=1)
    for g in range(B_KV_HEADS):
        qgt = _group_q_t(q_ref, g)
        pair = lambda base: slice(base + (g // 2) * PAIR, base + (g // 2 + 1) * PAIR)
        sel_t = sel_ref[:, g * PAIR:(g + 1) * PAIR].T
        far_vec = jnp.where((tile4(sel_t) > 0.5) & (blk < first_near), 0.0, -MASK_C).astype(BF16)
        near_sel = tile4(jnp.dot(to_near, sel_t.astype(BF16), preferred_element_type=F32))
        near_vec = jnp.where(near_sel > 0.5, 0.0, -MASK_C).astype(BF16) + exists_near
        qxt_far = jnp.concatenate([qgt, far_vec], axis=0)

        def far_step(t, st):
            keys = pl.ds(pl.multiple_of(KV_PAD + t * FAR_TILE, KV_PAD), FAR_TILE)
            kx = jnp.concatenate([k_ref[keys, pair(0)], oh_ref[keys, :]], axis=1)
            return _flash_step_t(st, kx, qxt_far, vt_ref[pair(0), keys])

        st = lax.fori_loop(0, n_far, far_step, init)
        st = _flash_step_t(st, jnp.concatenate([k_ref[near_keys, pair(0)], ohr_ref[:2 * nq, :]], axis=1),
                           jnp.concatenate([qgt, near_vec], axis=0), vt_ref[pair(0), near_keys], bias_ref[g])
        o_s = st[2] * _inv_or_zero(st[1])
        sw = _flash_step_t(init, jnp.concatenate([k_ref[win_keys, pair(KV_WIDTH)], ohr_ref[...]], axis=1),
                           jnp.concatenate([qgt, exists_win], axis=0), vt_ref[pair(KV_WIDTH), win_keys], wbias_ref[g])
        o_w = sw[2] * _inv_or_zero(sw[1])
        gate = lambda branch: jnp.concatenate(
            [gates_t[branch * B_HEADS + g * B_GROUP + r:branch * B_HEADS + g * B_GROUP + r + 1, :] for r in range(B_GROUP)],
            axis=1)
        o = (o_s * gate(1) + o_w * gate(2)).T
        cols = slice(g * 2 * PAIR, (g + 1) * 2 * PAIR)
        o_ref[:, cols] = _scatter_heads(o, g) + oc_ref[:, cols]


def _block_onehots(t):
    pos = np.arange(KV_PAD + t) - KV_PAD
    absolute = (pos[:, None] // SEL_BLOCK == np.arange(PAIR)[None, :]) & (pos[:, None] >= 0)
    relative = np.arange(WINDOW + Q_ROWS)[:, None] // SEL_BLOCK == np.arange(PAIR)[None, :]
    i = np.arange(B_GROUP * Q_ROWS)[:, None] % Q_ROWS
    in_window = np.arange(WINDOW - Q_ROWS)[None, :] > i
    return (jnp.asarray(absolute, BF16), jnp.asarray(relative, BF16),
            jnp.asarray(np.where(in_window, 0.0, -MASK_C), F32))


def _nsa_prompt(u, kv_pad, cmp_tok, bias_cmp, bias_kv):
    t = u.shape[0]
    nq = Q_ROWS
    q_col = (2 * A_QK + 2 * A_WIDTH) // B_WIDTH
    g_col = (IN_DIM - 3 * B_HEADS) // PAIR
    q_spec = pl.BlockSpec((nq, B_WIDTH), lambda i: (i, q_col))
    gate_spec = pl.BlockSpec((nq, PAIR), lambda i: (i, g_col))
    oc, sel = pl.pallas_call(
        _nsa_cmp_prompt_kernel,
        grid=(t // nq,),
        in_specs=[q_spec, gate_spec,
                  pl.BlockSpec(cmp_tok.shape, lambda i: (0, 0, 0)),
                  pl.BlockSpec(bias_cmp.shape, lambda i: (0, 0, 0))],
        out_specs=[pl.BlockSpec((nq, B_WIDTH), lambda i: (i, 0)), pl.BlockSpec((nq, B_KV_HEADS * PAIR), lambda i: (i, 0))],
        out_shape=[jax.ShapeDtypeStruct((t, B_WIDTH), F32), jax.ShapeDtypeStruct((t, B_KV_HEADS * PAIR), F32)],
        compiler_params=_params("arbitrary"),
        name="nsa_cmp_select_prompt",
    )(u, u, cmp_tok, bias_cmp)
    oh_abs, oh_rel, win_mask = _block_onehots(t)
    win_bias = jnp.concatenate([jnp.broadcast_to(win_mask, (B_KV_HEADS,) + win_mask.shape), bias_kv], axis=2)
    kw = KV_WIDTH
    k_pad = jnp.concatenate([kv_pad[:, :kw], kv_pad[:, 2 * kw:3 * kw]], axis=1)
    vt_pad = jnp.concatenate([kv_pad[:, kw:2 * kw], kv_pad[:, 3 * kw:]], axis=1).T
    bias_kv_t, win_bias_t = bias_kv.transpose(0, 2, 1), win_bias.transpose(0, 2, 1)
    whole = lambda a: pl.BlockSpec(a.shape, lambda i: (0,) * a.ndim, pipeline_mode=pl.Buffered(1))
    return pl.pallas_call(
        _nsa_selwin_prompt_kernel,
        grid=(t // nq,),
        in_specs=[q_spec, gate_spec,
                  pl.BlockSpec((nq, B_KV_HEADS * PAIR), lambda i: (i, 0)),
                  pl.BlockSpec((nq, B_WIDTH), lambda i: (i, 0)),
                  whole(k_pad), whole(vt_pad), whole(oh_abs), whole(oh_rel), whole(bias_kv_t), whole(win_bias_t)],
        out_specs=pl.BlockSpec((nq, B_WIDTH), lambda i: (i, 0)),
        out_shape=jax.ShapeDtypeStruct((t, B_WIDTH), F32),
        compiler_params=_params("arbitrary"),
        name="nsa_select_window_prompt",
    )(u, u, sel, oc, k_pad, vt_pad, oh_abs, oh_rel, bias_kv_t, win_bias_t)


SEL_KEYS = PAST_LEN + PAIR
WIN_KEYS = WINDOW + PAIR


NSA_SEQ_BLOCK = 2


def _piece_t(qg, kt, vt, mask, bias):
    s = jnp.dot(qg, kt, preferred_element_type=F32) + bias
    s = jnp.where(mask, s, NEG)
    m = jnp.max(s, axis=-1, keepdims=True)
    p = jnp.where(mask, jnp.exp(s - m), 0.0)
    return m, jnp.sum(p, axis=-1, keepdims=True), _nt_dot(p.astype(BF16), vt), p


def _nsa_sample_kernel(pt_ref, *refs):
    n_pg = NSA_SEQ_BLOCK * N_PAGES
    pages = refs[:n_pg]
    (q_ref, gate_ref, cmp_ref, selnew_ref, winnew_ref, win_ref, bc_ref, bs_ref, bw_ref,
     o_ref, winout_ref, kt_ref, vt_ref, wkt_ref, wvt_ref) = refs[n_pg:]
    seqs = [_nsa_sample_one(pages[s * N_PAGES:(s + 1) * N_PAGES], q_ref.at[s], gate_ref.at[s], cmp_ref.at[s],
                            selnew_ref.at[s], winnew_ref.at[s], win_ref.at[s], bc_ref, bs_ref, bw_ref,
                            o_ref.at[s], winout_ref.at[s], kt_ref.at[s], vt_ref.at[s], wkt_ref.at[s], wvt_ref.at[s])
            for s in range(NSA_SEQ_BLOCK)]
    for _ in range(NSA_SAMPLE_STAGES):
        for seq in seqs:
            next(seq)


NSA_SAMPLE_STAGES = 5


def _nsa_sample_one(pages, q_ref, gate_ref, cmp_ref, selnew_ref, winnew_ref, win_ref, bc_ref, bs_ref, bw_ref,
                    o_ref, winout_ref, kt_ref, vt_ref, wkt_ref, wvt_ref):
    nt = q_ref.shape[0]
    grp = B_GROUP * nt
    rows = B_KV_HEADS * grp
    half = 2 * PAIR
    pad = jnp.zeros((PAIR - nt, 2 * half), F32)
    sel_new_t = jnp.concatenate([selnew_ref[...], pad], axis=0).T
    win_new_t = jnp.concatenate([winnew_ref[...], pad], axis=0).T
    for j, pg in enumerate(pages):
        kt_ref[:, j * PAGE_SIZE:(j + 1) * PAGE_SIZE] = pg[:half, :].astype(BF16)
        vt_ref[:, j * PAGE_SIZE:(j + 1) * PAGE_SIZE] = pg[half:, :].astype(BF16)
    kt_ref[:, PAST_LEN:] = sel_new_t[:half].astype(BF16)
    vt_ref[:, PAST_LEN:] = sel_new_t[half:].astype(BF16)
    buf = win_ref[...]
    wkt_ref[:, :WINDOW] = buf[:half].astype(BF16)
    wvt_ref[:, :WINDOW] = buf[half:].astype(BF16)
    wkt_ref[:, WINDOW:] = win_new_t[:half].astype(BF16)
    wvt_ref[:, WINDOW:] = win_new_t[half:].astype(BF16)
    shifted = pltpu.roll(buf, WINDOW - nt, axis=1)
    tail = pltpu.roll(win_new_t, PAIR - nt, axis=1)
    lane = lax.broadcasted_iota(jnp.int32, tail.shape, 1)
    winout_ref[:, :WINDOW - PAIR] = shifted[:, :WINDOW - PAIR]
    winout_ref[:, WINDOW - PAIR:] = jnp.where(lane >= PAIR - nt, tail, shifted[:, WINDOW - PAIR:])
    yield

    zero = jnp.zeros((grp, PAIR), BF16)
    qq = jnp.concatenate(
        [jnp.concatenate([_group_q(q_ref, g), zero] if g // 2 == 0 else [zero, _group_q(q_ref, g)], axis=1)
         for g in range(B_KV_HEADS)], axis=0)
    take = lambda acc, g: acc[g * grp:(g + 1) * grp, (g // 2) * PAIR:(g // 2 + 1) * PAIR]
    qpos = PAST_LEN + lax.broadcasted_iota(jnp.int32, (rows, 1), 0) % nt

    n_tok = cmp_ref.shape[1]
    ck = jnp.concatenate([cmp_ref[0], cmp_ref[1]], axis=1).astype(BF16)
    cv = jnp.concatenate([cmp_ref[2], cmp_ref[3]], axis=1).astype(BF16)
    mask_c = lax.broadcasted_iota(jnp.int32, (rows, n_tok), 1) < n_tok - (N_SUB - 1)
    mc, lc, acc_c, pc = _piece(qq, ck, cv, mask_c, bc_ref[...])
    pc = pc * _inv_or_zero(lc)
    p_all = jnp.concatenate([jnp.sum(pc[g * grp:(g + 1) * grp].reshape(B_GROUP, nt, n_tok), axis=0)
                             for g in range(B_KV_HEADS)], axis=0)
    yield
    o01 = _overlap01(lax.broadcasted_iota(jnp.int32, (n_tok, PAIR), 0), lax.broadcasted_iota(jnp.int32, (n_tok, PAIR), 1))
    qpos_gt = PAST_LEN + lax.broadcasted_iota(jnp.int32, (B_KV_HEADS * nt, PAIR), 0) % nt
    n_blocks = pl.cdiv(PAST_LEN + nt, SEL_BLOCK)
    sel = _select_blocks(_dot01(p_all, o01), qpos_gt, n_cand=n_blocks).astype(BF16)
    sel_rows = jnp.concatenate([sel[g * nt:(g + 1) * nt] for g in range(B_KV_HEADS) for _ in range(B_GROUP)], axis=0)
    yield

    ks = lax.broadcasted_iota(jnp.int32, (rows, SEL_KEYS), 1)
    mask_s = (_expand_blocks(sel_rows, 0, SEL_KEYS) > 0.5) & (ks <= qpos)
    ms, ls, acc_s, _ = _piece_t(qq, kt_ref[...], vt_ref[...], mask_s, bs_ref[...])
    yield
    kw = PAST_LEN - WINDOW + lax.broadcasted_iota(jnp.int32, (rows, WIN_KEYS), 1)
    mask_w = (kw <= qpos) & (qpos - kw < WINDOW)
    mw, lw, acc_w, _ = _piece_t(qq, wkt_ref[...], wvt_ref[...], mask_w, bw_ref[...])
    acc_c, acc_s, acc_w = acc_c * _inv_or_zero(lc), acc_s * _inv_or_zero(ls), acc_w * _inv_or_zero(lw)
    for g in range(B_KV_HEADS):
        o = (take(acc_c, g) * _gate_rows(gate_ref, 0, g) + take(acc_s, g) * _gate_rows(gate_ref, 1, g)
             + take(acc_w, g) * _gate_rows(gate_ref, 2, g))
        o_ref[:, g * 2 * PAIR:(g + 1) * 2 * PAIR] = _scatter_heads(o, g)
    yield


def _nsa_sample(u_s, cmp_tok, cache_sel, win_buf, page_table, bias_c, bias_s, bias_w):
    n_seq, nt, _ = u_s.shape
    sb = NSA_SEQ_BLOCK
    q_col = (2 * A_QK + 2 * A_WIDTH) // B_WIDTH
    kv_col = (2 * A_QK + 2 * A_WIDTH + B_WIDTH) // (4 * PAIR)
    g_col = (IN_DIM - 3 * B_HEADS) // PAIR
    const = lambda a: pl.BlockSpec(a.shape, lambda b, pt: (0, 0))
    in_specs = [pl.BlockSpec((None, 4 * PAIR, PAGE_SIZE), lambda b, pt, s=s, j=j: (pt[b * sb + s, j], 0, 0))
                for s in range(sb) for j in range(N_PAGES)]
    in_specs += [
        pl.BlockSpec((sb, nt, B_WIDTH), lambda b, pt: (b, 0, q_col)),
        pl.BlockSpec((sb, nt, PAIR), lambda b, pt: (b, 0, g_col)),
        pl.BlockSpec((sb,) + cmp_tok.shape[1:], lambda b, pt: (b, 0, 0, 0)),
        pl.BlockSpec((sb, nt, 4 * PAIR), lambda b, pt: (b, 0, kv_col + 1)),
        pl.BlockSpec((sb, nt, 4 * PAIR), lambda b, pt: (b, 0, kv_col + 2)),
        pl.BlockSpec((sb, 4 * PAIR, WINDOW), lambda b, pt: (b, 0, 0)),
        const(bias_c), const(bias_s), const(bias_w),
    ]
    return pl.pallas_call(
        _nsa_sample_kernel,
        grid_spec=pltpu.PrefetchScalarGridSpec(
            num_scalar_prefetch=1,
            grid=(n_seq // sb,),
            in_specs=in_specs,
            out_specs=[pl.BlockSpec((sb, nt, B_WIDTH), lambda b, pt: (b, 0, 0)),
                       pl.BlockSpec((sb, 4 * PAIR, WINDOW), lambda b, pt: (b, 0, 0))],
            scratch_shapes=[pltpu.VMEM((sb, 2 * PAIR, SEL_KEYS), BF16), pltpu.VMEM((sb, 2 * PAIR, SEL_KEYS), BF16),
                            pltpu.VMEM((sb, 2 * PAIR, WIN_KEYS), BF16), pltpu.VMEM((sb, 2 * PAIR, WIN_KEYS), BF16)],
        ),
        out_shape=[jax.ShapeDtypeStruct((n_seq, nt, B_WIDTH), F32), jax.ShapeDtypeStruct(win_buf.shape, F32)],
        compiler_params=_params("arbitrary"),
        name="nsa_sample",
    )(page_table, *([cache_sel] * (sb * N_PAGES)), u_s, u_s, cmp_tok, u_s, u_s, win_buf, bias_c, bias_s, bias_w)


PROMPT_TM = 1024
PROJ_TN = 512
IN_PROJ_TN = 1024
OUT_PROJ_TN = 1024
COL_CMP = 2 * A_QK + 2 * A_WIDTH + B_WIDTH
COL_SEL = COL_CMP + 2 * KV_WIDTH
COL_WIN = COL_SEL + 2 * KV_WIDTH
COL_GATE = COL_WIN + 2 * KV_WIDTH


def _time_major(a):
    return a.transpose(1, 0, 2).reshape(a.shape[0] * a.shape[1], a.shape[2])


def _seq_major(a2d, n_seq):
    return a2d.reshape(a2d.shape[0] // n_seq, n_seq, a2d.shape[1]).transpose(1, 0, 2)


def _kv_rows(u3, col):
    return u3[..., col:col + 2 * KV_WIDTH].reshape(u3.shape[:-1] + (2, B_KV_HEADS, B_DH))


def kernel(x_prompt, x_sample, cache_cmp, cache_sel, state_win, state_hgrn, state_conv, page_table, c_prompt, c_sample, norm_g, ada_w, ada_b, ffn_w_gate, ffn_w_up, ffn_w_down, w_in_even, hgrn_lower_bound, hgrn_norm_g, cmp_pe, cmp_w1, cmp_b1, cmp_w2, cmp_b2, rel_bias, w_out_even, conv_w_pw1, conv_b_pw1, conv_w_dw, conv_b_dw, conv_ln_g, conv_ln_b, conv_w_pw2, conv_b_pw2, final_norm_g):
    P = {'norm_g': norm_g, 'ffn_w_gate': ffn_w_gate, 'ffn_w_up': ffn_w_up, 'ffn_w_down': ffn_w_down,
         'final_norm_g': final_norm_g}
    n_seq = x_sample.shape[0]
    n_pool = cache_cmp.shape[1]
    rows_p, rows_s = _prompt_rows(PROMPT_TM), _sample_rows()
    g4 = norm_g.reshape(DEPTH, 3, 1, D_MODEL)
    c_all = jnp.concatenate([c_sample, jnp.tile(c_prompt, (SUBLANES, 1))], axis=0)
    mod = _ada_mod(c_all, ada_w, ada_b)
    bias_pc, bias_pkv, bias_sc, bias_ss, bias_sw = _bias_tables(rel_bias)

    xp = x_prompt.reshape(SEQ, D_MODEL)
    xs = _time_major(x_sample)
    cmp_p, cmp_s, sel_p, sel_s, win_p, win_s, hgrn_p, hgrn_s, conv_p, conv_s = ([] for _ in range(10))
    for l in range(DEPTH):
        i = l // 2
        last = l == DEPTH - 1
        xs, w_bf16 = _ffn(xs, rows_s, mod, l, 0, P)
        xp, _ = _ffn(xp, rows_p, mod, l, 0, P, w_bf16=w_bf16)
        if l % 2 == 0:
            w_in_t = w_in_even[i:i + 1].transpose(0, 2, 1)
            up = _proj(xp, rows_p, mod, l, g4, w_in_t, IN_DIM, IN_PROJ_TN)
            us = _seq_major(_proj(xs, rows_s, mod, l, g4, w_in_t, IN_DIM, IN_PROJ_TN), n_seq)
            gn = hgrn_norm_g[i:i + 1]
            oa_p, hp = _hgrn(up, hgrn_lower_bound, gn, l)
            oa_s, hs = _hgrn(us.reshape(n_seq * DEC_SEQ, IN_DIM), hgrn_lower_bound, gn, l, state_hgrn[i])
            weights = _cmp_weights(cmp_pe[i], cmp_w1[i], cmp_b1[i], cmp_w2[i], cmp_b2[i])
            kv_pad = jnp.pad(up[:, COL_SEL:COL_GATE].astype(BF16), ((KV_PAD, 0), (0, 0)))
            ob_p = _nsa_prompt(up, kv_pad, _compress_prompt(up, weights), bias_pc, bias_pkv)
            cmp_t = cache_cmp[i].transpose(0, 2, 3, 4, 1).reshape(n_pool, 4 * PAIR, PAGE_SIZE)
            cmp_tok_s = _compress_sample(cmp_t, page_table, weights)
            sel_t = cache_sel[i].transpose(0, 2, 3, 4, 1).reshape(n_pool, 4 * PAIR, PAGE_SIZE)
            win_t = state_win[i].transpose(0, 2, 3, 4, 1).reshape(n_seq, 4 * PAIR, WINDOW)
            ob_s, wn = _nsa_sample(us, cmp_tok_s, sel_t, win_t, page_table, bias_sc, bias_ss, bias_sw)
            xp = _out_proj([oa_p, ob_p], w_out_even[i:i + 1], [0, A_WIDTH], None, xp, rows_p, mod, l, OUT_PROJ_TN)
            xs = _out_proj([_time_major(oa_s.reshape(n_seq, DEC_SEQ, A_WIDTH)), _time_major(ob_s)],
                           w_out_even[i:i + 1], [0, A_WIDTH], None, xs, rows_s, mod, l, PROJ_TN)
            up3 = up[None]
            cmp_p.append(_kv_rows(up3, COL_CMP))
            sel_p.append(_kv_rows(up3, COL_SEL))
            win_p.append(_kv_rows(up3[:, SEQ - min(WINDOW, SEQ):], COL_WIN))
            cmp_s.append(_kv_rows(us, COL_CMP))
            sel_s.append(_kv_rows(us, COL_SEL))
            win_s.append(wn.reshape(n_seq, 2, B_KV_HEADS, B_DH, WINDOW).transpose(0, 4, 1, 2, 3))
            hgrn_p.append(hp[None])
            hgrn_s.append(hs)
        else:
            b_pw1 = conv_b_pw1[i].reshape(1, 1, 2 * CONV_DIM)
            b_pw2 = conv_b_pw2[i].reshape(1, 1, D_MODEL)
            vec = lambda a: a[i].reshape(1, CONV_DIM)
            glu_p = _glu_proj(xp, rows_p, mod, l, g4, conv_w_pw1[i:i + 1], b_pw1, PROJ_TN)
            act_p = _conv_prompt(glu_p, conv_w_dw[i], vec(conv_b_dw), vec(conv_ln_g), vec(conv_ln_b))
            xp = _out_proj([act_p], conv_w_pw2[i:i + 1], [0], b_pw2, xp, rows_p, mod, l, OUT_PROJ_TN)
            glu_s = _glu_proj(xs, rows_s, mod, l, g4, conv_w_pw1[i:i + 1], b_pw1, PROJ_TN)
            act_s, nb = _conv_sample(glu_s.reshape(DEC_SEQ, n_seq, CONV_DIM), state_conv[i].transpose(1, 0, 2),
                                     conv_w_dw[i], vec(conv_b_dw), vec(conv_ln_g), vec(conv_ln_b))
            xs = _out_proj([act_s.reshape(DEC_SEQ * n_seq, CONV_DIM)], conv_w_pw2[i:i + 1], [0], b_pw2, xs, rows_s, mod, l, PROJ_TN)
            conv_p.append(glu_p[None, SEQ - (CONV_WIDTH - 1):])
            conv_s.append(nb.transpose(1, 0, 2))
        xs, w_bf16 = _ffn(xs, rows_s, mod, l, 2, P, final_norm=last)
        xp, _ = _ffn(xp, rows_p, mod, l, 2, P, w_bf16=w_bf16, final_norm=last)
    y_prompt = xp.reshape(1, SEQ, D_MODEL)
    y_sample = _seq_major(xs, n_seq)
    st = jnp.stack
    return (y_prompt, y_sample, st(cmp_p), st(cmp_s), st(sel_p), st(sel_s), st(win_p), st(win_s),
            st(hgrn_p), st(hgrn_s), st(conv_p), st(conv_s))
```
